```python
import jax, jax.numpy as jnp
from jax import lax
import numpy as np

D_MODEL = 2048
BATCH = 2
SEQ = 8192
DEPTH = 4

GRID_W = 64
CTX_LEN = 256

HEAD_DIM = 128
ATTN_HEADS = 8
ATTN_KV_HEADS = 2
ATTN_WIDTH = ATTN_HEADS * HEAD_DIM
ATTN_KV_WIDTH = ATTN_KV_HEADS * HEAD_DIM
Q_BLOCK = 128
ROPE_BASE = 10000.0

RET_HEADS = 4
RET_QK_DIM = 128
RET_V_DIM = 256
RET_QK_WIDTH = RET_HEADS * RET_QK_DIM
RET_V_WIDTH = RET_HEADS * RET_V_DIM

SSM_INNER = 1024
SSM_HEAD_DIM = 64
SSM_HEADS = SSM_INNER // SSM_HEAD_DIM
SSM_GROUPS = 2
SSM_HPG = SSM_HEADS // SSM_GROUPS
SSM_STATE = 128
SSM_CONV = 5
SSM_CONV_CH = SSM_INNER + 2 * SSM_GROUPS * SSM_STATE

CHUNK = 128
BRANCH_WIDTH = 1024
N_BRANCHES = 3

IN_SPLITS = (ATTN_WIDTH, ATTN_KV_WIDTH, ATTN_KV_WIDTH, ATTN_WIDTH,
             RET_QK_WIDTH, RET_QK_WIDTH, RET_V_WIDTH, RET_V_WIDTH,
             SSM_INNER, SSM_GROUPS * SSM_STATE, SSM_GROUPS * SSM_STATE,
             2 * SSM_HEADS, SSM_INNER,
             N_BRANCHES * D_MODEL)
IN_WIDTH = sum(IN_SPLITS)
IN_INDICES = [int(i) for i in np.cumsum(IN_SPLITS)[:-1]]

DEEPNORM_ALPHA = (2 * DEPTH) ** 0.25
DEEPNORM_BETA = (8 * DEPTH) ** -0.25
NORM_EPS = 1e-6

kernel_name = 'hybrid_attn_retention_ssd_prefix_trunk'


def _standardize(x):
    xf = x.astype(jnp.float32)
    mu = jnp.mean(xf, -1, keepdims=True)
    var = jnp.mean(jnp.square(xf - mu), -1, keepdims=True)
    return ((xf - mu) * lax.rsqrt(var + NORM_EPS)).astype(x.dtype)


def _layer_norm(x, g, b):
    return _standardize(x) * g + b


def _rms_normalize(x):
    xf = x.astype(jnp.float32)
    return (xf * lax.rsqrt(jnp.mean(xf * xf, -1, keepdims=True) + NORM_EPS)).astype(x.dtype)


def _axial_rope_tables(rows, dtype):
    row = jnp.repeat(jnp.arange(rows, dtype=jnp.float32), GRID_W)
    col = jnp.tile(jnp.arange(GRID_W, dtype=jnp.float32), rows)
    n_freq = HEAD_DIM // 4
    inv = ROPE_BASE ** (-jnp.arange(n_freq, dtype=jnp.float32) / n_freq)
    ang_r = row[:, None] * inv
    ang_c = col[:, None] * inv
    ang = jnp.concatenate([ang_r, ang_r, ang_c, ang_c], -1)
    return jnp.cos(ang).astype(dtype), jnp.sin(ang).astype(dtype)


def _apply_axial_rope(x, cos, sin):
    half = x.shape[-1] // 2
    quarter = half // 2
    def rot(z):
        return jnp.concatenate([-z[..., quarter:], z[..., :quarter]], -1)
    xrot = jnp.concatenate([rot(x[..., :half]), rot(x[..., half:])], -1)
    return x * cos[:, None, :] + xrot * sin[:, None, :]


def _dwconv(u, w, b):
    pad = (SSM_CONV - 1) // 2
    out = lax.conv_general_dilated(u, w[:, None, :], window_strides=(1,), padding=[(pad, pad)],
                                   dimension_numbers=('NWC', 'WIO', 'NWC'),
                                   feature_group_count=u.shape[-1])
    return out + b


def _attend_blocks(q, keys, vals):
    B, Lq, Hq, Dh = q.shape
    G = Hq // ATTN_KV_HEADS
    qb = q.reshape(B, Lq // Q_BLOCK, Q_BLOCK, ATTN_KV_HEADS, G, Dh).transpose(1, 0, 3, 4, 2, 5)
    scale = Dh ** -0.5
    def block(qblk):
        s = jnp.einsum('bhgqd,bkhd->bhgqk', qblk, keys).astype(jnp.float32) * scale
        p = jax.nn.softmax(s, axis=-1).astype(vals.dtype)
        return jnp.einsum('bhgqk,bkhd->bhgqd', p, vals)
    o = lax.map(block, qb)
    return o.transpose(1, 0, 4, 2, 3, 5).reshape(B, Lq, Hq * Dh)


def _chunked_scan(q, k, v, log_a, s0):
    B, L, G, N = q.shape
    nc = L // CHUNK
    def chunks(t):
        return jnp.moveaxis(t.reshape(B, nc, CHUNK, *t.shape[2:]), 1, 0)
    lower = jnp.tril(jnp.ones((CHUNK, CHUNK), dtype=bool))
    def body(state, inp):
        qc, kc, vc, ac = inp
        acum = jnp.cumsum(ac, axis=1)
        at = jnp.moveaxis(acum, 1, -1)
        diff = at[..., :, None] - at[..., None, :]
        decay = jnp.exp(jnp.where(lower, diff, -jnp.inf)).astype(vc.dtype)
        scores = jnp.einsum('bign,bjgn->bgij', qc, kc)
        y = jnp.einsum('bghij,bjghp->bighp', scores[:, :, None] * decay, vc)
        y = y + jnp.einsum('bign,bghnp->bighp', qc, state) * jnp.exp(acum)[..., None].astype(vc.dtype)
        last = acum[:, -1]
        w = jnp.exp(last[:, None] - acum).astype(vc.dtype)
        new = state * jnp.exp(last)[..., None, None].astype(state.dtype) \
            + jnp.einsum('bjgn,bjgh,bjghp->bghnp', kc, w, vc)
        return new, y
    s_final, ys = lax.scan(body, s0, (chunks(q), chunks(k), chunks(v), chunks(log_a)))
    y = jnp.moveaxis(ys, 0, 1).reshape(B, L, *v.shape[2:])
    return y, s_final


def _two_way(q, k, v_f, v_b, la_f, la_b, s0_f, s0_b):
    y_f, s_f = _chunked_scan(q, k, v_f, la_f, s0_f)
    fl = lambda t: jnp.flip(t, axis=1)
    y_b, s_b = _chunked_scan(fl(q), fl(k), fl(v_b), fl(la_b), s0_b)
    return y_f + fl(y_b), s_f, s_b


def _prepare(u, w_in, q_norm_w, k_norm_w, conv_w, conv_b, rope):
    B, L, _ = u.shape
    (aq, ak, av, ag, rq, rk, rv, rg, sx, sb, sc, sdt, sz, mg) = jnp.split(u @ w_in, IN_INDICES, axis=-1)
    aq = _rms_normalize(aq.reshape(B, L, ATTN_HEADS, HEAD_DIM)) * q_norm_w
    ak = _rms_normalize(ak.reshape(B, L, ATTN_KV_HEADS, HEAD_DIM)) * k_norm_w
    av = av.reshape(B, L, ATTN_KV_HEADS, HEAD_DIM)
    rq = rq.reshape(B, L, RET_HEADS, RET_QK_DIM)
    rk = rk.reshape(B, L, RET_HEADS, RET_QK_DIM) * (RET_QK_DIM ** -0.5)
    if rope is not None:
        cos, sin = rope
        aq, ak, rq, rk = (_apply_axial_rope(t, cos, sin) for t in (aq, ak, rq, rk))
    rv = rv.reshape(B, L, RET_HEADS, 1, RET_V_DIM)
    xbc = jax.nn.silu(_dwconv(jnp.concatenate([sx, sb, sc], -1), conv_w, conv_b))
    sx, sb, sc = jnp.split(xbc, [SSM_INNER, SSM_INNER + SSM_GROUPS * SSM_STATE], axis=-1)
    return {'aq': aq, 'ak': ak, 'av': av, 'ag': ag, 'rq': rq, 'rk': rk, 'rv': rv, 'rg': rg,
            'sx': sx, 'sb': sb, 'sc': sc, 'sdt': sdt, 'sz': sz, 'mg': mg}


def _retention(t, log_decay, s0_f, s0_b):
    B, L = t['rq'].shape[:2]
    la_f, la_b = (jnp.broadcast_to(log_decay[d].astype(jnp.float32)[:, None], (B, L, RET_HEADS, 1))
                  for d in range(2))
    return _two_way(t['rq'], t['rk'], t['rv'], t['rv'], la_f, la_b, s0_f, s0_b)


def _ssd(t, dt_bias, a_log, d_skip, s0_f, s0_b):
    B, L = t['sx'].shape[:2]
    x = t['sx'].reshape(B, L, SSM_GROUPS, SSM_HPG, SSM_HEAD_DIM)
    bm = t['sb'].reshape(B, L, SSM_GROUPS, SSM_STATE)
    cm = t['sc'].reshape(B, L, SSM_GROUPS, SSM_STATE)
    dt = jax.nn.softplus(t['sdt'].astype(jnp.float32).reshape(B, L, 2, SSM_HEADS)
                         + dt_bias.astype(jnp.float32))
    la = (dt * -jnp.exp(a_log.astype(jnp.float32))).reshape(B, L, 2, SSM_GROUPS, SSM_HPG)
    dt = dt.reshape(B, L, 2, SSM_GROUPS, SSM_HPG).astype(x.dtype)
    v_f = x * dt[:, :, 0, :, :, None]
    v_b = x * dt[:, :, 1, :, :, None]
    y, s_f, s_b = _two_way(cm, bm, v_f, v_b, la[:, :, 0], la[:, :, 1], s0_f, s0_b)
    y = y + d_skip.reshape(SSM_GROUPS, SSM_HPG, 1) * x
    return y, s_f, s_b


def _merge(t, attn, ret, ssm, ret_gn_w, ssm_norm_w, w_branch, w_out):
    B, L = attn.shape[:2]
    a = attn * jax.nn.silu(t['ag'])
    r = _standardize(ret.reshape(B, L, RET_HEADS, RET_V_DIM)).reshape(B, L, RET_V_WIDTH) \
        * ret_gn_w * jax.nn.silu(t['rg'])
    s = ssm.reshape(B, L, SSM_INNER) * jax.nn.silu(t['sz'])
    s = _rms_normalize(s.reshape(B, L, SSM_GROUPS, SSM_INNER // SSM_GROUPS)).reshape(B, L, SSM_INNER) * ssm_norm_w
    gates = jax.nn.sigmoid(t['mg'].reshape(B, L, N_BRANCHES, D_MODEL))
    merged = gates[:, :, 0] * (a @ w_branch[0]) + gates[:, :, 1] * (r @ w_branch[1]) \
        + gates[:, :, 2] * (s @ w_branch[2])
    return merged @ w_out


def _layer(x, xc, c, c_ctx, rope, ada_w, ada_b, w_in, q_norm_w, k_norm_w, ret_log_decay, ret_gn_w,
           conv_w, conv_b, dt_bias, a_log, d_skip, ssm_norm_w, w_branch, w_out, ln_g, ln_b, update_ctx):
    B = x.shape[0]
    shift, scale, gate = jnp.split(jax.nn.silu(c) @ ada_w + ada_b, 3, axis=-1)
    shift_c, scale_c, gate_c = jnp.split(jax.nn.silu(c_ctx) @ ada_w + ada_b, 3, axis=-1)
    u = x * (1 + scale[:, None]) + shift[:, None]
    uc = xc * (1 + scale_c) + shift_c
    t = _prepare(u, w_in, q_norm_w, k_norm_w, conv_w, conv_b, rope)
    tc = _prepare(uc, w_in, q_norm_w, k_norm_w, conv_w, conv_b, None)

    zr = jnp.zeros((B, RET_HEADS, 1, RET_QK_DIM, RET_V_DIM), x.dtype)
    zs = jnp.zeros((B, SSM_GROUPS, SSM_HPG, SSM_STATE, SSM_HEAD_DIM), x.dtype)
    ret_c, ret_sf, ret_sb = _retention(tc, ret_log_decay, zr, zr)
    ssm_c, ssm_sf, ssm_sb = _ssd(tc, dt_bias, a_log, d_skip, zs, zs)

    attn = _attend_blocks(t['aq'], jnp.concatenate([t['ak'], tc['ak']], axis=1),
                          jnp.concatenate([t['av'], tc['av']], axis=1))
    ret, _, _ = _retention(t, ret_log_decay, ret_sf, ret_sb)
    ssm, _, _ = _ssd(t, dt_bias, a_log, d_skip, ssm_sf, ssm_sb)
    y = _merge(t, attn, ret, ssm, ret_gn_w, ssm_norm_w, w_branch, w_out)
    x_new = _layer_norm(DEEPNORM_ALPHA * x + gate[:, None] * y, ln_g, ln_b)

    if update_ctx:
        attn_c = _attend_blocks(tc['aq'], tc['ak'], tc['av'])
        yc = _merge(tc, attn_c, ret_c, ssm_c, ret_gn_w, ssm_norm_w, w_branch, w_out)
        xc = _layer_norm(DEEPNORM_ALPHA * xc + gate_c * yc, ln_g, ln_b)
    return x_new, xc


def setup_inputs(seed: int = 0) -> dict:
    key = jax.random.key(seed)
    ks = jax.random.split(key, 24)
    f32 = jnp.float32
    def nrm(k, shape, s):
        return s * jax.random.normal(k, shape, f32)
    x = nrm(ks[0], (BATCH, SEQ, D_MODEL), 1.0)
    c = nrm(ks[1], (BATCH, D_MODEL), 1.0)
    ctx = nrm(ks[2], (BATCH, CTX_LEN, D_MODEL), 1.0)
    c_ctx = nrm(ks[3], (D_MODEL,), 1.0)
    ada_w = nrm(ks[4], (DEPTH, D_MODEL, 3 * D_MODEL), D_MODEL ** -0.5)
    ada_b = nrm(ks[5], (DEPTH, 3 * D_MODEL), 0.02)
    w_in = nrm(ks[6], (DEPTH, D_MODEL, IN_WIDTH), D_MODEL ** -0.5)
    attn_q_norm = 1.0 + nrm(ks[7], (DEPTH, HEAD_DIM), 0.02)
    attn_k_norm = 1.0 + nrm(ks[8], (DEPTH, HEAD_DIM), 0.02)
    base = jnp.log1p(-(2.0 ** (-5.0 - jnp.arange(RET_HEADS, dtype=f32))))
    ret_log_decay = base * (1.0 + nrm(ks[9], (DEPTH, 2, RET_HEADS), 0.05))
    ret_gn_w = 1.0 + nrm(ks[10], (DEPTH, RET_V_WIDTH), 0.02)
    ssm_conv_w = nrm(ks[11], (DEPTH, SSM_CONV, SSM_CONV_CH), SSM_CONV ** -0.5)
    ssm_conv_b = nrm(ks[12], (DEPTH, SSM_CONV_CH), 0.02)
    dt0 = jnp.exp(jax.random.uniform(ks[13], (DEPTH, 2, SSM_HEADS), f32, np.log(1e-3), np.log(1e-1)))
    ssm_dt_bias = dt0 + jnp.log(-jnp.expm1(-dt0))
    ssm_a_log = jnp.log(jax.random.uniform(ks[14], (DEPTH, 2, SSM_HEADS), f32, 1.0, 16.0))
    ssm_d = 1.0 + nrm(ks[15], (DEPTH, SSM_HEADS), 0.1)
    ssm_norm_w = 1.0 + nrm(ks[16], (DEPTH, SSM_INNER), 0.02)
    w_branch = nrm(ks[17], (DEPTH, N_BRANCHES, BRANCH_WIDTH, D_MODEL), DEEPNORM_BETA * BRANCH_WIDTH ** -0.5)
    w_out = nrm(ks[18], (DEPTH, D_MODEL, D_MODEL), DEEPNORM_BETA * D_MODEL ** -0.5)
    ln_g = 1.0 + nrm(ks[19], (DEPTH, D_MODEL), 0.02)
    ln_b = nrm(ks[20], (DEPTH, D_MODEL), 0.02)
    return {'x': x, 'c': c, 'ctx': ctx, 'c_ctx': c_ctx, 'ada_w': ada_w, 'ada_b': ada_b, 'w_in': w_in,
            'attn_q_norm': attn_q_norm, 'attn_k_norm': attn_k_norm, 'ret_log_decay': ret_log_decay,
            'ret_gn_w': ret_gn_w, 'ssm_conv_w': ssm_conv_w, 'ssm_conv_b': ssm_conv_b,
            'ssm_dt_bias': ssm_dt_bias, 'ssm_a_log': ssm_a_log, 'ssm_d': ssm_d, 'ssm_norm_w': ssm_norm_w,
            'w_branch': w_branch, 'w_out': w_out, 'ln_g': ln_g, 'ln_b': ln_b}


def reference(x, c, ctx, c_ctx, ada_w, ada_b, w_in, attn_q_norm, attn_k_norm, ret_log_decay, ret_gn_w,
              ssm_conv_w, ssm_conv_b, ssm_dt_bias, ssm_a_log, ssm_d, ssm_norm_w, w_branch, w_out,
              ln_g, ln_b):
    rows = x.shape[1] // GRID_W
    rope = _axial_rope_tables(rows, x.dtype)
    xc = ctx
    for l in range(DEPTH):
        x, xc = _layer(x, xc, c, c_ctx, rope, ada_w[l], ada_b[l], w_in[l], attn_q_norm[l], attn_k_norm[l],
                       ret_log_decay[l], ret_gn_w[l], ssm_conv_w[l], ssm_conv_b[l], ssm_dt_bias[l],
                       ssm_a_log[l], ssm_d[l], ssm_norm_w[l], w_branch[l], w_out[l], ln_g[l], ln_b[l],
                       l < DEPTH - 1)
    return x
```

```python
import functools

import numpy as np
import jax
import jax.numpy as jnp
from jax import lax
from jax.experimental import pallas as pl
from jax.experimental.pallas import tpu as pltpu

F32 = jnp.float32
BF16 = jnp.bfloat16

D_MODEL = 2048
GRID_W = 64
HEAD_DIM = 128
ATTN_HEADS = 8
ATTN_KV_HEADS = 2
ATTN_GROUP = ATTN_HEADS // ATTN_KV_HEADS
ROPE_BASE = 10000.0
RET_HEADS = 4
RET_QK_DIM = 128
RET_V_DIM = 256
SSM_INNER = 1024
SSM_HEAD_DIM = 64
SSM_HEADS = 16
SSM_GROUPS = 2
SSM_HPG = SSM_HEADS // SSM_GROUPS
SSM_STATE = 128
SSM_CONV = 5
N_BRANCHES = 3
BRANCH_WIDTH = 1024
DEPTH = 4
DEEPNORM_ALPHA = (2 * DEPTH) ** 0.25
NORM_EPS = 1e-6

_REF_SPLITS = dict(aq=(0, 1024), ak=(1024, 256), av=(1280, 256), ag=(1536, 1024),
                   rq=(2560, 512), rk=(3072, 512), rv=(3584, 1024), rg=(4608, 1024),
                   sx=(5632, 1024), sb=(6656, 256), sc=(6912, 256), sdt=(7168, 32),
                   sz=(7200, 1024), mg=(8224, 6144))

LANES = 128
SUBLANES = 8
VMEM_LIMIT = 56 * 1024 * 1024

PROJ_TN = 512
ZB_ORDER = ("aq", "rv", "ak", "av", "rq", "rk")
ZB_W = 3584
ZB_TILES = ZB_W // PROJ_TN
ZF_ORDER = ("mg", "ag", "rg", "sz", "sx", "sb", "sc", "sdt")
ZF_USED = 6144 + 3 * 1024 + 1024 + 256 + 256 + 32
ZF_W = 11264
ZF_TILES = ZF_W // PROJ_TN
ZF_XBC_OFF = 9216
ZF_XBC_W = 1536
ZF_DT_OFF = 10752
W_PACKED = ZB_W + ZF_W


def _sigmoid(x):
    return 1.0 / (1.0 + jnp.exp(-x))


def _silu(x):
    return x * _sigmoid(x)


def _cparams(sem):
    return pltpu.CompilerParams(dimension_semantics=sem, vmem_limit_bytes=VMEM_LIMIT)


def _dot(a, b):
    return jnp.dot(a, b, preferred_element_type=F32)


def _dot_nt(a, b):
    return lax.dot_general(a, b, (((1,), (1,)), ((), ())), preferred_element_type=F32)


def _dot_tn(a, b):
    return lax.dot_general(a, b, (((0,), (0,)), ((), ())), preferred_element_type=F32)


def _ada_kernel(c_ref, w_ref, b_ref, o_ref):
    s = _silu(c_ref[...]).astype(BF16)
    o_ref[0] = _dot(s, w_ref[0].astype(BF16)) + b_ref[0]


def _ada_call(cvec, ada_w, ada_b):
    depth, d, n = ada_w.shape
    tn = 768
    return pl.pallas_call(
        _ada_kernel,
        grid=(depth, n // tn),
        in_specs=[pl.BlockSpec((SUBLANES, d), lambda l, j: (0, 0)),
                  pl.BlockSpec((1, d, tn), lambda l, j: (l, 0, j)),
                  pl.BlockSpec((1, 1, tn), lambda l, j: (l, 0, j))],
        out_specs=pl.BlockSpec((1, SUBLANES, tn), lambda l, j: (l, 0, j)),
        out_shape=jax.ShapeDtypeStruct((depth, SUBLANES, n), F32),
        compiler_params=_cparams(("parallel", "parallel")),
    )(cvec, ada_w, ada_b.reshape(depth, 1, n))


def _rope(x, cos, sin_lo, sin_hi):
    return x * cos + pltpu.roll(x, 96, 1) * sin_lo + pltpu.roll(x, 32, 1) * sin_hi


def _modulation_rows(mod_ref, b, n_batch, is_ctx, lo, hi):
    mb = mod_ref[pl.ds(b, 1), lo:hi]
    mc = mod_ref[pl.ds(n_batch, 1), lo:hi]
    return jnp.where(is_ctx, mc, mb)


def _inproj_kernel(x_ref, mod_ref, w_ref, cos_ref, sl_ref, sh_ref, qn_ref, kn_ref,
                   zb_ref, zf_ref, u_ref, *, tm, n_batch, ctx_len):
    b = pl.program_id(0)
    i = pl.program_id(1)
    j = pl.program_id(2)
    d = D_MODEL

    @pl.when(j == 0)
    def _():
        row = i * tm + lax.broadcasted_iota(jnp.int32, (tm, 1), 0)
        is_ctx = row < ctx_len
        shift = _modulation_rows(mod_ref, b, n_batch, is_ctx, 0, d)
        scale = _modulation_rows(mod_ref, b, n_batch, is_ctx, d, 2 * d)
        u_ref[...] = (x_ref[0] * (1.0 + scale) + shift).astype(BF16)

    acc = _dot(u_ref[...], w_ref[...])

    def per_head(fn, n):
        return jnp.concatenate([fn(acc[:, h * LANES:(h + 1) * LANES]) for h in range(n)], axis=1)

    def rope(xh):
        return _rope(xh, cos_ref[...], sl_ref[...], sh_ref[...])

    def normed(w_row):
        def fn(xh):
            ms = jnp.mean(xh * xh, axis=-1, keepdims=True)
            return rope(xh * lax.rsqrt(ms + NORM_EPS) * w_row)
        return fn

    @pl.when(j < 2)
    def _():
        zb_ref[0] = per_head(normed(qn_ref[...]), 4).astype(BF16)

    @pl.when((j == 2) | (j == 3))
    def _():
        zb_ref[0] = acc.astype(BF16)

    @pl.when(j == 4)
    def _():
        k = jnp.concatenate([normed(kn_ref[...])(acc[:, h * LANES:(h + 1) * LANES])
                             for h in range(2)], axis=1)
        zb_ref[0] = jnp.concatenate([k, acc[:, 2 * LANES:]], axis=1).astype(BF16)

    @pl.when(j == 5)
    def _():
        zb_ref[0] = per_head(rope, 4).astype(BF16)

    @pl.when(j == 6)
    def _():
        zb_ref[0] = per_head(lambda xh: rope(xh * (RET_QK_DIM ** -0.5)), 4).astype(BF16)

    @pl.when(j >= ZB_TILES)
    def _():
        zf_ref[0] = acc


def _inproj_call(xs, mod, w, cos, sin_lo, sin_hi, qn, kn, *, n_batch, ctx_len, tm):
    bsz, t, d = xs.shape
    kern = functools.partial(_inproj_kernel, tm=tm, n_batch=n_batch, ctx_len=ctx_len)
    rope_spec = pl.BlockSpec((tm, LANES), lambda b, i, j: (i, 0))
    vec_spec = pl.BlockSpec((1, LANES), lambda b, i, j: (0, 0))
    return pl.pallas_call(
        kern,
        grid=(bsz, t // tm, W_PACKED // PROJ_TN),
        in_specs=[pl.BlockSpec((1, tm, d), lambda b, i, j: (b, i, 0)),
                  pl.BlockSpec((SUBLANES, 3 * d), lambda b, i, j: (0, 0)),
                  pl.BlockSpec((d, PROJ_TN), lambda b, i, j: (0, j)),
                  rope_spec, rope_spec, rope_spec, vec_spec, vec_spec],
        out_specs=[pl.BlockSpec((1, tm, PROJ_TN),
                                lambda b, i, j: (b, i, jnp.minimum(j, ZB_TILES - 1))),
                   pl.BlockSpec((1, tm, PROJ_TN),
                                lambda b, i, j: (b, i, jnp.maximum(j - ZB_TILES, 0)))],
        out_shape=[jax.ShapeDtypeStruct((bsz, t, ZB_W), BF16),
                   jax.ShapeDtypeStruct((bsz, t, ZF_W), F32)],
        scratch_shapes=[pltpu.VMEM((tm, d), BF16)],
        compiler_params=_cparams(("parallel", "parallel", "arbitrary")),
    )(xs, mod, w, cos, sin_lo, sin_hi, qn, kn)


NEG_BIG = -1e30


def _attn_kernel(q_ref, k_ref, v_ref, o_ref, m_ref, l_ref, acc_ref, *, tq, tk, ctx_len):
    qi = pl.program_id(2)
    ki = pl.program_id(3)
    scale = HEAD_DIM ** -0.5

    @pl.when(ki == 0)
    def _():
        m_ref[...] = jnp.full(m_ref.shape, NEG_BIG, F32)
        l_ref[...] = jnp.zeros(l_ref.shape, F32)
        acc_ref[...] = jnp.zeros(acc_ref.shape, F32)

    def step(mixed):
        k = k_ref[0]
        v = v_ref[0]
        for h in range(ATTN_GROUP):
            q = q_ref[0, :, h * HEAD_DIM:(h + 1) * HEAD_DIM]
            s = _dot_nt(q, k) * scale
            if mixed:
                row = lax.broadcasted_iota(jnp.int32, (tq, 1), 0)
                col = ki * tk + lax.broadcasted_iota(jnp.int32, (1, tk), 1)
                s = jnp.where(row < ctx_len, jnp.where(col < ctx_len, s, NEG_BIG), s)
            m_prev = m_ref[h]
            m_new = jnp.maximum(m_prev, jnp.max(s, axis=-1, keepdims=True))
            alpha = jnp.exp(m_prev - m_new)
            p = jnp.exp(s - m_new)
            l_ref[h] = alpha * l_ref[h] + jnp.sum(p, axis=-1, keepdims=True)
            acc_ref[h] = alpha * acc_ref[h] + _dot(p.astype(BF16), v)
            m_ref[h] = m_new

    pl.when(qi == 0)(lambda: step(True))
    pl.when(qi > 0)(lambda: step(False))

    @pl.when(ki == pl.num_programs(3) - 1)
    def _():
        o_ref[0] = jnp.concatenate(
            [acc_ref[h] * (1.0 / l_ref[h]) for h in range(ATTN_GROUP)], axis=1)


def _attn_call(zb, *, ctx_len, tq, tk):
    bsz, t, _ = zb.shape
    assert ctx_len <= tq and ctx_len <= tk
    gw = ATTN_GROUP * HEAD_DIM
    k_blk = 2048 // HEAD_DIM
    v_blk = 2304 // HEAD_DIM
    kern = functools.partial(_attn_kernel, tq=tq, tk=tk, ctx_len=ctx_len)
    return pl.pallas_call(
        kern,
        grid=(bsz, ATTN_KV_HEADS, t // tq, t // tk),
        in_specs=[pl.BlockSpec((1, tq, gw), lambda b, g, qi, ki: (b, qi, g)),
                  pl.BlockSpec((1, tk, HEAD_DIM), lambda b, g, qi, ki: (b, ki, k_blk + g)),
                  pl.BlockSpec((1, tk, HEAD_DIM), lambda b, g, qi, ki: (b, ki, v_blk + g))],
        out_specs=pl.BlockSpec((1, tq, gw), lambda b, g, qi, ki: (b, qi, g)),
        out_shape=jax.ShapeDtypeStruct((bsz, t, ATTN_HEADS * HEAD_DIM), F32),
        scratch_shapes=[pltpu.VMEM((ATTN_GROUP, tq, 1), F32),
                        pltpu.VMEM((ATTN_GROUP, tq, 1), F32),
                        pltpu.VMEM((ATTN_GROUP, tq, HEAD_DIM), F32)],
        compiler_params=_cparams(("parallel", "parallel", "parallel", "arbitrary")),
    )(zb, zb, zb)


def _bwd_chunk(p, n_ctx_chunks, n_chunks):
    return jnp.where(p < n_ctx_chunks, n_ctx_chunks - 1 - p, n_chunks + n_ctx_chunks - 1 - p)


def _ret_kernel(ld_ref, qf_ref, kf_ref, vf_ref, qb_ref, kb_ref, vb_ref, yf_ref, yb_ref,
                sf_ref, sb_ref, *, chunk):
    c = chunk
    p = pl.program_id(1)

    @pl.when(p == 0)
    def _():
        sf_ref[...] = jnp.zeros(sf_ref.shape, F32)
        sb_ref[...] = jnp.zeros(sb_ref.shape, F32)

    ii = lax.broadcasted_iota(jnp.int32, (c, c), 0)
    jj = lax.broadcasted_iota(jnp.int32, (c, c), 1)
    dist = (ii - jj).astype(F32)
    pos = lax.broadcasted_iota(jnp.int32, (c, 1), 0).astype(F32)
    full = jnp.full((1, 1), float(c), F32)

    for h in range(RET_HEADS):
        ldf = ld_ref[0, h]
        ldb = ld_ref[1, h]
        qs = slice(h * RET_QK_DIM, (h + 1) * RET_QK_DIM)
        vs = slice(h * RET_V_DIM, (h + 1) * RET_V_DIM)
        decay = jnp.where(dist >= 0, jnp.exp(ldf * dist), 0.0) \
            + jnp.where(dist <= 0, jnp.exp(-ldb * dist), 0.0)
        q = qf_ref[0, :, qs]
        k = kf_ref[0, :, qs]
        v = vf_ref[0, :, vs]
        y = _dot((_dot_nt(q, k) * decay).astype(BF16), v)
        y = y + _dot(q, sf_ref[h].astype(BF16)) * jnp.exp(ldf * (pos + 1.0))
        yf_ref[0, :, vs] = y
        wv = (v.astype(F32) * jnp.exp(ldf * (c - 1.0 - pos))).astype(BF16)
        sf_ref[h] = sf_ref[h] * jnp.exp(ldf * full) + _dot_tn(k, wv)
        q = qb_ref[0, :, qs]
        k = kb_ref[0, :, qs]
        v = vb_ref[0, :, vs]
        yb_ref[0, :, vs] = _dot(q, sb_ref[h].astype(BF16)) * jnp.exp(ldb * (c - pos))
        wv = (v.astype(F32) * jnp.exp(ldb * pos)).astype(BF16)
        sb_ref[h] = sb_ref[h] * jnp.exp(ldb * full) + _dot_tn(k, wv)


def _ret_call(zb, log_decay, *, ctx_len, chunk):
    bsz, t, _ = zb.shape
    nc = t // chunk
    ncc = ctx_len // chunk
    qw = RET_HEADS * RET_QK_DIM
    vw = RET_HEADS * RET_V_DIM
    q_blk, k_blk, v_blk = 2560 // qw, 3072 // qw, 1024 // vw
    bwd = functools.partial(_bwd_chunk, n_ctx_chunks=ncc, n_chunks=nc)
    kern = functools.partial(_ret_kernel, chunk=chunk)
    out = jax.ShapeDtypeStruct((bsz, t, vw), F32)
    return pl.pallas_call(
        kern,
        grid=(bsz, nc),
        in_specs=[pl.BlockSpec(memory_space=pltpu.SMEM),
                  pl.BlockSpec((1, chunk, qw), lambda b, p: (b, p, q_blk)),
                  pl.BlockSpec((1, chunk, qw), lambda b, p: (b, p, k_blk)),
                  pl.BlockSpec((1, chunk, vw), lambda b, p: (b, p, v_blk)),
                  pl.BlockSpec((1, chunk, qw), lambda b, p: (b, bwd(p), q_blk)),
                  pl.BlockSpec((1, chunk, qw), lambda b, p: (b, bwd(p), k_blk)),
                  pl.BlockSpec((1, chunk, vw), lambda b, p: (b, bwd(p), v_blk))],
        out_specs=[pl.BlockSpec((1, chunk, vw), lambda b, p: (b, p, 0)),
                   pl.BlockSpec((1, chunk, vw), lambda b, p: (b, bwd(p), 0))],
        out_shape=[out, out],
        scratch_shapes=[pltpu.VMEM((RET_HEADS, RET_QK_DIM, RET_V_DIM), F32),
                        pltpu.VMEM((RET_HEADS, RET_QK_DIM, RET_V_DIM), F32)],
        compiler_params=_cparams(("parallel", "arbitrary")),
    )(log_decay, zb, zb, zb, zb, zb, zb)


CONV_HALO = SUBLANES


def _conv_kernel(prev_ref, cur_ref, next_ref, w_ref, b_ref, o_ref, e_ref, *, tc, t_len, ctx_len):
    i = pl.program_id(1)
    e_ref[0:CONV_HALO] = prev_ref[0]
    e_ref[CONV_HALO:CONV_HALO + tc] = cur_ref[0]
    e_ref[CONV_HALO + tc:] = next_ref[0]
    tok = i * tc + lax.broadcasted_iota(jnp.int32, (tc, 1), 0)
    acc = jnp.broadcast_to(b_ref[...], (tc, ZF_XBC_W))
    pad = (SSM_CONV - 1) // 2
    for tap in range(SSM_CONV):
        off = tap - pad
        src = e_ref[CONV_HALO + off:CONV_HALO + off + tc, :]
        nb = tok + off
        same = ((tok - ctx_len) ^ (nb - ctx_len)) >= 0
        valid = same & (nb >= 0) & (nb < t_len)
        acc = acc + jnp.where(valid, src, 0.0) * w_ref[tap:tap + 1, :]
    o_ref[0] = _silu(acc)


def _conv_call(zf, conv_w, conv_b, *, ctx_len, tc):
    bsz, t, _ = zf.shape
    col = ZF_XBC_OFF // ZF_XBC_W
    per = tc // CONV_HALO
    last = t // CONV_HALO - 1
    kern = functools.partial(_conv_kernel, tc=tc, t_len=t, ctx_len=ctx_len)
    return pl.pallas_call(
        kern,
        grid=(bsz, t // tc),
        in_specs=[pl.BlockSpec((1, CONV_HALO, ZF_XBC_W),
                               lambda b, i: (b, jnp.maximum(i * per - 1, 0), col)),
                  pl.BlockSpec((1, tc, ZF_XBC_W), lambda b, i: (b, i, col)),
                  pl.BlockSpec((1, CONV_HALO, ZF_XBC_W),
                               lambda b, i: (b, jnp.minimum((i + 1) * per, last), col)),
                  pl.BlockSpec((SUBLANES, ZF_XBC_W), lambda b, i: (0, 0)),
                  pl.BlockSpec((1, ZF_XBC_W), lambda b, i: (0, 0))],
        out_specs=pl.BlockSpec((1, tc, ZF_XBC_W), lambda b, i: (b, i, 0)),
        out_shape=jax.ShapeDtypeStruct((bsz, t, ZF_XBC_W), F32),
        scratch_shapes=[pltpu.VMEM((tc + 2 * CONV_HALO, ZF_XBC_W), F32)],
        compiler_params=_cparams(("parallel", "parallel")),
    )(zf, zf, zf, conv_w, conv_b)


def _split3(x):
    p1 = x.astype(BF16)
    r1 = x - p1.astype(F32)
    p2 = r1.astype(BF16)
    p3 = (r1 - p2.astype(F32)).astype(BF16)
    return p1, p2, p3


def _cumsum_rows(x, tri):
    p1, p2, p3 = _split3(x)
    return _dot(tri, p1) + _dot(tri, p2) + _dot(tri, p3)


def _expand_heads(a, off):
    rows = a.shape[0]
    lo = lax.broadcasted_iota(jnp.int32, (rows, LANES), 1) < SSM_HEAD_DIM
    parts = []
    for m in range(SSM_HEADS // 2):
        even = jnp.broadcast_to(a[:, off + 2 * m:off + 2 * m + 1], (rows, LANES))
        odd = jnp.broadcast_to(a[:, off + 2 * m + 1:off + 2 * m + 2], (rows, LANES))
        parts.append(jnp.where(lo, even, odd))
    return jnp.concatenate(parts, axis=1)


def _ssd_gates(d_ref, bias_ref, alog_ref, tri):
    raw = d_ref[0] + bias_ref[...]
    dt = jnp.maximum(raw, 0.0) + jnp.log1p(jnp.exp(-jnp.abs(raw)))
    la = dt * -jnp.exp(alog_ref[...])
    return dt, la, _cumsum_rows(la, tri)


def _ssd_kernel(xf_ref, df_ref, xb_ref, db_ref, bias_ref, alog_ref, skip_ref, yf_ref, yb_ref,
                sf_ref, sb_ref, *, chunk):
    c = chunk
    p = pl.program_id(1)
    nh = SSM_HEADS
    gw = SSM_HPG * SSM_HEAD_DIM

    @pl.when(p == 0)
    def _():
        sf_ref[...] = jnp.zeros(sf_ref.shape, F32)
        sb_ref[...] = jnp.zeros(sb_ref.shape, F32)

    ii = lax.broadcasted_iota(jnp.int32, (c, c), 0)
    jj = lax.broadcasted_iota(jnp.int32, (c, c), 1)
    lower = ii >= jj
    upper = jj >= ii
    tri = jnp.where(lower, 1.0, 0.0).astype(BF16)
    lane_lo = lax.broadcasted_iota(jnp.int32, (c, LANES), 1) < SSM_HEAD_DIM

    xs = xf_ref[0, :, 0:SSM_INNER]
    bm = xf_ref[0, :, SSM_INNER:SSM_INNER + SSM_GROUPS * SSM_STATE].astype(BF16)
    cm = xf_ref[0, :, SSM_INNER + SSM_GROUPS * SSM_STATE:].astype(BF16)
    xs_b = xs.astype(BF16)
    dt, la, acum = _ssd_gates(df_ref, bias_ref, alog_ref, tri)
    excl = acum - la
    dt_t = dt.T
    acum_t = acum.T
    excl_t = excl.T

    pieces = []
    for g in range(SSM_GROUPS):
        gs = slice(g * SSM_STATE, (g + 1) * SSM_STATE)
        scores = _dot_nt(cm[:, gs], bm[:, gs])
        for m in range(SSM_HPG // 2):
            xp = xs_b[:, (g * SSM_HPG // 2 + m) * LANES:(g * SSM_HPG // 2 + m + 1) * LANES]
            acc = None
            for e in range(2):
                hf = g * SSM_HPG + 2 * m + e
                hb = nh + hf
                dec_f = jnp.where(lower, jnp.exp(acum[:, hf:hf + 1] - acum_t[hf:hf + 1, :]), 0.0)
                dec_b = jnp.where(upper, jnp.exp(excl_t[hb:hb + 1, :] - excl[:, hb:hb + 1]), 0.0)
                wm = scores * (dec_f * dt_t[hf:hf + 1, :] + dec_b * dt_t[hb:hb + 1, :])
                xh = jnp.where(lane_lo if e == 0 else jnp.logical_not(lane_lo), xp, 0.0)
                part = _dot(wm.astype(BF16), xh)
                acc = part if acc is None else acc + part
            pieces.append(acc)
    y = jnp.concatenate(pieces, axis=1)

    last = acum[c - 1:c, :]
    carry = _expand_heads(jnp.exp(acum), 0)
    wdt = _expand_heads(jnp.exp(last - acum) * dt, 0)
    keep = _expand_heads(jnp.exp(last), 0)
    xw = (xs * wdt).astype(BF16)
    inter = []
    for g in range(SSM_GROUPS):
        gs = slice(g * SSM_STATE, (g + 1) * SSM_STATE)
        cs = slice(g * gw, (g + 1) * gw)
        inter.append(_dot(cm[:, gs], sf_ref[g].astype(BF16)))
        sf_ref[g] = sf_ref[g] * keep[:, cs] + _dot_tn(bm[:, gs], xw[:, cs])
    yf_ref[0] = y + jnp.concatenate(inter, axis=1) * carry + skip_ref[...] * xs

    xs = xb_ref[0, :, 0:SSM_INNER]
    bm = xb_ref[0, :, SSM_INNER:SSM_INNER + SSM_GROUPS * SSM_STATE].astype(BF16)
    cm = xb_ref[0, :, SSM_INNER + SSM_GROUPS * SSM_STATE:].astype(BF16)
    dt, la, acum = _ssd_gates(db_ref, bias_ref, alog_ref, tri)
    excl = acum - la
    total = acum[c - 1:c, :]
    carry = _expand_heads(jnp.exp(total - excl), nh)
    wdt = _expand_heads(jnp.exp(excl) * dt, nh)
    keep = _expand_heads(jnp.exp(total), nh)
    xw = (xs * wdt).astype(BF16)
    inter = []
    for g in range(SSM_GROUPS):
        gs = slice(g * SSM_STATE, (g + 1) * SSM_STATE)
        cs = slice(g * gw, (g + 1) * gw)
        inter.append(_dot(cm[:, gs], sb_ref[g].astype(BF16)))
        sb_ref[g] = sb_ref[g] * keep[:, cs] + _dot_tn(bm[:, gs], xw[:, cs])
    yb_ref[0] = jnp.concatenate(inter, axis=1) * carry


def _ssd_call(xbc, zf, bias_row, alog_row, skip_row, *, ctx_len, chunk):
    bsz, t, _ = xbc.shape
    nc = t // chunk
    ncc = ctx_len // chunk
    dt_blk = ZF_DT_OFF // LANES
    bwd = functools.partial(_bwd_chunk, n_ctx_chunks=ncc, n_chunks=nc)
    kern = functools.partial(_ssd_kernel, chunk=chunk)
    out = jax.ShapeDtypeStruct((bsz, t, SSM_INNER), F32)
    row = lambda w: pl.BlockSpec((1, w), lambda b, p: (0, 0))
    return pl.pallas_call(
        kern,
        grid=(bsz, nc),
        in_specs=[pl.BlockSpec((1, chunk, ZF_XBC_W), lambda b, p: (b, p, 0)),
                  pl.BlockSpec((1, chunk, LANES), lambda b, p: (b, p, dt_blk)),
                  pl.BlockSpec((1, chunk, ZF_XBC_W), lambda b, p: (b, bwd(p), 0)),
                  pl.BlockSpec((1, chunk, LANES), lambda b, p: (b, bwd(p), dt_blk)),
                  row(LANES), row(LANES), row(SSM_INNER)],
        out_specs=[pl.BlockSpec((1, chunk, SSM_INNER), lambda b, p: (b, p, 0)),
                   pl.BlockSpec((1, chunk, SSM_INNER), lambda b, p: (b, bwd(p), 0))],
        out_shape=[out, out],
        scratch_shapes=[pltpu.VMEM((SSM_GROUPS, SSM_STATE, SSM_HPG * SSM_HEAD_DIM), F32),
                        pltpu.VMEM((SSM_GROUPS, SSM_STATE, SSM_HPG * SSM_HEAD_DIM), F32)],
        compiler_params=_cparams(("parallel", "arbitrary")),
    )(xbc, zf, xbc, zf, bias_row, alog_row, skip_row)


def _merge_kernel(attn_ref, retf_ref, retb_ref, ssmf_ref, ssmb_ref, mg_ref, ag_ref, rg_ref,
                  sz_ref, gnw_ref, snw_ref, wb_ref, o_ref):
    d = D_MODEL
    a = attn_ref[0] * _silu(ag_ref[0])

    ret = retf_ref[0] + retb_ref[0]
    parts = []
    for h in range(RET_HEADS):
        r = ret[:, h * RET_V_DIM:(h + 1) * RET_V_DIM]
        rc = r - jnp.mean(r, axis=-1, keepdims=True)
        var = jnp.mean(rc * rc, axis=-1, keepdims=True)
        parts.append(rc * lax.rsqrt(var + NORM_EPS))
    r = jnp.concatenate(parts, axis=1) * gnw_ref[...] * _silu(rg_ref[0])

    s = (ssmf_ref[0] + ssmb_ref[0]) * _silu(sz_ref[0])
    gwid = SSM_INNER // SSM_GROUPS
    parts = []
    for g in range(SSM_GROUPS):
        sg = s[:, g * gwid:(g + 1) * gwid]
        parts.append(sg * lax.rsqrt(jnp.mean(sg * sg, axis=-1, keepdims=True) + NORM_EPS))
    s = jnp.concatenate(parts, axis=1) * snw_ref[...]

    merged = _sigmoid(mg_ref[0, :, 0:d]) * _dot(a.astype(BF16), wb_ref[0])
    merged = merged + _sigmoid(mg_ref[0, :, d:2 * d]) * _dot(r.astype(BF16), wb_ref[1])
    merged = merged + _sigmoid(mg_ref[0, :, 2 * d:3 * d]) * _dot(s.astype(BF16), wb_ref[2])
    o_ref[0] = merged.astype(BF16)


def _merge_call(attn, retf, retb, ssmf, ssmb, zf, gnw, snw, wb, *, tm):
    bsz, t, _ = attn.shape
    d = D_MODEL
    bw = BRANCH_WIDTH
    act = lambda blk: pl.BlockSpec((1, tm, bw), lambda b, i: (b, i, blk))
    return pl.pallas_call(
        _merge_kernel,
        grid=(bsz, t // tm),
        in_specs=[act(0), act(0), act(0), act(0), act(0),
                  pl.BlockSpec((1, tm, N_BRANCHES * d), lambda b, i: (b, i, 0)),
                  act(6144 // bw), act(7168 // bw), act(8192 // bw),
                  pl.BlockSpec((1, bw), lambda b, i: (0, 0)),
                  pl.BlockSpec((1, bw), lambda b, i: (0, 0)),
                  pl.BlockSpec((N_BRANCHES, bw, d), lambda b, i: (0, 0, 0),
                               pipeline_mode=pl.Buffered(1))],
        out_specs=pl.BlockSpec((1, tm, d), lambda b, i: (b, i, 0)),
        out_shape=jax.ShapeDtypeStruct((bsz, t, d), BF16),
        compiler_params=_cparams(("parallel", "parallel")),
    )(attn, retf, retb, ssmf, ssmb, zf, zf, zf, zf, gnw, snw, wb)


def _out_kernel(m_ref, x_ref, mod_ref, w_ref, g_ref, b_ref, o_ref, *, tm, n_batch, ctx_len):
    b = pl.program_id(0)
    i = pl.program_id(1)
    d = D_MODEL
    row = i * tm + lax.broadcasted_iota(jnp.int32, (tm, 1), 0)
    gate = _modulation_rows(mod_ref, b, n_batch, row < ctx_len, 2 * d, 3 * d)
    y = _dot(m_ref[0], w_ref[...])
    z = DEEPNORM_ALPHA * x_ref[0] + gate * y
    zc = z - jnp.mean(z, axis=-1, keepdims=True)
    var = jnp.mean(zc * zc, axis=-1, keepdims=True)
    o_ref[0] = zc * lax.rsqrt(var + NORM_EPS) * g_ref[...] + b_ref[...]


def _out_call(merged, xs, mod, w_out, ln_g, ln_b, *, n_batch, ctx_len, tm):
    bsz, t, d = xs.shape
    kern = functools.partial(_out_kernel, tm=tm, n_batch=n_batch, ctx_len=ctx_len)
    tok = pl.BlockSpec((1, tm, d), lambda b, i: (b, i, 0))
    vec = pl.BlockSpec((1, d), lambda b, i: (0, 0))
    return pl.pallas_call(
        kern,
        grid=(bsz, t // tm),
        in_specs=[tok, tok,
                  pl.BlockSpec((SUBLANES, 3 * d), lambda b, i: (0, 0)),
                  pl.BlockSpec((d, d), lambda b, i: (0, 0), pipeline_mode=pl.Buffered(1)),
                  vec, vec],
        out_specs=tok,
        out_shape=jax.ShapeDtypeStruct((bsz, t, d), F32),
        compiler_params=_cparams(("parallel", "parallel")),
    )(merged, xs, mod, w_out, ln_g, ln_b)


def _pack_w_in(w_in):
    depth, d, _ = w_in.shape
    seg = lambda name: w_in[:, :, _REF_SPLITS[name][0]:_REF_SPLITS[name][0] + _REF_SPLITS[name][1]]
    cols = [seg(n) for n in ZB_ORDER + ZF_ORDER]
    cols.append(jnp.zeros((depth, d, ZF_W - ZF_USED), w_in.dtype))
    return jnp.concatenate(cols, axis=-1).astype(BF16)


def _rope_tables(rows, ctx_len):
    row = jnp.repeat(jnp.arange(rows, dtype=F32), GRID_W)
    col = jnp.tile(jnp.arange(GRID_W, dtype=F32), rows)
    n_freq = HEAD_DIM // 4
    inv = ROPE_BASE ** (-jnp.arange(n_freq, dtype=F32) / n_freq)
    ang_r = row[:, None] * inv
    ang_c = col[:, None] * inv
    ang = jnp.concatenate([ang_r, ang_r, ang_c, ang_c], -1)
    cos = jnp.concatenate([jnp.ones((ctx_len, HEAD_DIM), F32), jnp.cos(ang)], 0)
    sin = jnp.concatenate([jnp.zeros((ctx_len, HEAD_DIM), F32), jnp.sin(ang)], 0)
    first = (np.arange(HEAD_DIM) % (HEAD_DIM // 2)) < HEAD_DIM // 4
    sin_lo = jnp.where(first, -sin, 0.0)
    sin_hi = jnp.where(first, 0.0, sin)
    return cos, sin_lo, sin_hi


def _pad_lanes(v, width):
    return jnp.pad(v.reshape(1, -1), ((0, 0), (0, width - v.size)))


def _tile_choices(t, ctx_len):
    def largest(cap, mult):
        best = mult
        for cand in range(mult, cap + 1, mult):
            if t % cand == 0:
                best = cand
        return best
    return dict(proj=largest(1056, 16), attn=largest(768, LANES), conv=largest(528, SUBLANES),
                merge=largest(256, 16), out=largest(256, 16))


def kernel(x, c, ctx, c_ctx, ada_w, ada_b, w_in, attn_q_norm, attn_k_norm, ret_log_decay, ret_gn_w,
           ssm_conv_w, ssm_conv_b, ssm_dt_bias, ssm_a_log, ssm_d, ssm_norm_w, w_branch, w_out,
           ln_g, ln_b):
    bsz, seq, d = x.shape
    ctx_len = ctx.shape[1]
    depth = w_in.shape[0]
    t = seq + ctx_len
    chunk = 128
    assert d == D_MODEL and bsz < SUBLANES and seq % GRID_W == 0
    assert ctx_len % chunk == 0 and t % chunk == 0
    tiles = _tile_choices(t, ctx_len)

    xs = jnp.concatenate([ctx, x], axis=1)
    cvec = jnp.zeros((SUBLANES, d), F32).at[:bsz].set(c).at[bsz].set(c_ctx)
    mod = _ada_call(cvec, ada_w, ada_b)
    cos, sin_lo, sin_hi = _rope_tables(seq // GRID_W, ctx_len)
    w_packed = _pack_w_in(w_in)
    wb = w_branch.astype(BF16)
    wo = w_out.astype(BF16)
    conv_w = jnp.pad(ssm_conv_w, ((0, 0), (0, SUBLANES - SSM_CONV), (0, 0)))

    for l in range(depth):
        zb, zf = _inproj_call(xs, mod[l], w_packed[l], cos, sin_lo, sin_hi,
                              attn_q_norm[l].reshape(1, -1), attn_k_norm[l].reshape(1, -1),
                              n_batch=bsz, ctx_len=ctx_len, tm=tiles["proj"])
        attn = _attn_call(zb, ctx_len=ctx_len, tq=tiles["attn"], tk=tiles["attn"])
        retf, retb = _ret_call(zb, ret_log_decay[l], ctx_len=ctx_len, chunk=chunk)
        xbc = _conv_call(zf, conv_w[l], ssm_conv_b[l].reshape(1, -1), ctx_len=ctx_len,
                         tc=tiles["conv"])
        ssmf, ssmb = _ssd_call(xbc, zf, _pad_lanes(ssm_dt_bias[l], LANES),
                               _pad_lanes(ssm_a_log[l], LANES),
                               jnp.repeat(ssm_d[l], SSM_HEAD_DIM).reshape(1, -1),
                               ctx_len=ctx_len, chunk=chunk)
        merged = _merge_call(attn, retf, retb, ssmf, ssmb, zf, ret_gn_w[l].reshape(1, -1),
                             ssm_norm_w[l].reshape(1, -1), wb[l], tm=tiles["merge"])
        xs = _out_call(merged, xs, mod[l], wo[l], ln_g[l].reshape(1, -1), ln_b[l].reshape(1, -1),
                       n_batch=bsz, ctx_len=ctx_len, tm=tiles["out"])
    return xs[:, ctx_len:, :]
```

```python
import functools

import numpy as np
import jax
import jax.numpy as jnp
from jax import lax
from jax.experimental import pallas as pl
from jax.experimental.pallas import tpu as pltpu

F32 = jnp.float32
BF16 = jnp.bfloat16

D_MODEL = 2048
GRID_W = 64
HEAD_DIM = 128
ATTN_HEADS = 8
ATTN_KV_HEADS = 2
ATTN_GROUP = ATTN_HEADS // ATTN_KV_HEADS
ROPE_BASE = 10000.0
RET_HEADS = 4
RET_QK_DIM = 128
RET_V_DIM = 256
SSM_INNER = 1024
SSM_HEAD_DIM = 64
SSM_HEADS = 16
SSM_GROUPS = 2
SSM_HPG = SSM_HEADS // SSM_GROUPS
SSM_STATE = 128
SSM_CONV = 5
N_BRANCHES = 3
BRANCH_WIDTH = 1024
DEPTH = 4
DEEPNORM_ALPHA = (2 * DEPTH) ** 0.25
NORM_EPS = 1e-6
ATTN_Q_SCALE = HEAD_DIM ** -0.5 * float(np.log2(np.e))

_REF_SPLITS = dict(aq=(0, 1024), ak=(1024, 256), av=(1280, 256), ag=(1536, 1024),
                   rq=(2560, 512), rk=(3072, 512), rv=(3584, 1024), rg=(4608, 1024),
                   sx=(5632, 1024), sb=(6656, 256), sc=(6912, 256), sdt=(7168, 32),
                   sz=(7200, 1024), mg=(8224, 6144))

LANES = 128
SUBLANES = 8
VMEM_LIMIT = 56 * 1024 * 1024

PROJ_TN = 512
ZB_ORDER = ("aq", "rv", "ak", "av", "rq", "rk")
ZB_W = 3584
ZB_TILES = ZB_W // PROJ_TN
ZF_ORDER = ("mg", "ag", "rg", "sz", "sx", "sb", "sc", "sdt")
ZF_USED = 6144 + 3 * 1024 + 1024 + 256 + 256 + 32
ZF_W = 11264
ZF_TILES = ZF_W // PROJ_TN
ZF_XBC_OFF = 9216
ZF_XBC_W = 1536
ZF_DT_OFF = 10752
W_PACKED = ZB_W + ZF_W


def _sigmoid(x):
    return 1.0 / (1.0 + jnp.exp(-x))


def _silu(x):
    return x * _sigmoid(x)


def _cparams(sem):
    return pltpu.CompilerParams(dimension_semantics=sem, vmem_limit_bytes=VMEM_LIMIT)


def _dot(a, b):
    return jnp.dot(a, b, preferred_element_type=F32)


def _dot_nt(a, b):
    return lax.dot_general(a, b, (((1,), (1,)), ((), ())), preferred_element_type=F32)


def _dot_tn(a, b):
    return lax.dot_general(a, b, (((0,), (0,)), ((), ())), preferred_element_type=F32)


def _ada_kernel(c_ref, w_ref, b_ref, o_ref):
    s = _silu(c_ref[...]).astype(BF16)
    o_ref[0] = _dot(s, w_ref[0].astype(BF16)) + b_ref[0]


def _ada_call(cvec, ada_w, ada_b):
    depth, d, n = ada_w.shape
    tn = 768
    return pl.pallas_call(
        _ada_kernel,
        grid=(depth, n // tn),
        in_specs=[pl.BlockSpec((SUBLANES, d), lambda l, j: (0, 0)),
                  pl.BlockSpec((1, d, tn), lambda l, j: (l, 0, j)),
                  pl.BlockSpec((1, 1, tn), lambda l, j: (l, 0, j))],
        out_specs=pl.BlockSpec((1, SUBLANES, tn), lambda l, j: (l, 0, j)),
        out_shape=jax.ShapeDtypeStruct((depth, SUBLANES, n), F32),
        compiler_params=_cparams(("parallel", "parallel")),
    )(cvec, ada_w, ada_b.reshape(depth, 1, n))


def _rope(x, cos, sin_lo, sin_hi):
    return x * cos + pltpu.roll(x, 96, 1) * sin_lo + pltpu.roll(x, 32, 1) * sin_hi


def _modulation_rows(mod_ref, b, n_batch, is_ctx, lo, hi):
    mb = mod_ref[pl.ds(b, 1), lo:hi]
    mc = mod_ref[pl.ds(n_batch, 1), lo:hi]
    return jnp.where(is_ctx, mc, mb)


def _inproj_kernel(x_ref, mod_ref, w_ref, cos_ref, sl_ref, sh_ref, qn_ref, kn_ref,
                   zb_ref, zf_ref, u_ref, *, tm, n_batch, ctx_len):
    b = pl.program_id(0)
    i = pl.program_id(1)
    j = pl.program_id(2)
    d = D_MODEL

    @pl.when(j == 0)
    def _():
        row = i * tm + lax.broadcasted_iota(jnp.int32, (tm, 1), 0)
        is_ctx = row < ctx_len
        shift = _modulation_rows(mod_ref, b, n_batch, is_ctx, 0, d)
        scale = _modulation_rows(mod_ref, b, n_batch, is_ctx, d, 2 * d)
        u_ref[...] = (x_ref[0] * (1.0 + scale) + shift).astype(BF16)

    def project():
        return _dot(u_ref[...], w_ref[...])

    def rope(xh):
        return _rope(xh, cos_ref[...], sl_ref[...], sh_ref[...])

    def normed(w_row):
        def fn(xh):
            ms = jnp.mean(xh * xh, axis=-1, keepdims=True)
            return rope(xh * lax.rsqrt(ms + NORM_EPS) * w_row)
        return fn

    @pl.when(j >= ZB_TILES)
    def _():
        zf_ref[0] = project()

    @pl.when((j == 2) | (j == 3))
    def _():
        zb_ref[0] = project().astype(BF16)

    @pl.when((j < 2) | ((j >= 4) & (j < ZB_TILES)))
    def _():
        acc = project()

        def per_head(fn, n):
            return jnp.concatenate([fn(acc[:, h * LANES:(h + 1) * LANES]) for h in range(n)],
                                   axis=1)

        @pl.when(j < 2)
        def _():
            zb_ref[0] = (per_head(normed(qn_ref[...]), 4) * ATTN_Q_SCALE).astype(BF16)

        @pl.when(j == 4)
        def _():
            k = per_head(normed(kn_ref[...]), 2)
            zb_ref[0] = jnp.concatenate([k, acc[:, 2 * LANES:]], axis=1).astype(BF16)

        @pl.when(j == 5)
        def _():
            zb_ref[0] = per_head(rope, 4).astype(BF16)

        @pl.when(j == 6)
        def _():
            zb_ref[0] = per_head(lambda xh: rope(xh * (RET_QK_DIM ** -0.5)), 4).astype(BF16)


def _inproj_call(xs, mod, w, cos, sin_lo, sin_hi, qn, kn, *, n_batch, ctx_len, tm):
    bsz, t, d = xs.shape
    kern = functools.partial(_inproj_kernel, tm=tm, n_batch=n_batch, ctx_len=ctx_len)
    rope_spec = pl.BlockSpec((tm, LANES), lambda b, i, j: (i, 0))
    vec_spec = pl.BlockSpec((1, LANES), lambda b, i, j: (0, 0))
    return pl.pallas_call(
        kern,
        grid=(bsz, t // tm, W_PACKED // PROJ_TN),
        in_specs=[pl.BlockSpec((1, tm, d), lambda b, i, j: (b, i, 0)),
                  pl.BlockSpec((SUBLANES, 3 * d), lambda b, i, j: (0, 0)),
                  pl.BlockSpec((d, PROJ_TN), lambda b, i, j: (0, j)),
                  rope_spec, rope_spec, rope_spec, vec_spec, vec_spec],
        out_specs=[pl.BlockSpec((1, tm, PROJ_TN),
                                lambda b, i, j: (b, i, jnp.minimum(j, ZB_TILES - 1))),
                   pl.BlockSpec((1, tm, PROJ_TN),
                                lambda b, i, j: (b, i, jnp.maximum(j - ZB_TILES, 0)))],
        out_shape=[jax.ShapeDtypeStruct((bsz, t, ZB_W), BF16),
                   jax.ShapeDtypeStruct((bsz, t, ZF_W), F32)],
        scratch_shapes=[pltpu.VMEM((tm, d), BF16)],
        compiler_params=_cparams(("parallel", "parallel", "arbitrary")),
    )(xs, mod, w, cos, sin_lo, sin_hi, qn, kn)


NEG_BIG = -1e30


def _attn_kernel(q_ref, k_ref, v_ref, o_ref, m_ref, l_ref, acc_ref, *, tq, tk, rb, ctx_len):
    qi = pl.program_id(2)
    ki = pl.program_id(3)

    @pl.when(ki == 0)
    def _():
        m_ref[...] = jnp.full(m_ref.shape, NEG_BIG, F32)
        l_ref[...] = jnp.zeros(l_ref.shape, F32)
        acc_ref[...] = jnp.zeros(acc_ref.shape, F32)

    def step(mixed):
        k = k_ref[0]
        v = v_ref[0]
        units = [(h, r) for h in range(ATTN_GROUP) for r in range(tq // rb)]

        def scores(h, r):
            return _dot_nt(q_ref[0, pl.ds(r * rb, rb), h * HEAD_DIM:(h + 1) * HEAD_DIM], k)

        def softmax(h, r, s):
            rows = pl.ds(r * rb, rb)
            if mixed and r * rb < ctx_len:
                row = r * rb + lax.broadcasted_iota(jnp.int32, (rb, 1), 0)
                col = ki * tk + lax.broadcasted_iota(jnp.int32, (1, tk), 1)
                s = jnp.where(row < ctx_len, jnp.where(col < ctx_len, s, NEG_BIG), s)
            m_prev = m_ref[h, rows]
            m_new = jnp.maximum(m_prev, jnp.max(s, axis=-1, keepdims=True))
            alpha = jnp.exp2(m_prev - m_new)
            p = jnp.exp2(s - m_new)
            l_ref[h, rows] = alpha * l_ref[h, rows] + jnp.sum(p, axis=-1, keepdims=True)
            m_ref[h, rows] = m_new
            return alpha, p.astype(BF16)

        def accumulate(h, r, alpha, p):
            rows = pl.ds(r * rb, rb)
            acc_ref[h, rows] = alpha * acc_ref[h, rows] + _dot(p, v)

        n_units = len(units)
        s_q = {0: scores(*units[0])}
        if n_units > 1:
            s_q[1] = scores(*units[1])
        p_q = {0: softmax(*units[0], s_q.pop(0))}
        for n in range(n_units):
            if n + 2 < n_units:
                s_q[n + 2] = scores(*units[n + 2])
            if n + 1 < n_units:
                p_q[n + 1] = softmax(*units[n + 1], s_q.pop(n + 1))
            accumulate(*units[n], *p_q.pop(n))

    pl.when(qi == 0)(lambda: step(True))
    pl.when(qi > 0)(lambda: step(False))

    @pl.when(ki == pl.num_programs(3) - 1)
    def _():
        o_ref[0] = jnp.concatenate(
            [acc_ref[h] * (1.0 / l_ref[h]) for h in range(ATTN_GROUP)], axis=1)


def _attn_call(zb, *, ctx_len, tq, tk):
    bsz, t, _ = zb.shape
    assert ctx_len <= tq and ctx_len <= tk
    gw = ATTN_GROUP * HEAD_DIM
    k_blk = 2048 // HEAD_DIM
    v_blk = 2304 // HEAD_DIM
    rb = 256 if tq % 256 == 0 else tq
    kern = functools.partial(_attn_kernel, tq=tq, tk=tk, rb=rb, ctx_len=ctx_len)
    return pl.pallas_call(
        kern,
        grid=(bsz, ATTN_KV_HEADS, t // tq, t // tk),
        in_specs=[pl.BlockSpec((1, tq, gw), lambda b, g, qi, ki: (b, qi, g)),
                  pl.BlockSpec((1, tk, HEAD_DIM), lambda b, g, qi, ki: (b, ki, k_blk + g)),
                  pl.BlockSpec((1, tk, HEAD_DIM), lambda b, g, qi, ki: (b, ki, v_blk + g))],
        out_specs=pl.BlockSpec((1, tq, gw), lambda b, g, qi, ki: (b, qi, g)),
        out_shape=jax.ShapeDtypeStruct((bsz, t, ATTN_HEADS * HEAD_DIM), F32),
        scratch_shapes=[pltpu.VMEM((ATTN_GROUP, tq, 1), F32),
                        pltpu.VMEM((ATTN_GROUP, tq, 1), F32),
                        pltpu.VMEM((ATTN_GROUP, tq, HEAD_DIM), F32)],
        compiler_params=_cparams(("parallel", "parallel", "parallel", "arbitrary")),
    )(zb, zb, zb)


def _bwd_chunk(p, n_ctx_chunks, n_chunks):
    return jnp.where(p < n_ctx_chunks, n_ctx_chunks - 1 - p, n_chunks + n_ctx_chunks - 1 - p)


def _ret_kernel(ld_ref, qf_ref, kf_ref, vf_ref, qb_ref, kb_ref, vb_ref, yf_ref, yb_ref,
                sf_ref, sb_ref, *, chunk):
    c = chunk
    p = pl.program_id(1)

    @pl.when(p == 0)
    def _():
        sf_ref[...] = jnp.zeros(sf_ref.shape, F32)
        sb_ref[...] = jnp.zeros(sb_ref.shape, F32)

    ii = lax.broadcasted_iota(jnp.int32, (c, c), 0)
    jj = lax.broadcasted_iota(jnp.int32, (c, c), 1)
    dist = (ii - jj).astype(F32)
    pos = lax.broadcasted_iota(jnp.int32, (c, 1), 0).astype(F32)
    full = jnp.full((1, 1), float(c), F32)

    for h in range(RET_HEADS):
        ldf = ld_ref[0, h]
        ldb = ld_ref[1, h]
        qs = slice(h * RET_QK_DIM, (h + 1) * RET_QK_DIM)
        vs = slice(h * RET_V_DIM, (h + 1) * RET_V_DIM)
        decay = jnp.where(dist >= 0, jnp.exp(ldf * dist), 0.0) \
            + jnp.where(dist <= 0, jnp.exp(-ldb * dist), 0.0)
        q = qf_ref[0, :, qs]
        k = kf_ref[0, :, qs]
        v = vf_ref[0, :, vs]
        y = _dot((_dot_nt(q, k) * decay).astype(BF16), v)
        y = y + _dot(q, sf_ref[h].astype(BF16)) * jnp.exp(ldf * (pos + 1.0))
        yf_ref[0, :, vs] = y
        wv = (v.astype(F32) * jnp.exp(ldf * (c - 1.0 - pos))).astype(BF16)
        sf_ref[h] = sf_ref[h] * jnp.exp(ldf * full) + _dot_tn(k, wv)
        q = qb_ref[0, :, qs]
        k = kb_ref[0, :, qs]
        v = vb_ref[0, :, vs]
        yb_ref[0, :, vs] = _dot(q, sb_ref[h].astype(BF16)) * jnp.exp(ldb * (c - pos))
        wv = (v.astype(F32) * jnp.exp(ldb * pos)).astype(BF16)
        sb_ref[h] = sb_ref[h] * jnp.exp(ldb * full) + _dot_tn(k, wv)


def _ret_call(zb, log_decay, *, ctx_len, chunk):
    bsz, t, _ = zb.shape
    nc = t // chunk
    ncc = ctx_len // chunk
    qw = RET_HEADS * RET_QK_DIM
    vw = RET_HEADS * RET_V_DIM
    q_blk, k_blk, v_blk = 2560 // qw, 3072 // qw, 1024 // vw
    bwd = functools.partial(_bwd_chunk, n_ctx_chunks=ncc, n_chunks=nc)
    kern = functools.partial(_ret_kernel, chunk=chunk)
    out = jax.ShapeDtypeStruct((bsz, t, vw), F32)
    return pl.pallas_call(
        kern,
        grid=(bsz, nc),
        in_specs=[pl.BlockSpec(memory_space=pltpu.SMEM),
                  pl.BlockSpec((1, chunk, qw), lambda b, p: (b, p, q_blk)),
                  pl.BlockSpec((1, chunk, qw), lambda b, p: (b, p, k_blk)),
                  pl.BlockSpec((1, chunk, vw), lambda b, p: (b, p, v_blk)),
                  pl.BlockSpec((1, chunk, qw), lambda b, p: (b, bwd(p), q_blk)),
                  pl.BlockSpec((1, chunk, qw), lambda b, p: (b, bwd(p), k_blk)),
                  pl.BlockSpec((1, chunk, vw), lambda b, p: (b, bwd(p), v_blk))],
        out_specs=[pl.BlockSpec((1, chunk, vw), lambda b, p: (b, p, 0)),
                   pl.BlockSpec((1, chunk, vw), lambda b, p: (b, bwd(p), 0))],
        out_shape=[out, out],
        scratch_shapes=[pltpu.VMEM((RET_HEADS, RET_QK_DIM, RET_V_DIM), F32),
                        pltpu.VMEM((RET_HEADS, RET_QK_DIM, RET_V_DIM), F32)],
        compiler_params=_cparams(("parallel", "arbitrary")),
    )(log_decay, zb, zb, zb, zb, zb, zb)


CONV_HALO = SUBLANES


def _conv_kernel(prev_ref, cur_ref, next_ref, w_ref, b_ref, o_ref, e_ref, *, tc, t_len, ctx_len):
    i = pl.program_id(1)
    e_ref[0:CONV_HALO] = prev_ref[0]
    e_ref[CONV_HALO:CONV_HALO + tc] = cur_ref[0]
    e_ref[CONV_HALO + tc:] = next_ref[0]
    tok = i * tc + lax.broadcasted_iota(jnp.int32, (tc, 1), 0)
    acc = jnp.broadcast_to(b_ref[...], (tc, ZF_XBC_W))
    pad = (SSM_CONV - 1) // 2
    for tap in range(SSM_CONV):
        off = tap - pad
        src = e_ref[CONV_HALO + off:CONV_HALO + off + tc, :]
        nb = tok + off
        same = ((tok - ctx_len) ^ (nb - ctx_len)) >= 0
        valid = same & (nb >= 0) & (nb < t_len)
        acc = acc + jnp.where(valid, src, 0.0) * w_ref[tap:tap + 1, :]
    o_ref[0] = _silu(acc)


def _conv_call(zf, conv_w, conv_b, *, ctx_len, tc):
    bsz, t, _ = zf.shape
    col = ZF_XBC_OFF // ZF_XBC_W
    per = tc // CONV_HALO
    last = t // CONV_HALO - 1
    kern = functools.partial(_conv_kernel, tc=tc, t_len=t, ctx_len=ctx_len)
    return pl.pallas_call(
        kern,
        grid=(bsz, t // tc),
        in_specs=[pl.BlockSpec((1, CONV_HALO, ZF_XBC_W),
                               lambda b, i: (b, jnp.maximum(i * per - 1, 0), col)),
                  pl.BlockSpec((1, tc, ZF_XBC_W), lambda b, i: (b, i, col)),
                  pl.BlockSpec((1, CONV_HALO, ZF_XBC_W),
                               lambda b, i: (b, jnp.minimum((i + 1) * per, last), col)),
                  pl.BlockSpec((SUBLANES, ZF_XBC_W), lambda b, i: (0, 0)),
                  pl.BlockSpec((1, ZF_XBC_W), lambda b, i: (0, 0))],
        out_specs=pl.BlockSpec((1, tc, ZF_XBC_W), lambda b, i: (b, i, 0)),
        out_shape=jax.ShapeDtypeStruct((bsz, t, ZF_XBC_W), F32),
        scratch_shapes=[pltpu.VMEM((tc + 2 * CONV_HALO, ZF_XBC_W), F32)],
        compiler_params=_cparams(("parallel", "parallel")),
    )(zf, zf, zf, conv_w, conv_b)


def _split3(x):
    p1 = x.astype(BF16)
    r1 = x - p1.astype(F32)
    p2 = r1.astype(BF16)
    p3 = (r1 - p2.astype(F32)).astype(BF16)
    return p1, p2, p3


def _cumsum_rows(x, tri):
    p1, p2, p3 = _split3(x)
    return _dot(tri, p1) + _dot(tri, p2) + _dot(tri, p3)


def _expand_heads(a, off):
    rows = a.shape[0]
    lo = lax.broadcasted_iota(jnp.int32, (rows, LANES), 1) < SSM_HEAD_DIM
    parts = []
    for m in range(SSM_HEADS // 2):
        even = jnp.broadcast_to(a[:, off + 2 * m:off + 2 * m + 1], (rows, LANES))
        odd = jnp.broadcast_to(a[:, off + 2 * m + 1:off + 2 * m + 2], (rows, LANES))
        parts.append(jnp.where(lo, even, odd))
    return jnp.concatenate(parts, axis=1)


def _ssd_gates(d_ref, bias_ref, alog_ref, tri):
    raw = d_ref[0] + bias_ref[...]
    dt = jnp.maximum(raw, 0.0) + jnp.log1p(jnp.exp(-jnp.abs(raw)))
    la = dt * -jnp.exp(alog_ref[...])
    return dt, la, _cumsum_rows(la, tri)


def _ssd_kernel(xf_ref, df_ref, xb_ref, db_ref, bias_ref, alog_ref, skip_ref, yf_ref, yb_ref,
                sf_ref, sb_ref, *, chunk):
    c = chunk
    p = pl.program_id(1)
    nh = SSM_HEADS
    gw = SSM_HPG * SSM_HEAD_DIM

    @pl.when(p == 0)
    def _():
        sf_ref[...] = jnp.zeros(sf_ref.shape, F32)
        sb_ref[...] = jnp.zeros(sb_ref.shape, F32)

    ii = lax.broadcasted_iota(jnp.int32, (c, c), 0)
    jj = lax.broadcasted_iota(jnp.int32, (c, c), 1)
    lower = ii >= jj
    upper = jj >= ii
    tri = jnp.where(lower, 1.0, 0.0).astype(BF16)
    lane_lo = lax.broadcasted_iota(jnp.int32, (c, LANES), 1) < SSM_HEAD_DIM

    xs = xf_ref[0, :, 0:SSM_INNER]
    bm = xf_ref[0, :, SSM_INNER:SSM_INNER + SSM_GROUPS * SSM_STATE].astype(BF16)
    cm = xf_ref[0, :, SSM_INNER + SSM_GROUPS * SSM_STATE:].astype(BF16)
    xs_b = xs.astype(BF16)
    dt, la, acum = _ssd_gates(df_ref, bias_ref, alog_ref, tri)
    excl = acum - la
    dt_t = dt.T
    acum_t = acum.T
    excl_t = excl.T

    pieces = []
    for g in range(SSM_GROUPS):
        gs = slice(g * SSM_STATE, (g + 1) * SSM_STATE)
        scores = _dot_nt(cm[:, gs], bm[:, gs])
        for m in range(SSM_HPG // 2):
            xp = xs_b[:, (g * SSM_HPG // 2 + m) * LANES:(g * SSM_HPG // 2 + m + 1) * LANES]
            acc = None
            for e in range(2):
                hf = g * SSM_HPG + 2 * m + e
                hb = nh + hf
                dec_f = jnp.where(lower, jnp.exp(acum[:, hf:hf + 1] - acum_t[hf:hf + 1, :]), 0.0)
                dec_b = jnp.where(upper, jnp.exp(excl_t[hb:hb + 1, :] - excl[:, hb:hb + 1]), 0.0)
                wm = scores * (dec_f * dt_t[hf:hf + 1, :] + dec_b * dt_t[hb:hb + 1, :])
                xh = jnp.where(lane_lo if e == 0 else jnp.logical_not(lane_lo), xp, 0.0)
                part = _dot(wm.astype(BF16), xh)
                acc = part if acc is None else acc + part
            pieces.append(acc)
    y = jnp.concatenate(pieces, axis=1)

    last = acum[c - 1:c, :]
    carry = _expand_heads(jnp.exp(acum), 0)
    wdt = _expand_heads(jnp.exp(last - acum) * dt, 0)
    keep = _expand_heads(jnp.exp(last), 0)
    xw = (xs * wdt).astype(BF16)
    inter = []
    for g in range(SSM_GROUPS):
        gs = slice(g * SSM_STATE, (g + 1) * SSM_STATE)
        cs = slice(g * gw, (g + 1) * gw)
        inter.append(_dot(cm[:, gs], sf_ref[g].astype(BF16)))
        sf_ref[g] = sf_ref[g] * keep[:, cs] + _dot_tn(bm[:, gs], xw[:, cs])
    yf_ref[0] = y + jnp.concatenate(inter, axis=1) * carry + skip_ref[...] * xs

    xs = xb_ref[0, :, 0:SSM_INNER]
    bm = xb_ref[0, :, SSM_INNER:SSM_INNER + SSM_GROUPS * SSM_STATE].astype(BF16)
    cm = xb_ref[0, :, SSM_INNER + SSM_GROUPS * SSM_STATE:].astype(BF16)
    dt, la, acum = _ssd_gates(db_ref, bias_ref, alog_ref, tri)
    excl = acum - la
    total = acum[c - 1:c, :]
    carry = _expand_heads(jnp.exp(total - excl), nh)
    wdt = _expand_heads(jnp.exp(excl) * dt, nh)
    keep = _expand_heads(jnp.exp(total), nh)
    xw = (xs * wdt).astype(BF16)
    inter = []
    for g in range(SSM_GROUPS):
        gs = slice(g * SSM_STATE, (g + 1) * SSM_STATE)
        cs = slice(g * gw, (g + 1) * gw)
        inter.append(_dot(cm[:, gs], sb_ref[g].astype(BF16)))
        sb_ref[g] = sb_ref[g] * keep[:, cs] + _dot_tn(bm[:, gs], xw[:, cs])
    yb_ref[0] = jnp.concatenate(inter, axis=1) * carry


def _ssd_call(xbc, zf, bias_row, alog_row, skip_row, *, ctx_len, chunk):
    bsz, t, _ = xbc.shape
    nc = t // chunk
    ncc = ctx_len // chunk
    dt_blk = ZF_DT_OFF // LANES
    bwd = functools.partial(_bwd_chunk, n_ctx_chunks=ncc, n_chunks=nc)
    kern = functools.partial(_ssd_kernel, chunk=chunk)
    out = jax.ShapeDtypeStruct((bsz, t, SSM_INNER), F32)
    row = lambda w: pl.BlockSpec((1, w), lambda b, p: (0, 0))
    return pl.pallas_call(
        kern,
        grid=(bsz, nc),
        in_specs=[pl.BlockSpec((1, chunk, ZF_XBC_W), lambda b, p: (b, p, 0)),
                  pl.BlockSpec((1, chunk, LANES), lambda b, p: (b, p, dt_blk)),
                  pl.BlockSpec((1, chunk, ZF_XBC_W), lambda b, p: (b, bwd(p), 0)),
                  pl.BlockSpec((1, chunk, LANES), lambda b, p: (b, bwd(p), dt_blk)),
                  row(LANES), row(LANES), row(SSM_INNER)],
        out_specs=[pl.BlockSpec((1, chunk, SSM_INNER), lambda b, p: (b, p, 0)),
                   pl.BlockSpec((1, chunk, SSM_INNER), lambda b, p: (b, bwd(p), 0))],
        out_shape=[out, out],
        scratch_shapes=[pltpu.VMEM((SSM_GROUPS, SSM_STATE, SSM_HPG * SSM_HEAD_DIM), F32),
                        pltpu.VMEM((SSM_GROUPS, SSM_STATE, SSM_HPG * SSM_HEAD_DIM), F32)],
        compiler_params=_cparams(("parallel", "arbitrary")),
    )(xbc, zf, xbc, zf, bias_row, alog_row, skip_row)


def _merge_kernel(attn_ref, retf_ref, retb_ref, ssmf_ref, ssmb_ref, mg_ref, ag_ref, rg_ref,
                  sz_ref, gnw_ref, snw_ref, wb_ref, o_ref):
    d = D_MODEL
    a = attn_ref[0] * _silu(ag_ref[0])

    ret = retf_ref[0] + retb_ref[0]
    parts = []
    for h in range(RET_HEADS):
        r = ret[:, h * RET_V_DIM:(h + 1) * RET_V_DIM]
        rc = r - jnp.mean(r, axis=-1, keepdims=True)
        var = jnp.mean(rc * rc, axis=-1, keepdims=True)
        parts.append(rc * lax.rsqrt(var + NORM_EPS))
    r = jnp.concatenate(parts, axis=1) * gnw_ref[...] * _silu(rg_ref[0])

    s = (ssmf_ref[0] + ssmb_ref[0]) * _silu(sz_ref[0])
    gwid = SSM_INNER // SSM_GROUPS
    parts = []
    for g in range(SSM_GROUPS):
        sg = s[:, g * gwid:(g + 1) * gwid]
        parts.append(sg * lax.rsqrt(jnp.mean(sg * sg, axis=-1, keepdims=True) + NORM_EPS))
    s = jnp.concatenate(parts, axis=1) * snw_ref[...]

    merged = _sigmoid(mg_ref[0, :, 0:d]) * _dot(a.astype(BF16), wb_ref[0])
    merged = merged + _sigmoid(mg_ref[0, :, d:2 * d]) * _dot(r.astype(BF16), wb_ref[1])
    merged = merged + _sigmoid(mg_ref[0, :, 2 * d:3 * d]) * _dot(s.astype(BF16), wb_ref[2])
    o_ref[0] = merged.astype(BF16)


def _merge_call(attn, retf, retb, ssmf, ssmb, zf, gnw, snw, wb, *, tm):
    bsz, t, _ = attn.shape
    d = D_MODEL
    bw = BRANCH_WIDTH
    act = lambda blk: pl.BlockSpec((1, tm, bw), lambda b, i: (b, i, blk))
    return pl.pallas_call(
        _merge_kernel,
        grid=(bsz, t // tm),
        in_specs=[act(0), act(0), act(0), act(0), act(0),
                  pl.BlockSpec((1, tm, N_BRANCHES * d), lambda b, i: (b, i, 0)),
                  act(6144 // bw), act(7168 // bw), act(8192 // bw),
                  pl.BlockSpec((1, bw), lambda b, i: (0, 0)),
                  pl.BlockSpec((1, bw), lambda b, i: (0, 0)),
                  pl.BlockSpec((N_BRANCHES, bw, d), lambda b, i: (0, 0, 0),
                               pipeline_mode=pl.Buffered(1))],
        out_specs=pl.BlockSpec((1, tm, d), lambda b, i: (b, i, 0)),
        out_shape=jax.ShapeDtypeStruct((bsz, t, d), BF16),
        compiler_params=_cparams(("parallel", "parallel")),
    )(attn, retf, retb, ssmf, ssmb, zf, zf, zf, zf, gnw, snw, wb)


def _out_kernel(m_ref, x_ref, mod_ref, w_ref, g_ref, b_ref, o_ref, *, tm, n_batch, ctx_len):
    b = pl.program_id(0)
    i = pl.program_id(1)
    d = D_MODEL
    row = i * tm + lax.broadcasted_iota(jnp.int32, (tm, 1), 0)
    gate = _modulation_rows(mod_ref, b, n_batch, row < ctx_len, 2 * d, 3 * d)
    y = _dot(m_ref[0], w_ref[...])
    z = DEEPNORM_ALPHA * x_ref[0] + gate * y
    zc = z - jnp.mean(z, axis=-1, keepdims=True)
    var = jnp.mean(zc * zc, axis=-1, keepdims=True)
    o_ref[0] = zc * lax.rsqrt(var + NORM_EPS) * g_ref[...] + b_ref[...]


def _out_call(merged, xs, mod, w_out, ln_g, ln_b, *, n_batch, ctx_len, tm):
    bsz, t, d = xs.shape
    kern = functools.partial(_out_kernel, tm=tm, n_batch=n_batch, ctx_len=ctx_len)
    tok = pl.BlockSpec((1, tm, d), lambda b, i: (b, i, 0))
    vec = pl.BlockSpec((1, d), lambda b, i: (0, 0))
    return pl.pallas_call(
        kern,
        grid=(bsz, t // tm),
        in_specs=[tok, tok,
                  pl.BlockSpec((SUBLANES, 3 * d), lambda b, i: (0, 0)),
                  pl.BlockSpec((d, d), lambda b, i: (0, 0), pipeline_mode=pl.Buffered(1)),
                  vec, vec],
        out_specs=tok,
        out_shape=jax.ShapeDtypeStruct((bsz, t, d), F32),
        compiler_params=_cparams(("parallel", "parallel")),
    )(merged, xs, mod, w_out, ln_g, ln_b)


def _pack_w_in(w_in):
    depth, d, _ = w_in.shape
    seg = lambda name: w_in[:, :, _REF_SPLITS[name][0]:_REF_SPLITS[name][0] + _REF_SPLITS[name][1]]
    cols = [seg(n) for n in ZB_ORDER + ZF_ORDER]
    cols.append(jnp.zeros((depth, d, ZF_W - ZF_USED), w_in.dtype))
    return jnp.concatenate(cols, axis=-1).astype(BF16)


def _rope_tables(rows, ctx_len):
    row = jnp.repeat(jnp.arange(rows, dtype=F32), GRID_W)
    col = jnp.tile(jnp.arange(GRID_W, dtype=F32), rows)
    n_freq = HEAD_DIM // 4
    inv = ROPE_BASE ** (-jnp.arange(n_freq, dtype=F32) / n_freq)
    ang_r = row[:, None] * inv
    ang_c = col[:, None] * inv
    ang = jnp.concatenate([ang_r, ang_r, ang_c, ang_c], -1)
    cos = jnp.concatenate([jnp.ones((ctx_len, HEAD_DIM), F32), jnp.cos(ang)], 0)
    sin = jnp.concatenate([jnp.zeros((ctx_len, HEAD_DIM), F32), jnp.sin(ang)], 0)
    first = (np.arange(HEAD_DIM) % (HEAD_DIM // 2)) < HEAD_DIM // 4
    sin_lo = jnp.where(first, -sin, 0.0)
    sin_hi = jnp.where(first, 0.0, sin)
    return cos, sin_lo, sin_hi


def _pad_lanes(v, width):
    return jnp.pad(v.reshape(1, -1), ((0, 0), (0, width - v.size)))


def _tile_choices(t, ctx_len):
    def largest(cap, mult):
        best = mult
        for cand in range(mult, cap + 1, mult):
            if t % cand == 0:
                best = cand
        return best
    return dict(proj=largest(1056, 16), attn=largest(768, LANES), conv=largest(528, SUBLANES),
                merge=largest(256, 16), out=largest(256, 16))


def kernel(x, c, ctx, c_ctx, ada_w, ada_b, w_in, attn_q_norm, attn_k_norm, ret_log_decay, ret_gn_w,
           ssm_conv_w, ssm_conv_b, ssm_dt_bias, ssm_a_log, ssm_d, ssm_norm_w, w_branch, w_out,
           ln_g, ln_b):
    bsz, seq, d = x.shape
    ctx_len = ctx.shape[1]
    depth = w_in.shape[0]
    t = seq + ctx_len
    chunk = 128
    assert d == D_MODEL and bsz < SUBLANES and seq % GRID_W == 0
    assert ctx_len % chunk == 0 and t % chunk == 0
    tiles = _tile_choices(t, ctx_len)

    xs = jnp.concatenate([ctx, x], axis=1)
    cvec = jnp.zeros((SUBLANES, d), F32).at[:bsz].set(c).at[bsz].set(c_ctx)
    mod = _ada_call(cvec, ada_w, ada_b)
    cos, sin_lo, sin_hi = _rope_tables(seq // GRID_W, ctx_len)
    w_packed = _pack_w_in(w_in)
    wb = w_branch.astype(BF16)
    wo = w_out.astype(BF16)
    conv_w = jnp.pad(ssm_conv_w, ((0, 0), (0, SUBLANES - SSM_CONV), (0, 0)))

    for l in range(depth):
        zb, zf = _inproj_call(xs, mod[l], w_packed[l], cos, sin_lo, sin_hi,
                              attn_q_norm[l].reshape(1, -1), attn_k_norm[l].reshape(1, -1),
                              n_batch=bsz, ctx_len=ctx_len, tm=tiles["proj"])
        attn = _attn_call(zb, ctx_len=ctx_len, tq=tiles["attn"], tk=tiles["attn"])
        retf, retb = _ret_call(zb, ret_log_decay[l], ctx_len=ctx_len, chunk=chunk)
        xbc = _conv_call(zf, conv_w[l], ssm_conv_b[l].reshape(1, -1), ctx_len=ctx_len,
                         tc=tiles["conv"])
        ssmf, ssmb = _ssd_call(xbc, zf, _pad_lanes(ssm_dt_bias[l], LANES),
                               _pad_lanes(ssm_a_log[l], LANES),
                               jnp.repeat(ssm_d[l], SSM_HEAD_DIM).reshape(1, -1),
                               ctx_len=ctx_len, chunk=chunk)
        merged = _merge_call(attn, retf, retb, ssmf, ssmb, zf, ret_gn_w[l].reshape(1, -1),
                             ssm_norm_w[l].reshape(1, -1), wb[l], tm=tiles["merge"])
        xs = _out_call(merged, xs, mod[l], wo[l], ln_g[l].reshape(1, -1), ln_b[l].reshape(1, -1),
                       n_batch=bsz, ctx_len=ctx_len, tm=tiles["out"])
    return xs[:, ctx_len:, :]
```

```python
import functools

import numpy as np
import jax
import jax.numpy as jnp
from jax import lax
from jax.experimental import pallas as pl
from jax.experimental.pallas import tpu as pltpu

F32 = jnp.float32
BF16 = jnp.bfloat16

D_MODEL = 2048
GRID_W = 64
HEAD_DIM = 128
ATTN_HEADS = 8
ATTN_KV_HEADS = 2
ATTN_GROUP = ATTN_HEADS // ATTN_KV_HEADS
ROPE_BASE = 10000.0
RET_HEADS = 4
RET_QK_DIM = 128
RET_V_DIM = 256
SSM_INNER = 1024
SSM_HEAD_DIM = 64
SSM_HEADS = 16
SSM_GROUPS = 2
SSM_HPG = SSM_HEADS // SSM_GROUPS
SSM_STATE = 128
SSM_CONV = 5
N_BRANCHES = 3
BRANCH_WIDTH = 1024
DEPTH = 4
DEEPNORM_ALPHA = (2 * DEPTH) ** 0.25
NORM_EPS = 1e-6
ATTN_Q_SCALE = HEAD_DIM ** -0.5 * float(np.log2(np.e))

_REF_SPLITS = dict(aq=(0, 1024), ak=(1024, 256), av=(1280, 256), ag=(1536, 1024),
                   rq=(2560, 512), rk=(3072, 512), rv=(3584, 1024), rg=(4608, 1024),
                   sx=(5632, 1024), sb=(6656, 256), sc=(6912, 256), sdt=(7168, 32),
                   sz=(7200, 1024), mg=(8224, 6144))

LANES = 128
SUBLANES = 8
VMEM_LIMIT = 56 * 1024 * 1024

PROJ_TN = 512
ZQ_TILES = ATTN_HEADS * HEAD_DIM // PROJ_TN
ZB_ORDER = ("rv", "ak", "av", "rq", "rk")
ZB_RV_OFF, ZB_AK_OFF, ZB_AV_OFF, ZB_RQ_OFF, ZB_RK_OFF = 0, 1024, 1280, 1536, 2048
ZB_W = 2560
ZB_TILES = ZB_W // PROJ_TN
ZF_TILE0 = ZQ_TILES + ZB_TILES
ZF_ORDER = ("mg", "ag", "rg", "sz", "sx", "sb", "sc", "sdt")
ZF_USED = 6144 + 3 * 1024 + 1024 + 256 + 256 + 32
ZF_W = 11264
ZF_TILES = ZF_W // PROJ_TN
ZF_XBC_OFF = 9216
ZF_XBC_W = 1536
ZF_DT_OFF = 10752
W_PACKED = ZQ_TILES * PROJ_TN + ZB_W + ZF_W


def _sigmoid(x):
    return 1.0 / (1.0 + jnp.exp(-x))


def _silu(x):
    return x * _sigmoid(x)


def _cparams(sem):
    return pltpu.CompilerParams(dimension_semantics=sem, vmem_limit_bytes=VMEM_LIMIT)


def _dot(a, b):
    return jnp.dot(a, b, preferred_element_type=F32)


def _dot_nt(a, b):
    return lax.dot_general(a, b, (((1,), (1,)), ((), ())), preferred_element_type=F32)


def _dot_tn(a, b):
    return lax.dot_general(a, b, (((0,), (0,)), ((), ())), preferred_element_type=F32)


def _ada_kernel(c_ref, w_ref, b_ref, o_ref):
    s = _silu(c_ref[...]).astype(BF16)
    o_ref[0] = _dot(s, w_ref[0].astype(BF16)) + b_ref[0]


def _ada_call(cvec, ada_w, ada_b):
    depth, d, n = ada_w.shape
    tn = 768
    return pl.pallas_call(
        _ada_kernel,
        grid=(depth, n // tn),
        in_specs=[pl.BlockSpec((SUBLANES, d), lambda l, j: (0, 0)),
                  pl.BlockSpec((1, d, tn), lambda l, j: (l, 0, j)),
                  pl.BlockSpec((1, 1, tn), lambda l, j: (l, 0, j))],
        out_specs=pl.BlockSpec((1, SUBLANES, tn), lambda l, j: (l, 0, j)),
        out_shape=jax.ShapeDtypeStruct((depth, SUBLANES, n), F32),
        compiler_params=_cparams(("parallel", "parallel")),
    )(cvec, ada_w, ada_b.reshape(depth, 1, n))


def _rope(x, cos, sin_lo, sin_hi):
    return x * cos + pltpu.roll(x, 96, 1) * sin_lo + pltpu.roll(x, 32, 1) * sin_hi


def _modulation_rows(mod_ref, b, n_batch, is_ctx, lo, hi):
    mb = mod_ref[pl.ds(b, 1), lo:hi]
    mc = mod_ref[pl.ds(n_batch, 1), lo:hi]
    return jnp.where(is_ctx, mc, mb)


def _inproj_kernel(x_ref, mod_ref, w_ref, cos_ref, sl_ref, sh_ref, qn_ref, kn_ref,
                   zq_ref, zb_ref, zf_ref, u_ref, *, tm, n_batch, ctx_len):
    b = pl.program_id(0)
    i = pl.program_id(1)
    j = pl.program_id(2)
    d = D_MODEL
    jb = j - ZQ_TILES

    @pl.when(j == 0)
    def _():
        row = i * tm + lax.broadcasted_iota(jnp.int32, (tm, 1), 0)
        is_ctx = row < ctx_len
        shift = _modulation_rows(mod_ref, b, n_batch, is_ctx, 0, d)
        scale = _modulation_rows(mod_ref, b, n_batch, is_ctx, d, 2 * d)
        u_ref[...] = (x_ref[0] * (1.0 + scale) + shift).astype(BF16)

    def project():
        return _dot(u_ref[...], w_ref[...])

    def rope(xh):
        return _rope(xh, cos_ref[...], sl_ref[...], sh_ref[...])

    def normed(w_row):
        def fn(xh):
            ms = jnp.mean(xh * xh, axis=-1, keepdims=True)
            return rope(xh * lax.rsqrt(ms + NORM_EPS) * w_row)
        return fn

    @pl.when(j >= ZF_TILE0)
    def _():
        zf_ref[0] = project()

    @pl.when((jb >= 0) & (jb < 2))
    def _():
        zb_ref[0] = project().astype(BF16)

    @pl.when((j < ZQ_TILES) | ((jb >= 2) & (j < ZF_TILE0)))
    def _():
        acc = project()

        def head(h):
            return acc[:, h * LANES:(h + 1) * LANES]

        def per_head(fn, n):
            return jnp.concatenate([fn(head(h)) for h in range(n)], axis=1)

        @pl.when(j < ZQ_TILES)
        def _():
            fn = normed(qn_ref[...])
            for h in range(ATTN_GROUP):
                zq_ref[0, h] = (fn(head(h)) * ATTN_Q_SCALE).astype(BF16)

        @pl.when(jb == 2)
        def _():
            k = per_head(normed(kn_ref[...]), 2)
            zb_ref[0] = jnp.concatenate([k, acc[:, 2 * LANES:]], axis=1).astype(BF16)

        @pl.when(jb == 3)
        def _():
            zb_ref[0] = per_head(rope, 4).astype(BF16)

        @pl.when(jb == 4)
        def _():
            zb_ref[0] = per_head(lambda xh: rope(xh * (RET_QK_DIM ** -0.5)), 4).astype(BF16)


def _inproj_call(xs, mod, w, cos, sin_lo, sin_hi, qn, kn, *, n_batch, ctx_len, tm):
    bsz, t, d = xs.shape
    kern = functools.partial(_inproj_kernel, tm=tm, n_batch=n_batch, ctx_len=ctx_len)
    rope_spec = pl.BlockSpec((tm, LANES), lambda b, i, j: (i, 0))
    vec_spec = pl.BlockSpec((1, LANES), lambda b, i, j: (0, 0))
    return pl.pallas_call(
        kern,
        grid=(bsz, t // tm, W_PACKED // PROJ_TN),
        in_specs=[pl.BlockSpec((1, tm, d), lambda b, i, j: (b, i, 0)),
                  pl.BlockSpec((SUBLANES, 3 * d), lambda b, i, j: (0, 0)),
                  pl.BlockSpec((d, PROJ_TN), lambda b, i, j: (0, j)),
                  rope_spec, rope_spec, rope_spec, vec_spec, vec_spec],
        out_specs=[pl.BlockSpec((1, ATTN_GROUP, tm, HEAD_DIM),
                                lambda b, i, j: (b, jnp.minimum(j, ZQ_TILES - 1), i, 0)),
                   pl.BlockSpec((1, tm, PROJ_TN),
                                lambda b, i, j: (b, i, jnp.clip(j - ZQ_TILES, 0, ZB_TILES - 1))),
                   pl.BlockSpec((1, tm, PROJ_TN),
                                lambda b, i, j: (b, i, jnp.maximum(j - ZF_TILE0, 0)))],
        out_shape=[jax.ShapeDtypeStruct((bsz, ATTN_HEADS, t, HEAD_DIM), BF16),
                   jax.ShapeDtypeStruct((bsz, t, ZB_W), BF16),
                   jax.ShapeDtypeStruct((bsz, t, ZF_W), F32)],
        scratch_shapes=[pltpu.VMEM((tm, d), BF16)],
        compiler_params=_cparams(("parallel", "parallel", "arbitrary")),
    )(xs, mod, w, cos, sin_lo, sin_hi, qn, kn)


NEG_BIG = -1e30


def _attn_kernel(q_ref, k_ref, v_ref, o_ref, m_ref, l_ref, acc_ref, s_ref, *,
                 tq, tk, rb, t_len, ctx_len, unroll):
    qi = pl.program_id(2)
    nr = tq // rb
    per_kb = ATTN_GROUP * nr
    n_units = (t_len // tk) * per_kb

    m_ref[...] = jnp.full(m_ref.shape, NEG_BIG, F32)
    l_ref[...] = jnp.zeros(l_ref.shape, F32)
    acc_ref[...] = jnp.zeros(acc_ref.shape, F32)

    def unit(n):
        kb = n // per_kb
        rem = n - kb * per_kb
        h = rem // nr
        r = rem - h * nr
        return pl.multiple_of(kb * tk, tk), h, pl.multiple_of(r * rb, rb)

    def scores(n):
        k0, h, r0 = unit(n)
        return _dot_nt(q_ref[0, h, pl.ds(r0, rb), :], k_ref[0, pl.ds(k0, tk), :])

    def update(n, s, mixed):
        k0, h, r0 = unit(n)
        rows = pl.ds(r0, rb)
        if mixed:
            row = r0 + lax.broadcasted_iota(jnp.int32, (rb, 1), 0)
            col = k0 + lax.broadcasted_iota(jnp.int32, (1, tk), 1)
            s = jnp.where(row < ctx_len, jnp.where(col < ctx_len, s, NEG_BIG), s)
        m_prev = m_ref[h, rows]
        m_new = jnp.maximum(m_prev, jnp.max(s, axis=-1, keepdims=True))
        alpha = jnp.exp2(m_prev - m_new)
        p = jnp.exp2(s - m_new)
        l_ref[h, rows] = alpha * l_ref[h, rows] + jnp.sum(p, axis=-1, keepdims=True)
        m_ref[h, rows] = m_new
        acc_ref[h, rows] = alpha * acc_ref[h, rows] \
            + _dot(p.astype(BF16), v_ref[0, pl.ds(k0, tk), :])

    def run(mixed):
        s_ref[...] = scores(0)

        def body(it, carry):
            n0 = it * unroll
            s = s_ref[...]
            for u in range(unroll):
                s_next = scores(jnp.minimum(n0 + u + 1, n_units - 1))
                update(n0 + u, s, mixed)
                s = s_next
            s_ref[...] = s
            return carry

        lax.fori_loop(0, n_units // unroll, body, 0)

    pl.when(qi == 0)(lambda: run(True))
    pl.when(qi > 0)(lambda: run(False))

    o_ref[0] = jnp.concatenate(
        [acc_ref[h] * (1.0 / l_ref[h]) for h in range(ATTN_GROUP)], axis=1)


def _attn_call(zq, zb, *, ctx_len, tq, tk):
    bsz, t, _ = zb.shape
    assert ctx_len <= tq and ctx_len <= tk
    rb = 256 if tq % 256 == 0 else tq
    per_kb = ATTN_GROUP * (tq // rb)
    unroll = max(u for u in range(1, 13) if per_kb % u == 0)
    k_blk = ZB_AK_OFF // HEAD_DIM
    v_blk = ZB_AV_OFF // HEAD_DIM
    kern = functools.partial(_attn_kernel, tq=tq, tk=tk, rb=rb, t_len=t, ctx_len=ctx_len,
                             unroll=unroll)
    return pl.pallas_call(
        kern,
        grid=(bsz, ATTN_KV_HEADS, t // tq),
        in_specs=[pl.BlockSpec((1, ATTN_GROUP, tq, HEAD_DIM), lambda b, g, qi: (b, g, qi, 0)),
                  pl.BlockSpec((1, t, HEAD_DIM), lambda b, g, qi: (b, 0, k_blk + g)),
                  pl.BlockSpec((1, t, HEAD_DIM), lambda b, g, qi: (b, 0, v_blk + g))],
        out_specs=pl.BlockSpec((1, tq, ATTN_GROUP * HEAD_DIM), lambda b, g, qi: (b, qi, g)),
        out_shape=jax.ShapeDtypeStruct((bsz, t, ATTN_HEADS * HEAD_DIM), F32),
        scratch_shapes=[pltpu.VMEM((ATTN_GROUP, tq, 1), F32),
                        pltpu.VMEM((ATTN_GROUP, tq, 1), F32),
                        pltpu.VMEM((ATTN_GROUP, tq, HEAD_DIM), F32),
                        pltpu.VMEM((rb, tk), F32)],
        compiler_params=_cparams(("parallel", "parallel", "parallel")),
    )(zq, zb, zb)


def _bwd_chunk(p, n_ctx_chunks, n_chunks):
    return jnp.where(p < n_ctx_chunks, n_ctx_chunks - 1 - p, n_chunks + n_ctx_chunks - 1 - p)


def _ret_kernel(ld_ref, qf_ref, kf_ref, vf_ref, qb_ref, kb_ref, vb_ref, yf_ref, yb_ref,
                sf_ref, sb_ref, *, chunk):
    c = chunk
    p = pl.program_id(1)

    @pl.when(p == 0)
    def _():
        sf_ref[...] = jnp.zeros(sf_ref.shape, F32)
        sb_ref[...] = jnp.zeros(sb_ref.shape, F32)

    ii = lax.broadcasted_iota(jnp.int32, (c, c), 0)
    jj = lax.broadcasted_iota(jnp.int32, (c, c), 1)
    dist = (ii - jj).astype(F32)
    pos = lax.broadcasted_iota(jnp.int32, (c, 1), 0).astype(F32)
    full = jnp.full((1, 1), float(c), F32)

    for h in range(RET_HEADS):
        ldf = ld_ref[0, h]
        ldb = ld_ref[1, h]
        qs = slice(h * RET_QK_DIM, (h + 1) * RET_QK_DIM)
        vs = slice(h * RET_V_DIM, (h + 1) * RET_V_DIM)
        decay = jnp.where(dist >= 0, jnp.exp(ldf * dist), 0.0) \
            + jnp.where(dist <= 0, jnp.exp(-ldb * dist), 0.0)
        q = qf_ref[0, :, qs]
        k = kf_ref[0, :, qs]
        v = vf_ref[0, :, vs]
        y = _dot((_dot_nt(q, k) * decay).astype(BF16), v)
        y = y + _dot(q, sf_ref[h].astype(BF16)) * jnp.exp(ldf * (pos + 1.0))
        yf_ref[0, :, vs] = y
        wv = (v.astype(F32) * jnp.exp(ldf * (c - 1.0 - pos))).astype(BF16)
        sf_ref[h] = sf_ref[h] * jnp.exp(ldf * full) + _dot_tn(k, wv)
        q = qb_ref[0, :, qs]
        k = kb_ref[0, :, qs]
        v = vb_ref[0, :, vs]
        yb_ref[0, :, vs] = _dot(q, sb_ref[h].astype(BF16)) * jnp.exp(ldb * (c - pos))
        wv = (v.astype(F32) * jnp.exp(ldb * pos)).astype(BF16)
        sb_ref[h] = sb_ref[h] * jnp.exp(ldb * full) + _dot_tn(k, wv)


def _ret_call(zb, log_decay, *, ctx_len, chunk):
    bsz, t, _ = zb.shape
    nc = t // chunk
    ncc = ctx_len // chunk
    qw = RET_HEADS * RET_QK_DIM
    vw = RET_HEADS * RET_V_DIM
    q_blk, k_blk, v_blk = ZB_RQ_OFF // qw, ZB_RK_OFF // qw, ZB_RV_OFF // vw
    bwd = functools.partial(_bwd_chunk, n_ctx_chunks=ncc, n_chunks=nc)
    kern = functools.partial(_ret_kernel, chunk=chunk)
    out = jax.ShapeDtypeStruct((bsz, t, vw), F32)
    return pl.pallas_call(
        kern,
        grid=(bsz, nc),
        in_specs=[pl.BlockSpec(memory_space=pltpu.SMEM),
                  pl.BlockSpec((1, chunk, qw), lambda b, p: (b, p, q_blk)),
                  pl.BlockSpec((1, chunk, qw), lambda b, p: (b, p, k_blk)),
                  pl.BlockSpec((1, chunk, vw), lambda b, p: (b, p, v_blk)),
                  pl.BlockSpec((1, chunk, qw), lambda b, p: (b, bwd(p), q_blk)),
                  pl.BlockSpec((1, chunk, qw), lambda b, p: (b, bwd(p), k_blk)),
                  pl.BlockSpec((1, chunk, vw), lambda b, p: (b, bwd(p), v_blk))],
        out_specs=[pl.BlockSpec((1, chunk, vw), lambda b, p: (b, p, 0)),
                   pl.BlockSpec((1, chunk, vw), lambda b, p: (b, bwd(p), 0))],
        out_shape=[out, out],
        scratch_shapes=[pltpu.VMEM((RET_HEADS, RET_QK_DIM, RET_V_DIM), F32),
                        pltpu.VMEM((RET_HEADS, RET_QK_DIM, RET_V_DIM), F32)],
        compiler_params=_cparams(("parallel", "arbitrary")),
    )(log_decay, zb, zb, zb, zb, zb, zb)


CONV_HALO = SUBLANES


def _conv_kernel(prev_ref, cur_ref, next_ref, w_ref, b_ref, o_ref, e_ref, *, tc, t_len, ctx_len):
    i = pl.program_id(1)
    e_ref[0:CONV_HALO] = prev_ref[0]
    e_ref[CONV_HALO:CONV_HALO + tc] = cur_ref[0]
    e_ref[CONV_HALO + tc:] = next_ref[0]
    tok = i * tc + lax.broadcasted_iota(jnp.int32, (tc, 1), 0)
    acc = jnp.broadcast_to(b_ref[...], (tc, ZF_XBC_W))
    pad = (SSM_CONV - 1) // 2
    for tap in range(SSM_CONV):
        off = tap - pad
        src = e_ref[CONV_HALO + off:CONV_HALO + off + tc, :]
        nb = tok + off
        same = ((tok - ctx_len) ^ (nb - ctx_len)) >= 0
        valid = same & (nb >= 0) & (nb < t_len)
        acc = acc + jnp.where(valid, src, 0.0) * w_ref[tap:tap + 1, :]
    o_ref[0] = _silu(acc)


def _conv_call(zf, conv_w, conv_b, *, ctx_len, tc):
    bsz, t, _ = zf.shape
    col = ZF_XBC_OFF // ZF_XBC_W
    per = tc // CONV_HALO
    last = t // CONV_HALO - 1
    kern = functools.partial(_conv_kernel, tc=tc, t_len=t, ctx_len=ctx_len)
    return pl.pallas_call(
        kern,
        grid=(bsz, t // tc),
        in_specs=[pl.BlockSpec((1, CONV_HALO, ZF_XBC_W),
                               lambda b, i: (b, jnp.maximum(i * per - 1, 0), col)),
                  pl.BlockSpec((1, tc, ZF_XBC_W), lambda b, i: (b, i, col)),
                  pl.BlockSpec((1, CONV_HALO, ZF_XBC_W),
                               lambda b, i: (b, jnp.minimum((i + 1) * per, last), col)),
                  pl.BlockSpec((SUBLANES, ZF_XBC_W), lambda b, i: (0, 0)),
                  pl.BlockSpec((1, ZF_XBC_W), lambda b, i: (0, 0))],
        out_specs=pl.BlockSpec((1, tc, ZF_XBC_W), lambda b, i: (b, i, 0)),
        out_shape=jax.ShapeDtypeStruct((bsz, t, ZF_XBC_W), F32),
        scratch_shapes=[pltpu.VMEM((tc + 2 * CONV_HALO, ZF_XBC_W), F32)],
        compiler_params=_cparams(("parallel", "parallel")),
    )(zf, zf, zf, conv_w, conv_b)


def _split3(x):
    p1 = x.astype(BF16)
    r1 = x - p1.astype(F32)
    p2 = r1.astype(BF16)
    p3 = (r1 - p2.astype(F32)).astype(BF16)
    return p1, p2, p3


def _cumsum_rows(x, tri):
    p1, p2, p3 = _split3(x)
    return _dot(tri, p1) + _dot(tri, p2) + _dot(tri, p3)


def _expand_heads(a, off):
    rows = a.shape[0]
    lo = lax.broadcasted_iota(jnp.int32, (rows, LANES), 1) < SSM_HEAD_DIM
    parts = []
    for m in range(SSM_HEADS // 2):
        even = jnp.broadcast_to(a[:, off + 2 * m:off + 2 * m + 1], (rows, LANES))
        odd = jnp.broadcast_to(a[:, off + 2 * m + 1:off + 2 * m + 2], (rows, LANES))
        parts.append(jnp.where(lo, even, odd))
    return jnp.concatenate(parts, axis=1)


def _ssd_gates(d_ref, bias_ref, alog_ref, tri):
    raw = d_ref[0] + bias_ref[...]
    dt = jnp.maximum(raw, 0.0) + jnp.log1p(jnp.exp(-jnp.abs(raw)))
    la = dt * -jnp.exp(alog_ref[...])
    return dt, la, _cumsum_rows(la, tri)


def _ssd_kernel(xf_ref, df_ref, xb_ref, db_ref, bias_ref, alog_ref, skip_ref, yf_ref, yb_ref,
                sf_ref, sb_ref, *, chunk):
    c = chunk
    p = pl.program_id(1)
    nh = SSM_HEADS
    gw = SSM_HPG * SSM_HEAD_DIM

    @pl.when(p == 0)
    def _():
        sf_ref[...] = jnp.zeros(sf_ref.shape, F32)
        sb_ref[...] = jnp.zeros(sb_ref.shape, F32)

    ii = lax.broadcasted_iota(jnp.int32, (c, c), 0)
    jj = lax.broadcasted_iota(jnp.int32, (c, c), 1)
    lower = ii >= jj
    upper = jj >= ii
    tri = jnp.where(lower, 1.0, 0.0).astype(BF16)
    lane_lo = lax.broadcasted_iota(jnp.int32, (c, LANES), 1) < SSM_HEAD_DIM

    xs = xf_ref[0, :, 0:SSM_INNER]
    bm = xf_ref[0, :, SSM_INNER:SSM_INNER + SSM_GROUPS * SSM_STATE].astype(BF16)
    cm = xf_ref[0, :, SSM_INNER + SSM_GROUPS * SSM_STATE:].astype(BF16)
    xs_b = xs.astype(BF16)
    dt, la, acum = _ssd_gates(df_ref, bias_ref, alog_ref, tri)
    excl = acum - la
    dt_t = dt.T
    acum_t = acum.T
    excl_t = excl.T

    pieces = []
    for g in range(SSM_GROUPS):
        gs = slice(g * SSM_STATE, (g + 1) * SSM_STATE)
        scores = _dot_nt(cm[:, gs], bm[:, gs])
        for m in range(SSM_HPG // 2):
            xp = xs_b[:, (g * SSM_HPG // 2 + m) * LANES:(g * SSM_HPG // 2 + m + 1) * LANES]
            acc = None
            for e in range(2):
                hf = g * SSM_HPG + 2 * m + e
                hb = nh + hf
                dec_f = jnp.where(lower, jnp.exp(acum[:, hf:hf + 1] - acum_t[hf:hf + 1, :]), 0.0)
                dec_b = jnp.where(upper, jnp.exp(excl_t[hb:hb + 1, :] - excl[:, hb:hb + 1]), 0.0)
                wm = scores * (dec_f * dt_t[hf:hf + 1, :] + dec_b * dt_t[hb:hb + 1, :])
                xh = jnp.where(lane_lo if e == 0 else jnp.logical_not(lane_lo), xp, 0.0)
                part = _dot(wm.astype(BF16), xh)
                acc = part if acc is None else acc + part
            pieces.append(acc)
    y = jnp.concatenate(pieces, axis=1)

    last = acum[c - 1:c, :]
    carry = _expand_heads(jnp.exp(acum), 0)
    wdt = _expand_heads(jnp.exp(last - acum) * dt, 0)
    keep = _expand_heads(jnp.exp(last), 0)
    xw = (xs * wdt).astype(BF16)
    inter = []
    for g in range(SSM_GROUPS):
        gs = slice(g * SSM_STATE, (g + 1) * SSM_STATE)
        cs = slice(g * gw, (g + 1) * gw)
        inter.append(_dot(cm[:, gs], sf_ref[g].astype(BF16)))
        sf_ref[g] = sf_ref[g] * keep[:, cs] + _dot_tn(bm[:, gs], xw[:, cs])
    yf_ref[0] = y + jnp.concatenate(inter, axis=1) * carry + skip_ref[...] * xs

    xs = xb_ref[0, :, 0:SSM_INNER]
    bm = xb_ref[0, :, SSM_INNER:SSM_INNER + SSM_GROUPS * SSM_STATE].astype(BF16)
    cm = xb_ref[0, :, SSM_INNER + SSM_GROUPS * SSM_STATE:].astype(BF16)
    dt, la, acum = _ssd_gates(db_ref, bias_ref, alog_ref, tri)
    excl = acum - la
    total = acum[c - 1:c, :]
    carry = _expand_heads(jnp.exp(total - excl), nh)
    wdt = _expand_heads(jnp.exp(excl) * dt, nh)
    keep = _expand_heads(jnp.exp(total), nh)
    xw = (xs * wdt).astype(BF16)
    inter = []
    for g in range(SSM_GROUPS):
        gs = slice(g * SSM_STATE, (g + 1) * SSM_STATE)
        cs = slice(g * gw, (g + 1) * gw)
        inter.append(_dot(cm[:, gs], sb_ref[g].astype(BF16)))
        sb_ref[g] = sb_ref[g] * keep[:, cs] + _dot_tn(bm[:, gs], xw[:, cs])
    yb_ref[0] = jnp.concatenate(inter, axis=1) * carry


def _ssd_call(xbc, zf, bias_row, alog_row, skip_row, *, ctx_len, chunk):
    bsz, t, _ = xbc.shape
    nc = t // chunk
    ncc = ctx_len // chunk
    dt_blk = ZF_DT_OFF // LANES
    bwd = functools.partial(_bwd_chunk, n_ctx_chunks=ncc, n_chunks=nc)
    kern = functools.partial(_ssd_kernel, chunk=chunk)
    out = jax.ShapeDtypeStruct((bsz, t, SSM_INNER), F32)
    row = lambda w: pl.BlockSpec((1, w), lambda b, p: (0, 0))
    return pl.pallas_call(
        kern,
        grid=(bsz, nc),
        in_specs=[pl.BlockSpec((1, chunk, ZF_XBC_W), lambda b, p: (b, p, 0)),
                  pl.BlockSpec((1, chunk, LANES), lambda b, p: (b, p, dt_blk)),
                  pl.BlockSpec((1, chunk, ZF_XBC_W), lambda b, p: (b, bwd(p), 0)),
                  pl.BlockSpec((1, chunk, LANES), lambda b, p: (b, bwd(p), dt_blk)),
                  row(LANES), row(LANES), row(SSM_INNER)],
        out_specs=[pl.BlockSpec((1, chunk, SSM_INNER), lambda b, p: (b, p, 0)),
                   pl.BlockSpec((1, chunk, SSM_INNER), lambda b, p: (b, bwd(p), 0))],
        out_shape=[out, out],
        scratch_shapes=[pltpu.VMEM((SSM_GROUPS, SSM_STATE, SSM_HPG * SSM_HEAD_DIM), F32),
                        pltpu.VMEM((SSM_GROUPS, SSM_STATE, SSM_HPG * SSM_HEAD_DIM), F32)],
        compiler_params=_cparams(("parallel", "arbitrary")),
    )(xbc, zf, xbc, zf, bias_row, alog_row, skip_row)


def _merge_kernel(attn_ref, retf_ref, retb_ref, ssmf_ref, ssmb_ref, mg_ref, ag_ref, rg_ref,
                  sz_ref, gnw_ref, snw_ref, wb_ref, o_ref):
    d = D_MODEL
    a = attn_ref[0] * _silu(ag_ref[0])

    ret = retf_ref[0] + retb_ref[0]
    parts = []
    for h in range(RET_HEADS):
        r = ret[:, h * RET_V_DIM:(h + 1) * RET_V_DIM]
        rc = r - jnp.mean(r, axis=-1, keepdims=True)
        var = jnp.mean(rc * rc, axis=-1, keepdims=True)
        parts.append(rc * lax.rsqrt(var + NORM_EPS))
    r = jnp.concatenate(parts, axis=1) * gnw_ref[...] * _silu(rg_ref[0])

    s = (ssmf_ref[0] + ssmb_ref[0]) * _silu(sz_ref[0])
    gwid = SSM_INNER // SSM_GROUPS
    parts = []
    for g in range(SSM_GROUPS):
        sg = s[:, g * gwid:(g + 1) * gwid]
        parts.append(sg * lax.rsqrt(jnp.mean(sg * sg, axis=-1, keepdims=True) + NORM_EPS))
    s = jnp.concatenate(parts, axis=1) * snw_ref[...]

    merged = _sigmoid(mg_ref[0, :, 0:d]) * _dot(a.astype(BF16), wb_ref[0])
    merged = merged + _sigmoid(mg_ref[0, :, d:2 * d]) * _dot(r.astype(BF16), wb_ref[1])
    merged = merged + _sigmoid(mg_ref[0, :, 2 * d:3 * d]) * _dot(s.astype(BF16), wb_ref[2])
    o_ref[0] = merged.astype(BF16)


def _merge_call(attn, retf, retb, ssmf, ssmb, zf, gnw, snw, wb, *, tm):
    bsz, t, _ = attn.shape
    d = D_MODEL
    bw = BRANCH_WIDTH
    act = lambda blk: pl.BlockSpec((1, tm, bw), lambda b, i: (b, i, blk))
    return pl.pallas_call(
        _merge_kernel,
        grid=(bsz, t // tm),
        in_specs=[act(0), act(0), act(0), act(0), act(0),
                  pl.BlockSpec((1, tm, N_BRANCHES * d), lambda b, i: (b, i, 0)),
                  act(6144 // bw), act(7168 // bw), act(8192 // bw),
                  pl.BlockSpec((1, bw), lambda b, i: (0, 0)),
                  pl.BlockSpec((1, bw), lambda b, i: (0, 0)),
                  pl.BlockSpec((N_BRANCHES, bw, d), lambda b, i: (0, 0, 0),
                               pipeline_mode=pl.Buffered(1))],
        out_specs=pl.BlockSpec((1, tm, d), lambda b, i: (b, i, 0)),
        out_shape=jax.ShapeDtypeStruct((bsz, t, d), BF16),
        compiler_params=_cparams(("parallel", "parallel")),
    )(attn, retf, retb, ssmf, ssmb, zf, zf, zf, zf, gnw, snw, wb)


def _out_kernel(m_ref, x_ref, mod_ref, w_ref, g_ref, b_ref, o_ref, *, tm, n_batch, ctx_len):
    b = pl.program_id(0)
    i = pl.program_id(1)
    d = D_MODEL
    row = i * tm + lax.broadcasted_iota(jnp.int32, (tm, 1), 0)
    gate = _modulation_rows(mod_ref, b, n_batch, row < ctx_len, 2 * d, 3 * d)
    y = _dot(m_ref[0], w_ref[...])
    z = DEEPNORM_ALPHA * x_ref[0] + gate * y
    zc = z - jnp.mean(z, axis=-1, keepdims=True)
    var = jnp.mean(zc * zc, axis=-1, keepdims=True)
    o_ref[0] = zc * lax.rsqrt(var + NORM_EPS) * g_ref[...] + b_ref[...]


def _out_call(merged, xs, mod, w_out, ln_g, ln_b, *, n_batch, ctx_len, tm):
    bsz, t, d = xs.shape
    kern = functools.partial(_out_kernel, tm=tm, n_batch=n_batch, ctx_len=ctx_len)
    tok = pl.BlockSpec((1, tm, d), lambda b, i: (b, i, 0))
    vec = pl.BlockSpec((1, d), lambda b, i: (0, 0))
    return pl.pallas_call(
        kern,
        grid=(bsz, t // tm),
        in_specs=[tok, tok,
                  pl.BlockSpec((SUBLANES, 3 * d), lambda b, i: (0, 0)),
                  pl.BlockSpec((d, d), lambda b, i: (0, 0), pipeline_mode=pl.Buffered(1)),
                  vec, vec],
        out_specs=tok,
        out_shape=jax.ShapeDtypeStruct((bsz, t, d), F32),
        compiler_params=_cparams(("parallel", "parallel")),
    )(merged, xs, mod, w_out, ln_g, ln_b)


def _pack_w_in(w_in):
    depth, d, _ = w_in.shape
    seg = lambda name: w_in[:, :, _REF_SPLITS[name][0]:_REF_SPLITS[name][0] + _REF_SPLITS[name][1]]
    cols = [seg(n) for n in ("aq",) + ZB_ORDER + ZF_ORDER]
    cols.append(jnp.zeros((depth, d, ZF_W - ZF_USED), w_in.dtype))
    return jnp.concatenate(cols, axis=-1).astype(BF16)


def _rope_tables(rows, ctx_len):
    row = jnp.repeat(jnp.arange(rows, dtype=F32), GRID_W)
    col = jnp.tile(jnp.arange(GRID_W, dtype=F32), rows)
    n_freq = HEAD_DIM // 4
    inv = ROPE_BASE ** (-jnp.arange(n_freq, dtype=F32) / n_freq)
    ang_r = row[:, None] * inv
    ang_c = col[:, None] * inv
    ang = jnp.concatenate([ang_r, ang_r, ang_c, ang_c], -1)
    cos = jnp.concatenate([jnp.ones((ctx_len, HEAD_DIM), F32), jnp.cos(ang)], 0)
    sin = jnp.concatenate([jnp.zeros((ctx_len, HEAD_DIM), F32), jnp.sin(ang)], 0)
    first = (np.arange(HEAD_DIM) % (HEAD_DIM // 2)) < HEAD_DIM // 4
    sin_lo = jnp.where(first, -sin, 0.0)
    sin_hi = jnp.where(first, 0.0, sin)
    return cos, sin_lo, sin_hi


def _pad_lanes(v, width):
    return jnp.pad(v.reshape(1, -1), ((0, 0), (0, width - v.size)))


def _tile_choices(t, ctx_len):
    def largest(cap, mult):
        best = mult
        for cand in range(mult, cap + 1, mult):
            if t % cand == 0:
                best = cand
        return best
    return dict(proj=largest(1056, 16), attn=largest(768, LANES), conv=largest(528, SUBLANES),
                merge=largest(256, 16), out=largest(528, 16))


def kernel(x, c, ctx, c_ctx, ada_w, ada_b, w_in, attn_q_norm, attn_k_norm, ret_log_decay, ret_gn_w,
           ssm_conv_w, ssm_conv_b, ssm_dt_bias, ssm_a_log, ssm_d, ssm_norm_w, w_branch, w_out,
           ln_g, ln_b):
    bsz, seq, d = x.shape
    ctx_len = ctx.shape[1]
    depth = w_in.shape[0]
    t = seq + ctx_len
    chunk = LANES
    ret_chunk = 256 if ctx_len % 256 == 0 and t % 256 == 0 else chunk
    assert d == D_MODEL and bsz < SUBLANES and seq % GRID_W == 0
    assert ctx_len % chunk == 0 and t % chunk == 0
    tiles = _tile_choices(t, ctx_len)

    xs = jnp.concatenate([ctx, x], axis=1)
    cvec = jnp.zeros((SUBLANES, d), F32).at[:bsz].set(c).at[bsz].set(c_ctx)
    mod = _ada_call(cvec, ada_w, ada_b)
    cos, sin_lo, sin_hi = _rope_tables(seq // GRID_W, ctx_len)
    w_packed = _pack_w_in(w_in)
    wb = w_branch.astype(BF16)
    wo = w_out.astype(BF16)
    conv_w = jnp.pad(ssm_conv_w, ((0, 0), (0, SUBLANES - SSM_CONV), (0, 0)))

    for l in range(depth):
        zq, zb, zf = _inproj_call(xs, mod[l], w_packed[l], cos, sin_lo, sin_hi,
                              attn_q_norm[l].reshape(1, -1), attn_k_norm[l].reshape(1, -1),
                              n_batch=bsz, ctx_len=ctx_len, tm=tiles["proj"])
        attn = _attn_call(zq, zb, ctx_len=ctx_len, tq=tiles["attn"], tk=tiles["attn"])
        retf, retb = _ret_call(zb, ret_log_decay[l], ctx_len=ctx_len, chunk=ret_chunk)
        xbc = _conv_call(zf, conv_w[l], ssm_conv_b[l].reshape(1, -1), ctx_len=ctx_len,
                         tc=tiles["conv"])
        ssmf, ssmb = _ssd_call(xbc, zf, _pad_lanes(ssm_dt_bias[l], LANES),
                               _pad_lanes(ssm_a_log[l], LANES),
                               jnp.repeat(ssm_d[l], SSM_HEAD_DIM).reshape(1, -1),
                               ctx_len=ctx_len, chunk=chunk)
        merged = _merge_call(attn, retf, retb, ssmf, ssmb, zf, ret_gn_w[l].reshape(1, -1),
                             ssm_norm_w[l].reshape(1, -1), wb[l], tm=tiles["merge"])
        xs = _out_call(merged, xs, mod[l], wo[l], ln_g[l].reshape(1, -1), ln_b[l].reshape(1, -1),
                       n_batch=bsz, ctx_len=ctx_len, tm=tiles["out"])
    return xs[:, ctx_len:, :]
```

```python
import functools

import numpy as np
import jax
import jax.numpy as jnp
from jax import lax
from jax.experimental import pallas as pl
from jax.experimental.pallas import tpu as pltpu

F32 = jnp.float32
BF16 = jnp.bfloat16

D_MODEL = 2048
GRID_W = 64
HEAD_DIM = 128
ATTN_HEADS = 8
ATTN_KV_HEADS = 2
ATTN_GROUP = ATTN_HEADS // ATTN_KV_HEADS
ROPE_BASE = 10000.0
RET_HEADS = 4
RET_QK_DIM = 128
RET_V_DIM = 256
SSM_INNER = 1024
SSM_HEAD_DIM = 64
SSM_HEADS = 16
SSM_GROUPS = 2
SSM_HPG = SSM_HEADS // SSM_GROUPS
SSM_STATE = 128
SSM_CONV = 5
N_BRANCHES = 3
BRANCH_WIDTH = 1024
DEPTH = 4
DEEPNORM_ALPHA = (2 * DEPTH) ** 0.25
NORM_EPS = 1e-6
ATTN_Q_SCALE = HEAD_DIM ** -0.5 * float(np.log2(np.e))

_REF_SPLITS = dict(aq=(0, 1024), ak=(1024, 256), av=(1280, 256), ag=(1536, 1024),
                   rq=(2560, 512), rk=(3072, 512), rv=(3584, 1024), rg=(4608, 1024),
                   sx=(5632, 1024), sb=(6656, 256), sc=(6912, 256), sdt=(7168, 32),
                   sz=(7200, 1024), mg=(8224, 6144))

LANES = 128
SUBLANES = 8
VMEM_LIMIT = 56 * 1024 * 1024

PROJ_TN = 512
ZQ_TILES = ATTN_HEADS * HEAD_DIM // PROJ_TN
ZB_ORDER = ("mg", "ag", "rg", "sz", "rv", "ak", "av", "rq", "rk")
ZB_MG_OFF, ZB_AG_OFF, ZB_RG_OFF, ZB_SZ_OFF = 0, 6144, 7168, 8192
ZB_RV_OFF, ZB_AK_OFF, ZB_AV_OFF, ZB_RQ_OFF, ZB_RK_OFF = 9216, 10240, 10496, 10752, 11264
ZB_W = 11776
ZB_TILES = ZB_W // PROJ_TN
ZB_RAW_TILES = ZB_AK_OFF // PROJ_TN
ZF_TILE0 = ZQ_TILES + ZB_TILES
ZF_ORDER = ("sx", "sb", "sc", "sdt")
ZF_USED = 1024 + 256 + 256 + 32
ZF_W = 2048
ZF_XBC_OFF = 0
ZF_XBC_W = 1536
ZF_DT_OFF = 1536
W_PACKED = ZQ_TILES * PROJ_TN + ZB_W + ZF_W


def _sigmoid(x):
    return 1.0 / (1.0 + jnp.exp(-x))


def _silu(x):
    return x * _sigmoid(x)


def _cparams(sem):
    return pltpu.CompilerParams(dimension_semantics=sem, vmem_limit_bytes=VMEM_LIMIT)


def _dot(a, b):
    return jnp.dot(a, b, preferred_element_type=F32)


def _dot_nt(a, b):
    return lax.dot_general(a, b, (((1,), (1,)), ((), ())), preferred_element_type=F32)


def _dot_tn(a, b):
    return lax.dot_general(a, b, (((0,), (0,)), ((), ())), preferred_element_type=F32)


def _ada_kernel(c_ref, w_ref, b_ref, o_ref):
    s = _silu(c_ref[...]).astype(BF16)
    o_ref[0] = _dot(s, w_ref[0].astype(BF16)) + b_ref[0]


def _ada_call(cvec, ada_w, ada_b):
    depth, d, n = ada_w.shape
    tn = 768
    return pl.pallas_call(
        _ada_kernel,
        grid=(depth, n // tn),
        in_specs=[pl.BlockSpec((SUBLANES, d), lambda l, j: (0, 0)),
                  pl.BlockSpec((1, d, tn), lambda l, j: (l, 0, j)),
                  pl.BlockSpec((1, 1, tn), lambda l, j: (l, 0, j))],
        out_specs=pl.BlockSpec((1, SUBLANES, tn), lambda l, j: (l, 0, j)),
        out_shape=jax.ShapeDtypeStruct((depth, SUBLANES, n), F32),
        compiler_params=_cparams(("parallel", "parallel")),
    )(cvec, ada_w, ada_b.reshape(depth, 1, n))


def _rope(x, cos, sin_lo, sin_hi):
    return x * cos + pltpu.roll(x, 96, 1) * sin_lo + pltpu.roll(x, 32, 1) * sin_hi


def _modulation_rows(mod_ref, b, n_batch, is_ctx, lo, hi):
    mb = mod_ref[pl.ds(b, 1), lo:hi]
    mc = mod_ref[pl.ds(n_batch, 1), lo:hi]
    return jnp.where(is_ctx, mc, mb)


def _inproj_kernel(x_ref, mod_ref, w_ref, cos_ref, sl_ref, sh_ref, qn_ref, kn_ref,
                   zq_ref, zb_ref, zf_ref, u_ref, *, tm, n_batch, ctx_len):
    b = pl.program_id(0)
    i = pl.program_id(1)
    j = pl.program_id(2)
    d = D_MODEL
    jb = j - ZQ_TILES

    @pl.when(j == 0)
    def _():
        row = i * tm + lax.broadcasted_iota(jnp.int32, (tm, 1), 0)
        is_ctx = row < ctx_len
        shift = _modulation_rows(mod_ref, b, n_batch, is_ctx, 0, d)
        scale = _modulation_rows(mod_ref, b, n_batch, is_ctx, d, 2 * d)
        u_ref[...] = (x_ref[0] * (1.0 + scale) + shift).astype(BF16)

    def project():
        return _dot(u_ref[...], w_ref[...])

    def rope(xh):
        return _rope(xh, cos_ref[...], sl_ref[...], sh_ref[...])

    def normed(w_row):
        def fn(xh):
            ms = jnp.mean(xh * xh, axis=-1, keepdims=True)
            return rope(xh * lax.rsqrt(ms + NORM_EPS) * w_row)
        return fn

    @pl.when(j >= ZF_TILE0)
    def _():
        zf_ref[0] = project()

    @pl.when((jb >= 0) & (jb < ZB_RAW_TILES))
    def _():
        zb_ref[0] = project().astype(BF16)

    @pl.when((j < ZQ_TILES) | ((jb >= ZB_RAW_TILES) & (j < ZF_TILE0)))
    def _():
        acc = project()

        def head(h):
            return acc[:, h * LANES:(h + 1) * LANES]

        def per_head(fn, n):
            return jnp.concatenate([fn(head(h)) for h in range(n)], axis=1)

        @pl.when(j < ZQ_TILES)
        def _():
            fn = normed(qn_ref[...])
            for h in range(ATTN_GROUP):
                zq_ref[0, h] = (fn(head(h)) * ATTN_Q_SCALE).astype(BF16)

        @pl.when(jb == ZB_AK_OFF // PROJ_TN)
        def _():
            k = per_head(normed(kn_ref[...]), 2)
            zb_ref[0] = jnp.concatenate([k, acc[:, 2 * LANES:]], axis=1).astype(BF16)

        @pl.when(jb == ZB_RQ_OFF // PROJ_TN)
        def _():
            zb_ref[0] = per_head(rope, 4).astype(BF16)

        @pl.when(jb == ZB_RK_OFF // PROJ_TN)
        def _():
            zb_ref[0] = per_head(lambda xh: rope(xh * (RET_QK_DIM ** -0.5)), 4).astype(BF16)


def _inproj_call(xs, mod, w, cos, sin_lo, sin_hi, qn, kn, *, n_batch, ctx_len, tm):
    bsz, t, d = xs.shape
    kern = functools.partial(_inproj_kernel, tm=tm, n_batch=n_batch, ctx_len=ctx_len)
    rope_spec = pl.BlockSpec((tm, LANES), lambda b, i, j: (i, 0))
    vec_spec = pl.BlockSpec((1, LANES), lambda b, i, j: (0, 0))
    return pl.pallas_call(
        kern,
        grid=(bsz, t // tm, W_PACKED // PROJ_TN),
        in_specs=[pl.BlockSpec((1, tm, d), lambda b, i, j: (b, i, 0)),
                  pl.BlockSpec((SUBLANES, 3 * d), lambda b, i, j: (0, 0)),
                  pl.BlockSpec((d, PROJ_TN), lambda b, i, j: (0, j)),
                  rope_spec, rope_spec, rope_spec, vec_spec, vec_spec],
        out_specs=[pl.BlockSpec((1, ATTN_GROUP, tm, HEAD_DIM),
                                lambda b, i, j: (b, jnp.minimum(j, ZQ_TILES - 1), i, 0)),
                   pl.BlockSpec((1, tm, PROJ_TN),
                                lambda b, i, j: (b, i, jnp.clip(j - ZQ_TILES, 0, ZB_TILES - 1))),
                   pl.BlockSpec((1, tm, PROJ_TN),
                                lambda b, i, j: (b, i, jnp.maximum(j - ZF_TILE0, 0)))],
        out_shape=[jax.ShapeDtypeStruct((bsz, ATTN_HEADS, t, HEAD_DIM), BF16),
                   jax.ShapeDtypeStruct((bsz, t, ZB_W), BF16),
                   jax.ShapeDtypeStruct((bsz, t, ZF_W), F32)],
        scratch_shapes=[pltpu.VMEM((tm, d), BF16)],
        compiler_params=_cparams(("parallel", "parallel", "arbitrary")),
    )(xs, mod, w, cos, sin_lo, sin_hi, qn, kn)


NEG_BIG = -1e30
AHEAD = 2


def _attn_kernel(q_ref, k_ref, v_ref, o_ref, m_ref, l_ref, acc_ref, s_ref, *,
                 tq, tk, rb, t_len, ctx_len, unroll):
    qi = pl.program_id(2)
    nr = tq // rb
    per_kb = ATTN_GROUP * nr
    n_units = (t_len // tk) * per_kb

    m_ref[...] = jnp.full(m_ref.shape, NEG_BIG, F32)
    l_ref[...] = jnp.zeros(l_ref.shape, F32)
    acc_ref[...] = jnp.zeros(acc_ref.shape, F32)

    def unit(n):
        kb = n // per_kb
        return kb, n - kb * per_kb

    def scores(n):
        kb, slot = unit(n)
        h = slot // nr
        r0 = pl.multiple_of((slot - h * nr) * rb, rb)
        k0 = pl.multiple_of(kb * tk, tk)
        return _dot_nt(k_ref[0, pl.ds(k0, tk), :], q_ref[0, h, pl.ds(r0, rb), :])

    def update(n, s, mixed):
        kb, slot = unit(n)
        if mixed:
            r0 = (slot - (slot // nr) * nr) * rb
            qrow = r0 + lax.broadcasted_iota(jnp.int32, (1, rb), 1)
            key = kb * tk + lax.broadcasted_iota(jnp.int32, (tk, 1), 0)
            s = jnp.where(qrow < ctx_len, jnp.where(key < ctx_len, s, NEG_BIG), s)
        m_prev = m_ref[slot]
        m_new = jnp.maximum(m_prev, jnp.max(s, axis=0, keepdims=True))
        alpha = jnp.exp2(m_prev - m_new)
        p = jnp.exp2(s - m_new)
        l_ref[slot] = alpha * l_ref[slot] + jnp.sum(p, axis=0, keepdims=True)
        m_ref[slot] = m_new
        acc_ref[slot] = alpha * acc_ref[slot] + _dot(v_ref[0, 0, kb], p.astype(BF16))

    def run(mixed):
        for a in range(AHEAD):
            s_ref[a] = scores(a)

        def body(it, carry):
            n0 = it * unroll
            s = [s_ref[a] for a in range(AHEAD)]
            for u in range(0, unroll, AHEAD):
                s_next = [scores(jnp.minimum(n0 + u + AHEAD + a, n_units - 1))
                          for a in range(AHEAD)]
                for a in range(AHEAD):
                    update(n0 + u + a, s[a], mixed)
                s = s_next
            for a in range(AHEAD):
                s_ref[a] = s[a]
            return carry

        lax.fori_loop(0, n_units // unroll, body, 0)

    pl.when(qi == 0)(lambda: run(True))
    pl.when(qi > 0)(lambda: run(False))

    for h in range(ATTN_GROUP):
        for r in range(nr):
            slot = h * nr + r
            o_t = acc_ref[slot] * (1.0 / l_ref[slot])
            o_ref[0, r * rb:(r + 1) * rb, h * HEAD_DIM:(h + 1) * HEAD_DIM] = o_t.T


def _attn_call(zq, zb, *, ctx_len, tq, tk):
    bsz, t, _ = zb.shape
    assert ctx_len <= tq and ctx_len <= tk
    rb = 256 if tq % 256 == 0 else tq
    per_kb = ATTN_GROUP * (tq // rb)
    unroll = max(u for u in range(1, 13) if per_kb % u == 0)
    k_blk = ZB_AK_OFF // HEAD_DIM
    v_t = zb[:, :, ZB_AV_OFF:ZB_AV_OFF + ATTN_KV_HEADS * HEAD_DIM]
    v_t = v_t.reshape(bsz, t // tk, tk, ATTN_KV_HEADS, HEAD_DIM).transpose(0, 3, 1, 4, 2)
    kern = functools.partial(_attn_kernel, tq=tq, tk=tk, rb=rb, t_len=t, ctx_len=ctx_len,
                             unroll=unroll)
    return pl.pallas_call(
        kern,
        grid=(bsz, ATTN_KV_HEADS, t // tq),
        in_specs=[pl.BlockSpec((1, ATTN_GROUP, tq, HEAD_DIM), lambda b, g, qi: (b, g, qi, 0)),
                  pl.BlockSpec((1, t, HEAD_DIM), lambda b, g, qi: (b, 0, k_blk + g)),
                  pl.BlockSpec((1, 1, t // tk, HEAD_DIM, tk), lambda b, g, qi: (b, g, 0, 0, 0))],
        out_specs=pl.BlockSpec((1, tq, ATTN_GROUP * HEAD_DIM), lambda b, g, qi: (b, qi, g)),
        out_shape=jax.ShapeDtypeStruct((bsz, t, ATTN_HEADS * HEAD_DIM), F32),
        scratch_shapes=[pltpu.VMEM((per_kb, 1, rb), F32),
                        pltpu.VMEM((per_kb, 1, rb), F32),
                        pltpu.VMEM((per_kb, HEAD_DIM, rb), F32),
                        pltpu.VMEM((AHEAD, tk, rb), F32)],
        compiler_params=_cparams(("parallel", "parallel", "parallel")),
    )(zq, zb, v_t)


def _bwd_chunk(p, n_ctx_chunks, n_chunks):
    return jnp.where(p < n_ctx_chunks, n_ctx_chunks - 1 - p, n_chunks + n_ctx_chunks - 1 - p)


def _ret_kernel(ld_ref, qf_ref, kf_ref, vf_ref, qb_ref, kb_ref, vb_ref, yf_ref, yb_ref,
                sf_ref, sb_ref, *, chunk):
    c = chunk
    p = pl.program_id(1)

    @pl.when(p == 0)
    def _():
        sf_ref[...] = jnp.zeros(sf_ref.shape, F32)
        sb_ref[...] = jnp.zeros(sb_ref.shape, F32)

    ii = lax.broadcasted_iota(jnp.int32, (c, c), 0)
    jj = lax.broadcasted_iota(jnp.int32, (c, c), 1)
    dist = (ii - jj).astype(F32)
    pos = lax.broadcasted_iota(jnp.int32, (c, 1), 0).astype(F32)
    full = jnp.full((1, 1), float(c), F32)

    for h in range(RET_HEADS):
        ldf = ld_ref[0, h]
        ldb = ld_ref[1, h]
        qs = slice(h * RET_QK_DIM, (h + 1) * RET_QK_DIM)
        vs = slice(h * RET_V_DIM, (h + 1) * RET_V_DIM)
        decay = jnp.where(dist >= 0, jnp.exp(ldf * dist), 0.0) \
            + jnp.where(dist <= 0, jnp.exp(-ldb * dist), 0.0)
        q = qf_ref[0, :, qs]
        k = kf_ref[0, :, qs]
        v = vf_ref[0, :, vs]
        y = _dot((_dot_nt(q, k) * decay).astype(BF16), v)
        y = y + _dot(q, sf_ref[h].astype(BF16)) * jnp.exp(ldf * (pos + 1.0))
        yf_ref[0, :, vs] = y
        wv = (v.astype(F32) * jnp.exp(ldf * (c - 1.0 - pos))).astype(BF16)
        sf_ref[h] = sf_ref[h] * jnp.exp(ldf * full) + _dot_tn(k, wv)
        q = qb_ref[0, :, qs]
        k = kb_ref[0, :, qs]
        v = vb_ref[0, :, vs]
        yb_ref[0, :, vs] = _dot(q, sb_ref[h].astype(BF16)) * jnp.exp(ldb * (c - pos))
        wv = (v.astype(F32) * jnp.exp(ldb * pos)).astype(BF16)
        sb_ref[h] = sb_ref[h] * jnp.exp(ldb * full) + _dot_tn(k, wv)


def _ret_call(zb, log_decay, *, ctx_len, chunk):
    bsz, t, _ = zb.shape
    nc = t // chunk
    ncc = ctx_len // chunk
    qw = RET_HEADS * RET_QK_DIM
    vw = RET_HEADS * RET_V_DIM
    q_blk, k_blk, v_blk = ZB_RQ_OFF // qw, ZB_RK_OFF // qw, ZB_RV_OFF // vw
    bwd = functools.partial(_bwd_chunk, n_ctx_chunks=ncc, n_chunks=nc)
    kern = functools.partial(_ret_kernel, chunk=chunk)
    out = jax.ShapeDtypeStruct((bsz, t, vw), F32)
    return pl.pallas_call(
        kern,
        grid=(bsz, nc),
        in_specs=[pl.BlockSpec(memory_space=pltpu.SMEM),
                  pl.BlockSpec((1, chunk, qw), lambda b, p: (b, p, q_blk)),
                  pl.BlockSpec((1, chunk, qw), lambda b, p: (b, p, k_blk)),
                  pl.BlockSpec((1, chunk, vw), lambda b, p: (b, p, v_blk)),
                  pl.BlockSpec((1, chunk, qw), lambda b, p: (b, bwd(p), q_blk)),
                  pl.BlockSpec((1, chunk, qw), lambda b, p: (b, bwd(p), k_blk)),
                  pl.BlockSpec((1, chunk, vw), lambda b, p: (b, bwd(p), v_blk))],
        out_specs=[pl.BlockSpec((1, chunk, vw), lambda b, p: (b, p, 0)),
                   pl.BlockSpec((1, chunk, vw), lambda b, p: (b, bwd(p), 0))],
        out_shape=[out, out],
        scratch_shapes=[pltpu.VMEM((RET_HEADS, RET_QK_DIM, RET_V_DIM), F32),
                        pltpu.VMEM((RET_HEADS, RET_QK_DIM, RET_V_DIM), F32)],
        compiler_params=_cparams(("parallel", "arbitrary")),
    )(log_decay, zb, zb, zb, zb, zb, zb)


CONV_HALO = SUBLANES


def _conv_kernel(prev_ref, cur_ref, next_ref, w_ref, b_ref, o_ref, e_ref, *, tc, t_len, ctx_len):
    i = pl.program_id(1)
    e_ref[0:CONV_HALO] = prev_ref[0]
    e_ref[CONV_HALO:CONV_HALO + tc] = cur_ref[0]
    e_ref[CONV_HALO + tc:] = next_ref[0]
    tok = i * tc + lax.broadcasted_iota(jnp.int32, (tc, 1), 0)
    acc = jnp.broadcast_to(b_ref[...], (tc, ZF_XBC_W))
    pad = (SSM_CONV - 1) // 2
    for tap in range(SSM_CONV):
        off = tap - pad
        src = e_ref[CONV_HALO + off:CONV_HALO + off + tc, :]
        nb = tok + off
        same = ((tok - ctx_len) ^ (nb - ctx_len)) >= 0
        valid = same & (nb >= 0) & (nb < t_len)
        acc = acc + jnp.where(valid, src, 0.0) * w_ref[tap:tap + 1, :]
    o_ref[0] = _silu(acc)


def _conv_call(zf, conv_w, conv_b, *, ctx_len, tc):
    bsz, t, _ = zf.shape
    col = ZF_XBC_OFF // ZF_XBC_W
    per = tc // CONV_HALO
    last = t // CONV_HALO - 1
    kern = functools.partial(_conv_kernel, tc=tc, t_len=t, ctx_len=ctx_len)
    return pl.pallas_call(
        kern,
        grid=(bsz, t // tc),
        in_specs=[pl.BlockSpec((1, CONV_HALO, ZF_XBC_W),
                               lambda b, i: (b, jnp.maximum(i * per - 1, 0), col)),
                  pl.BlockSpec((1, tc, ZF_XBC_W), lambda b, i: (b, i, col)),
                  pl.BlockSpec((1, CONV_HALO, ZF_XBC_W),
                               lambda b, i: (b, jnp.minimum((i + 1) * per, last), col)),
                  pl.BlockSpec((SUBLANES, ZF_XBC_W), lambda b, i: (0, 0)),
                  pl.BlockSpec((1, ZF_XBC_W), lambda b, i: (0, 0))],
        out_specs=pl.BlockSpec((1, tc, ZF_XBC_W), lambda b, i: (b, i, 0)),
        out_shape=jax.ShapeDtypeStruct((bsz, t, ZF_XBC_W), F32),
        scratch_shapes=[pltpu.VMEM((tc + 2 * CONV_HALO, ZF_XBC_W), F32)],
        compiler_params=_cparams(("parallel", "parallel")),
    )(zf, zf, zf, conv_w, conv_b)


def _split3(x):
    p1 = x.astype(BF16)
    r1 = x - p1.astype(F32)
    p2 = r1.astype(BF16)
    p3 = (r1 - p2.astype(F32)).astype(BF16)
    return p1, p2, p3


def _cumsum_rows(x, tri):
    p1, p2, p3 = _split3(x)
    return _dot(tri, p1) + _dot(tri, p2) + _dot(tri, p3)


def _expand_heads(a, sel_ref):
    hi = a.astype(BF16)
    lo = (a - hi.astype(F32)).astype(BF16)
    return _dot(jnp.concatenate([hi, lo], axis=1), sel_ref[...])


def _head_selector(off):
    sel = np.zeros((LANES, SSM_INNER), np.float32)
    for h in range(SSM_HEADS):
        sel[off + h, h * SSM_HEAD_DIM:(h + 1) * SSM_HEAD_DIM] = 1.0
    return jnp.asarray(np.concatenate([sel, sel], axis=0), BF16)


def _ssd_kernel(xf_ref, df_ref, xb_ref, db_ref, bias_ref, alog_ref, skip_ref, self_ref, selb_ref,
                yf_ref, yb_ref, sf_ref, sb_ref, *, chunk, sub):
    c = chunk
    p = pl.program_id(1)
    nh = SSM_HEADS
    gw = SSM_HPG * SSM_HEAD_DIM
    bc_w = SSM_GROUPS * SSM_STATE

    @pl.when(p == 0)
    def _():
        sf_ref[...] = jnp.zeros(sf_ref.shape, F32)
        sb_ref[...] = jnp.zeros(sb_ref.shape, F32)

    ii = lax.broadcasted_iota(jnp.int32, (c, c), 0)
    jj = lax.broadcasted_iota(jnp.int32, (c, c), 1)
    lower = ii >= jj
    upper = jj >= ii
    tri = jnp.where(lower, 1.0, 0.0).astype(BF16)
    lane_lo = lax.broadcasted_iota(jnp.int32, (c, LANES), 1) < SSM_HEAD_DIM
    lane_hi = jnp.logical_not(lane_lo)

    def operands(x_ref, rows):
        xs = x_ref[0, rows, 0:SSM_INNER]
        bm = x_ref[0, rows, SSM_INNER:SSM_INNER + bc_w].astype(BF16)
        cm = x_ref[0, rows, SSM_INNER + bc_w:].astype(BF16)
        return xs, bm, cm

    def gates(d_ref, rows):
        raw = d_ref[0, rows, :] + bias_ref[...]
        dt = jnp.maximum(raw, 0.0) + jnp.log1p(jnp.exp(-jnp.abs(raw)))
        la = dt * -jnp.exp(alog_ref[...])
        acum = _cumsum_rows(la, tri)
        return dt, acum, acum - la

    def within(xs, bm, cm, dt, acum, excl):
        dt_t = dt.T
        acum_t = acum.T
        excl_t = excl.T
        xs_b = xs.astype(BF16)
        pieces = []
        for g in range(SSM_GROUPS):
            gs = slice(g * SSM_STATE, (g + 1) * SSM_STATE)
            scores = _dot_nt(cm[:, gs], bm[:, gs])
            for m in range(SSM_HPG // 2):
                pair = g * SSM_HPG // 2 + m
                xp = xs_b[:, pair * LANES:(pair + 1) * LANES]
                acc = None
                for e in range(2):
                    hf = 2 * pair + e
                    hb = nh + hf
                    dec_f = jnp.where(lower, jnp.exp(acum[:, hf:hf + 1] - acum_t[hf:hf + 1, :]), 0.0)
                    dec_b = jnp.where(upper, jnp.exp(excl_t[hb:hb + 1, :] - excl[:, hb:hb + 1]), 0.0)
                    wm = scores * (dec_f * dt_t[hf:hf + 1, :] + dec_b * dt_t[hb:hb + 1, :])
                    xh = jnp.where(lane_lo if e == 0 else lane_hi, xp, 0.0)
                    part = _dot(wm.astype(BF16), xh)
                    acc = part if acc is None else acc + part
                pieces.append(acc)
        return jnp.concatenate(pieces, axis=1)

    def carried(s_ref, xs, bm, cm, carry, wdt, keep):
        xw = (xs * wdt).astype(BF16)
        inter = []
        for g in range(SSM_GROUPS):
            gs = slice(g * SSM_STATE, (g + 1) * SSM_STATE)
            cs = slice(g * gw, (g + 1) * gw)
            inter.append(_dot(cm[:, gs], s_ref[g].astype(BF16)))
            s_ref[g] = s_ref[g] * keep[:, cs] + _dot_tn(bm[:, gs], xw[:, cs])
        return jnp.concatenate(inter, axis=1) * carry

    rows = [pl.ds(i * c, c) for i in range(sub)]
    fwd = [operands(xf_ref, r) + gates(df_ref, r) for r in rows]
    bwd = [operands(xb_ref, r) + gates(db_ref, r) for r in rows]

    fwd_scale = []
    for xs, bm, cm, dt, acum, excl in fwd:
        last = acum[c - 1:c, :]
        both = _expand_heads(
            jnp.concatenate([jnp.exp(acum), jnp.exp(last - acum) * dt], axis=0), self_ref)
        fwd_scale.append((both[:c], both[c:], both[c - 1:c]))
    bwd_scale = []
    for xs, bm, cm, dt, acum, excl in bwd:
        total = acum[c - 1:c, :]
        both = _expand_heads(
            jnp.concatenate([jnp.exp(total - excl), jnp.exp(excl) * dt], axis=0), selb_ref)
        bwd_scale.append((both[:c], both[c:], both[0:1]))

    y_within = [within(*f) for f in fwd]

    for i in range(sub):
        xs, bm, cm = fwd[i][:3]
        y = y_within[i] + carried(sf_ref, xs, bm, cm, *fwd_scale[i]) + skip_ref[...] * xs
        yf_ref[0, rows[i], :] = y
    for i in reversed(range(sub)):
        xs, bm, cm = bwd[i][:3]
        yb_ref[0, rows[i], :] = carried(sb_ref, xs, bm, cm, *bwd_scale[i])


def _ssd_call(xbc, zf, bias_row, alog_row, skip_row, *, ctx_len, chunk):
    bsz, t, _ = xbc.shape
    sub = 2 if ctx_len % (2 * chunk) == 0 and t % (2 * chunk) == 0 else 1
    blk = sub * chunk
    nblk = t // blk
    dt_blk = ZF_DT_OFF // LANES
    bwd = functools.partial(_bwd_chunk, n_ctx_chunks=ctx_len // blk, n_chunks=nblk)
    kern = functools.partial(_ssd_kernel, chunk=chunk, sub=sub)
    out = jax.ShapeDtypeStruct((bsz, t, SSM_INNER), F32)
    row = lambda w: pl.BlockSpec((1, w), lambda b, p: (0, 0))
    return pl.pallas_call(
        kern,
        grid=(bsz, nblk),
        in_specs=[pl.BlockSpec((1, blk, ZF_XBC_W), lambda b, p: (b, p, 0)),
                  pl.BlockSpec((1, blk, LANES), lambda b, p: (b, p, dt_blk)),
                  pl.BlockSpec((1, blk, ZF_XBC_W), lambda b, p: (b, bwd(p), 0)),
                  pl.BlockSpec((1, blk, LANES), lambda b, p: (b, bwd(p), dt_blk)),
                  row(LANES), row(LANES), row(SSM_INNER),
                  pl.BlockSpec((2 * LANES, SSM_INNER), lambda b, p: (0, 0)),
                  pl.BlockSpec((2 * LANES, SSM_INNER), lambda b, p: (0, 0))],
        out_specs=[pl.BlockSpec((1, blk, SSM_INNER), lambda b, p: (b, p, 0)),
                   pl.BlockSpec((1, blk, SSM_INNER), lambda b, p: (b, bwd(p), 0))],
        out_shape=[out, out],
        scratch_shapes=[pltpu.VMEM((SSM_GROUPS, SSM_STATE, SSM_HPG * SSM_HEAD_DIM), F32),
                        pltpu.VMEM((SSM_GROUPS, SSM_STATE, SSM_HPG * SSM_HEAD_DIM), F32)],
        compiler_params=_cparams(("parallel", "arbitrary")),
    )(xbc, zf, xbc, zf, bias_row, alog_row, skip_row,
      _head_selector(0), _head_selector(SSM_HEADS))


def _merge_kernel(attn_ref, retf_ref, retb_ref, ssmf_ref, ssmb_ref, mg_ref, ag_ref, rg_ref,
                  sz_ref, gnw_ref, snw_ref, wb_ref, o_ref):
    d = D_MODEL
    gate_in = lambda ref, lo=0, hi=None: ref[0, :, lo:hi].astype(F32)
    a = attn_ref[0] * _silu(gate_in(ag_ref))

    ret = retf_ref[0] + retb_ref[0]
    parts = []
    for h in range(RET_HEADS):
        r = ret[:, h * RET_V_DIM:(h + 1) * RET_V_DIM]
        rc = r - jnp.mean(r, axis=-1, keepdims=True)
        var = jnp.mean(rc * rc, axis=-1, keepdims=True)
        parts.append(rc * lax.rsqrt(var + NORM_EPS))
    r = jnp.concatenate(parts, axis=1) * gnw_ref[...] * _silu(gate_in(rg_ref))

    s = (ssmf_ref[0] + ssmb_ref[0]) * _silu(gate_in(sz_ref))
    gwid = SSM_INNER // SSM_GROUPS
    parts = []
    for g in range(SSM_GROUPS):
        sg = s[:, g * gwid:(g + 1) * gwid]
        parts.append(sg * lax.rsqrt(jnp.mean(sg * sg, axis=-1, keepdims=True) + NORM_EPS))
    s = jnp.concatenate(parts, axis=1) * snw_ref[...]

    merged = _sigmoid(gate_in(mg_ref, 0, d)) * _dot(a.astype(BF16), wb_ref[0])
    merged = merged + _sigmoid(gate_in(mg_ref, d, 2 * d)) * _dot(r.astype(BF16), wb_ref[1])
    merged = merged + _sigmoid(gate_in(mg_ref, 2 * d, 3 * d)) * _dot(s.astype(BF16), wb_ref[2])
    o_ref[0] = merged.astype(BF16)


def _merge_call(attn, retf, retb, ssmf, ssmb, zb, gnw, snw, wb, *, tm):
    bsz, t, _ = attn.shape
    d = D_MODEL
    bw = BRANCH_WIDTH
    act = lambda blk: pl.BlockSpec((1, tm, bw), lambda b, i: (b, i, blk))
    return pl.pallas_call(
        _merge_kernel,
        grid=(bsz, t // tm),
        in_specs=[act(0), act(0), act(0), act(0), act(0),
                  pl.BlockSpec((1, tm, N_BRANCHES * d),
                               lambda b, i: (b, i, ZB_MG_OFF // (N_BRANCHES * d))),
                  act(ZB_AG_OFF // bw), act(ZB_RG_OFF // bw), act(ZB_SZ_OFF // bw),
                  pl.BlockSpec((1, bw), lambda b, i: (0, 0)),
                  pl.BlockSpec((1, bw), lambda b, i: (0, 0)),
                  pl.BlockSpec((N_BRANCHES, bw, d), lambda b, i: (0, 0, 0),
                               pipeline_mode=pl.Buffered(1))],
        out_specs=pl.BlockSpec((1, tm, d), lambda b, i: (b, i, 0)),
        out_shape=jax.ShapeDtypeStruct((bsz, t, d), BF16),
        compiler_params=_cparams(("parallel", "parallel")),
    )(attn, retf, retb, ssmf, ssmb, zb, zb, zb, zb, gnw, snw, wb)


def _out_kernel(m_ref, x_ref, mod_ref, w_ref, g_ref, b_ref, o_ref, *, tm, n_batch, ctx_len):
    b = pl.program_id(0)
    i = pl.program_id(1)
    d = D_MODEL
    row = i * tm + lax.broadcasted_iota(jnp.int32, (tm, 1), 0)
    gate = _modulation_rows(mod_ref, b, n_batch, row < ctx_len, 2 * d, 3 * d)
    y = _dot(m_ref[0], w_ref[...])
    z = DEEPNORM_ALPHA * x_ref[0] + gate * y
    zc = z - jnp.mean(z, axis=-1, keepdims=True)
    var = jnp.mean(zc * zc, axis=-1, keepdims=True)
    o_ref[0] = zc * lax.rsqrt(var + NORM_EPS) * g_ref[...] + b_ref[...]


def _out_call(merged, xs, mod, w_out, ln_g, ln_b, *, n_batch, ctx_len, tm):
    bsz, t, d = xs.shape
    kern = functools.partial(_out_kernel, tm=tm, n_batch=n_batch, ctx_len=ctx_len)
    tok = pl.BlockSpec((1, tm, d), lambda b, i: (b, i, 0))
    vec = pl.BlockSpec((1, d), lambda b, i: (0, 0))
    return pl.pallas_call(
        kern,
        grid=(bsz, t // tm),
        in_specs=[tok, tok,
                  pl.BlockSpec((SUBLANES, 3 * d), lambda b, i: (0, 0)),
                  pl.BlockSpec((d, d), lambda b, i: (0, 0), pipeline_mode=pl.Buffered(1)),
                  vec, vec],
        out_specs=tok,
        out_shape=jax.ShapeDtypeStruct((bsz, t, d), F32),
        compiler_params=_cparams(("parallel", "parallel")),
    )(merged, xs, mod, w_out, ln_g, ln_b)


def _pack_w_in(w_in):
    depth, d, _ = w_in.shape
    seg = lambda name: w_in[:, :, _REF_SPLITS[name][0]:_REF_SPLITS[name][0] + _REF_SPLITS[name][1]]
    cols = [seg(n) for n in ("aq",) + ZB_ORDER + ZF_ORDER]
    cols.append(jnp.zeros((depth, d, ZF_W - ZF_USED), w_in.dtype))
    return jnp.concatenate(cols, axis=-1).astype(BF16)


def _rope_tables(rows, ctx_len):
    row = jnp.repeat(jnp.arange(rows, dtype=F32), GRID_W)
    col = jnp.tile(jnp.arange(GRID_W, dtype=F32), rows)
    n_freq = HEAD_DIM // 4
    inv = ROPE_BASE ** (-jnp.arange(n_freq, dtype=F32) / n_freq)
    ang_r = row[:, None] * inv
    ang_c = col[:, None] * inv
    ang = jnp.concatenate([ang_r, ang_r, ang_c, ang_c], -1)
    cos = jnp.concatenate([jnp.ones((ctx_len, HEAD_DIM), F32), jnp.cos(ang)], 0)
    sin = jnp.concatenate([jnp.zeros((ctx_len, HEAD_DIM), F32), jnp.sin(ang)], 0)
    first = (np.arange(HEAD_DIM) % (HEAD_DIM // 2)) < HEAD_DIM // 4
    sin_lo = jnp.where(first, -sin, 0.0)
    sin_hi = jnp.where(first, 0.0, sin)
    return cos, sin_lo, sin_hi


def _pad_lanes(v, width):
    return jnp.pad(v.reshape(1, -1), ((0, 0), (0, width - v.size)))


def _tile_choices(t, ctx_len):
    def largest(cap, mult):
        best = mult
        for cand in range(mult, cap + 1, mult):
            if t % cand == 0:
                best = cand
        return best
    return dict(proj=largest(1056, 16), attn=largest(768, LANES), conv=largest(528, SUBLANES),
                merge=largest(256, 16), out=largest(528, 16))


def kernel(x, c, ctx, c_ctx, ada_w, ada_b, w_in, attn_q_norm, attn_k_norm, ret_log_decay, ret_gn_w,
           ssm_conv_w, ssm_conv_b, ssm_dt_bias, ssm_a_log, ssm_d, ssm_norm_w, w_branch, w_out,
           ln_g, ln_b):
    bsz, seq, d = x.shape
    ctx_len = ctx.shape[1]
    depth = w_in.shape[0]
    t = seq + ctx_len
    chunk = LANES
    ret_chunk = 256 if ctx_len % 256 == 0 and t % 256 == 0 else chunk
    assert d == D_MODEL and bsz < SUBLANES and seq % GRID_W == 0
    assert ctx_len % chunk == 0 and t % chunk == 0
    tiles = _tile_choices(t, ctx_len)

    xs = jnp.concatenate([ctx, x], axis=1)
    cvec = jnp.zeros((SUBLANES, d), F32).at[:bsz].set(c).at[bsz].set(c_ctx)
    mod = _ada_call(cvec, ada_w, ada_b)
    cos, sin_lo, sin_hi = _rope_tables(seq // GRID_W, ctx_len)
    w_packed = _pack_w_in(w_in)
    wb = w_branch.astype(BF16)
    wo = w_out.astype(BF16)
    conv_w = jnp.pad(ssm_conv_w, ((0, 0), (0, SUBLANES - SSM_CONV), (0, 0)))

    for l in range(depth):
        zq, zb, zf = _inproj_call(xs, mod[l], w_packed[l], cos, sin_lo, sin_hi,
                              attn_q_norm[l].reshape(1, -1), attn_k_norm[l].reshape(1, -1),
                              n_batch=bsz, ctx_len=ctx_len, tm=tiles["proj"])
        attn = _attn_call(zq, zb, ctx_len=ctx_len, tq=tiles["attn"], tk=tiles["attn"])
        retf, retb = _ret_call(zb, ret_log_decay[l], ctx_len=ctx_len, chunk=ret_chunk)
        xbc = _conv_call(zf, conv_w[l], ssm_conv_b[l].reshape(1, -1), ctx_len=ctx_len,
                         tc=tiles["conv"])
        ssmf, ssmb = _ssd_call(xbc, zf, _pad_lanes(ssm_dt_bias[l], LANES),
                               _pad_lanes(ssm_a_log[l], LANES),
                               jnp.repeat(ssm_d[l], SSM_HEAD_DIM).reshape(1, -1),
                               ctx_len=ctx_len, chunk=chunk)
        merged = _merge_call(attn, retf, retb, ssmf, ssmb, zb, ret_gn_w[l].reshape(1, -1),
                             ssm_norm_w[l].reshape(1, -1), wb[l], tm=tiles["merge"])
        xs = _out_call(merged, xs, mod[l], wo[l], ln_g[l].reshape(1, -1), ln_b[l].reshape(1, -1),
                       n_batch=bsz, ctx_len=ctx_len, tm=tiles["out"])
    return xs[:, ctx_len:, :]
```

```python
import functools

import numpy as np
import jax
import jax.numpy as jnp
from jax import lax
from jax.experimental import pallas as pl
from jax.experimental.pallas import tpu as pltpu

F32 = jnp.float32
BF16 = jnp.bfloat16

D_MODEL = 2048
GRID_W = 64
HEAD_DIM = 128
ATTN_HEADS = 8
ATTN_KV_HEADS = 2
ATTN_GROUP = ATTN_HEADS // ATTN_KV_HEADS
ROPE_BASE = 10000.0
RET_HEADS = 4
RET_QK_DIM = 128
RET_V_DIM = 256
SSM_INNER = 1024
SSM_HEAD_DIM = 64
SSM_HEADS = 16
SSM_GROUPS = 2
SSM_HPG = SSM_HEADS // SSM_GROUPS
SSM_STATE = 128
SSM_CONV = 5
N_BRANCHES = 3
BRANCH_WIDTH = 1024
DEPTH = 4
DEEPNORM_ALPHA = (2 * DEPTH) ** 0.25
NORM_EPS = 1e-6
ATTN_Q_SCALE = HEAD_DIM ** -0.5 * float(np.log2(np.e))

_REF_SPLITS = dict(aq=(0, 1024), ak=(1024, 256), av=(1280, 256), ag=(1536, 1024),
                   rq=(2560, 512), rk=(3072, 512), rv=(3584, 1024), rg=(4608, 1024),
                   sx=(5632, 1024), sb=(6656, 256), sc=(6912, 256), sdt=(7168, 32),
                   sz=(7200, 1024), mg=(8224, 6144))

LANES = 128
SUBLANES = 8
VMEM_LIMIT = 56 * 1024 * 1024

PROJ_TN = 512
ZQ_TILES = ATTN_HEADS * HEAD_DIM // PROJ_TN
ZB_ORDER = ("mg", "ag", "rg", "sz", "rv", "ak", "av", "rq", "rk")
ZB_MG_OFF, ZB_AG_OFF, ZB_RG_OFF, ZB_SZ_OFF = 0, 6144, 7168, 8192
ZB_RV_OFF, ZB_AK_OFF, ZB_AV_OFF, ZB_RQ_OFF, ZB_RK_OFF = 9216, 10240, 10496, 10752, 11264
ZB_W = 11776
ZB_TILES = ZB_W // PROJ_TN
ZB_RAW_TILES = ZB_AK_OFF // PROJ_TN
ZF_TILE0 = ZQ_TILES + ZB_TILES
ZF_ORDER = ("sx", "sb", "sc", "sdt")
ZF_USED = 1024 + 256 + 256 + 32
ZF_W = 2048
ZF_XBC_OFF = 0
ZF_XBC_W = 1536
ZF_DT_OFF = 1536
W_PACKED = ZQ_TILES * PROJ_TN + ZB_W + ZF_W


def _sigmoid(x):
    return 1.0 / (1.0 + jnp.exp(-x))


def _silu(x):
    return x * _sigmoid(x)


def _cparams(sem):
    return pltpu.CompilerParams(dimension_semantics=sem, vmem_limit_bytes=VMEM_LIMIT)


def _dot(a, b):
    return jnp.dot(a, b, preferred_element_type=F32)


def _dot_nt(a, b):
    return lax.dot_general(a, b, (((1,), (1,)), ((), ())), preferred_element_type=F32)


def _dot_tn(a, b):
    return lax.dot_general(a, b, (((0,), (0,)), ((), ())), preferred_element_type=F32)


def _ada_kernel(c_ref, w_ref, b_ref, o_ref):
    s = _silu(c_ref[...]).astype(BF16)
    o_ref[0] = _dot(s, w_ref[0].astype(BF16)) + b_ref[0]


def _ada_call(cvec, ada_w, ada_b):
    depth, d, n = ada_w.shape
    tn = 768
    return pl.pallas_call(
        _ada_kernel,
        grid=(depth, n // tn),
        in_specs=[pl.BlockSpec((SUBLANES, d), lambda l, j: (0, 0)),
                  pl.BlockSpec((1, d, tn), lambda l, j: (l, 0, j)),
                  pl.BlockSpec((1, 1, tn), lambda l, j: (l, 0, j))],
        out_specs=pl.BlockSpec((1, SUBLANES, tn), lambda l, j: (l, 0, j)),
        out_shape=jax.ShapeDtypeStruct((depth, SUBLANES, n), F32),
        compiler_params=_cparams(("parallel", "parallel")),
    )(cvec, ada_w, ada_b.reshape(depth, 1, n))


def _rope(x, cos, sin_lo, sin_hi):
    return x * cos + pltpu.roll(x, 96, 1) * sin_lo + pltpu.roll(x, 32, 1) * sin_hi


def _modulation_rows(mod_ref, b, n_batch, is_ctx, lo, hi):
    mb = mod_ref[pl.ds(b, 1), lo:hi]
    mc = mod_ref[pl.ds(n_batch, 1), lo:hi]
    return jnp.where(is_ctx, mc, mb)


def _inproj_kernel(u_ref, w_ref, cos_ref, sl_ref, sh_ref, qn_ref, kn_ref,
                   zq_ref, zb_ref, zf_ref):
    j = pl.program_id(2)
    jb = j - ZQ_TILES

    def project():
        return _dot(u_ref[0], w_ref[...])

    def rope(xh):
        return _rope(xh, cos_ref[...], sl_ref[...], sh_ref[...])

    def normed(w_row):
        def fn(xh):
            ms = jnp.mean(xh * xh, axis=-1, keepdims=True)
            return rope(xh * lax.rsqrt(ms + NORM_EPS) * w_row)
        return fn

    @pl.when(j >= ZF_TILE0)
    def _():
        zf_ref[0] = project()

    @pl.when((jb >= 0) & (jb < ZB_RAW_TILES))
    def _():
        zb_ref[0] = project().astype(BF16)

    @pl.when((j < ZQ_TILES) | ((jb >= ZB_RAW_TILES) & (j < ZF_TILE0)))
    def _():
        acc = project()

        def head(h):
            return acc[:, h * LANES:(h + 1) * LANES]

        def per_head(fn, n):
            return jnp.concatenate([fn(head(h)) for h in range(n)], axis=1)

        @pl.when(j < ZQ_TILES)
        def _():
            fn = normed(qn_ref[...])
            for h in range(ATTN_GROUP):
                zq_ref[0, h] = (fn(head(h)) * ATTN_Q_SCALE).astype(BF16)

        @pl.when(jb == ZB_AK_OFF // PROJ_TN)
        def _():
            k = per_head(normed(kn_ref[...]), 2)
            zb_ref[0] = jnp.concatenate([k, acc[:, 2 * LANES:]], axis=1).astype(BF16)

        @pl.when(jb == ZB_RQ_OFF // PROJ_TN)
        def _():
            zb_ref[0] = per_head(rope, 4).astype(BF16)

        @pl.when(jb == ZB_RK_OFF // PROJ_TN)
        def _():
            zb_ref[0] = per_head(lambda xh: rope(xh * (RET_QK_DIM ** -0.5)), 4).astype(BF16)


def _inproj_call(u, w, cos, sin_lo, sin_hi, qn, kn, *, layer, tm):
    bsz, t, d = u.shape
    rope_spec = pl.BlockSpec((tm, LANES), lambda b, i, j: (i, 0))
    vec_spec = pl.BlockSpec((None, 1, LANES), lambda b, i, j: (layer, 0, 0))
    return pl.pallas_call(
        _inproj_kernel,
        grid=(bsz, t // tm, W_PACKED // PROJ_TN),
        in_specs=[pl.BlockSpec((1, tm, d), lambda b, i, j: (b, i, 0)),
                  pl.BlockSpec((None, d, PROJ_TN), lambda b, i, j: (layer, 0, j)),
                  rope_spec, rope_spec, rope_spec, vec_spec, vec_spec],
        out_specs=[pl.BlockSpec((1, ATTN_GROUP, tm, HEAD_DIM),
                                lambda b, i, j: (b, jnp.minimum(j, ZQ_TILES - 1), i, 0)),
                   pl.BlockSpec((1, tm, PROJ_TN),
                                lambda b, i, j: (b, i, jnp.clip(j - ZQ_TILES, 0, ZB_TILES - 1))),
                   pl.BlockSpec((1, tm, PROJ_TN),
                                lambda b, i, j: (b, i, jnp.maximum(j - ZF_TILE0, 0)))],
        out_shape=[jax.ShapeDtypeStruct((bsz, ATTN_HEADS, t, HEAD_DIM), BF16),
                   jax.ShapeDtypeStruct((bsz, t, ZB_W), BF16),
                   jax.ShapeDtypeStruct((bsz, t, ZF_W), F32)],
        compiler_params=_cparams(("parallel", "parallel", "arbitrary")),
    )(u, w, cos, sin_lo, sin_hi, qn, kn)


NEG_BIG = -1e30
AHEAD = 2


def _attn_kernel(q_ref, k_ref, v_ref, o_ref, m_ref, l_ref, acc_ref, s_ref, *,
                 tq, tk, rb, t_len, ctx_len, unroll):
    qi = pl.program_id(2)
    nr = tq // rb
    per_kb = ATTN_GROUP * nr
    n_units = (t_len // tk) * per_kb

    m_ref[...] = jnp.full(m_ref.shape, NEG_BIG, F32)
    l_ref[...] = jnp.zeros(l_ref.shape, F32)
    acc_ref[...] = jnp.zeros(acc_ref.shape, F32)

    def unit(n):
        kb = n // per_kb
        return kb, n - kb * per_kb

    def scores(n):
        kb, slot = unit(n)
        h = slot // nr
        r0 = pl.multiple_of((slot - h * nr) * rb, rb)
        k0 = pl.multiple_of(kb * tk, tk)
        return _dot_nt(k_ref[0, pl.ds(k0, tk), :], q_ref[0, h, pl.ds(r0, rb), :])

    def update(n, s, mixed):
        kb, slot = unit(n)
        if mixed:
            r0 = (slot - (slot // nr) * nr) * rb
            qrow = r0 + lax.broadcasted_iota(jnp.int32, (1, rb), 1)
            key = kb * tk + lax.broadcasted_iota(jnp.int32, (tk, 1), 0)
            s = jnp.where(qrow < ctx_len, jnp.where(key < ctx_len, s, NEG_BIG), s)
        m_prev = m_ref[slot]
        m_new = jnp.maximum(m_prev, jnp.max(s, axis=0, keepdims=True))
        alpha = jnp.exp2(m_prev - m_new)
        p = jnp.exp2(s - m_new)
        l_ref[slot] = alpha * l_ref[slot] + jnp.sum(p, axis=0, keepdims=True)
        m_ref[slot] = m_new
        acc_ref[slot] = alpha * acc_ref[slot] + _dot(v_ref[0, 0, kb], p.astype(BF16))

    def run(mixed):
        for a in range(AHEAD):
            s_ref[a] = scores(a)

        def body(it, carry):
            n0 = it * unroll
            s = [s_ref[a] for a in range(AHEAD)]
            for u in range(0, unroll, AHEAD):
                s_next = [scores(jnp.minimum(n0 + u + AHEAD + a, n_units - 1))
                          for a in range(AHEAD)]
                for a in range(AHEAD):
                    update(n0 + u + a, s[a], mixed)
                s = s_next
            for a in range(AHEAD):
                s_ref[a] = s[a]
            return carry

        lax.fori_loop(0, n_units // unroll, body, 0)

    pl.when(qi == 0)(lambda: run(True))
    pl.when(qi > 0)(lambda: run(False))

    for h in range(ATTN_GROUP):
        for r in range(nr):
            slot = h * nr + r
            o_t = acc_ref[slot] * (1.0 / l_ref[slot])
            o_ref[0, r * rb:(r + 1) * rb, h * HEAD_DIM:(h + 1) * HEAD_DIM] = o_t.T


def _attn_call(zq, zb, *, ctx_len, tq, tk):
    bsz, t, _ = zb.shape
    assert ctx_len <= tq and ctx_len <= tk
    rb = 256 if tq % 256 == 0 else tq
    per_kb = ATTN_GROUP * (tq // rb)
    unroll = max(u for u in range(1, 13) if per_kb % u == 0)
    k_blk = ZB_AK_OFF // HEAD_DIM
    v_t = zb[:, :, ZB_AV_OFF:ZB_AV_OFF + ATTN_KV_HEADS * HEAD_DIM]
    v_t = v_t.reshape(bsz, t // tk, tk, ATTN_KV_HEADS, HEAD_DIM).transpose(0, 3, 1, 4, 2)
    kern = functools.partial(_attn_kernel, tq=tq, tk=tk, rb=rb, t_len=t, ctx_len=ctx_len,
                             unroll=unroll)
    return pl.pallas_call(
        kern,
        grid=(bsz, ATTN_KV_HEADS, t // tq),
        in_specs=[pl.BlockSpec((1, ATTN_GROUP, tq, HEAD_DIM), lambda b, g, qi: (b, g, qi, 0)),
                  pl.BlockSpec((1, t, HEAD_DIM), lambda b, g, qi: (b, 0, k_blk + g)),
                  pl.BlockSpec((1, 1, t // tk, HEAD_DIM, tk), lambda b, g, qi: (b, g, 0, 0, 0))],
        out_specs=pl.BlockSpec((1, tq, ATTN_GROUP * HEAD_DIM), lambda b, g, qi: (b, qi, g)),
        out_shape=jax.ShapeDtypeStruct((bsz, t, ATTN_HEADS * HEAD_DIM), F32),
        scratch_shapes=[pltpu.VMEM((per_kb, 1, rb), F32),
                        pltpu.VMEM((per_kb, 1, rb), F32),
                        pltpu.VMEM((per_kb, HEAD_DIM, rb), F32),
                        pltpu.VMEM((AHEAD, tk, rb), F32)],
        compiler_params=_cparams(("parallel", "parallel", "parallel")),
    )(zq, zb, v_t)


def _bwd_chunk(p, n_ctx_chunks, n_chunks):
    return jnp.where(p < n_ctx_chunks, n_ctx_chunks - 1 - p, n_chunks + n_ctx_chunks - 1 - p)


def _ret_kernel(ld_ref, qf_ref, kf_ref, vf_ref, qb_ref, kb_ref, vb_ref, yf_ref, yb_ref,
                sf_ref, sb_ref, *, chunk):
    c = chunk
    p = pl.program_id(1)

    @pl.when(p == 0)
    def _():
        sf_ref[...] = jnp.zeros(sf_ref.shape, F32)
        sb_ref[...] = jnp.zeros(sb_ref.shape, F32)

    ii = lax.broadcasted_iota(jnp.int32, (c, c), 0)
    jj = lax.broadcasted_iota(jnp.int32, (c, c), 1)
    dist = (ii - jj).astype(F32)
    pos = lax.broadcasted_iota(jnp.int32, (c, 1), 0).astype(F32)
    full = jnp.full((1, 1), float(c), F32)

    for h in range(RET_HEADS):
        ldf = ld_ref[0, h]
        ldb = ld_ref[1, h]
        qs = slice(h * RET_QK_DIM, (h + 1) * RET_QK_DIM)
        vs = slice(h * RET_V_DIM, (h + 1) * RET_V_DIM)
        decay = jnp.where(dist >= 0, jnp.exp(ldf * dist), 0.0) \
            + jnp.where(dist <= 0, jnp.exp(-ldb * dist), 0.0)
        q = qf_ref[0, :, qs]
        k = kf_ref[0, :, qs]
        v = vf_ref[0, :, vs]
        y = _dot((_dot_nt(q, k) * decay).astype(BF16), v)
        y = y + _dot(q, sf_ref[h].astype(BF16)) * jnp.exp(ldf * (pos + 1.0))
        yf_ref[0, :, vs] = y
        wv = (v.astype(F32) * jnp.exp(ldf * (c - 1.0 - pos))).astype(BF16)
        sf_ref[h] = sf_ref[h] * jnp.exp(ldf * full) + _dot_tn(k, wv)
        q = qb_ref[0, :, qs]
        k = kb_ref[0, :, qs]
        v = vb_ref[0, :, vs]
        yb_ref[0, :, vs] = _dot(q, sb_ref[h].astype(BF16)) * jnp.exp(ldb * (c - pos))
        wv = (v.astype(F32) * jnp.exp(ldb * pos)).astype(BF16)
        sb_ref[h] = sb_ref[h] * jnp.exp(ldb * full) + _dot_tn(k, wv)


def _ret_call(zb, log_decay, *, ctx_len, chunk):
    bsz, t, _ = zb.shape
    nc = t // chunk
    ncc = ctx_len // chunk
    qw = RET_HEADS * RET_QK_DIM
    vw = RET_HEADS * RET_V_DIM
    q_blk, k_blk, v_blk = ZB_RQ_OFF // qw, ZB_RK_OFF // qw, ZB_RV_OFF // vw
    bwd = functools.partial(_bwd_chunk, n_ctx_chunks=ncc, n_chunks=nc)
    kern = functools.partial(_ret_kernel, chunk=chunk)
    out = jax.ShapeDtypeStruct((bsz, t, vw), F32)
    return pl.pallas_call(
        kern,
        grid=(bsz, nc),
        in_specs=[pl.BlockSpec(memory_space=pltpu.SMEM),
                  pl.BlockSpec((1, chunk, qw), lambda b, p: (b, p, q_blk)),
                  pl.BlockSpec((1, chunk, qw), lambda b, p: (b, p, k_blk)),
                  pl.BlockSpec((1, chunk, vw), lambda b, p: (b, p, v_blk)),
                  pl.BlockSpec((1, chunk, qw), lambda b, p: (b, bwd(p), q_blk)),
                  pl.BlockSpec((1, chunk, qw), lambda b, p: (b, bwd(p), k_blk)),
                  pl.BlockSpec((1, chunk, vw), lambda b, p: (b, bwd(p), v_blk))],
        out_specs=[pl.BlockSpec((1, chunk, vw), lambda b, p: (b, p, 0)),
                   pl.BlockSpec((1, chunk, vw), lambda b, p: (b, bwd(p), 0))],
        out_shape=[out, out],
        scratch_shapes=[pltpu.VMEM((RET_HEADS, RET_QK_DIM, RET_V_DIM), F32),
                        pltpu.VMEM((RET_HEADS, RET_QK_DIM, RET_V_DIM), F32)],
        compiler_params=_cparams(("parallel", "arbitrary")),
    )(log_decay, zb, zb, zb, zb, zb, zb)


CONV_HALO = SUBLANES


def _conv_kernel(prev_ref, cur_ref, next_ref, w_ref, b_ref, o_ref, e_ref, *, tc, t_len, ctx_len):
    i = pl.program_id(1)
    e_ref[0:CONV_HALO] = prev_ref[0]
    e_ref[CONV_HALO:CONV_HALO + tc] = cur_ref[0]
    e_ref[CONV_HALO + tc:] = next_ref[0]
    tok = i * tc + lax.broadcasted_iota(jnp.int32, (tc, 1), 0)
    acc = jnp.broadcast_to(b_ref[...], (tc, ZF_XBC_W))
    pad = (SSM_CONV - 1) // 2
    for tap in range(SSM_CONV):
        off = tap - pad
        src = e_ref[CONV_HALO + off:CONV_HALO + off + tc, :]
        nb = tok + off
        same = ((tok - ctx_len) ^ (nb - ctx_len)) >= 0
        valid = same & (nb >= 0) & (nb < t_len)
        acc = acc + jnp.where(valid, src, 0.0) * w_ref[tap:tap + 1, :]
    o_ref[0] = _silu(acc)


def _conv_call(zf, conv_w, conv_b, *, ctx_len, tc):
    bsz, t, _ = zf.shape
    col = ZF_XBC_OFF // ZF_XBC_W
    per = tc // CONV_HALO
    last = t // CONV_HALO - 1
    kern = functools.partial(_conv_kernel, tc=tc, t_len=t, ctx_len=ctx_len)
    return pl.pallas_call(
        kern,
        grid=(bsz, t // tc),
        in_specs=[pl.BlockSpec((1, CONV_HALO, ZF_XBC_W),
                               lambda b, i: (b, jnp.maximum(i * per - 1, 0), col)),
                  pl.BlockSpec((1, tc, ZF_XBC_W), lambda b, i: (b, i, col)),
                  pl.BlockSpec((1, CONV_HALO, ZF_XBC_W),
                               lambda b, i: (b, jnp.minimum((i + 1) * per, last), col)),
                  pl.BlockSpec((SUBLANES, ZF_XBC_W), lambda b, i: (0, 0)),
                  pl.BlockSpec((1, ZF_XBC_W), lambda b, i: (0, 0))],
        out_specs=pl.BlockSpec((1, tc, ZF_XBC_W), lambda b, i: (b, i, 0)),
        out_shape=jax.ShapeDtypeStruct((bsz, t, ZF_XBC_W), F32),
        scratch_shapes=[pltpu.VMEM((tc + 2 * CONV_HALO, ZF_XBC_W), F32)],
        compiler_params=_cparams(("parallel", "parallel")),
    )(zf, zf, zf, conv_w, conv_b)


def _split3(x):
    p1 = x.astype(BF16)
    r1 = x - p1.astype(F32)
    p2 = r1.astype(BF16)
    p3 = (r1 - p2.astype(F32)).astype(BF16)
    return p1, p2, p3


def _cumsum_rows(x, tri):
    p1, p2, p3 = _split3(x)
    return _dot(tri, p1) + _dot(tri, p2) + _dot(tri, p3)


def _expand_heads(a, sel_ref):
    hi = a.astype(BF16)
    lo = (a - hi.astype(F32)).astype(BF16)
    return _dot(jnp.concatenate([hi, lo], axis=1), sel_ref[...])


def _head_selector(off):
    sel = np.zeros((LANES, SSM_INNER), np.float32)
    for h in range(SSM_HEADS):
        sel[off + h, h * SSM_HEAD_DIM:(h + 1) * SSM_HEAD_DIM] = 1.0
    return jnp.asarray(np.concatenate([sel, sel], axis=0), BF16)


def _ssd_kernel(xf_ref, df_ref, xb_ref, db_ref, bias_ref, alog_ref, skip_ref, self_ref, selb_ref,
                yf_ref, yb_ref, sf_ref, sb_ref, *, chunk, sub):
    c = chunk
    p = pl.program_id(1)
    nh = SSM_HEADS
    gw = SSM_HPG * SSM_HEAD_DIM
    bc_w = SSM_GROUPS * SSM_STATE

    @pl.when(p == 0)
    def _():
        sf_ref[...] = jnp.zeros(sf_ref.shape, F32)
        sb_ref[...] = jnp.zeros(sb_ref.shape, F32)

    ii = lax.broadcasted_iota(jnp.int32, (c, c), 0)
    jj = lax.broadcasted_iota(jnp.int32, (c, c), 1)
    lower = ii >= jj
    upper = jj >= ii
    tri = jnp.where(lower, 1.0, 0.0).astype(BF16)
    lane_lo = lax.broadcasted_iota(jnp.int32, (c, LANES), 1) < SSM_HEAD_DIM
    lane_hi = jnp.logical_not(lane_lo)

    def operands(x_ref, rows):
        xs = x_ref[0, rows, 0:SSM_INNER]
        bm = x_ref[0, rows, SSM_INNER:SSM_INNER + bc_w].astype(BF16)
        cm = x_ref[0, rows, SSM_INNER + bc_w:].astype(BF16)
        return xs, bm, cm

    def gates(d_ref, rows):
        raw = d_ref[0, rows, :] + bias_ref[...]
        dt = jnp.maximum(raw, 0.0) + jnp.log1p(jnp.exp(-jnp.abs(raw)))
        la = dt * -jnp.exp(alog_ref[...])
        acum = _cumsum_rows(la, tri)
        return dt, acum, acum - la

    def within(xs, bm, cm, dt, acum, excl):
        dt_t = dt.T
        acum_t = acum.T
        excl_t = excl.T
        xs_b = xs.astype(BF16)
        pieces = []
        for g in range(SSM_GROUPS):
            gs = slice(g * SSM_STATE, (g + 1) * SSM_STATE)
            scores = _dot_nt(cm[:, gs], bm[:, gs])
            for m in range(SSM_HPG // 2):
                pair = g * SSM_HPG // 2 + m
                xp = xs_b[:, pair * LANES:(pair + 1) * LANES]
                acc = None
                for e in range(2):
                    hf = 2 * pair + e
                    hb = nh + hf
                    dec_f = jnp.where(lower, jnp.exp(acum[:, hf:hf + 1] - acum_t[hf:hf + 1, :]), 0.0)
                    dec_b = jnp.where(upper, jnp.exp(excl_t[hb:hb + 1, :] - excl[:, hb:hb + 1]), 0.0)
                    wm = scores * (dec_f * dt_t[hf:hf + 1, :] + dec_b * dt_t[hb:hb + 1, :])
                    xh = jnp.where(lane_lo if e == 0 else lane_hi, xp, 0.0)
                    part = _dot(wm.astype(BF16), xh)
                    acc = part if acc is None else acc + part
                pieces.append(acc)
        return jnp.concatenate(pieces, axis=1)

    def carried(s_ref, xs, bm, cm, carry, wdt, keep):
        xw = (xs * wdt).astype(BF16)
        inter = []
        for g in range(SSM_GROUPS):
            gs = slice(g * SSM_STATE, (g + 1) * SSM_STATE)
            cs = slice(g * gw, (g + 1) * gw)
            inter.append(_dot(cm[:, gs], s_ref[g].astype(BF16)))
            s_ref[g] = s_ref[g] * keep[:, cs] + _dot_tn(bm[:, gs], xw[:, cs])
        return jnp.concatenate(inter, axis=1) * carry

    rows = [pl.ds(i * c, c) for i in range(sub)]
    fwd = [operands(xf_ref, r) + gates(df_ref, r) for r in rows]
    bwd = [operands(xb_ref, r) + gates(db_ref, r) for r in rows]

    fwd_scale = []
    for xs, bm, cm, dt, acum, excl in fwd:
        last = acum[c - 1:c, :]
        both = _expand_heads(
            jnp.concatenate([jnp.exp(acum), jnp.exp(last - acum) * dt], axis=0), self_ref)
        fwd_scale.append((both[:c], both[c:], both[c - 1:c]))
    bwd_scale = []
    for xs, bm, cm, dt, acum, excl in bwd:
        total = acum[c - 1:c, :]
        both = _expand_heads(
            jnp.concatenate([jnp.exp(total - excl), jnp.exp(excl) * dt], axis=0), selb_ref)
        bwd_scale.append((both[:c], both[c:], both[0:1]))

    y_within = [within(*f) for f in fwd]

    for i in range(sub):
        xs, bm, cm = fwd[i][:3]
        y = y_within[i] + carried(sf_ref, xs, bm, cm, *fwd_scale[i]) + skip_ref[...] * xs
        yf_ref[0, rows[i], :] = y
    for i in reversed(range(sub)):
        xs, bm, cm = bwd[i][:3]
        yb_ref[0, rows[i], :] = carried(sb_ref, xs, bm, cm, *bwd_scale[i])


def _ssd_call(xbc, zf, bias_row, alog_row, skip_row, *, ctx_len, chunk):
    bsz, t, _ = xbc.shape
    sub = 2 if ctx_len % (2 * chunk) == 0 and t % (2 * chunk) == 0 else 1
    blk = sub * chunk
    nblk = t // blk
    dt_blk = ZF_DT_OFF // LANES
    bwd = functools.partial(_bwd_chunk, n_ctx_chunks=ctx_len // blk, n_chunks=nblk)
    kern = functools.partial(_ssd_kernel, chunk=chunk, sub=sub)
    out = jax.ShapeDtypeStruct((bsz, t, SSM_INNER), F32)
    row = lambda w: pl.BlockSpec((1, w), lambda b, p: (0, 0))
    return pl.pallas_call(
        kern,
        grid=(bsz, nblk),
        in_specs=[pl.BlockSpec((1, blk, ZF_XBC_W), lambda b, p: (b, p, 0)),
                  pl.BlockSpec((1, blk, LANES), lambda b, p: (b, p, dt_blk)),
                  pl.BlockSpec((1, blk, ZF_XBC_W), lambda b, p: (b, bwd(p), 0)),
                  pl.BlockSpec((1, blk, LANES), lambda b, p: (b, bwd(p), dt_blk)),
                  row(LANES), row(LANES), row(SSM_INNER),
                  pl.BlockSpec((2 * LANES, SSM_INNER), lambda b, p: (0, 0)),
                  pl.BlockSpec((2 * LANES, SSM_INNER), lambda b, p: (0, 0))],
        out_specs=[pl.BlockSpec((1, blk, SSM_INNER), lambda b, p: (b, p, 0)),
                   pl.BlockSpec((1, blk, SSM_INNER), lambda b, p: (b, bwd(p), 0))],
        out_shape=[out, out],
        scratch_shapes=[pltpu.VMEM((SSM_GROUPS, SSM_STATE, SSM_HPG * SSM_HEAD_DIM), F32),
                        pltpu.VMEM((SSM_GROUPS, SSM_STATE, SSM_HPG * SSM_HEAD_DIM), F32)],
        compiler_params=_cparams(("parallel", "arbitrary")),
    )(xbc, zf, xbc, zf, bias_row, alog_row, skip_row,
      _head_selector(0), _head_selector(SSM_HEADS))


def _merge_kernel(attn_ref, retf_ref, retb_ref, ssmf_ref, ssmb_ref, mg_ref, ag_ref, rg_ref,
                  sz_ref, gnw_ref, snw_ref, wb_ref, o_ref):
    d = D_MODEL
    gate_in = lambda ref, lo=0, hi=None: ref[0, :, lo:hi].astype(F32)
    a = attn_ref[0] * _silu(gate_in(ag_ref))

    ret = retf_ref[0] + retb_ref[0]
    parts = []
    for h in range(RET_HEADS):
        r = ret[:, h * RET_V_DIM:(h + 1) * RET_V_DIM]
        rc = r - jnp.mean(r, axis=-1, keepdims=True)
        var = jnp.mean(rc * rc, axis=-1, keepdims=True)
        parts.append(rc * lax.rsqrt(var + NORM_EPS))
    r = jnp.concatenate(parts, axis=1) * gnw_ref[...] * _silu(gate_in(rg_ref))

    s = (ssmf_ref[0] + ssmb_ref[0]) * _silu(gate_in(sz_ref))
    gwid = SSM_INNER // SSM_GROUPS
    parts = []
    for g in range(SSM_GROUPS):
        sg = s[:, g * gwid:(g + 1) * gwid]
        parts.append(sg * lax.rsqrt(jnp.mean(sg * sg, axis=-1, keepdims=True) + NORM_EPS))
    s = jnp.concatenate(parts, axis=1) * snw_ref[...]

    merged = _sigmoid(gate_in(mg_ref, 0, d)) * _dot(a.astype(BF16), wb_ref[0])
    merged = merged + _sigmoid(gate_in(mg_ref, d, 2 * d)) * _dot(r.astype(BF16), wb_ref[1])
    merged = merged + _sigmoid(gate_in(mg_ref, 2 * d, 3 * d)) * _dot(s.astype(BF16), wb_ref[2])
    o_ref[0] = merged.astype(BF16)


def _merge_call(attn, retf, retb, ssmf, ssmb, zb, gnw, snw, wb, *, layer, tm):
    bsz, t, _ = attn.shape
    d = D_MODEL
    bw = BRANCH_WIDTH
    act = lambda blk: pl.BlockSpec((1, tm, bw), lambda b, i: (b, i, blk))
    vec = pl.BlockSpec((None, 1, bw), lambda b, i: (layer, 0, 0))
    return pl.pallas_call(
        _merge_kernel,
        grid=(bsz, t // tm),
        in_specs=[act(0), act(0), act(0), act(0), act(0),
                  pl.BlockSpec((1, tm, N_BRANCHES * d),
                               lambda b, i: (b, i, ZB_MG_OFF // (N_BRANCHES * d))),
                  act(ZB_AG_OFF // bw), act(ZB_RG_OFF // bw), act(ZB_SZ_OFF // bw),
                  vec, vec,
                  pl.BlockSpec((None, N_BRANCHES, bw, d), lambda b, i: (layer, 0, 0, 0),
                               pipeline_mode=pl.Buffered(1))],
        out_specs=pl.BlockSpec((1, tm, d), lambda b, i: (b, i, 0)),
        out_shape=jax.ShapeDtypeStruct((bsz, t, d), BF16),
        compiler_params=_cparams(("parallel", "parallel")),
    )(attn, retf, retb, ssmf, ssmb, zb, zb, zb, zb, gnw, snw, wb)


def _modulate(x, mod_ref, b, n_batch, is_ctx):
    d = D_MODEL
    shift = _modulation_rows(mod_ref, b, n_batch, is_ctx, 0, d)
    scale = _modulation_rows(mod_ref, b, n_batch, is_ctx, d, 2 * d)
    return (x * (1.0 + scale) + shift).astype(BF16)


def _modulate_kernel(x_ref, mod_ref, u_ref, *, tm, n_batch, ctx_len):
    row = pl.program_id(1) * tm + lax.broadcasted_iota(jnp.int32, (tm, 1), 0)
    u_ref[0] = _modulate(x_ref[0], mod_ref, pl.program_id(0), n_batch, row < ctx_len)


def _modulate_call(xs, mod, *, layer, n_batch, ctx_len, tm):
    bsz, t, d = xs.shape
    kern = functools.partial(_modulate_kernel, tm=tm, n_batch=n_batch, ctx_len=ctx_len)
    tok = pl.BlockSpec((1, tm, d), lambda b, i: (b, i, 0))
    return pl.pallas_call(
        kern,
        grid=(bsz, t // tm),
        in_specs=[tok, pl.BlockSpec((None, SUBLANES, 3 * d), lambda b, i: (layer, 0, 0))],
        out_specs=tok,
        out_shape=jax.ShapeDtypeStruct((bsz, t, d), BF16),
        compiler_params=_cparams(("parallel", "parallel")),
    )(xs, mod)


def _out_kernel(m_ref, x_ref, mod_ref, nxt_ref, w_ref, g_ref, b_ref, o_ref, *u_ref,
                tm, row0, n_batch, ctx_len):
    b = pl.program_id(0)
    d = D_MODEL
    row = row0 + pl.program_id(1) * tm + lax.broadcasted_iota(jnp.int32, (tm, 1), 0)
    is_ctx = row < ctx_len
    gate = _modulation_rows(mod_ref, b, n_batch, is_ctx, 2 * d, 3 * d)
    y = _dot(m_ref[0], w_ref[...])
    z = DEEPNORM_ALPHA * x_ref[0] + gate * y
    zc = z - jnp.mean(z, axis=-1, keepdims=True)
    var = jnp.mean(zc * zc, axis=-1, keepdims=True)
    x_new = zc * lax.rsqrt(var + NORM_EPS) * g_ref[...] + b_ref[...]
    o_ref[0] = x_new
    if u_ref:
        u_ref[0][0] = _modulate(x_new, nxt_ref, b, n_batch, is_ctx)


def _out_call(merged, xs, mod, w_out, ln_g, ln_b, *, layer, last, n_batch, ctx_len, tm):
    bsz, t, d = xs.shape
    depth = mod.shape[0]
    blk0 = ctx_len // tm if last else 0
    assert not last or ctx_len % tm == 0
    rows = t - blk0 * tm
    kern = functools.partial(_out_kernel, tm=tm, row0=blk0 * tm, n_batch=n_batch,
                             ctx_len=ctx_len)
    tok_in = pl.BlockSpec((1, tm, d), lambda b, i: (b, i + blk0, 0))
    tok_out = pl.BlockSpec((1, tm, d), lambda b, i: (b, i, 0))
    mod_at = lambda l: pl.BlockSpec((None, SUBLANES, 3 * d), lambda b, i: (l, 0, 0))
    vec = pl.BlockSpec((None, 1, d), lambda b, i: (layer, 0, 0))
    x_shape = jax.ShapeDtypeStruct((bsz, rows, d), F32)
    return pl.pallas_call(
        kern,
        grid=(bsz, rows // tm),
        in_specs=[tok_in, tok_in, mod_at(layer), mod_at(min(layer + 1, depth - 1)),
                  pl.BlockSpec((None, d, d), lambda b, i: (layer, 0, 0),
                               pipeline_mode=pl.Buffered(1)),
                  vec, vec],
        out_specs=tok_out if last else [tok_out, tok_out],
        out_shape=x_shape if last else [x_shape, jax.ShapeDtypeStruct((bsz, rows, d), BF16)],
        compiler_params=_cparams(("parallel", "parallel")),
    )(merged, xs, mod, mod, w_out, ln_g, ln_b)


def _pack_w_in(w_in):
    depth, d, _ = w_in.shape
    seg = lambda name: w_in[:, :, _REF_SPLITS[name][0]:_REF_SPLITS[name][0] + _REF_SPLITS[name][1]]
    cols = [seg(n) for n in ("aq",) + ZB_ORDER + ZF_ORDER]
    cols.append(jnp.zeros((depth, d, ZF_W - ZF_USED), w_in.dtype))
    return jnp.concatenate(cols, axis=-1).astype(BF16)


def _rope_tables(rows, ctx_len):
    row = jnp.repeat(jnp.arange(rows, dtype=F32), GRID_W)
    col = jnp.tile(jnp.arange(GRID_W, dtype=F32), rows)
    n_freq = HEAD_DIM // 4
    inv = ROPE_BASE ** (-jnp.arange(n_freq, dtype=F32) / n_freq)
    ang_r = row[:, None] * inv
    ang_c = col[:, None] * inv
    ang = jnp.concatenate([ang_r, ang_r, ang_c, ang_c], -1)
    cos = jnp.concatenate([jnp.ones((ctx_len, HEAD_DIM), F32), jnp.cos(ang)], 0)
    sin = jnp.concatenate([jnp.zeros((ctx_len, HEAD_DIM), F32), jnp.sin(ang)], 0)
    first = (np.arange(HEAD_DIM) % (HEAD_DIM // 2)) < HEAD_DIM // 4
    sin_lo = jnp.where(first, -sin, 0.0)
    sin_hi = jnp.where(first, 0.0, sin)
    return cos, sin_lo, sin_hi


def _pad_lanes(v, width):
    return jnp.pad(v.reshape(1, -1), ((0, 0), (0, width - v.size)))


def _tile_choices(t, ctx_len):
    def largest(cap, mult, also=0):
        best = mult
        for cand in range(mult, cap + 1, mult):
            if t % cand == 0 and also % cand == 0:
                best = cand
        return best
    return dict(proj=largest(2112, 16), attn=largest(768, LANES), conv=largest(528, SUBLANES),
                merge=largest(256, 16), out=largest(528, 16),
                out_last=largest(528, 16, also=ctx_len))


def kernel(x, c, ctx, c_ctx, ada_w, ada_b, w_in, attn_q_norm, attn_k_norm, ret_log_decay, ret_gn_w,
           ssm_conv_w, ssm_conv_b, ssm_dt_bias, ssm_a_log, ssm_d, ssm_norm_w, w_branch, w_out,
           ln_g, ln_b):
    bsz, seq, d = x.shape
    ctx_len = ctx.shape[1]
    depth = w_in.shape[0]
    t = seq + ctx_len
    chunk = LANES
    ret_chunk = 256 if ctx_len % 256 == 0 and t % 256 == 0 else chunk
    assert d == D_MODEL and bsz < SUBLANES and seq % GRID_W == 0
    assert ctx_len % chunk == 0 and t % chunk == 0
    tiles = _tile_choices(t, ctx_len)

    xs = jnp.concatenate([ctx, x], axis=1)
    cvec = jnp.zeros((SUBLANES, d), F32).at[:bsz].set(c).at[bsz].set(c_ctx)
    mod = _ada_call(cvec, ada_w, ada_b)
    cos, sin_lo, sin_hi = _rope_tables(seq // GRID_W, ctx_len)
    w_packed = _pack_w_in(w_in.astype(BF16))
    wb = w_branch.astype(BF16)
    wo = w_out.astype(BF16)
    conv_w = jnp.pad(ssm_conv_w, ((0, 0), (0, SUBLANES - SSM_CONV), (0, 0)))
    per_layer_row = lambda v: v.reshape(depth, 1, -1)
    qn, kn = per_layer_row(attn_q_norm), per_layer_row(attn_k_norm)
    gnw, snw = per_layer_row(ret_gn_w), per_layer_row(ssm_norm_w)
    lng, lnb = per_layer_row(ln_g), per_layer_row(ln_b)

    u = _modulate_call(xs, mod, layer=0, n_batch=bsz, ctx_len=ctx_len, tm=tiles["out"])
    for l in range(depth):
        zq, zb, zf = _inproj_call(u, w_packed, cos, sin_lo, sin_hi, qn, kn, layer=l,
                                  tm=tiles["proj"])
        attn = _attn_call(zq, zb, ctx_len=ctx_len, tq=tiles["attn"], tk=tiles["attn"])
        retf, retb = _ret_call(zb, ret_log_decay[l], ctx_len=ctx_len, chunk=ret_chunk)
        xbc = _conv_call(zf, conv_w[l], ssm_conv_b[l].reshape(1, -1), ctx_len=ctx_len,
                         tc=tiles["conv"])
        ssmf, ssmb = _ssd_call(xbc, zf, _pad_lanes(ssm_dt_bias[l], LANES),
                               _pad_lanes(ssm_a_log[l], LANES),
                               jnp.repeat(ssm_d[l], SSM_HEAD_DIM).reshape(1, -1),
                               ctx_len=ctx_len, chunk=chunk)
        merged = _merge_call(attn, retf, retb, ssmf, ssmb, zb, gnw, snw, wb, layer=l,
                             tm=tiles["merge"])
        last = l == depth - 1
        out = _out_call(merged, xs, mod, wo, lng, lnb, layer=l, last=last, n_batch=bsz,
                        ctx_len=ctx_len, tm=tiles["out_last" if last else "out"])
        if last:
            return out
        xs, u = out
```

```python
import functools

import numpy as np
import jax
import jax.numpy as jnp
from jax import lax
from jax.experimental import pallas as pl
from jax.experimental.pallas import tpu as pltpu

F32 = jnp.float32
BF16 = jnp.bfloat16

D_MODEL = 2048
GRID_W = 64
HEAD_DIM = 128
ATTN_HEADS = 8
ATTN_KV_HEADS = 2
ATTN_GROUP = ATTN_HEADS // ATTN_KV_HEADS
ROPE_BASE = 10000.0
RET_HEADS = 4
RET_QK_DIM = 128
RET_V_DIM = 256
SSM_INNER = 1024
SSM_HEAD_DIM = 64
SSM_HEADS = 16
SSM_GROUPS = 2
SSM_HPG = SSM_HEADS // SSM_GROUPS
SSM_STATE = 128
SSM_CONV = 5
N_BRANCHES = 3
BRANCH_WIDTH = 1024
DEPTH = 4
DEEPNORM_ALPHA = (2 * DEPTH) ** 0.25
NORM_EPS = 1e-6
ATTN_Q_SCALE = HEAD_DIM ** -0.5 * float(np.log2(np.e))

_REF_SPLITS = dict(aq=(0, 1024), ak=(1024, 256), av=(1280, 256), ag=(1536, 1024),
                   rq=(2560, 512), rk=(3072, 512), rv=(3584, 1024), rg=(4608, 1024),
                   sx=(5632, 1024), sb=(6656, 256), sc=(6912, 256), sdt=(7168, 32),
                   sz=(7200, 1024), mg=(8224, 6144))

LANES = 128
SUBLANES = 8
VMEM_LIMIT = 56 * 1024 * 1024

PROJ_TN = 512
ZQ_TILES = ATTN_HEADS * HEAD_DIM // PROJ_TN
ZB_ORDER = ("mg", "ag", "rg", "sz", "rv", "ak", "av", "rq", "rk")
ZB_MG_OFF, ZB_AG_OFF, ZB_RG_OFF, ZB_SZ_OFF = 0, 6144, 7168, 8192
ZB_RV_OFF, ZB_AK_OFF, ZB_AV_OFF, ZB_RQ_OFF, ZB_RK_OFF = 9216, 10240, 10496, 10752, 11264
ZB_W = 11776
ZB_TILES = ZB_W // PROJ_TN
ZB_RAW_TILES = ZB_AK_OFF // PROJ_TN
ZF_TILE0 = ZQ_TILES + ZB_TILES
ZF_ORDER = ("sx", "sb", "sc", "sdt")
ZF_USED = 1024 + 256 + 256 + 32
ZF_W = 2048
ZF_XBC_OFF = 0
ZF_XBC_W = 1536
ZF_DT_OFF = 1536
W_PACKED = ZQ_TILES * PROJ_TN + ZB_W + ZF_W


def _sigmoid(x):
    return 1.0 / (1.0 + jnp.exp(-x))


def _silu(x):
    return x * _sigmoid(x)


def _cparams(sem):
    return pltpu.CompilerParams(dimension_semantics=sem, vmem_limit_bytes=VMEM_LIMIT)


def _dot(a, b):
    return jnp.dot(a, b, preferred_element_type=F32)


def _dot_nt(a, b):
    return lax.dot_general(a, b, (((1,), (1,)), ((), ())), preferred_element_type=F32)


def _dot_tn(a, b):
    return lax.dot_general(a, b, (((0,), (0,)), ((), ())), preferred_element_type=F32)


def _ada_kernel(c_ref, w_ref, b_ref, o_ref):
    s = _silu(c_ref[...]).astype(BF16)
    o_ref[0] = _dot(s, w_ref[0].astype(BF16)) + b_ref[0]


def _ada_call(cvec, ada_w, ada_b):
    depth, d, n = ada_w.shape
    tn = 768
    return pl.pallas_call(
        _ada_kernel,
        grid=(depth, n // tn),
        in_specs=[pl.BlockSpec((SUBLANES, d), lambda l, j: (0, 0)),
                  pl.BlockSpec((1, d, tn), lambda l, j: (l, 0, j)),
                  pl.BlockSpec((1, 1, tn), lambda l, j: (l, 0, j))],
        out_specs=pl.BlockSpec((1, SUBLANES, tn), lambda l, j: (l, 0, j)),
        out_shape=jax.ShapeDtypeStruct((depth, SUBLANES, n), F32),
        compiler_params=_cparams(("parallel", "parallel")),
    )(cvec, ada_w, ada_b.reshape(depth, 1, n))


def _rope(x, cos, sin_lo, sin_hi):
    return x * cos + pltpu.roll(x, 96, 1) * sin_lo + pltpu.roll(x, 32, 1) * sin_hi


def _modulation_rows(mod_ref, b, n_batch, is_ctx, lo, hi):
    mb = mod_ref[pl.ds(b, 1), lo:hi]
    mc = mod_ref[pl.ds(n_batch, 1), lo:hi]
    return jnp.where(is_ctx, mc, mb)


def _inproj_kernel(u_ref, w_ref, cos_ref, sl_ref, sh_ref, qn_ref, kn_ref,
                   zq_ref, zb_ref, zf_ref):
    j = pl.program_id(2)
    jb = j - ZQ_TILES

    def project():
        return _dot(u_ref[0], w_ref[...])

    def rope(xh):
        return _rope(xh, cos_ref[...], sl_ref[...], sh_ref[...])

    def normed(w_row):
        def fn(xh):
            ms = jnp.mean(xh * xh, axis=-1, keepdims=True)
            return rope(xh * lax.rsqrt(ms + NORM_EPS) * w_row)
        return fn

    @pl.when(j >= ZF_TILE0)
    def _():
        zf_ref[0] = project()

    @pl.when((jb >= 0) & (jb < ZB_RAW_TILES))
    def _():
        zb_ref[0] = project().astype(BF16)

    @pl.when((j < ZQ_TILES) | ((jb >= ZB_RAW_TILES) & (j < ZF_TILE0)))
    def _():
        acc = project()

        def head(h):
            return acc[:, h * LANES:(h + 1) * LANES]

        def per_head(fn, n):
            return jnp.concatenate([fn(head(h)) for h in range(n)], axis=1)

        @pl.when(j < ZQ_TILES)
        def _():
            fn = normed(qn_ref[...])
            for h in range(ATTN_GROUP):
                zq_ref[0, h] = (fn(head(h)) * ATTN_Q_SCALE).astype(BF16)

        @pl.when(jb == ZB_AK_OFF // PROJ_TN)
        def _():
            k = per_head(normed(kn_ref[...]), 2)
            zb_ref[0] = jnp.concatenate([k, acc[:, 2 * LANES:]], axis=1).astype(BF16)

        @pl.when(jb == ZB_RQ_OFF // PROJ_TN)
        def _():
            zb_ref[0] = per_head(rope, 4).astype(BF16)

        @pl.when(jb == ZB_RK_OFF // PROJ_TN)
        def _():
            zb_ref[0] = per_head(lambda xh: rope(xh * (RET_QK_DIM ** -0.5)), 4).astype(BF16)


def _inproj_call(u, w, cos, sin_lo, sin_hi, qn, kn, *, layer, tm):
    bsz, t, d = u.shape
    rope_spec = pl.BlockSpec((tm, LANES), lambda b, i, j: (i, 0))
    vec_spec = pl.BlockSpec((None, 1, LANES), lambda b, i, j: (layer, 0, 0))
    return pl.pallas_call(
        _inproj_kernel,
        grid=(bsz, t // tm, W_PACKED // PROJ_TN),
        in_specs=[pl.BlockSpec((1, tm, d), lambda b, i, j: (b, i, 0)),
                  pl.BlockSpec((None, d, PROJ_TN), lambda b, i, j: (layer, 0, j)),
                  rope_spec, rope_spec, rope_spec, vec_spec, vec_spec],
        out_specs=[pl.BlockSpec((1, ATTN_GROUP, tm, HEAD_DIM),
                                lambda b, i, j: (b, jnp.minimum(j, ZQ_TILES - 1), i, 0)),
                   pl.BlockSpec((1, tm, PROJ_TN),
                                lambda b, i, j: (b, i, jnp.clip(j - ZQ_TILES, 0, ZB_TILES - 1))),
                   pl.BlockSpec((1, tm, PROJ_TN),
                                lambda b, i, j: (b, i, jnp.maximum(j - ZF_TILE0, 0)))],
        out_shape=[jax.ShapeDtypeStruct((bsz, ATTN_HEADS, t, HEAD_DIM), BF16),
                   jax.ShapeDtypeStruct((bsz, t, ZB_W), BF16),
                   jax.ShapeDtypeStruct((bsz, t, ZF_W), F32)],
        compiler_params=_cparams(("parallel", "parallel", "arbitrary")),
    )(u, w, cos, sin_lo, sin_hi, qn, kn)


NEG_BIG = -1e30
AHEAD = 2
ONES_ROWS = 2 * SUBLANES


def _attn_kernel(q_ref, k_ref, v_ref, o_ref, m_ref, acc_ref, s_ref, *,
                 tq, tk, rb, t_len, ctx_len, unroll):
    qi = pl.program_id(2)
    nr = tq // rb
    per_kb = ATTN_GROUP * nr
    n_units = (t_len // tk) * per_kb

    m_ref[...] = jnp.full(m_ref.shape, NEG_BIG, F32)
    acc_ref[...] = jnp.zeros(acc_ref.shape, F32)

    def unit(n):
        kb = n // per_kb
        return kb, n - kb * per_kb

    def scores(n):
        kb, slot = unit(n)
        h = slot // nr
        r0 = pl.multiple_of((slot - h * nr) * rb, rb)
        k0 = pl.multiple_of(kb * tk, tk)
        return _dot_nt(k_ref[0, pl.ds(k0, tk), :], q_ref[0, h, pl.ds(r0, rb), :])

    def update(n, s, mixed):
        kb, slot = unit(n)
        if mixed:
            r0 = (slot - (slot // nr) * nr) * rb
            qrow = r0 + lax.broadcasted_iota(jnp.int32, (1, rb), 1)
            key = kb * tk + lax.broadcasted_iota(jnp.int32, (tk, 1), 0)
            s = jnp.where(qrow < ctx_len, jnp.where(key < ctx_len, s, NEG_BIG), s)
        m_prev = m_ref[slot]
        m_new = jnp.maximum(m_prev, jnp.max(s, axis=0, keepdims=True))
        alpha = jnp.exp2(m_prev - m_new)
        p = jnp.exp2((s - m_new).astype(BF16))
        m_ref[slot] = m_new
        acc_ref[slot] = alpha * acc_ref[slot] + _dot(v_ref[0, 0, kb], p)

    def run(mixed):
        for a in range(AHEAD):
            s_ref[a] = scores(a)

        def body(it, carry):
            n0 = it * unroll
            s = [s_ref[a] for a in range(AHEAD)]
            for u in range(0, unroll, AHEAD):
                s_next = [scores(jnp.minimum(n0 + u + AHEAD + a, n_units - 1))
                          for a in range(AHEAD)]
                for a in range(AHEAD):
                    update(n0 + u + a, s[a], mixed)
                s = s_next
            for a in range(AHEAD):
                s_ref[a] = s[a]
            return carry

        lax.fori_loop(0, n_units // unroll, body, 0)

    pl.when(qi == 0)(lambda: run(True))
    pl.when(qi > 0)(lambda: run(False))

    for h in range(ATTN_GROUP):
        for r in range(nr):
            slot = h * nr + r
            o_t = acc_ref[slot, :HEAD_DIM] * (1.0 / acc_ref[slot, HEAD_DIM:HEAD_DIM + 1])
            o_ref[0, r * rb:(r + 1) * rb, h * HEAD_DIM:(h + 1) * HEAD_DIM] = o_t.T


def _attn_call(zq, zb, *, ctx_len, tq, tk):
    bsz, t, _ = zb.shape
    assert ctx_len <= tq and ctx_len <= tk
    rb = 256 if tq % 256 == 0 else tq
    per_kb = ATTN_GROUP * (tq // rb)
    unroll = max(u for u in range(AHEAD, 13, AHEAD) if per_kb % u == 0)
    k_blk = ZB_AK_OFF // HEAD_DIM
    v_t = zb[:, :, ZB_AV_OFF:ZB_AV_OFF + ATTN_KV_HEADS * HEAD_DIM]
    v_t = v_t.reshape(bsz, t // tk, tk, ATTN_KV_HEADS, HEAD_DIM).transpose(0, 3, 1, 4, 2)
    v_t = jnp.concatenate([v_t, jnp.ones(v_t.shape[:3] + (ONES_ROWS, tk), v_t.dtype)], axis=3)
    kern = functools.partial(_attn_kernel, tq=tq, tk=tk, rb=rb, t_len=t, ctx_len=ctx_len,
                             unroll=unroll)
    return pl.pallas_call(
        kern,
        grid=(bsz, ATTN_KV_HEADS, t // tq),
        in_specs=[pl.BlockSpec((1, ATTN_GROUP, tq, HEAD_DIM), lambda b, g, qi: (b, g, qi, 0)),
                  pl.BlockSpec((1, t, HEAD_DIM), lambda b, g, qi: (b, 0, k_blk + g)),
                  pl.BlockSpec((1, 1, t // tk, HEAD_DIM + ONES_ROWS, tk),
                               lambda b, g, qi: (b, g, 0, 0, 0))],
        out_specs=pl.BlockSpec((1, tq, ATTN_GROUP * HEAD_DIM), lambda b, g, qi: (b, qi, g)),
        out_shape=jax.ShapeDtypeStruct((bsz, t, ATTN_HEADS * HEAD_DIM), F32),
        scratch_shapes=[pltpu.VMEM((per_kb, 1, rb), F32),
                        pltpu.VMEM((per_kb, HEAD_DIM + ONES_ROWS, rb), F32),
                        pltpu.VMEM((AHEAD, tk, rb), F32)],
        compiler_params=_cparams(("parallel", "parallel", "parallel")),
    )(zq, zb, v_t)


def _bwd_chunk(p, n_ctx_chunks, n_chunks):
    return jnp.where(p < n_ctx_chunks, n_ctx_chunks - 1 - p, n_chunks + n_ctx_chunks - 1 - p)


def _ret_kernel(ld_ref, qf_ref, kf_ref, vf_ref, qb_ref, kb_ref, vb_ref, yf_ref, yb_ref,
                sf_ref, sb_ref, *, chunk):
    c = chunk
    p = pl.program_id(1)

    @pl.when(p == 0)
    def _():
        sf_ref[...] = jnp.zeros(sf_ref.shape, F32)
        sb_ref[...] = jnp.zeros(sb_ref.shape, F32)

    ii = lax.broadcasted_iota(jnp.int32, (c, c), 0)
    jj = lax.broadcasted_iota(jnp.int32, (c, c), 1)
    dist = (ii - jj).astype(F32)
    pos = lax.broadcasted_iota(jnp.int32, (c, 1), 0).astype(F32)
    full = jnp.full((1, 1), float(c), F32)

    for h in range(RET_HEADS):
        ldf = ld_ref[0, h]
        ldb = ld_ref[1, h]
        qs = slice(h * RET_QK_DIM, (h + 1) * RET_QK_DIM)
        vs = slice(h * RET_V_DIM, (h + 1) * RET_V_DIM)
        decay = jnp.where(dist >= 0, jnp.exp(ldf * dist), 0.0) \
            + jnp.where(dist <= 0, jnp.exp(-ldb * dist), 0.0)
        q = qf_ref[0, :, qs]
        k = kf_ref[0, :, qs]
        v = vf_ref[0, :, vs]
        y = _dot((_dot_nt(q, k) * decay).astype(BF16), v)
        y = y + _dot(q, sf_ref[h].astype(BF16)) * jnp.exp(ldf * (pos + 1.0))
        yf_ref[0, :, vs] = y
        wv = (v.astype(F32) * jnp.exp(ldf * (c - 1.0 - pos))).astype(BF16)
        sf_ref[h] = sf_ref[h] * jnp.exp(ldf * full) + _dot_tn(k, wv)
        q = qb_ref[0, :, qs]
        k = kb_ref[0, :, qs]
        v = vb_ref[0, :, vs]
        yb_ref[0, :, vs] = _dot(q, sb_ref[h].astype(BF16)) * jnp.exp(ldb * (c - pos))
        wv = (v.astype(F32) * jnp.exp(ldb * pos)).astype(BF16)
        sb_ref[h] = sb_ref[h] * jnp.exp(ldb * full) + _dot_tn(k, wv)


def _ret_call(zb, log_decay, *, ctx_len, chunk):
    bsz, t, _ = zb.shape
    nc = t // chunk
    ncc = ctx_len // chunk
    qw = RET_HEADS * RET_QK_DIM
    vw = RET_HEADS * RET_V_DIM
    q_blk, k_blk, v_blk = ZB_RQ_OFF // qw, ZB_RK_OFF // qw, ZB_RV_OFF // vw
    bwd = functools.partial(_bwd_chunk, n_ctx_chunks=ncc, n_chunks=nc)
    kern = functools.partial(_ret_kernel, chunk=chunk)
    out = jax.ShapeDtypeStruct((bsz, t, vw), F32)
    return pl.pallas_call(
        kern,
        grid=(bsz, nc),
        in_specs=[pl.BlockSpec(memory_space=pltpu.SMEM),
                  pl.BlockSpec((1, chunk, qw), lambda b, p: (b, p, q_blk)),
                  pl.BlockSpec((1, chunk, qw), lambda b, p: (b, p, k_blk)),
                  pl.BlockSpec((1, chunk, vw), lambda b, p: (b, p, v_blk)),
                  pl.BlockSpec((1, chunk, qw), lambda b, p: (b, bwd(p), q_blk)),
                  pl.BlockSpec((1, chunk, qw), lambda b, p: (b, bwd(p), k_blk)),
                  pl.BlockSpec((1, chunk, vw), lambda b, p: (b, bwd(p), v_blk))],
        out_specs=[pl.BlockSpec((1, chunk, vw), lambda b, p: (b, p, 0)),
                   pl.BlockSpec((1, chunk, vw), lambda b, p: (b, bwd(p), 0))],
        out_shape=[out, out],
        scratch_shapes=[pltpu.VMEM((RET_HEADS, RET_QK_DIM, RET_V_DIM), F32),
                        pltpu.VMEM((RET_HEADS, RET_QK_DIM, RET_V_DIM), F32)],
        compiler_params=_cparams(("parallel", "arbitrary")),
    )(log_decay, zb, zb, zb, zb, zb, zb)


CONV_HALO = SUBLANES


def _conv_kernel(prev_ref, cur_ref, next_ref, w_ref, b_ref, o_ref, e_ref, *, tc, t_len, ctx_len):
    i = pl.program_id(1)
    e_ref[0:CONV_HALO] = prev_ref[0]
    e_ref[CONV_HALO:CONV_HALO + tc] = cur_ref[0]
    e_ref[CONV_HALO + tc:] = next_ref[0]
    pad = (SSM_CONV - 1) // 2
    lo = i * tc

    def conv(masked):
        tok = lo + lax.broadcasted_iota(jnp.int32, (tc, 1), 0)
        acc = jnp.broadcast_to(b_ref[...], (tc, ZF_XBC_W))
        for tap in range(SSM_CONV):
            off = tap - pad
            src = e_ref[CONV_HALO + off:CONV_HALO + off + tc, :]
            if masked:
                nb = tok + off
                same = ((tok - ctx_len) ^ (nb - ctx_len)) >= 0
                src = jnp.where(same & (nb >= 0) & (nb < t_len), src, 0.0)
            acc = acc + src * w_ref[tap:tap + 1, :]
        o_ref[0] = _silu(acc)

    def near(edge):
        return (lo - pad < edge) & (edge < lo + tc + pad)
    touches = near(0) | near(ctx_len) | near(t_len)
    pl.when(touches)(lambda: conv(True))
    pl.when(jnp.logical_not(touches))(lambda: conv(False))


def _conv_call(zf, conv_w, conv_b, *, ctx_len, tc):
    bsz, t, _ = zf.shape
    col = ZF_XBC_OFF // ZF_XBC_W
    per = tc // CONV_HALO
    last = t // CONV_HALO - 1
    kern = functools.partial(_conv_kernel, tc=tc, t_len=t, ctx_len=ctx_len)
    return pl.pallas_call(
        kern,
        grid=(bsz, t // tc),
        in_specs=[pl.BlockSpec((1, CONV_HALO, ZF_XBC_W),
                               lambda b, i: (b, jnp.maximum(i * per - 1, 0), col)),
                  pl.BlockSpec((1, tc, ZF_XBC_W), lambda b, i: (b, i, col)),
                  pl.BlockSpec((1, CONV_HALO, ZF_XBC_W),
                               lambda b, i: (b, jnp.minimum((i + 1) * per, last), col)),
                  pl.BlockSpec((SUBLANES, ZF_XBC_W), lambda b, i: (0, 0)),
                  pl.BlockSpec((1, ZF_XBC_W), lambda b, i: (0, 0))],
        out_specs=pl.BlockSpec((1, tc, ZF_XBC_W), lambda b, i: (b, i, 0)),
        out_shape=jax.ShapeDtypeStruct((bsz, t, ZF_XBC_W), F32),
        scratch_shapes=[pltpu.VMEM((tc + 2 * CONV_HALO, ZF_XBC_W), F32)],
        compiler_params=_cparams(("parallel", "parallel")),
    )(zf, zf, zf, conv_w, conv_b)


def _split3(x):
    p1 = x.astype(BF16)
    r1 = x - p1.astype(F32)
    p2 = r1.astype(BF16)
    p3 = (r1 - p2.astype(F32)).astype(BF16)
    return p1, p2, p3


def _cumsum_rows(x, tri):
    p1, p2, p3 = _split3(x)
    return _dot(tri, p1) + _dot(tri, p2) + _dot(tri, p3)


def _expand_heads(a, sel_ref):
    hi = a.astype(BF16)
    lo = (a - hi.astype(F32)).astype(BF16)
    return _dot(jnp.concatenate([hi, lo], axis=1), sel_ref[...])


def _head_selector(off):
    sel = np.zeros((LANES, SSM_INNER), np.float32)
    for h in range(SSM_HEADS):
        sel[off + h, h * SSM_HEAD_DIM:(h + 1) * SSM_HEAD_DIM] = 1.0
    return jnp.asarray(np.concatenate([sel, sel], axis=0), BF16)


def _ssd_kernel(xf_ref, df_ref, xb_ref, db_ref, bias_ref, alog_ref, skip_ref, self_ref, selb_ref,
                yf_ref, yb_ref, sf_ref, sb_ref, *, chunk, sub):
    c = chunk
    p = pl.program_id(1)
    nh = SSM_HEADS
    gw = SSM_HPG * SSM_HEAD_DIM
    bc_w = SSM_GROUPS * SSM_STATE

    @pl.when(p == 0)
    def _():
        sf_ref[...] = jnp.zeros(sf_ref.shape, F32)
        sb_ref[...] = jnp.zeros(sb_ref.shape, F32)

    ii = lax.broadcasted_iota(jnp.int32, (c, c), 0)
    jj = lax.broadcasted_iota(jnp.int32, (c, c), 1)
    lower = ii >= jj
    upper = jj >= ii
    tri = jnp.where(lower, 1.0, 0.0).astype(BF16)
    lane_lo = lax.broadcasted_iota(jnp.int32, (c, LANES), 1) < SSM_HEAD_DIM
    lane_hi = jnp.logical_not(lane_lo)

    def operands(x_ref, rows):
        xs = x_ref[0, rows, 0:SSM_INNER]
        bm = x_ref[0, rows, SSM_INNER:SSM_INNER + bc_w].astype(BF16)
        cm = x_ref[0, rows, SSM_INNER + bc_w:].astype(BF16)
        return xs, bm, cm

    def gates(d_ref, rows):
        raw = d_ref[0, rows, :] + bias_ref[...]
        dt = jnp.maximum(raw, 0.0) + jnp.log1p(jnp.exp(-jnp.abs(raw)))
        la = dt * -jnp.exp(alog_ref[...])
        acum = _cumsum_rows(la, tri)
        return dt, acum, acum - la

    def within(xs, bm, cm, dt, acum, excl):
        dt_t = dt.T
        acum_t = acum.T
        excl_t = excl.T
        xs_b = xs.astype(BF16)
        pieces = []
        for g in range(SSM_GROUPS):
            gs = slice(g * SSM_STATE, (g + 1) * SSM_STATE)
            scores = _dot_nt(cm[:, gs], bm[:, gs])
            for m in range(SSM_HPG // 2):
                pair = g * SSM_HPG // 2 + m
                xp = xs_b[:, pair * LANES:(pair + 1) * LANES]
                acc = None
                for e in range(2):
                    hf = 2 * pair + e
                    hb = nh + hf
                    dec_f = jnp.where(lower, jnp.exp(acum[:, hf:hf + 1] - acum_t[hf:hf + 1, :]), 0.0)
                    dec_b = jnp.where(upper, jnp.exp(excl_t[hb:hb + 1, :] - excl[:, hb:hb + 1]), 0.0)
                    wm = scores * (dec_f * dt_t[hf:hf + 1, :] + dec_b * dt_t[hb:hb + 1, :])
                    xh = jnp.where(lane_lo if e == 0 else lane_hi, xp, 0.0)
                    part = _dot(wm.astype(BF16), xh)
                    acc = part if acc is None else acc + part
                pieces.append(acc)
        return jnp.concatenate(pieces, axis=1)

    def carried(s_ref, xs, bm, cm, carry, wdt, keep):
        xw = (xs * wdt).astype(BF16)
        inter = []
        for g in range(SSM_GROUPS):
            gs = slice(g * SSM_STATE, (g + 1) * SSM_STATE)
            cs = slice(g * gw, (g + 1) * gw)
            inter.append(_dot(cm[:, gs], s_ref[g].astype(BF16)))
            s_ref[g] = s_ref[g] * keep[:, cs] + _dot_tn(bm[:, gs], xw[:, cs])
        return jnp.concatenate(inter, axis=1) * carry

    rows = [pl.ds(i * c, c) for i in range(sub)]
    fwd = [operands(xf_ref, r) + gates(df_ref, r) for r in rows]
    bwd = [operands(xb_ref, r) + gates(db_ref, r) for r in rows]

    fwd_scale = []
    for xs, bm, cm, dt, acum, excl in fwd:
        last = acum[c - 1:c, :]
        both = _expand_heads(
            jnp.concatenate([jnp.exp(acum), jnp.exp(last - acum) * dt], axis=0), self_ref)
        fwd_scale.append((both[:c], both[c:], both[c - 1:c]))
    bwd_scale = []
    for xs, bm, cm, dt, acum, excl in bwd:
        total = acum[c - 1:c, :]
        both = _expand_heads(
            jnp.concatenate([jnp.exp(total - excl), jnp.exp(excl) * dt], axis=0), selb_ref)
        bwd_scale.append((both[:c], both[c:], both[0:1]))

    y_within = [within(*f) for f in fwd]

    for i in range(sub):
        xs, bm, cm = fwd[i][:3]
        y = y_within[i] + carried(sf_ref, xs, bm, cm, *fwd_scale[i]) + skip_ref[...] * xs
        yf_ref[0, rows[i], :] = y
    for i in reversed(range(sub)):
        xs, bm, cm = bwd[i][:3]
        yb_ref[0, rows[i], :] = carried(sb_ref, xs, bm, cm, *bwd_scale[i])


def _ssd_call(xbc, zf, bias_row, alog_row, skip_row, *, ctx_len, chunk):
    bsz, t, _ = xbc.shape
    sub = 2 if ctx_len % (2 * chunk) == 0 and t % (2 * chunk) == 0 else 1
    blk = sub * chunk
    nblk = t // blk
    dt_blk = ZF_DT_OFF // LANES
    bwd = functools.partial(_bwd_chunk, n_ctx_chunks=ctx_len // blk, n_chunks=nblk)
    kern = functools.partial(_ssd_kernel, chunk=chunk, sub=sub)
    out = jax.ShapeDtypeStruct((bsz, t, SSM_INNER), F32)
    row = lambda w: pl.BlockSpec((1, w), lambda b, p: (0, 0))
    return pl.pallas_call(
        kern,
        grid=(bsz, nblk),
        in_specs=[pl.BlockSpec((1, blk, ZF_XBC_W), lambda b, p: (b, p, 0)),
                  pl.BlockSpec((1, blk, LANES), lambda b, p: (b, p, dt_blk)),
                  pl.BlockSpec((1, blk, ZF_XBC_W), lambda b, p: (b, bwd(p), 0)),
                  pl.BlockSpec((1, blk, LANES), lambda b, p: (b, bwd(p), dt_blk)),
                  row(LANES), row(LANES), row(SSM_INNER),
                  pl.BlockSpec((2 * LANES, SSM_INNER), lambda b, p: (0, 0)),
                  pl.BlockSpec((2 * LANES, SSM_INNER), lambda b, p: (0, 0))],
        out_specs=[pl.BlockSpec((1, blk, SSM_INNER), lambda b, p: (b, p, 0)),
                   pl.BlockSpec((1, blk, SSM_INNER), lambda b, p: (b, bwd(p), 0))],
        out_shape=[out, out],
        scratch_shapes=[pltpu.VMEM((SSM_GROUPS, SSM_STATE, SSM_HPG * SSM_HEAD_DIM), F32),
                        pltpu.VMEM((SSM_GROUPS, SSM_STATE, SSM_HPG * SSM_HEAD_DIM), F32)],
        compiler_params=_cparams(("parallel", "arbitrary")),
    )(xbc, zf, xbc, zf, bias_row, alog_row, skip_row,
      _head_selector(0), _head_selector(SSM_HEADS))


def _merge_kernel(attn_ref, retf_ref, retb_ref, ssmf_ref, ssmb_ref, mg_ref, ag_ref, rg_ref,
                  sz_ref, gnw_ref, snw_ref, wb_ref, o_ref):
    d = D_MODEL
    gate_in = lambda ref, lo=0, hi=None: ref[0, :, lo:hi].astype(F32)
    a = attn_ref[0] * _silu(gate_in(ag_ref))

    ret = retf_ref[0] + retb_ref[0]
    parts = []
    for h in range(RET_HEADS):
        r = ret[:, h * RET_V_DIM:(h + 1) * RET_V_DIM]
        rc = r - jnp.mean(r, axis=-1, keepdims=True)
        var = jnp.mean(rc * rc, axis=-1, keepdims=True)
        parts.append(rc * lax.rsqrt(var + NORM_EPS))
    r = jnp.concatenate(parts, axis=1) * gnw_ref[...] * _silu(gate_in(rg_ref))

    s = (ssmf_ref[0] + ssmb_ref[0]) * _silu(gate_in(sz_ref))
    gwid = SSM_INNER // SSM_GROUPS
    parts = []
    for g in range(SSM_GROUPS):
        sg = s[:, g * gwid:(g + 1) * gwid]
        parts.append(sg * lax.rsqrt(jnp.mean(sg * sg, axis=-1, keepdims=True) + NORM_EPS))
    s = jnp.concatenate(parts, axis=1) * snw_ref[...]

    merged = _sigmoid(gate_in(mg_ref, 0, d)) * _dot(a.astype(BF16), wb_ref[0])
    merged = merged + _sigmoid(gate_in(mg_ref, d, 2 * d)) * _dot(r.astype(BF16), wb_ref[1])
    merged = merged + _sigmoid(gate_in(mg_ref, 2 * d, 3 * d)) * _dot(s.astype(BF16), wb_ref[2])
    o_ref[0] = merged.astype(BF16)


def _merge_call(attn, retf, retb, ssmf, ssmb, zb, gnw, snw, wb, *, layer, tm):
    bsz, t, _ = attn.shape
    d = D_MODEL
    bw = BRANCH_WIDTH
    act = lambda blk: pl.BlockSpec((1, tm, bw), lambda b, i: (b, i, blk))
    vec = pl.BlockSpec((None, 1, bw), lambda b, i: (layer, 0, 0))
    return pl.pallas_call(
        _merge_kernel,
        grid=(bsz, t // tm),
        in_specs=[act(0), act(0), act(0), act(0), act(0),
                  pl.BlockSpec((1, tm, N_BRANCHES * d),
                               lambda b, i: (b, i, ZB_MG_OFF // (N_BRANCHES * d))),
                  act(ZB_AG_OFF // bw), act(ZB_RG_OFF // bw), act(ZB_SZ_OFF // bw),
                  vec, vec,
                  pl.BlockSpec((None, N_BRANCHES, bw, d), lambda b, i: (layer, 0, 0, 0),
                               pipeline_mode=pl.Buffered(1))],
        out_specs=pl.BlockSpec((1, tm, d), lambda b, i: (b, i, 0)),
        out_shape=jax.ShapeDtypeStruct((bsz, t, d), BF16),
        compiler_params=_cparams(("parallel", "parallel")),
    )(attn, retf, retb, ssmf, ssmb, zb, zb, zb, zb, gnw, snw, wb)


def _modulate(x, mod_ref, b, n_batch, is_ctx):
    d = D_MODEL
    shift = _modulation_rows(mod_ref, b, n_batch, is_ctx, 0, d)
    scale = _modulation_rows(mod_ref, b, n_batch, is_ctx, d, 2 * d)
    return (x * (1.0 + scale) + shift).astype(BF16)


def _modulate_kernel(x_ref, mod_ref, u_ref, *, tm, n_batch, ctx_len):
    row = pl.program_id(1) * tm + lax.broadcasted_iota(jnp.int32, (tm, 1), 0)
    u_ref[0] = _modulate(x_ref[0], mod_ref, pl.program_id(0), n_batch, row < ctx_len)


def _modulate_call(xs, mod, *, layer, n_batch, ctx_len, tm):
    bsz, t, d = xs.shape
    kern = functools.partial(_modulate_kernel, tm=tm, n_batch=n_batch, ctx_len=ctx_len)
    tok = pl.BlockSpec((1, tm, d), lambda b, i: (b, i, 0))
    return pl.pallas_call(
        kern,
        grid=(bsz, t // tm),
        in_specs=[tok, pl.BlockSpec((None, SUBLANES, 3 * d), lambda b, i: (layer, 0, 0))],
        out_specs=tok,
        out_shape=jax.ShapeDtypeStruct((bsz, t, d), BF16),
        compiler_params=_cparams(("parallel", "parallel")),
    )(xs, mod)


def _out_kernel(m_ref, x_ref, mod_ref, nxt_ref, w_ref, g_ref, b_ref, o_ref, *u_ref,
                tm, row0, n_batch, ctx_len):
    b = pl.program_id(0)
    d = D_MODEL
    row = row0 + pl.program_id(1) * tm + lax.broadcasted_iota(jnp.int32, (tm, 1), 0)
    is_ctx = row < ctx_len
    gate = _modulation_rows(mod_ref, b, n_batch, is_ctx, 2 * d, 3 * d)
    y = _dot(m_ref[0], w_ref[...])
    z = DEEPNORM_ALPHA * x_ref[0] + gate * y
    zc = z - jnp.mean(z, axis=-1, keepdims=True)
    var = jnp.mean(zc * zc, axis=-1, keepdims=True)
    x_new = zc * lax.rsqrt(var + NORM_EPS) * g_ref[...] + b_ref[...]
    o_ref[0] = x_new
    if u_ref:
        u_ref[0][0] = _modulate(x_new, nxt_ref, b, n_batch, is_ctx)


def _out_call(merged, xs, mod, w_out, ln_g, ln_b, *, layer, last, n_batch, ctx_len, tm):
    bsz, t, d = xs.shape
    depth = mod.shape[0]
    blk0 = ctx_len // tm if last else 0
    assert not last or ctx_len % tm == 0
    rows = t - blk0 * tm
    kern = functools.partial(_out_kernel, tm=tm, row0=blk0 * tm, n_batch=n_batch,
                             ctx_len=ctx_len)
    tok_in = pl.BlockSpec((1, tm, d), lambda b, i: (b, i + blk0, 0))
    tok_out = pl.BlockSpec((1, tm, d), lambda b, i: (b, i, 0))
    mod_at = lambda l: pl.BlockSpec((None, SUBLANES, 3 * d), lambda b, i: (l, 0, 0))
    vec = pl.BlockSpec((None, 1, d), lambda b, i: (layer, 0, 0))
    x_shape = jax.ShapeDtypeStruct((bsz, rows, d), F32)
    return pl.pallas_call(
        kern,
        grid=(bsz, rows // tm),
        in_specs=[tok_in, tok_in, mod_at(layer), mod_at(min(layer + 1, depth - 1)),
                  pl.BlockSpec((None, d, d), lambda b, i: (layer, 0, 0),
                               pipeline_mode=pl.Buffered(1)),
                  vec, vec],
        out_specs=tok_out if last else [tok_out, tok_out],
        out_shape=x_shape if last else [x_shape, jax.ShapeDtypeStruct((bsz, rows, d), BF16)],
        compiler_params=_cparams(("parallel", "parallel")),
    )(merged, xs, mod, mod, w_out, ln_g, ln_b)


PACK_ROWS = 128


def _pack_kernel(w_ref, o_ref):
    col = 0
    for name in ("aq",) + ZB_ORDER + ZF_ORDER:
        src, width = _REF_SPLITS[name]
        o_ref[0, :, col:col + width] = w_ref[0, :, src:src + width].astype(BF16)
        col += width
    o_ref[0, :, col:] = jnp.zeros((o_ref.shape[1], W_PACKED - col), BF16)


def _pack_w_in(w_in):
    depth, d, width = w_in.shape
    return pl.pallas_call(
        _pack_kernel,
        grid=(depth, d // PACK_ROWS),
        in_specs=[pl.BlockSpec((1, PACK_ROWS, width), lambda l, i: (l, i, 0))],
        out_specs=pl.BlockSpec((1, PACK_ROWS, W_PACKED), lambda l, i: (l, i, 0)),
        out_shape=jax.ShapeDtypeStruct((depth, d, W_PACKED), BF16),
        compiler_params=_cparams(("parallel", "parallel")),
    )(w_in)


def _rope_tables(rows, ctx_len):
    row = jnp.repeat(jnp.arange(rows, dtype=F32), GRID_W)
    col = jnp.tile(jnp.arange(GRID_W, dtype=F32), rows)
    n_freq = HEAD_DIM // 4
    inv = ROPE_BASE ** (-jnp.arange(n_freq, dtype=F32) / n_freq)
    ang_r = row[:, None] * inv
    ang_c = col[:, None] * inv
    ang = jnp.concatenate([ang_r, ang_r, ang_c, ang_c], -1)
    cos = jnp.concatenate([jnp.ones((ctx_len, HEAD_DIM), F32), jnp.cos(ang)], 0)
    sin = jnp.concatenate([jnp.zeros((ctx_len, HEAD_DIM), F32), jnp.sin(ang)], 0)
    first = (np.arange(HEAD_DIM) % (HEAD_DIM // 2)) < HEAD_DIM // 4
    sin_lo = jnp.where(first, -sin, 0.0)
    sin_hi = jnp.where(first, 0.0, sin)
    return cos, sin_lo, sin_hi


def _pad_lanes(v, width):
    return jnp.pad(v.reshape(1, -1), ((0, 0), (0, width - v.size)))


def _tile_choices(t, ctx_len):
    def largest(cap, mult, also=0):
        best = mult
        for cand in range(mult, cap + 1, mult):
            if t % cand == 0 and also % cand == 0:
                best = cand
        return best
    return dict(proj=largest(2112, 16), attn=largest(768, LANES), conv=largest(528, SUBLANES),
                merge=largest(256, 16), out=largest(528, 16),
                out_last=largest(528, 16, also=ctx_len))


def kernel(x, c, ctx, c_ctx, ada_w, ada_b, w_in, attn_q_norm, attn_k_norm, ret_log_decay, ret_gn_w,
           ssm_conv_w, ssm_conv_b, ssm_dt_bias, ssm_a_log, ssm_d, ssm_norm_w, w_branch, w_out,
           ln_g, ln_b):
    bsz, seq, d = x.shape
    ctx_len = ctx.shape[1]
    depth = w_in.shape[0]
    t = seq + ctx_len
    chunk = LANES
    ret_chunk = 256 if ctx_len % 256 == 0 and t % 256 == 0 else chunk
    assert d == D_MODEL and bsz < SUBLANES and seq % GRID_W == 0
    assert ctx_len % chunk == 0 and t % chunk == 0
    tiles = _tile_choices(t, ctx_len)

    xs = jnp.concatenate([ctx, x], axis=1)
    cvec = jnp.zeros((SUBLANES, d), F32).at[:bsz].set(c).at[bsz].set(c_ctx)
    mod = _ada_call(cvec, ada_w, ada_b)
    cos, sin_lo, sin_hi = _rope_tables(seq // GRID_W, ctx_len)
    w_packed = _pack_w_in(w_in)
    wb = w_branch.astype(BF16)
    wo = w_out.astype(BF16)
    conv_w = jnp.pad(ssm_conv_w, ((0, 0), (0, SUBLANES - SSM_CONV), (0, 0)))
    per_layer_row = lambda v: v.reshape(depth, 1, -1)
    qn, kn = per_layer_row(attn_q_norm), per_layer_row(attn_k_norm)
    gnw, snw = per_layer_row(ret_gn_w), per_layer_row(ssm_norm_w)
    lng, lnb = per_layer_row(ln_g), per_layer_row(ln_b)

    u = _modulate_call(xs, mod, layer=0, n_batch=bsz, ctx_len=ctx_len, tm=tiles["out"])
    for l in range(depth):
        zq, zb, zf = _inproj_call(u, w_packed, cos, sin_lo, sin_hi, qn, kn, layer=l,
                                  tm=tiles["proj"])
        attn = _attn_call(zq, zb, ctx_len=ctx_len, tq=tiles["attn"], tk=tiles["attn"])
        retf, retb = _ret_call(zb, ret_log_decay[l], ctx_len=ctx_len, chunk=ret_chunk)
        xbc = _conv_call(zf, conv_w[l], ssm_conv_b[l].reshape(1, -1), ctx_len=ctx_len,
                         tc=tiles["conv"])
        ssmf, ssmb = _ssd_call(xbc, zf, _pad_lanes(ssm_dt_bias[l], LANES),
                               _pad_lanes(ssm_a_log[l], LANES),
                               jnp.repeat(ssm_d[l], SSM_HEAD_DIM).reshape(1, -1),
                               ctx_len=ctx_len, chunk=chunk)
        merged = _merge_call(attn, retf, retb, ssmf, ssmb, zb, gnw, snw, wb, layer=l,
                             tm=tiles["merge"])
        last = l == depth - 1
        out = _out_call(merged, xs, mod, wo, lng, lnb, layer=l, last=last, n_batch=bsz,
                        ctx_len=ctx_len, tm=tiles["out_last" if last else "out"])
        if last:
            return out
        xs, u = out
```

```python
import functools

import numpy as np
import jax
import jax.numpy as jnp
from jax import lax
from jax.experimental import pallas as pl
from jax.experimental.pallas import tpu as pltpu

F32 = jnp.float32
BF16 = jnp.bfloat16

D_MODEL = 2048
GRID_W = 64
HEAD_DIM = 128
ATTN_HEADS = 8
ATTN_KV_HEADS = 2
ATTN_GROUP = ATTN_HEADS // ATTN_KV_HEADS
ROPE_BASE = 10000.0
RET_HEADS = 4
RET_QK_DIM = 128
RET_V_DIM = 256
SSM_INNER = 1024
SSM_HEAD_DIM = 64
SSM_HEADS = 16
SSM_GROUPS = 2
SSM_HPG = SSM_HEADS // SSM_GROUPS
SSM_STATE = 128
SSM_CONV = 5
N_BRANCHES = 3
BRANCH_WIDTH = 1024
DEPTH = 4
DEEPNORM_ALPHA = (2 * DEPTH) ** 0.25
NORM_EPS = 1e-6
ATTN_Q_SCALE = HEAD_DIM ** -0.5 * float(np.log2(np.e))

_REF_SPLITS = dict(aq=(0, 1024), ak=(1024, 256), av=(1280, 256), ag=(1536, 1024),
                   rq=(2560, 512), rk=(3072, 512), rv=(3584, 1024), rg=(4608, 1024),
                   sx=(5632, 1024), sb=(6656, 256), sc=(6912, 256), sdt=(7168, 32),
                   sz=(7200, 1024), mg=(8224, 6144))

LANES = 128
SUBLANES = 8
VMEM_LIMIT = 56 * 1024 * 1024

PROJ_TN = 512
ZQ_TILES = ATTN_HEADS * HEAD_DIM // PROJ_TN
ZB_ORDER = ("mg", "ag", "rg", "sz", "rv", "ak", "av", "rq", "rk")
ZB_MG_OFF, ZB_AG_OFF, ZB_RG_OFF, ZB_SZ_OFF = 0, 6144, 7168, 8192
ZB_RV_OFF, ZB_AK_OFF, ZB_AV_OFF, ZB_RQ_OFF, ZB_RK_OFF = 9216, 10240, 10496, 10752, 11264
ZB_W = 11776
ZB_TILES = ZB_W // PROJ_TN
ZB_RAW_TILES = ZB_AK_OFF // PROJ_TN
ZF_TILE0 = ZQ_TILES + ZB_TILES
ZF_ORDER = ("sx", "sb", "sc", "sdt")
ZF_USED = 1024 + 256 + 256 + 32
ZF_W = 2048
ZF_XBC_OFF = 0
ZF_XBC_W = 1536
ZF_DT_OFF = 1536
W_PACKED = ZQ_TILES * PROJ_TN + ZB_W + ZF_W


def _sigmoid(x):
    return 1.0 / (1.0 + jnp.exp(-x))


def _silu(x):
    return x * _sigmoid(x)


def _cparams(sem):
    return pltpu.CompilerParams(dimension_semantics=sem, vmem_limit_bytes=VMEM_LIMIT)


def _dot(a, b):
    return jnp.dot(a, b, preferred_element_type=F32)


def _dot_nt(a, b):
    return lax.dot_general(a, b, (((1,), (1,)), ((), ())), preferred_element_type=F32)


def _dot_tn(a, b):
    return lax.dot_general(a, b, (((0,), (0,)), ((), ())), preferred_element_type=F32)


def _ada_kernel(c_ref, w_ref, b_ref, o_ref):
    s = _silu(c_ref[...]).astype(BF16)
    o_ref[0] = _dot(s, w_ref[0].astype(BF16)) + b_ref[0]


def _ada_call(cvec, ada_w, ada_b):
    depth, d, n = ada_w.shape
    tn = 768
    return pl.pallas_call(
        _ada_kernel,
        grid=(depth, n // tn),
        in_specs=[pl.BlockSpec((SUBLANES, d), lambda l, j: (0, 0)),
                  pl.BlockSpec((1, d, tn), lambda l, j: (l, 0, j)),
                  pl.BlockSpec((1, 1, tn), lambda l, j: (l, 0, j))],
        out_specs=pl.BlockSpec((1, SUBLANES, tn), lambda l, j: (l, 0, j)),
        out_shape=jax.ShapeDtypeStruct((depth, SUBLANES, n), F32),
        compiler_params=_cparams(("parallel", "parallel")),
    )(cvec, ada_w, ada_b.reshape(depth, 1, n))


def _rope(x, cos, sin_lo, sin_hi):
    return x * cos + pltpu.roll(x, 96, 1) * sin_lo + pltpu.roll(x, 32, 1) * sin_hi


def _modulation_rows(mod_ref, b, n_batch, is_ctx, lo, hi):
    mb = mod_ref[pl.ds(b, 1), lo:hi]
    mc = mod_ref[pl.ds(n_batch, 1), lo:hi]
    return jnp.where(is_ctx, mc, mb)


def _inproj_kernel(u_ref, w_ref, cos_ref, sl_ref, sh_ref, qn_ref, kn_ref,
                   zq_ref, zb_ref, zf_ref):
    j = pl.program_id(2)
    jb = j - ZQ_TILES

    def project():
        return _dot_nt(u_ref[0], w_ref[...])

    def rope(xh):
        return _rope(xh, cos_ref[...], sl_ref[...], sh_ref[...])

    def normed(w_row):
        def fn(xh):
            ms = jnp.mean(xh * xh, axis=-1, keepdims=True)
            return rope(xh * lax.rsqrt(ms + NORM_EPS) * w_row)
        return fn

    @pl.when(j >= ZF_TILE0)
    def _():
        zf_ref[0] = project()

    @pl.when((jb >= 0) & (jb < ZB_RAW_TILES))
    def _():
        zb_ref[0] = project().astype(BF16)

    @pl.when((j < ZQ_TILES) | ((jb >= ZB_RAW_TILES) & (j < ZF_TILE0)))
    def _():
        acc = project()

        def head(h):
            return acc[:, h * LANES:(h + 1) * LANES]

        def per_head(fn, n):
            return jnp.concatenate([fn(head(h)) for h in range(n)], axis=1)

        @pl.when(j < ZQ_TILES)
        def _():
            fn = normed(qn_ref[...])
            for h in range(ATTN_GROUP):
                zq_ref[0, h] = (fn(head(h)) * ATTN_Q_SCALE).astype(BF16)

        @pl.when(jb == ZB_AK_OFF // PROJ_TN)
        def _():
            k = per_head(normed(kn_ref[...]), 2)
            zb_ref[0] = jnp.concatenate([k, acc[:, 2 * LANES:]], axis=1).astype(BF16)

        @pl.when(jb == ZB_RQ_OFF // PROJ_TN)
        def _():
            zb_ref[0] = per_head(rope, 4).astype(BF16)

        @pl.when(jb == ZB_RK_OFF // PROJ_TN)
        def _():
            zb_ref[0] = per_head(lambda xh: rope(xh * (RET_QK_DIM ** -0.5)), 4).astype(BF16)


def _inproj_call(u, w, cos, sin_lo, sin_hi, qn, kn, *, layer, tm):
    bsz, t, d = u.shape
    rope_spec = pl.BlockSpec((tm, LANES), lambda b, i, j: (i, 0))
    vec_spec = pl.BlockSpec((None, 1, LANES), lambda b, i, j: (layer, 0, 0))
    return pl.pallas_call(
        _inproj_kernel,
        grid=(bsz, t // tm, W_PACKED // PROJ_TN),
        in_specs=[pl.BlockSpec((1, tm, d), lambda b, i, j: (b, i, 0)),
                  pl.BlockSpec((None, PROJ_TN, d), lambda b, i, j: (layer, j, 0)),
                  rope_spec, rope_spec, rope_spec, vec_spec, vec_spec],
        out_specs=[pl.BlockSpec((1, ATTN_GROUP, tm, HEAD_DIM),
                                lambda b, i, j: (b, jnp.minimum(j, ZQ_TILES - 1), i, 0)),
                   pl.BlockSpec((1, tm, PROJ_TN),
                                lambda b, i, j: (b, i, jnp.clip(j - ZQ_TILES, 0, ZB_TILES - 1))),
                   pl.BlockSpec((1, tm, PROJ_TN),
                                lambda b, i, j: (b, i, jnp.maximum(j - ZF_TILE0, 0)))],
        out_shape=[jax.ShapeDtypeStruct((bsz, ATTN_HEADS, t, HEAD_DIM), BF16),
                   jax.ShapeDtypeStruct((bsz, t, ZB_W), BF16),
                   jax.ShapeDtypeStruct((bsz, t, ZF_W), F32)],
        compiler_params=_cparams(("parallel", "parallel", "arbitrary")),
    )(u, w, cos, sin_lo, sin_hi, qn, kn)


NEG_BIG = -1e30
AHEAD = 2
ONES_ROWS = 2 * SUBLANES


def _attn_kernel(q_ref, k_ref, v_ref, o_ref, m_ref, acc_ref, s_ref, *,
                 tq, tk, rb, t_len, ctx_len, unroll):
    qi = pl.program_id(2)
    nr = tq // rb
    per_kb = ATTN_GROUP * nr
    n_units = (t_len // tk) * per_kb

    m_ref[...] = jnp.full(m_ref.shape, NEG_BIG, F32)
    acc_ref[...] = jnp.zeros(acc_ref.shape, F32)

    def unit(n):
        kb = n // per_kb
        return kb, n - kb * per_kb

    def scores(n):
        kb, slot = unit(n)
        h = slot // nr
        r0 = pl.multiple_of((slot - h * nr) * rb, rb)
        k0 = pl.multiple_of(kb * tk, tk)
        return _dot_nt(k_ref[0, pl.ds(k0, tk), :], q_ref[0, h, pl.ds(r0, rb), :])

    def update(n, s, mixed):
        kb, slot = unit(n)
        if mixed:
            r0 = (slot - (slot // nr) * nr) * rb
            qrow = r0 + lax.broadcasted_iota(jnp.int32, (1, rb), 1)
            key = kb * tk + lax.broadcasted_iota(jnp.int32, (tk, 1), 0)
            s = jnp.where(qrow < ctx_len, jnp.where(key < ctx_len, s, NEG_BIG), s)
        m_prev = m_ref[slot]
        m_new = jnp.maximum(m_prev, jnp.max(s, axis=0, keepdims=True))
        alpha = jnp.exp2(m_prev - m_new)
        p = jnp.exp2((s - m_new).astype(BF16))
        m_ref[slot] = m_new
        acc_ref[slot] = alpha * acc_ref[slot] + _dot(v_ref[0, 0, kb], p)

    def run(mixed):
        for a in range(AHEAD):
            s_ref[a] = scores(a)

        def body(it, carry):
            n0 = it * unroll
            s = [s_ref[a] for a in range(AHEAD)]
            for u in range(0, unroll, AHEAD):
                s_next = [scores(jnp.minimum(n0 + u + AHEAD + a, n_units - 1))
                          for a in range(AHEAD)]
                for a in range(AHEAD):
                    update(n0 + u + a, s[a], mixed)
                s = s_next
            for a in range(AHEAD):
                s_ref[a] = s[a]
            return carry

        lax.fori_loop(0, n_units // unroll, body, 0)

    pl.when(qi == 0)(lambda: run(True))
    pl.when(qi > 0)(lambda: run(False))

    for h in range(ATTN_GROUP):
        for r in range(nr):
            slot = h * nr + r
            o_t = acc_ref[slot, :HEAD_DIM] * (1.0 / acc_ref[slot, HEAD_DIM:HEAD_DIM + 1])
            o_ref[0, r * rb:(r + 1) * rb, h * HEAD_DIM:(h + 1) * HEAD_DIM] = o_t.T


def _attn_call(zq, zb, *, ctx_len, tq, tk):
    bsz, t, _ = zb.shape
    assert ctx_len <= tq and ctx_len <= tk
    rb = 256 if tq % 256 == 0 else tq
    per_kb = ATTN_GROUP * (tq // rb)
    unroll = max(u for u in range(AHEAD, 13, AHEAD) if per_kb % u == 0)
    k_blk = ZB_AK_OFF // HEAD_DIM
    v_t = zb[:, :, ZB_AV_OFF:ZB_AV_OFF + ATTN_KV_HEADS * HEAD_DIM]
    v_t = v_t.reshape(bsz, t // tk, tk, ATTN_KV_HEADS, HEAD_DIM).transpose(0, 3, 1, 4, 2)
    v_t = jnp.concatenate([v_t, jnp.ones(v_t.shape[:3] + (ONES_ROWS, tk), v_t.dtype)], axis=3)
    kern = functools.partial(_attn_kernel, tq=tq, tk=tk, rb=rb, t_len=t, ctx_len=ctx_len,
                             unroll=unroll)
    return pl.pallas_call(
        kern,
        grid=(bsz, ATTN_KV_HEADS, t // tq),
        in_specs=[pl.BlockSpec((1, ATTN_GROUP, tq, HEAD_DIM), lambda b, g, qi: (b, g, qi, 0)),
                  pl.BlockSpec((1, t, HEAD_DIM), lambda b, g, qi: (b, 0, k_blk + g)),
                  pl.BlockSpec((1, 1, t // tk, HEAD_DIM + ONES_ROWS, tk),
                               lambda b, g, qi: (b, g, 0, 0, 0))],
        out_specs=pl.BlockSpec((1, tq, ATTN_GROUP * HEAD_DIM), lambda b, g, qi: (b, qi, g)),
        out_shape=jax.ShapeDtypeStruct((bsz, t, ATTN_HEADS * HEAD_DIM), F32),
        scratch_shapes=[pltpu.VMEM((per_kb, 1, rb), F32),
                        pltpu.VMEM((per_kb, HEAD_DIM + ONES_ROWS, rb), F32),
                        pltpu.VMEM((AHEAD, tk, rb), F32)],
        compiler_params=_cparams(("parallel", "parallel", "parallel")),
    )(zq, zb, v_t)


def _bwd_chunk(p, n_ctx_chunks, n_chunks):
    return jnp.where(p < n_ctx_chunks, n_ctx_chunks - 1 - p, n_chunks + n_ctx_chunks - 1 - p)


def _ret_kernel(ld_ref, qf_ref, kf_ref, vf_ref, qb_ref, kb_ref, vb_ref, yf_ref, yb_ref,
                sf_ref, sb_ref, *, chunk):
    c = chunk
    p = pl.program_id(1)

    @pl.when(p == 0)
    def _():
        sf_ref[...] = jnp.zeros(sf_ref.shape, F32)
        sb_ref[...] = jnp.zeros(sb_ref.shape, F32)

    ii = lax.broadcasted_iota(jnp.int32, (c, c), 0)
    jj = lax.broadcasted_iota(jnp.int32, (c, c), 1)
    dist = (ii - jj).astype(F32)
    pos = lax.broadcasted_iota(jnp.int32, (c, 1), 0).astype(F32)
    full = jnp.full((1, 1), float(c), F32)

    for h in range(RET_HEADS):
        ldf = ld_ref[0, h]
        ldb = ld_ref[1, h]
        qs = slice(h * RET_QK_DIM, (h + 1) * RET_QK_DIM)
        vs = slice(h * RET_V_DIM, (h + 1) * RET_V_DIM)
        decay = jnp.where(dist >= 0, jnp.exp(ldf * dist), 0.0) \
            + jnp.where(dist <= 0, jnp.exp(-ldb * dist), 0.0)
        q = qf_ref[0, :, qs]
        k = kf_ref[0, :, qs]
        v = vf_ref[0, :, vs]
        y = _dot((_dot_nt(q, k) * decay).astype(BF16), v)
        y = y + _dot(q, sf_ref[h].astype(BF16)) * jnp.exp(ldf * (pos + 1.0))
        yf_ref[0, :, vs] = y
        wv = (v.astype(F32) * jnp.exp(ldf * (c - 1.0 - pos))).astype(BF16)
        sf_ref[h] = sf_ref[h] * jnp.exp(ldf * full) + _dot_tn(k, wv)
        q = qb_ref[0, :, qs]
        k = kb_ref[0, :, qs]
        v = vb_ref[0, :, vs]
        yb_ref[0, :, vs] = _dot(q, sb_ref[h].astype(BF16)) * jnp.exp(ldb * (c - pos))
        wv = (v.astype(F32) * jnp.exp(ldb * pos)).astype(BF16)
        sb_ref[h] = sb_ref[h] * jnp.exp(ldb * full) + _dot_tn(k, wv)


def _ret_call(zb, log_decay, *, ctx_len, chunk):
    bsz, t, _ = zb.shape
    nc = t // chunk
    ncc = ctx_len // chunk
    qw = RET_HEADS * RET_QK_DIM
    vw = RET_HEADS * RET_V_DIM
    q_blk, k_blk, v_blk = ZB_RQ_OFF // qw, ZB_RK_OFF // qw, ZB_RV_OFF // vw
    bwd = functools.partial(_bwd_chunk, n_ctx_chunks=ncc, n_chunks=nc)
    kern = functools.partial(_ret_kernel, chunk=chunk)
    out = jax.ShapeDtypeStruct((bsz, t, vw), F32)
    return pl.pallas_call(
        kern,
        grid=(bsz, nc),
        in_specs=[pl.BlockSpec(memory_space=pltpu.SMEM),
                  pl.BlockSpec((1, chunk, qw), lambda b, p: (b, p, q_blk)),
                  pl.BlockSpec((1, chunk, qw), lambda b, p: (b, p, k_blk)),
                  pl.BlockSpec((1, chunk, vw), lambda b, p: (b, p, v_blk)),
                  pl.BlockSpec((1, chunk, qw), lambda b, p: (b, bwd(p), q_blk)),
                  pl.BlockSpec((1, chunk, qw), lambda b, p: (b, bwd(p), k_blk)),
                  pl.BlockSpec((1, chunk, vw), lambda b, p: (b, bwd(p), v_blk))],
        out_specs=[pl.BlockSpec((1, chunk, vw), lambda b, p: (b, p, 0)),
                   pl.BlockSpec((1, chunk, vw), lambda b, p: (b, bwd(p), 0))],
        out_shape=[out, out],
        scratch_shapes=[pltpu.VMEM((RET_HEADS, RET_QK_DIM, RET_V_DIM), F32),
                        pltpu.VMEM((RET_HEADS, RET_QK_DIM, RET_V_DIM), F32)],
        compiler_params=_cparams(("parallel", "arbitrary")),
    )(log_decay, zb, zb, zb, zb, zb, zb)


CONV_HALO = SUBLANES


def _conv_kernel(prev_ref, cur_ref, next_ref, w_ref, b_ref, o_ref, e_ref, *, tc, t_len, ctx_len):
    i = pl.program_id(1)
    e_ref[0:CONV_HALO] = prev_ref[0]
    e_ref[CONV_HALO:CONV_HALO + tc] = cur_ref[0]
    e_ref[CONV_HALO + tc:] = next_ref[0]
    pad = (SSM_CONV - 1) // 2
    lo = i * tc

    def conv(masked):
        tok = lo + lax.broadcasted_iota(jnp.int32, (tc, 1), 0)
        acc = jnp.broadcast_to(b_ref[...], (tc, ZF_XBC_W))
        for tap in range(SSM_CONV):
            off = tap - pad
            src = e_ref[CONV_HALO + off:CONV_HALO + off + tc, :]
            if masked:
                nb = tok + off
                same = ((tok - ctx_len) ^ (nb - ctx_len)) >= 0
                src = jnp.where(same & (nb >= 0) & (nb < t_len), src, 0.0)
            acc = acc + src * w_ref[tap:tap + 1, :]
        o_ref[0] = _silu(acc)

    def near(edge):
        return (lo - pad < edge) & (edge < lo + tc + pad)
    touches = near(0) | near(ctx_len) | near(t_len)
    pl.when(touches)(lambda: conv(True))
    pl.when(jnp.logical_not(touches))(lambda: conv(False))


def _conv_call(zf, conv_w, conv_b, *, ctx_len, tc):
    bsz, t, _ = zf.shape
    col = ZF_XBC_OFF // ZF_XBC_W
    per = tc // CONV_HALO
    last = t // CONV_HALO - 1
    kern = functools.partial(_conv_kernel, tc=tc, t_len=t, ctx_len=ctx_len)
    return pl.pallas_call(
        kern,
        grid=(bsz, t // tc),
        in_specs=[pl.BlockSpec((1, CONV_HALO, ZF_XBC_W),
                               lambda b, i: (b, jnp.maximum(i * per - 1, 0), col)),
                  pl.BlockSpec((1, tc, ZF_XBC_W), lambda b, i: (b, i, col)),
                  pl.BlockSpec((1, CONV_HALO, ZF_XBC_W),
                               lambda b, i: (b, jnp.minimum((i + 1) * per, last), col)),
                  pl.BlockSpec((SUBLANES, ZF_XBC_W), lambda b, i: (0, 0)),
                  pl.BlockSpec((1, ZF_XBC_W), lambda b, i: (0, 0))],
        out_specs=pl.BlockSpec((1, tc, ZF_XBC_W), lambda b, i: (b, i, 0)),
        out_shape=jax.ShapeDtypeStruct((bsz, t, ZF_XBC_W), F32),
        scratch_shapes=[pltpu.VMEM((tc + 2 * CONV_HALO, ZF_XBC_W), F32)],
        compiler_params=_cparams(("parallel", "parallel")),
    )(zf, zf, zf, conv_w, conv_b)


def _split3(x):
    p1 = x.astype(BF16)
    r1 = x - p1.astype(F32)
    p2 = r1.astype(BF16)
    p3 = (r1 - p2.astype(F32)).astype(BF16)
    return p1, p2, p3


def _cumsum_rows(x, tri):
    p1, p2, p3 = _split3(x)
    return _dot(tri, p1) + _dot(tri, p2) + _dot(tri, p3)


def _expand_heads(a, sel_ref):
    hi = a.astype(BF16)
    lo = (a - hi.astype(F32)).astype(BF16)
    return _dot(jnp.concatenate([hi, lo], axis=1), sel_ref[...])


def _head_selector(off):
    sel = np.zeros((LANES, SSM_INNER), np.float32)
    for h in range(SSM_HEADS):
        sel[off + h, h * SSM_HEAD_DIM:(h + 1) * SSM_HEAD_DIM] = 1.0
    return jnp.asarray(np.concatenate([sel, sel], axis=0), BF16)


def _ssd_kernel(xf_ref, df_ref, xb_ref, db_ref, bias_ref, alog_ref, skip_ref, self_ref, selb_ref,
                yf_ref, yb_ref, sf_ref, sb_ref, *, chunk, sub):
    c = chunk
    p = pl.program_id(1)
    nh = SSM_HEADS
    gw = SSM_HPG * SSM_HEAD_DIM
    bc_w = SSM_GROUPS * SSM_STATE

    @pl.when(p == 0)
    def _():
        sf_ref[...] = jnp.zeros(sf_ref.shape, F32)
        sb_ref[...] = jnp.zeros(sb_ref.shape, F32)

    ii = lax.broadcasted_iota(jnp.int32, (c, c), 0)
    jj = lax.broadcasted_iota(jnp.int32, (c, c), 1)
    lower = ii >= jj
    upper = jj >= ii
    tri = jnp.where(lower, 1.0, 0.0).astype(BF16)
    lane_lo = lax.broadcasted_iota(jnp.int32, (c, LANES), 1) < SSM_HEAD_DIM
    lane_hi = jnp.logical_not(lane_lo)

    def operands(x_ref, rows):
        xs = x_ref[0, rows, 0:SSM_INNER]
        bm = x_ref[0, rows, SSM_INNER:SSM_INNER + bc_w].astype(BF16)
        cm = x_ref[0, rows, SSM_INNER + bc_w:].astype(BF16)
        return xs, bm, cm

    def gates(d_ref, rows):
        raw = d_ref[0, rows, :] + bias_ref[...]
        dt = jnp.maximum(raw, 0.0) + jnp.log1p(jnp.exp(-jnp.abs(raw)))
        la = dt * -jnp.exp(alog_ref[...])
        acum = _cumsum_rows(la, tri)
        return dt, acum, acum - la

    def within(xs, bm, cm, dt, acum, excl):
        dt_t = dt.T
        acum_t = acum.T
        excl_t = excl.T
        xs_b = xs.astype(BF16)
        pieces = []
        for g in range(SSM_GROUPS):
            gs = slice(g * SSM_STATE, (g + 1) * SSM_STATE)
            scores = _dot_nt(cm[:, gs], bm[:, gs])
            for m in range(SSM_HPG // 2):
                pair = g * SSM_HPG // 2 + m
                xp = xs_b[:, pair * LANES:(pair + 1) * LANES]
                acc = None
                for e in range(2):
                    hf = 2 * pair + e
                    hb = nh + hf
                    dec_f = jnp.where(lower, jnp.exp(acum[:, hf:hf + 1] - acum_t[hf:hf + 1, :]), 0.0)
                    dec_b = jnp.where(upper, jnp.exp(excl_t[hb:hb + 1, :] - excl[:, hb:hb + 1]), 0.0)
                    wm = scores * (dec_f * dt_t[hf:hf + 1, :] + dec_b * dt_t[hb:hb + 1, :])
                    xh = jnp.where(lane_lo if e == 0 else lane_hi, xp, 0.0)
                    part = _dot(wm.astype(BF16), xh)
                    acc = part if acc is None else acc + part
                pieces.append(acc)
        return jnp.concatenate(pieces, axis=1)

    def carried(s_ref, xs, bm, cm, carry, wdt, keep):
        xw = (xs * wdt).astype(BF16)
        inter = []
        for g in range(SSM_GROUPS):
            gs = slice(g * SSM_STATE, (g + 1) * SSM_STATE)
            cs = slice(g * gw, (g + 1) * gw)
            inter.append(_dot(cm[:, gs], s_ref[g].astype(BF16)))
            s_ref[g] = s_ref[g] * keep[:, cs] + _dot_tn(bm[:, gs], xw[:, cs])
        return jnp.concatenate(inter, axis=1) * carry

    rows = [pl.ds(i * c, c) for i in range(sub)]
    fwd = [operands(xf_ref, r) + gates(df_ref, r) for r in rows]
    bwd = [operands(xb_ref, r) + gates(db_ref, r) for r in rows]

    fwd_scale = []
    for xs, bm, cm, dt, acum, excl in fwd:
        last = acum[c - 1:c, :]
        both = _expand_heads(
            jnp.concatenate([jnp.exp(acum), jnp.exp(last - acum) * dt], axis=0), self_ref)
        fwd_scale.append((both[:c], both[c:], both[c - 1:c]))
    bwd_scale = []
    for xs, bm, cm, dt, acum, excl in bwd:
        total = acum[c - 1:c, :]
        both = _expand_heads(
            jnp.concatenate([jnp.exp(total - excl), jnp.exp(excl) * dt], axis=0), selb_ref)
        bwd_scale.append((both[:c], both[c:], both[0:1]))

    y_within = [within(*f) for f in fwd]

    for i in range(sub):
        xs, bm, cm = fwd[i][:3]
        y = y_within[i] + carried(sf_ref, xs, bm, cm, *fwd_scale[i]) + skip_ref[...] * xs
        yf_ref[0, rows[i], :] = y
    for i in reversed(range(sub)):
        xs, bm, cm = bwd[i][:3]
        yb_ref[0, rows[i], :] = carried(sb_ref, xs, bm, cm, *bwd_scale[i])


def _ssd_call(xbc, zf, bias_row, alog_row, skip_row, *, ctx_len, chunk):
    bsz, t, _ = xbc.shape
    sub = 2 if ctx_len % (2 * chunk) == 0 and t % (2 * chunk) == 0 else 1
    blk = sub * chunk
    nblk = t // blk
    dt_blk = ZF_DT_OFF // LANES
    bwd = functools.partial(_bwd_chunk, n_ctx_chunks=ctx_len // blk, n_chunks=nblk)
    kern = functools.partial(_ssd_kernel, chunk=chunk, sub=sub)
    out = jax.ShapeDtypeStruct((bsz, t, SSM_INNER), F32)
    row = lambda w: pl.BlockSpec((1, w), lambda b, p: (0, 0))
    return pl.pallas_call(
        kern,
        grid=(bsz, nblk),
        in_specs=[pl.BlockSpec((1, blk, ZF_XBC_W), lambda b, p: (b, p, 0)),
                  pl.BlockSpec((1, blk, LANES), lambda b, p: (b, p, dt_blk)),
                  pl.BlockSpec((1, blk, ZF_XBC_W), lambda b, p: (b, bwd(p), 0)),
                  pl.BlockSpec((1, blk, LANES), lambda b, p: (b, bwd(p), dt_blk)),
                  row(LANES), row(LANES), row(SSM_INNER),
                  pl.BlockSpec((2 * LANES, SSM_INNER), lambda b, p: (0, 0)),
                  pl.BlockSpec((2 * LANES, SSM_INNER), lambda b, p: (0, 0))],
        out_specs=[pl.BlockSpec((1, blk, SSM_INNER), lambda b, p: (b, p, 0)),
                   pl.BlockSpec((1, blk, SSM_INNER), lambda b, p: (b, bwd(p), 0))],
        out_shape=[out, out],
        scratch_shapes=[pltpu.VMEM((SSM_GROUPS, SSM_STATE, SSM_HPG * SSM_HEAD_DIM), F32),
                        pltpu.VMEM((SSM_GROUPS, SSM_STATE, SSM_HPG * SSM_HEAD_DIM), F32)],
        compiler_params=_cparams(("parallel", "arbitrary")),
    )(xbc, zf, xbc, zf, bias_row, alog_row, skip_row,
      _head_selector(0), _head_selector(SSM_HEADS))


def _merge_kernel(attn_ref, retf_ref, retb_ref, ssmf_ref, ssmb_ref, mg_ref, ag_ref, rg_ref,
                  sz_ref, gnw_ref, snw_ref, wb_ref, o_ref):
    d = D_MODEL
    gate_in = lambda ref, lo=0, hi=None: ref[0, :, lo:hi].astype(F32)
    a = attn_ref[0] * _silu(gate_in(ag_ref))

    ret = retf_ref[0] + retb_ref[0]
    parts = []
    for h in range(RET_HEADS):
        r = ret[:, h * RET_V_DIM:(h + 1) * RET_V_DIM]
        rc = r - jnp.mean(r, axis=-1, keepdims=True)
        var = jnp.mean(rc * rc, axis=-1, keepdims=True)
        parts.append(rc * lax.rsqrt(var + NORM_EPS))
    r = jnp.concatenate(parts, axis=1) * gnw_ref[...] * _silu(gate_in(rg_ref))

    s = (ssmf_ref[0] + ssmb_ref[0]) * _silu(gate_in(sz_ref))
    gwid = SSM_INNER // SSM_GROUPS
    parts = []
    for g in range(SSM_GROUPS):
        sg = s[:, g * gwid:(g + 1) * gwid]
        parts.append(sg * lax.rsqrt(jnp.mean(sg * sg, axis=-1, keepdims=True) + NORM_EPS))
    s = jnp.concatenate(parts, axis=1) * snw_ref[...]

    merged = _sigmoid(gate_in(mg_ref, 0, d)) * _dot(a.astype(BF16), wb_ref[0])
    merged = merged + _sigmoid(gate_in(mg_ref, d, 2 * d)) * _dot(r.astype(BF16), wb_ref[1])
    merged = merged + _sigmoid(gate_in(mg_ref, 2 * d, 3 * d)) * _dot(s.astype(BF16), wb_ref[2])
    o_ref[0] = merged.astype(BF16)


def _merge_call(attn, retf, retb, ssmf, ssmb, zb, gnw, snw, wb, *, layer, tm):
    bsz, t, _ = attn.shape
    d = D_MODEL
    bw = BRANCH_WIDTH
    act = lambda blk: pl.BlockSpec((1, tm, bw), lambda b, i: (b, i, blk))
    vec = pl.BlockSpec((None, 1, bw), lambda b, i: (layer, 0, 0))
    return pl.pallas_call(
        _merge_kernel,
        grid=(bsz, t // tm),
        in_specs=[act(0), act(0), act(0), act(0), act(0),
                  pl.BlockSpec((1, tm, N_BRANCHES * d),
                               lambda b, i: (b, i, ZB_MG_OFF // (N_BRANCHES * d))),
                  act(ZB_AG_OFF // bw), act(ZB_RG_OFF // bw), act(ZB_SZ_OFF // bw),
                  vec, vec,
                  pl.BlockSpec((None, N_BRANCHES, bw, d), lambda b, i: (layer, 0, 0, 0),
                               pipeline_mode=pl.Buffered(1))],
        out_specs=pl.BlockSpec((1, tm, d), lambda b, i: (b, i, 0)),
        out_shape=jax.ShapeDtypeStruct((bsz, t, d), BF16),
        compiler_params=_cparams(("parallel", "parallel")),
    )(attn, retf, retb, ssmf, ssmb, zb, zb, zb, zb, gnw, snw, wb)


def _modulate(x, mod_ref, b, n_batch, is_ctx):
    d = D_MODEL
    shift = _modulation_rows(mod_ref, b, n_batch, is_ctx, 0, d)
    scale = _modulation_rows(mod_ref, b, n_batch, is_ctx, d, 2 * d)
    return (x * (1.0 + scale) + shift).astype(BF16)


def _modulate_kernel(x_ref, mod_ref, u_ref, *, tm, n_batch, ctx_len):
    row = pl.program_id(1) * tm + lax.broadcasted_iota(jnp.int32, (tm, 1), 0)
    u_ref[0] = _modulate(x_ref[0], mod_ref, pl.program_id(0), n_batch, row < ctx_len)


def _modulate_call(xs, mod, *, layer, n_batch, ctx_len, tm):
    bsz, t, d = xs.shape
    kern = functools.partial(_modulate_kernel, tm=tm, n_batch=n_batch, ctx_len=ctx_len)
    tok = pl.BlockSpec((1, tm, d), lambda b, i: (b, i, 0))
    return pl.pallas_call(
        kern,
        grid=(bsz, t // tm),
        in_specs=[tok, pl.BlockSpec((None, SUBLANES, 3 * d), lambda b, i: (layer, 0, 0))],
        out_specs=tok,
        out_shape=jax.ShapeDtypeStruct((bsz, t, d), BF16),
        compiler_params=_cparams(("parallel", "parallel")),
    )(xs, mod)


def _out_kernel(m_ref, x_ref, mod_ref, nxt_ref, w_ref, g_ref, b_ref, o_ref, *u_ref,
                tm, row0, n_batch, ctx_len):
    b = pl.program_id(0)
    d = D_MODEL
    row = row0 + pl.program_id(1) * tm + lax.broadcasted_iota(jnp.int32, (tm, 1), 0)
    is_ctx = row < ctx_len
    gate = _modulation_rows(mod_ref, b, n_batch, is_ctx, 2 * d, 3 * d)
    y = _dot(m_ref[0], w_ref[...])
    z = DEEPNORM_ALPHA * x_ref[0] + gate * y
    zc = z - jnp.mean(z, axis=-1, keepdims=True)
    var = jnp.mean(zc * zc, axis=-1, keepdims=True)
    x_new = zc * lax.rsqrt(var + NORM_EPS) * g_ref[...] + b_ref[...]
    o_ref[0] = x_new
    if u_ref:
        u_ref[0][0] = _modulate(x_new, nxt_ref, b, n_batch, is_ctx)


def _out_call(merged, xs, mod, w_out, ln_g, ln_b, *, layer, last, n_batch, ctx_len, tm):
    bsz, t, d = xs.shape
    depth = mod.shape[0]
    blk0 = ctx_len // tm if last else 0
    assert not last or ctx_len % tm == 0
    rows = t - blk0 * tm
    kern = functools.partial(_out_kernel, tm=tm, row0=blk0 * tm, n_batch=n_batch,
                             ctx_len=ctx_len)
    tok_in = pl.BlockSpec((1, tm, d), lambda b, i: (b, i + blk0, 0))
    tok_out = pl.BlockSpec((1, tm, d), lambda b, i: (b, i, 0))
    mod_at = lambda l: pl.BlockSpec((None, SUBLANES, 3 * d), lambda b, i: (l, 0, 0))
    vec = pl.BlockSpec((None, 1, d), lambda b, i: (layer, 0, 0))
    x_shape = jax.ShapeDtypeStruct((bsz, rows, d), F32)
    return pl.pallas_call(
        kern,
        grid=(bsz, rows // tm),
        in_specs=[tok_in, tok_in, mod_at(layer), mod_at(min(layer + 1, depth - 1)),
                  pl.BlockSpec((None, d, d), lambda b, i: (layer, 0, 0),
                               pipeline_mode=pl.Buffered(1)),
                  vec, vec],
        out_specs=tok_out if last else [tok_out, tok_out],
        out_shape=x_shape if last else [x_shape, jax.ShapeDtypeStruct((bsz, rows, d), BF16)],
        compiler_params=_cparams(("parallel", "parallel")),
    )(merged, xs, mod, mod, w_out, ln_g, ln_b)


PACK_COLS = 256


def _pack_kernel(w_ref, o_ref):
    row = 0
    for name in ("aq",) + ZB_ORDER + ZF_ORDER:
        src, width = _REF_SPLITS[name]
        o_ref[0, row:row + width, :] = w_ref[0, src:src + width, :].astype(BF16)
        row += width
    o_ref[0, row:, :] = jnp.zeros((W_PACKED - row, o_ref.shape[2]), BF16)


def _pack_w_in(w_in):
    depth, d, width = w_in.shape
    w_t = jnp.swapaxes(w_in, 1, 2)
    return pl.pallas_call(
        _pack_kernel,
        grid=(depth, d // PACK_COLS),
        in_specs=[pl.BlockSpec((1, width, PACK_COLS), lambda l, i: (l, 0, i))],
        out_specs=pl.BlockSpec((1, W_PACKED, PACK_COLS), lambda l, i: (l, 0, i)),
        out_shape=jax.ShapeDtypeStruct((depth, W_PACKED, d), BF16),
        compiler_params=_cparams(("parallel", "parallel")),
    )(w_t)


def _rope_tables(rows, ctx_len):
    row = jnp.repeat(jnp.arange(rows, dtype=F32), GRID_W)
    col = jnp.tile(jnp.arange(GRID_W, dtype=F32), rows)
    n_freq = HEAD_DIM // 4
    inv = ROPE_BASE ** (-jnp.arange(n_freq, dtype=F32) / n_freq)
    ang_r = row[:, None] * inv
    ang_c = col[:, None] * inv
    ang = jnp.concatenate([ang_r, ang_r, ang_c, ang_c], -1)
    cos = jnp.concatenate([jnp.ones((ctx_len, HEAD_DIM), F32), jnp.cos(ang)], 0)
    sin = jnp.concatenate([jnp.zeros((ctx_len, HEAD_DIM), F32), jnp.sin(ang)], 0)
    first = (np.arange(HEAD_DIM) % (HEAD_DIM // 2)) < HEAD_DIM // 4
    sin_lo = jnp.where(first, -sin, 0.0)
    sin_hi = jnp.where(first, 0.0, sin)
    return cos, sin_lo, sin_hi


def _pad_lanes(v, width):
    return jnp.pad(v.reshape(1, -1), ((0, 0), (0, width - v.size)))


def _tile_choices(t, ctx_len):
    def largest(cap, mult, also=0):
        best = mult
        for cand in range(mult, cap + 1, mult):
            if t % cand == 0 and also % cand == 0:
                best = cand
        return best
    return dict(proj=largest(2112, 16), attn=largest(768, LANES), conv=largest(528, SUBLANES),
                merge=largest(256, 16), out=largest(528, 16),
                out_last=largest(528, 16, also=ctx_len))


def kernel(x, c, ctx, c_ctx, ada_w, ada_b, w_in, attn_q_norm, attn_k_norm, ret_log_decay, ret_gn_w,
           ssm_conv_w, ssm_conv_b, ssm_dt_bias, ssm_a_log, ssm_d, ssm_norm_w, w_branch, w_out,
           ln_g, ln_b):
    bsz, seq, d = x.shape
    ctx_len = ctx.shape[1]
    depth = w_in.shape[0]
    t = seq + ctx_len
    chunk = LANES
    ret_chunk = 256 if ctx_len % 256 == 0 and t % 256 == 0 else chunk
    assert d == D_MODEL and bsz < SUBLANES and seq % GRID_W == 0
    assert ctx_len % chunk == 0 and t % chunk == 0
    tiles = _tile_choices(t, ctx_len)

    xs = jnp.concatenate([ctx, x], axis=1)
    cvec = jnp.zeros((SUBLANES, d), F32).at[:bsz].set(c).at[bsz].set(c_ctx)
    mod = _ada_call(cvec, ada_w, ada_b)
    cos, sin_lo, sin_hi = _rope_tables(seq // GRID_W, ctx_len)
    w_packed = _pack_w_in(w_in)
    wb = w_branch.astype(BF16)
    wo = w_out.astype(BF16)
    conv_w = jnp.pad(ssm_conv_w, ((0, 0), (0, SUBLANES - SSM_CONV), (0, 0)))
    per_layer_row = lambda v: v.reshape(depth, 1, -1)
    qn, kn = per_layer_row(attn_q_norm), per_layer_row(attn_k_norm)
    gnw, snw = per_layer_row(ret_gn_w), per_layer_row(ssm_norm_w)
    lng, lnb = per_layer_row(ln_g), per_layer_row(ln_b)

    u = _modulate_call(xs, mod, layer=0, n_batch=bsz, ctx_len=ctx_len, tm=tiles["out"])
    for l in range(depth):
        zq, zb, zf = _inproj_call(u, w_packed, cos, sin_lo, sin_hi, qn, kn, layer=l,
                                  tm=tiles["proj"])
        attn = _attn_call(zq, zb, ctx_len=ctx_len, tq=tiles["attn"], tk=tiles["attn"])
        retf, retb = _ret_call(zb, ret_log_decay[l], ctx_len=ctx_len, chunk=ret_chunk)
        xbc = _conv_call(zf, conv_w[l], ssm_conv_b[l].reshape(1, -1), ctx_len=ctx_len,
                         tc=tiles["conv"])
        ssmf, ssmb = _ssd_call(xbc, zf, _pad_lanes(ssm_dt_bias[l], LANES),
                               _pad_lanes(ssm_a_log[l], LANES),
                               jnp.repeat(ssm_d[l], SSM_HEAD_DIM).reshape(1, -1),
                               ctx_len=ctx_len, chunk=chunk)
        merged = _merge_call(attn, retf, retb, ssmf, ssmb, zb, gnw, snw, wb, layer=l,
                             tm=tiles["merge"])
        last = l == depth - 1
        out = _out_call(merged, xs, mod, wo, lng, lnb, layer=l, last=last, n_batch=bsz,
                        ctx_len=ctx_len, tm=tiles["out_last" if last else "out"])
        if last:
            return out
        xs, u = out
```

```python
import functools

import numpy as np
import jax
import jax.numpy as jnp
from jax import lax
from jax.experimental import pallas as pl
from jax.experimental.pallas import tpu as pltpu

F32 = jnp.float32
BF16 = jnp.bfloat16

D_MODEL = 2048
GRID_W = 64
HEAD_DIM = 128
ATTN_HEADS = 8
ATTN_KV_HEADS = 2
ATTN_GROUP = ATTN_HEADS // ATTN_KV_HEADS
ROPE_BASE = 10000.0
RET_HEADS = 4
RET_QK_DIM = 128
RET_V_DIM = 256
SSM_INNER = 1024
SSM_HEAD_DIM = 64
SSM_HEADS = 16
SSM_GROUPS = 2
SSM_HPG = SSM_HEADS // SSM_GROUPS
SSM_STATE = 128
SSM_CONV = 5
N_BRANCHES = 3
BRANCH_WIDTH = 1024
DEPTH = 4
DEEPNORM_ALPHA = (2 * DEPTH) ** 0.25
NORM_EPS = 1e-6
ATTN_Q_SCALE = HEAD_DIM ** -0.5 * float(np.log2(np.e))

_REF_SPLITS = dict(aq=(0, 1024), ak=(1024, 256), av=(1280, 256), ag=(1536, 1024),
                   rq=(2560, 512), rk=(3072, 512), rv=(3584, 1024), rg=(4608, 1024),
                   sx=(5632, 1024), sb=(6656, 256), sc=(6912, 256), sdt=(7168, 32),
                   sz=(7200, 1024), mg=(8224, 6144))

LANES = 128
SUBLANES = 8
VMEM_LIMIT = 56 * 1024 * 1024

PROJ_TN = 512
ZQ_TILES = ATTN_HEADS * HEAD_DIM // PROJ_TN
ZB_ORDER = ("mg", "ag", "rg", "sz", "rv", "ak", "av", "rq", "rk")
ZB_MG_OFF, ZB_AG_OFF, ZB_RG_OFF, ZB_SZ_OFF = 0, 6144, 7168, 8192
ZB_RV_OFF, ZB_AK_OFF, ZB_AV_OFF, ZB_RQ_OFF, ZB_RK_OFF = 9216, 10240, 10496, 10752, 11264
ZB_W = 11776
ZB_TILES = ZB_W // PROJ_TN
ZB_RAW_TILES = ZB_AK_OFF // PROJ_TN
ZF_TILE0 = ZQ_TILES + ZB_TILES
ZF_ORDER = ("sx", "sb", "sc", "sdt")
ZF_USED = 1024 + 256 + 256 + 32
ZF_W = 2048
ZF_XBC_OFF = 0
ZF_XBC_W = 1536
ZF_DT_OFF = 1536
W_PACKED = ZQ_TILES * PROJ_TN + ZB_W + ZF_W


def _sigmoid(x):
    return 1.0 / (1.0 + jnp.exp(-x))


def _silu(x):
    return x * _sigmoid(x)


def _cparams(sem):
    return pltpu.CompilerParams(dimension_semantics=sem, vmem_limit_bytes=VMEM_LIMIT)


def _dot(a, b):
    return jnp.dot(a, b, preferred_element_type=F32)


def _dot_nt(a, b):
    return lax.dot_general(a, b, (((1,), (1,)), ((), ())), preferred_element_type=F32)


def _dot_tn(a, b):
    return lax.dot_general(a, b, (((0,), (0,)), ((), ())), preferred_element_type=F32)


def _ada_kernel(c_ref, w_ref, b_ref, o_ref):
    s = _silu(c_ref[...]).astype(BF16)
    o_ref[0] = _dot(s, w_ref[0].astype(BF16)) + b_ref[0]


def _ada_call(cvec, ada_w, ada_b):
    depth, d, n = ada_w.shape
    tn = 768
    return pl.pallas_call(
        _ada_kernel,
        grid=(depth, n // tn),
        in_specs=[pl.BlockSpec((SUBLANES, d), lambda l, j: (0, 0)),
                  pl.BlockSpec((1, d, tn), lambda l, j: (l, 0, j)),
                  pl.BlockSpec((1, 1, tn), lambda l, j: (l, 0, j))],
        out_specs=pl.BlockSpec((1, SUBLANES, tn), lambda l, j: (l, 0, j)),
        out_shape=jax.ShapeDtypeStruct((depth, SUBLANES, n), F32),
        compiler_params=_cparams(("parallel", "parallel")),
    )(cvec, ada_w, ada_b.reshape(depth, 1, n))


def _rope(x, cos, sin_lo, sin_hi):
    return x * cos + pltpu.roll(x, 96, 1) * sin_lo + pltpu.roll(x, 32, 1) * sin_hi


def _modulation_rows(mod_ref, b, n_batch, is_ctx, lo, hi):
    mb = mod_ref[pl.ds(b, 1), lo:hi]
    mc = mod_ref[pl.ds(n_batch, 1), lo:hi]
    return jnp.where(is_ctx, mc, mb)


def _inproj_kernel(u_ref, w_ref, cos_ref, sl_ref, sh_ref, qn_ref, kn_ref,
                   zq_ref, zb_ref, zf_ref):
    j = pl.program_id(2)
    jb = j - ZQ_TILES

    def project():
        return _dot_nt(u_ref[0], w_ref[...])

    def rope(xh):
        return _rope(xh, cos_ref[...], sl_ref[...], sh_ref[...])

    def normed(w_row):
        def fn(xh):
            ms = jnp.mean(xh * xh, axis=-1, keepdims=True)
            return rope(xh * lax.rsqrt(ms + NORM_EPS) * w_row)
        return fn

    @pl.when(j >= ZF_TILE0)
    def _():
        zf_ref[0] = project()

    @pl.when((jb >= 0) & (jb < ZB_RAW_TILES))
    def _():
        zb_ref[0] = project().astype(BF16)

    @pl.when((j < ZQ_TILES) | ((jb >= ZB_RAW_TILES) & (j < ZF_TILE0)))
    def _():
        acc = project()

        def head(h):
            return acc[:, h * LANES:(h + 1) * LANES]

        def per_head(fn, n):
            return jnp.concatenate([fn(head(h)) for h in range(n)], axis=1)

        @pl.when(j < ZQ_TILES)
        def _():
            fn = normed(qn_ref[...])
            for h in range(ATTN_GROUP):
                zq_ref[0, h] = (fn(head(h)) * ATTN_Q_SCALE).astype(BF16)

        @pl.when(jb == ZB_AK_OFF // PROJ_TN)
        def _():
            k = per_head(normed(kn_ref[...]), 2)
            zb_ref[0] = jnp.concatenate([k, acc[:, 2 * LANES:]], axis=1).astype(BF16)

        @pl.when(jb == ZB_RQ_OFF // PROJ_TN)
        def _():
            zb_ref[0] = per_head(rope, 4).astype(BF16)

        @pl.when(jb == ZB_RK_OFF // PROJ_TN)
        def _():
            zb_ref[0] = per_head(lambda xh: rope(xh * (RET_QK_DIM ** -0.5)), 4).astype(BF16)


def _inproj_call(u, w, cos, sin_lo, sin_hi, qn, kn, *, layer, tm):
    bsz, t, d = u.shape
    rope_spec = pl.BlockSpec((tm, LANES), lambda b, i, j: (i, 0))
    vec_spec = pl.BlockSpec((None, 1, LANES), lambda b, i, j: (layer, 0, 0))
    return pl.pallas_call(
        _inproj_kernel,
        grid=(bsz, t // tm, W_PACKED // PROJ_TN),
        in_specs=[pl.BlockSpec((1, tm, d), lambda b, i, j: (b, i, 0)),
                  pl.BlockSpec((None, PROJ_TN, d), lambda b, i, j: (layer, j, 0)),
                  rope_spec, rope_spec, rope_spec, vec_spec, vec_spec],
        out_specs=[pl.BlockSpec((1, ATTN_GROUP, tm, HEAD_DIM),
                                lambda b, i, j: (b, jnp.minimum(j, ZQ_TILES - 1), i, 0)),
                   pl.BlockSpec((1, tm, PROJ_TN),
                                lambda b, i, j: (b, i, jnp.clip(j - ZQ_TILES, 0, ZB_TILES - 1))),
                   pl.BlockSpec((1, tm, PROJ_TN),
                                lambda b, i, j: (b, i, jnp.maximum(j - ZF_TILE0, 0)))],
        out_shape=[jax.ShapeDtypeStruct((bsz, ATTN_HEADS, t, HEAD_DIM), BF16),
                   jax.ShapeDtypeStruct((bsz, t, ZB_W), BF16),
                   jax.ShapeDtypeStruct((bsz, t, ZF_W), F32)],
        compiler_params=_cparams(("parallel", "parallel", "arbitrary")),
    )(u, w, cos, sin_lo, sin_hi, qn, kn)


NEG_BIG = -1e30
AHEAD = 2
ONES_ROWS = 2 * SUBLANES
BOUND_MARGIN = 1.02
MAX_SAFE_SHIFT = 50.0


def _attn_kernel(q_ref, k_ref, v_ref, o_ref, m_ref, acc_ref, s_ref, kmax_ref, *,
                 tq, tk, rb, t_len, ctx_len, unroll):
    qi = pl.program_id(2)
    nr = tq // rb
    per_kb = ATTN_GROUP * nr
    n_units = (t_len // tk) * per_kb

    @pl.when(qi == 0)
    def _():
        k = k_ref[0].astype(F32)
        kmax_ref[...] = jnp.broadcast_to(
            jnp.max(jnp.sum(k * k, axis=1, keepdims=True), axis=0, keepdims=True),
            kmax_ref.shape)

    ones = jnp.ones((SUBLANES, HEAD_DIM), BF16)
    for h in range(ATTN_GROUP):
        for r in range(nr):
            q = q_ref[0, h, r * rb:(r + 1) * rb, :].astype(F32)
            qsq = _dot_nt(ones, (q * q).astype(BF16))[0:1]
            m_ref[h * nr + r] = jnp.sqrt(qsq * kmax_ref[0:1, 0:1]) * BOUND_MARGIN
    bounded = jnp.max(m_ref[...]) <= MAX_SAFE_SHIFT
    acc_ref[...] = jnp.zeros(acc_ref.shape, F32)

    def unit(n):
        kb = n // per_kb
        return kb, n - kb * per_kb

    def scores(n):
        kb, slot = unit(n)
        h = slot // nr
        r0 = pl.multiple_of((slot - h * nr) * rb, rb)
        k0 = pl.multiple_of(kb * tk, tk)
        return _dot_nt(k_ref[0, pl.ds(k0, tk), :], q_ref[0, h, pl.ds(r0, rb), :])

    def update(n, s, mixed, fixed_shift):
        kb, slot = unit(n)
        if mixed:
            r0 = (slot - (slot // nr) * nr) * rb
            qrow = r0 + lax.broadcasted_iota(jnp.int32, (1, rb), 1)
            key = kb * tk + lax.broadcasted_iota(jnp.int32, (tk, 1), 0)
            s = jnp.where(qrow < ctx_len, jnp.where(key < ctx_len, s, NEG_BIG), s)
        if fixed_shift:
            p = jnp.exp2(s - m_ref[slot]).astype(BF16)
            acc_ref[slot] += _dot(v_ref[0, 0, kb], p)
        else:
            m_prev = m_ref[slot]
            m_new = jnp.maximum(m_prev, jnp.max(s, axis=0, keepdims=True))
            alpha = jnp.exp2(m_prev - m_new)
            p = jnp.exp2((s - m_new).astype(BF16))
            m_ref[slot] = m_new
            acc_ref[slot] = alpha * acc_ref[slot] + _dot(v_ref[0, 0, kb], p)

    def run(mixed, fixed_shift):
        if not fixed_shift:
            m_ref[...] = jnp.full(m_ref.shape, NEG_BIG, F32)
        for a in range(AHEAD):
            s_ref[a] = scores(a)

        def body(it, carry):
            n0 = it * unroll
            s = [s_ref[a] for a in range(AHEAD)]
            for u in range(0, unroll, AHEAD):
                s_next = [scores(jnp.minimum(n0 + u + AHEAD + a, n_units - 1))
                          for a in range(AHEAD)]
                for a in range(AHEAD):
                    update(n0 + u + a, s[a], mixed, fixed_shift)
                s = s_next
            for a in range(AHEAD):
                s_ref[a] = s[a]
            return carry

        lax.fori_loop(0, n_units // unroll, body, 0)

    first = qi == 0
    later = jnp.logical_not(first)
    unbounded = jnp.logical_not(bounded)
    pl.when(first & bounded)(lambda: run(True, True))
    pl.when(later & bounded)(lambda: run(False, True))
    pl.when(first & unbounded)(lambda: run(True, False))
    pl.when(later & unbounded)(lambda: run(False, False))

    for h in range(ATTN_GROUP):
        for r in range(nr):
            slot = h * nr + r
            o_t = acc_ref[slot, :HEAD_DIM] * (1.0 / acc_ref[slot, HEAD_DIM:HEAD_DIM + 1])
            o_ref[0, r * rb:(r + 1) * rb, h * HEAD_DIM:(h + 1) * HEAD_DIM] = o_t.T


def _attn_call(zq, zb, *, ctx_len, tq, tk):
    bsz, t, _ = zb.shape
    assert ctx_len <= tq and ctx_len <= tk
    rb = 256 if tq % 256 == 0 else tq
    per_kb = ATTN_GROUP * (tq // rb)
    unroll = max(u for u in range(AHEAD, 13, AHEAD) if per_kb % u == 0)
    k_blk = ZB_AK_OFF // HEAD_DIM
    v_t = zb[:, :, ZB_AV_OFF:ZB_AV_OFF + ATTN_KV_HEADS * HEAD_DIM]
    v_t = v_t.reshape(bsz, t // tk, tk, ATTN_KV_HEADS, HEAD_DIM).transpose(0, 3, 1, 4, 2)
    v_t = jnp.concatenate([v_t, jnp.ones(v_t.shape[:3] + (ONES_ROWS, tk), v_t.dtype)], axis=3)
    kern = functools.partial(_attn_kernel, tq=tq, tk=tk, rb=rb, t_len=t, ctx_len=ctx_len,
                             unroll=unroll)
    return pl.pallas_call(
        kern,
        grid=(bsz, ATTN_KV_HEADS, t // tq),
        in_specs=[pl.BlockSpec((1, ATTN_GROUP, tq, HEAD_DIM), lambda b, g, qi: (b, g, qi, 0)),
                  pl.BlockSpec((1, t, HEAD_DIM), lambda b, g, qi: (b, 0, k_blk + g)),
                  pl.BlockSpec((1, 1, t // tk, HEAD_DIM + ONES_ROWS, tk),
                               lambda b, g, qi: (b, g, 0, 0, 0))],
        out_specs=pl.BlockSpec((1, tq, ATTN_GROUP * HEAD_DIM), lambda b, g, qi: (b, qi, g)),
        out_shape=jax.ShapeDtypeStruct((bsz, t, ATTN_HEADS * HEAD_DIM), F32),
        scratch_shapes=[pltpu.VMEM((per_kb, 1, rb), F32),
                        pltpu.VMEM((per_kb, HEAD_DIM + ONES_ROWS, rb), F32),
                        pltpu.VMEM((AHEAD, tk, rb), F32),
                        pltpu.VMEM((SUBLANES, LANES), F32)],
        compiler_params=_cparams(("parallel", "parallel", "arbitrary")),
    )(zq, zb, v_t)


def _bwd_chunk(p, n_ctx_chunks, n_chunks):
    return jnp.where(p < n_ctx_chunks, n_ctx_chunks - 1 - p, n_chunks + n_ctx_chunks - 1 - p)


def _ret_kernel(ld_ref, qf_ref, kf_ref, vf_ref, qb_ref, kb_ref, vb_ref, yf_ref, yb_ref,
                sf_ref, sb_ref, *, chunk):
    c = chunk
    p = pl.program_id(1)

    @pl.when(p == 0)
    def _():
        sf_ref[...] = jnp.zeros(sf_ref.shape, F32)
        sb_ref[...] = jnp.zeros(sb_ref.shape, F32)

    ii = lax.broadcasted_iota(jnp.int32, (c, c), 0)
    jj = lax.broadcasted_iota(jnp.int32, (c, c), 1)
    dist = (ii - jj).astype(F32)
    pos = lax.broadcasted_iota(jnp.int32, (c, 1), 0).astype(F32)
    full = jnp.full((1, 1), float(c), F32)

    for h in range(RET_HEADS):
        ldf = ld_ref[0, h]
        ldb = ld_ref[1, h]
        qs = slice(h * RET_QK_DIM, (h + 1) * RET_QK_DIM)
        vs = slice(h * RET_V_DIM, (h + 1) * RET_V_DIM)
        decay = jnp.where(dist >= 0, jnp.exp(ldf * dist), 0.0) \
            + jnp.where(dist <= 0, jnp.exp(-ldb * dist), 0.0)
        q = qf_ref[0, :, qs]
        k = kf_ref[0, :, qs]
        v = vf_ref[0, :, vs]
        y = _dot((_dot_nt(q, k) * decay).astype(BF16), v)
        y = y + _dot(q, sf_ref[h].astype(BF16)) * jnp.exp(ldf * (pos + 1.0))
        yf_ref[0, :, vs] = y
        wv = (v.astype(F32) * jnp.exp(ldf * (c - 1.0 - pos))).astype(BF16)
        sf_ref[h] = sf_ref[h] * jnp.exp(ldf * full) + _dot_tn(k, wv)
        q = qb_ref[0, :, qs]
        k = kb_ref[0, :, qs]
        v = vb_ref[0, :, vs]
        yb_ref[0, :, vs] = _dot(q, sb_ref[h].astype(BF16)) * jnp.exp(ldb * (c - pos))
        wv = (v.astype(F32) * jnp.exp(ldb * pos)).astype(BF16)
        sb_ref[h] = sb_ref[h] * jnp.exp(ldb * full) + _dot_tn(k, wv)


def _ret_call(zb, log_decay, *, ctx_len, chunk):
    bsz, t, _ = zb.shape
    nc = t // chunk
    ncc = ctx_len // chunk
    qw = RET_HEADS * RET_QK_DIM
    vw = RET_HEADS * RET_V_DIM
    q_blk, k_blk, v_blk = ZB_RQ_OFF // qw, ZB_RK_OFF // qw, ZB_RV_OFF // vw
    bwd = functools.partial(_bwd_chunk, n_ctx_chunks=ncc, n_chunks=nc)
    kern = functools.partial(_ret_kernel, chunk=chunk)
    out = jax.ShapeDtypeStruct((bsz, t, vw), F32)
    return pl.pallas_call(
        kern,
        grid=(bsz, nc),
        in_specs=[pl.BlockSpec(memory_space=pltpu.SMEM),
                  pl.BlockSpec((1, chunk, qw), lambda b, p: (b, p, q_blk)),
                  pl.BlockSpec((1, chunk, qw), lambda b, p: (b, p, k_blk)),
                  pl.BlockSpec((1, chunk, vw), lambda b, p: (b, p, v_blk)),
                  pl.BlockSpec((1, chunk, qw), lambda b, p: (b, bwd(p), q_blk)),
                  pl.BlockSpec((1, chunk, qw), lambda b, p: (b, bwd(p), k_blk)),
                  pl.BlockSpec((1, chunk, vw), lambda b, p: (b, bwd(p), v_blk))],
        out_specs=[pl.BlockSpec((1, chunk, vw), lambda b, p: (b, p, 0)),
                   pl.BlockSpec((1, chunk, vw), lambda b, p: (b, bwd(p), 0))],
        out_shape=[out, out],
        scratch_shapes=[pltpu.VMEM((RET_HEADS, RET_QK_DIM, RET_V_DIM), F32),
                        pltpu.VMEM((RET_HEADS, RET_QK_DIM, RET_V_DIM), F32)],
        compiler_params=_cparams(("parallel", "arbitrary")),
    )(log_decay, zb, zb, zb, zb, zb, zb)


CONV_HALO = SUBLANES


def _conv_kernel(prev_ref, cur_ref, next_ref, w_ref, b_ref, o_ref, e_ref, *, tc, t_len, ctx_len):
    i = pl.program_id(1)
    e_ref[0:CONV_HALO] = prev_ref[0]
    e_ref[CONV_HALO:CONV_HALO + tc] = cur_ref[0]
    e_ref[CONV_HALO + tc:] = next_ref[0]
    pad = (SSM_CONV - 1) // 2
    lo = i * tc

    def conv(masked):
        tok = lo + lax.broadcasted_iota(jnp.int32, (tc, 1), 0)
        acc = jnp.broadcast_to(b_ref[...], (tc, ZF_XBC_W))
        for tap in range(SSM_CONV):
            off = tap - pad
            src = e_ref[CONV_HALO + off:CONV_HALO + off + tc, :]
            if masked:
                nb = tok + off
                same = ((tok - ctx_len) ^ (nb - ctx_len)) >= 0
                src = jnp.where(same & (nb >= 0) & (nb < t_len), src, 0.0)
            acc = acc + src * w_ref[tap:tap + 1, :]
        o_ref[0] = _silu(acc)

    def near(edge):
        return (lo - pad < edge) & (edge < lo + tc + pad)
    touches = near(0) | near(ctx_len) | near(t_len)
    pl.when(touches)(lambda: conv(True))
    pl.when(jnp.logical_not(touches))(lambda: conv(False))


def _conv_call(zf, conv_w, conv_b, *, ctx_len, tc):
    bsz, t, _ = zf.shape
    col = ZF_XBC_OFF // ZF_XBC_W
    per = tc // CONV_HALO
    last = t // CONV_HALO - 1
    kern = functools.partial(_conv_kernel, tc=tc, t_len=t, ctx_len=ctx_len)
    return pl.pallas_call(
        kern,
        grid=(bsz, t // tc),
        in_specs=[pl.BlockSpec((1, CONV_HALO, ZF_XBC_W),
                               lambda b, i: (b, jnp.maximum(i * per - 1, 0), col)),
                  pl.BlockSpec((1, tc, ZF_XBC_W), lambda b, i: (b, i, col)),
                  pl.BlockSpec((1, CONV_HALO, ZF_XBC_W),
                               lambda b, i: (b, jnp.minimum((i + 1) * per, last), col)),
                  pl.BlockSpec((SUBLANES, ZF_XBC_W), lambda b, i: (0, 0)),
                  pl.BlockSpec((1, ZF_XBC_W), lambda b, i: (0, 0))],
        out_specs=pl.BlockSpec((1, tc, ZF_XBC_W), lambda b, i: (b, i, 0)),
        out_shape=jax.ShapeDtypeStruct((bsz, t, ZF_XBC_W), F32),
        scratch_shapes=[pltpu.VMEM((tc + 2 * CONV_HALO, ZF_XBC_W), F32)],
        compiler_params=_cparams(("parallel", "parallel")),
    )(zf, zf, zf, conv_w, conv_b)


def _split3(x):
    p1 = x.astype(BF16)
    r1 = x - p1.astype(F32)
    p2 = r1.astype(BF16)
    p3 = (r1 - p2.astype(F32)).astype(BF16)
    return p1, p2, p3


def _cumsum_rows(x, tri):
    p1, p2, p3 = _split3(x)
    return _dot(tri, p1) + _dot(tri, p2) + _dot(tri, p3)


def _expand_heads(a, sel_ref):
    hi = a.astype(BF16)
    lo = (a - hi.astype(F32)).astype(BF16)
    return _dot(jnp.concatenate([hi, lo], axis=1), sel_ref[...])


def _head_selector(off):
    sel = np.zeros((LANES, SSM_INNER), np.float32)
    for h in range(SSM_HEADS):
        sel[off + h, h * SSM_HEAD_DIM:(h + 1) * SSM_HEAD_DIM] = 1.0
    return jnp.asarray(np.concatenate([sel, sel], axis=0), BF16)


def _ssd_kernel(xf_ref, df_ref, xb_ref, db_ref, bias_ref, alog_ref, skip_ref, self_ref, selb_ref,
                yf_ref, yb_ref, sf_ref, sb_ref, *, chunk, sub):
    c = chunk
    p = pl.program_id(1)
    nh = SSM_HEADS
    gw = SSM_HPG * SSM_HEAD_DIM
    bc_w = SSM_GROUPS * SSM_STATE

    @pl.when(p == 0)
    def _():
        sf_ref[...] = jnp.zeros(sf_ref.shape, F32)
        sb_ref[...] = jnp.zeros(sb_ref.shape, F32)

    ii = lax.broadcasted_iota(jnp.int32, (c, c), 0)
    jj = lax.broadcasted_iota(jnp.int32, (c, c), 1)
    lower = ii >= jj
    upper = jj >= ii
    tri = jnp.where(lower, 1.0, 0.0).astype(BF16)
    lane_lo = lax.broadcasted_iota(jnp.int32, (c, LANES), 1) < SSM_HEAD_DIM
    lane_hi = jnp.logical_not(lane_lo)

    def operands(x_ref, rows):
        xs = x_ref[0, rows, 0:SSM_INNER]
        bm = x_ref[0, rows, SSM_INNER:SSM_INNER + bc_w].astype(BF16)
        cm = x_ref[0, rows, SSM_INNER + bc_w:].astype(BF16)
        return xs, bm, cm

    def gates(d_ref, rows):
        raw = d_ref[0, rows, :] + bias_ref[...]
        dt = jnp.maximum(raw, 0.0) + jnp.log1p(jnp.exp(-jnp.abs(raw)))
        la = dt * -jnp.exp(alog_ref[...])
        acum = _cumsum_rows(la, tri)
        return dt, acum, acum - la

    def within(xs, bm, cm, dt, acum, excl):
        dt_t = dt.T
        acum_t = acum.T
        excl_t = excl.T
        xs_b = xs.astype(BF16)
        pieces = []
        for g in range(SSM_GROUPS):
            gs = slice(g * SSM_STATE, (g + 1) * SSM_STATE)
            scores = _dot_nt(cm[:, gs], bm[:, gs])
            for m in range(SSM_HPG // 2):
                pair = g * SSM_HPG // 2 + m
                xp = xs_b[:, pair * LANES:(pair + 1) * LANES]
                acc = None
                for e in range(2):
                    hf = 2 * pair + e
                    hb = nh + hf
                    dec_f = jnp.where(lower, jnp.exp(acum[:, hf:hf + 1] - acum_t[hf:hf + 1, :]), 0.0)
                    dec_b = jnp.where(upper, jnp.exp(excl_t[hb:hb + 1, :] - excl[:, hb:hb + 1]), 0.0)
                    wm = scores * (dec_f * dt_t[hf:hf + 1, :] + dec_b * dt_t[hb:hb + 1, :])
                    xh = jnp.where(lane_lo if e == 0 else lane_hi, xp, 0.0)
                    part = _dot(wm.astype(BF16), xh)
                    acc = part if acc is None else acc + part
                pieces.append(acc)
        return jnp.concatenate(pieces, axis=1)

    def carried(s_ref, xs, bm, cm, carry, wdt, keep):
        xw = (xs * wdt).astype(BF16)
        inter = []
        for g in range(SSM_GROUPS):
            gs = slice(g * SSM_STATE, (g + 1) * SSM_STATE)
            cs = slice(g * gw, (g + 1) * gw)
            inter.append(_dot(cm[:, gs], s_ref[g].astype(BF16)))
            s_ref[g] = s_ref[g] * keep[:, cs] + _dot_tn(bm[:, gs], xw[:, cs])
        return jnp.concatenate(inter, axis=1) * carry

    rows = [pl.ds(i * c, c) for i in range(sub)]
    fwd = [operands(xf_ref, r) + gates(df_ref, r) for r in rows]
    bwd = [operands(xb_ref, r) + gates(db_ref, r) for r in rows]

    fwd_scale = []
    for xs, bm, cm, dt, acum, excl in fwd:
        last = acum[c - 1:c, :]
        both = _expand_heads(
            jnp.concatenate([jnp.exp(acum), jnp.exp(last - acum) * dt], axis=0), self_ref)
        fwd_scale.append((both[:c], both[c:], both[c - 1:c]))
    bwd_scale = []
    for xs, bm, cm, dt, acum, excl in bwd:
        total = acum[c - 1:c, :]
        both = _expand_heads(
            jnp.concatenate([jnp.exp(total - excl), jnp.exp(excl) * dt], axis=0), selb_ref)
        bwd_scale.append((both[:c], both[c:], both[0:1]))

    y_within = [within(*f) for f in fwd]

    for i in range(sub):
        xs, bm, cm = fwd[i][:3]
        y = y_within[i] + carried(sf_ref, xs, bm, cm, *fwd_scale[i]) + skip_ref[...] * xs
        yf_ref[0, rows[i], :] = y
    for i in reversed(range(sub)):
        xs, bm, cm = bwd[i][:3]
        yb_ref[0, rows[i], :] = carried(sb_ref, xs, bm, cm, *bwd_scale[i])


def _ssd_call(xbc, zf, bias_row, alog_row, skip_row, *, ctx_len, chunk):
    bsz, t, _ = xbc.shape
    sub = 2 if ctx_len % (2 * chunk) == 0 and t % (2 * chunk) == 0 else 1
    blk = sub * chunk
    nblk = t // blk
    dt_blk = ZF_DT_OFF // LANES
    bwd = functools.partial(_bwd_chunk, n_ctx_chunks=ctx_len // blk, n_chunks=nblk)
    kern = functools.partial(_ssd_kernel, chunk=chunk, sub=sub)
    out = jax.ShapeDtypeStruct((bsz, t, SSM_INNER), F32)
    row = lambda w: pl.BlockSpec((1, w), lambda b, p: (0, 0))
    return pl.pallas_call(
        kern,
        grid=(bsz, nblk),
        in_specs=[pl.BlockSpec((1, blk, ZF_XBC_W), lambda b, p: (b, p, 0)),
                  pl.BlockSpec((1, blk, LANES), lambda b, p: (b, p, dt_blk)),
                  pl.BlockSpec((1, blk, ZF_XBC_W), lambda b, p: (b, bwd(p), 0)),
                  pl.BlockSpec((1, blk, LANES), lambda b, p: (b, bwd(p), dt_blk)),
                  row(LANES), row(LANES), row(SSM_INNER),
                  pl.BlockSpec((2 * LANES, SSM_INNER), lambda b, p: (0, 0)),
                  pl.BlockSpec((2 * LANES, SSM_INNER), lambda b, p: (0, 0))],
        out_specs=[pl.BlockSpec((1, blk, SSM_INNER), lambda b, p: (b, p, 0)),
                   pl.BlockSpec((1, blk, SSM_INNER), lambda b, p: (b, bwd(p), 0))],
        out_shape=[out, out],
        scratch_shapes=[pltpu.VMEM((SSM_GROUPS, SSM_STATE, SSM_HPG * SSM_HEAD_DIM), F32),
                        pltpu.VMEM((SSM_GROUPS, SSM_STATE, SSM_HPG * SSM_HEAD_DIM), F32)],
        compiler_params=_cparams(("parallel", "arbitrary")),
    )(xbc, zf, xbc, zf, bias_row, alog_row, skip_row,
      _head_selector(0), _head_selector(SSM_HEADS))


def _merge_kernel(attn_ref, retf_ref, retb_ref, ssmf_ref, ssmb_ref, mg_ref, ag_ref, rg_ref,
                  sz_ref, gnw_ref, snw_ref, wb_ref, o_ref):
    d = D_MODEL
    gate_in = lambda ref, lo=0, hi=None: ref[0, :, lo:hi].astype(F32)
    a = attn_ref[0] * _silu(gate_in(ag_ref))

    ret = retf_ref[0] + retb_ref[0]
    parts = []
    for h in range(RET_HEADS):
        r = ret[:, h * RET_V_DIM:(h + 1) * RET_V_DIM]
        rc = r - jnp.mean(r, axis=-1, keepdims=True)
        var = jnp.mean(rc * rc, axis=-1, keepdims=True)
        parts.append(rc * lax.rsqrt(var + NORM_EPS))
    r = jnp.concatenate(parts, axis=1) * gnw_ref[...] * _silu(gate_in(rg_ref))

    s = (ssmf_ref[0] + ssmb_ref[0]) * _silu(gate_in(sz_ref))
    gwid = SSM_INNER // SSM_GROUPS
    parts = []
    for g in range(SSM_GROUPS):
        sg = s[:, g * gwid:(g + 1) * gwid]
        parts.append(sg * lax.rsqrt(jnp.mean(sg * sg, axis=-1, keepdims=True) + NORM_EPS))
    s = jnp.concatenate(parts, axis=1) * snw_ref[...]

    merged = _sigmoid(gate_in(mg_ref, 0, d)) * _dot(a.astype(BF16), wb_ref[0])
    merged = merged + _sigmoid(gate_in(mg_ref, d, 2 * d)) * _dot(r.astype(BF16), wb_ref[1])
    merged = merged + _sigmoid(gate_in(mg_ref, 2 * d, 3 * d)) * _dot(s.astype(BF16), wb_ref[2])
    o_ref[0] = merged.astype(BF16)


def _merge_call(attn, retf, retb, ssmf, ssmb, zb, gnw, snw, wb, *, layer, tm):
    bsz, t, _ = attn.shape
    d = D_MODEL
    bw = BRANCH_WIDTH
    act = lambda blk: pl.BlockSpec((1, tm, bw), lambda b, i: (b, i, blk))
    vec = pl.BlockSpec((None, 1, bw), lambda b, i: (layer, 0, 0))
    return pl.pallas_call(
        _merge_kernel,
        grid=(bsz, t // tm),
        in_specs=[act(0), act(0), act(0), act(0), act(0),
                  pl.BlockSpec((1, tm, N_BRANCHES * d),
                               lambda b, i: (b, i, ZB_MG_OFF // (N_BRANCHES * d))),
                  act(ZB_AG_OFF // bw), act(ZB_RG_OFF // bw), act(ZB_SZ_OFF // bw),
                  vec, vec,
                  pl.BlockSpec((None, N_BRANCHES, bw, d), lambda b, i: (layer, 0, 0, 0),
                               pipeline_mode=pl.Buffered(1))],
        out_specs=pl.BlockSpec((1, tm, d), lambda b, i: (b, i, 0)),
        out_shape=jax.ShapeDtypeStruct((bsz, t, d), BF16),
        compiler_params=_cparams(("parallel", "parallel")),
    )(attn, retf, retb, ssmf, ssmb, zb, zb, zb, zb, gnw, snw, wb)


def _modulate(x, mod_ref, b, n_batch, is_ctx):
    d = D_MODEL
    shift = _modulation_rows(mod_ref, b, n_batch, is_ctx, 0, d)
    scale = _modulation_rows(mod_ref, b, n_batch, is_ctx, d, 2 * d)
    return (x * (1.0 + scale) + shift).astype(BF16)


def _modulate_kernel(x_ref, mod_ref, u_ref, *, tm, n_batch, ctx_len):
    row = pl.program_id(1) * tm + lax.broadcasted_iota(jnp.int32, (tm, 1), 0)
    u_ref[0] = _modulate(x_ref[0], mod_ref, pl.program_id(0), n_batch, row < ctx_len)


def _modulate_call(xs, mod, *, layer, n_batch, ctx_len, tm):
    bsz, t, d = xs.shape
    kern = functools.partial(_modulate_kernel, tm=tm, n_batch=n_batch, ctx_len=ctx_len)
    tok = pl.BlockSpec((1, tm, d), lambda b, i: (b, i, 0))
    return pl.pallas_call(
        kern,
        grid=(bsz, t // tm),
        in_specs=[tok, pl.BlockSpec((None, SUBLANES, 3 * d), lambda b, i: (layer, 0, 0))],
        out_specs=tok,
        out_shape=jax.ShapeDtypeStruct((bsz, t, d), BF16),
        compiler_params=_cparams(("parallel", "parallel")),
    )(xs, mod)


def _out_kernel(m_ref, x_ref, mod_ref, nxt_ref, w_ref, g_ref, b_ref, o_ref, *u_ref,
                tm, row0, n_batch, ctx_len):
    b = pl.program_id(0)
    d = D_MODEL
    row = row0 + pl.program_id(1) * tm + lax.broadcasted_iota(jnp.int32, (tm, 1), 0)
    is_ctx = row < ctx_len
    gate = _modulation_rows(mod_ref, b, n_batch, is_ctx, 2 * d, 3 * d)
    y = _dot(m_ref[0], w_ref[...])
    z = DEEPNORM_ALPHA * x_ref[0] + gate * y
    zc = z - jnp.mean(z, axis=-1, keepdims=True)
    var = jnp.mean(zc * zc, axis=-1, keepdims=True)
    x_new = zc * lax.rsqrt(var + NORM_EPS) * g_ref[...] + b_ref[...]
    o_ref[0] = x_new
    if u_ref:
        u_ref[0][0] = _modulate(x_new, nxt_ref, b, n_batch, is_ctx)


def _out_call(merged, xs, mod, w_out, ln_g, ln_b, *, layer, last, n_batch, ctx_len, tm):
    bsz, t, d = xs.shape
    depth = mod.shape[0]
    blk0 = ctx_len // tm if last else 0
    assert not last or ctx_len % tm == 0
    rows = t - blk0 * tm
    kern = functools.partial(_out_kernel, tm=tm, row0=blk0 * tm, n_batch=n_batch,
                             ctx_len=ctx_len)
    tok_in = pl.BlockSpec((1, tm, d), lambda b, i: (b, i + blk0, 0))
    tok_out = pl.BlockSpec((1, tm, d), lambda b, i: (b, i, 0))
    mod_at = lambda l: pl.BlockSpec((None, SUBLANES, 3 * d), lambda b, i: (l, 0, 0))
    vec = pl.BlockSpec((None, 1, d), lambda b, i: (layer, 0, 0))
    x_shape = jax.ShapeDtypeStruct((bsz, rows, d), F32)
    return pl.pallas_call(
        kern,
        grid=(bsz, rows // tm),
        in_specs=[tok_in, tok_in, mod_at(layer), mod_at(min(layer + 1, depth - 1)),
                  pl.BlockSpec((None, d, d), lambda b, i: (layer, 0, 0),
                               pipeline_mode=pl.Buffered(1)),
                  vec, vec],
        out_specs=tok_out if last else [tok_out, tok_out],
        out_shape=x_shape if last else [x_shape, jax.ShapeDtypeStruct((bsz, rows, d), BF16)],
        compiler_params=_cparams(("parallel", "parallel")),
    )(merged, xs, mod, mod, w_out, ln_g, ln_b)


PACK_COLS = 256


def _pack_kernel(w_ref, o_ref):
    row = 0
    for name in ("aq",) + ZB_ORDER + ZF_ORDER:
        src, width = _REF_SPLITS[name]
        o_ref[0, row:row + width, :] = w_ref[0, src:src + width, :].astype(BF16)
        row += width
    o_ref[0, row:, :] = jnp.zeros((W_PACKED - row, o_ref.shape[2]), BF16)


def _pack_w_in(w_in):
    depth, d, width = w_in.shape
    w_t = jnp.swapaxes(w_in, 1, 2)
    return pl.pallas_call(
        _pack_kernel,
        grid=(depth, d // PACK_COLS),
        in_specs=[pl.BlockSpec((1, width, PACK_COLS), lambda l, i: (l, 0, i))],
        out_specs=pl.BlockSpec((1, W_PACKED, PACK_COLS), lambda l, i: (l, 0, i)),
        out_shape=jax.ShapeDtypeStruct((depth, W_PACKED, d), BF16),
        compiler_params=_cparams(("parallel", "parallel")),
    )(w_t)


def _rope_tables(rows, ctx_len):
    row = jnp.repeat(jnp.arange(rows, dtype=F32), GRID_W)
    col = jnp.tile(jnp.arange(GRID_W, dtype=F32), rows)
    n_freq = HEAD_DIM // 4
    inv = ROPE_BASE ** (-jnp.arange(n_freq, dtype=F32) / n_freq)
    ang_r = row[:, None] * inv
    ang_c = col[:, None] * inv
    ang = jnp.concatenate([ang_r, ang_r, ang_c, ang_c], -1)
    cos = jnp.concatenate([jnp.ones((ctx_len, HEAD_DIM), F32), jnp.cos(ang)], 0)
    sin = jnp.concatenate([jnp.zeros((ctx_len, HEAD_DIM), F32), jnp.sin(ang)], 0)
    first = (np.arange(HEAD_DIM) % (HEAD_DIM // 2)) < HEAD_DIM // 4
    sin_lo = jnp.where(first, -sin, 0.0)
    sin_hi = jnp.where(first, 0.0, sin)
    return cos, sin_lo, sin_hi


def _pad_lanes(v, width):
    return jnp.pad(v.reshape(1, -1), ((0, 0), (0, width - v.size)))


def _tile_choices(t, ctx_len):
    def largest(cap, mult, also=0):
        best = mult
        for cand in range(mult, cap + 1, mult):
            if t % cand == 0 and also % cand == 0:
                best = cand
        return best
    return dict(proj=largest(2112, 16), attn=largest(768, LANES), conv=largest(528, SUBLANES),
                merge=largest(256, 16), out=largest(528, 16),
                out_last=largest(528, 16, also=ctx_len))


def kernel(x, c, ctx, c_ctx, ada_w, ada_b, w_in, attn_q_norm, attn_k_norm, ret_log_decay, ret_gn_w,
           ssm_conv_w, ssm_conv_b, ssm_dt_bias, ssm_a_log, ssm_d, ssm_norm_w, w_branch, w_out,
           ln_g, ln_b):
    bsz, seq, d = x.shape
    ctx_len = ctx.shape[1]
    depth = w_in.shape[0]
    t = seq + ctx_len
    chunk = LANES
    ret_chunk = 256 if ctx_len % 256 == 0 and t % 256 == 0 else chunk
    assert d == D_MODEL and bsz < SUBLANES and seq % GRID_W == 0
    assert ctx_len % chunk == 0 and t % chunk == 0
    tiles = _tile_choices(t, ctx_len)

    xs = jnp.concatenate([ctx, x], axis=1)
    cvec = jnp.zeros((SUBLANES, d), F32).at[:bsz].set(c).at[bsz].set(c_ctx)
    mod = _ada_call(cvec, ada_w, ada_b)
    cos, sin_lo, sin_hi = _rope_tables(seq // GRID_W, ctx_len)
    w_packed = _pack_w_in(w_in)
    wb = w_branch.astype(BF16)
    wo = w_out.astype(BF16)
    conv_w = jnp.pad(ssm_conv_w, ((0, 0), (0, SUBLANES - SSM_CONV), (0, 0)))
    per_layer_row = lambda v: v.reshape(depth, 1, -1)
    qn, kn = per_layer_row(attn_q_norm), per_layer_row(attn_k_norm)
    gnw, snw = per_layer_row(ret_gn_w), per_layer_row(ssm_norm_w)
    lng, lnb = per_layer_row(ln_g), per_layer_row(ln_b)

    u = _modulate_call(xs, mod, layer=0, n_batch=bsz, ctx_len=ctx_len, tm=tiles["out"])
    for l in range(depth):
        zq, zb, zf = _inproj_call(u, w_packed, cos, sin_lo, sin_hi, qn, kn, layer=l,
                                  tm=tiles["proj"])
        attn = _attn_call(zq, zb, ctx_len=ctx_len, tq=tiles["attn"], tk=tiles["attn"])
        retf, retb = _ret_call(zb, ret_log_decay[l], ctx_len=ctx_len, chunk=ret_chunk)
        xbc = _conv_call(zf, conv_w[l], ssm_conv_b[l].reshape(1, -1), ctx_len=ctx_len,
                         tc=tiles["conv"])
        ssmf, ssmb = _ssd_call(xbc, zf, _pad_lanes(ssm_dt_bias[l], LANES),
                               _pad_lanes(ssm_a_log[l], LANES),
                               jnp.repeat(ssm_d[l], SSM_HEAD_DIM).reshape(1, -1),
                               ctx_len=ctx_len, chunk=chunk)
        merged = _merge_call(attn, retf, retb, ssmf, ssmb, zb, gnw, snw, wb, layer=l,
                             tm=tiles["merge"])
        last = l == depth - 1
        out = _out_call(merged, xs, mod, wo, lng, lnb, layer=l, last=last, n_batch=bsz,
                        ctx_len=ctx_len, tm=tiles["out_last" if last else "out"])
        if last:
            return out
        xs, u = out
```

```python
import functools

import numpy as np
import jax
import jax.numpy as jnp
from jax import lax
from jax.experimental import pallas as pl
from jax.experimental.pallas import tpu as pltpu

F32 = jnp.float32
BF16 = jnp.bfloat16

D_MODEL = 2048
GRID_W = 64
HEAD_DIM = 128
ATTN_HEADS = 8
ATTN_KV_HEADS = 2
ATTN_GROUP = ATTN_HEADS // ATTN_KV_HEADS
ROPE_BASE = 10000.0
RET_HEADS = 4
RET_QK_DIM = 128
RET_V_DIM = 256
SSM_INNER = 1024
SSM_HEAD_DIM = 64
SSM_HEADS = 16
SSM_GROUPS = 2
SSM_HPG = SSM_HEADS // SSM_GROUPS
SSM_STATE = 128
SSM_CONV = 5
N_BRANCHES = 3
BRANCH_WIDTH = 1024
DEPTH = 4
DEEPNORM_ALPHA = (2 * DEPTH) ** 0.25
NORM_EPS = 1e-6
ATTN_Q_SCALE = HEAD_DIM ** -0.5 * float(np.log2(np.e))

_REF_SPLITS = dict(aq=(0, 1024), ak=(1024, 256), av=(1280, 256), ag=(1536, 1024),
                   rq=(2560, 512), rk=(3072, 512), rv=(3584, 1024), rg=(4608, 1024),
                   sx=(5632, 1024), sb=(6656, 256), sc=(6912, 256), sdt=(7168, 32),
                   sz=(7200, 1024), mg=(8224, 6144))

LANES = 128
SUBLANES = 8
VMEM_LIMIT = 56 * 1024 * 1024

PROJ_TN = 512
ZQ_TILES = ATTN_HEADS * HEAD_DIM // PROJ_TN
ZB_ORDER = ("mg", "ag", "rg", "sz", "rv", "ak", "av", "rq", "rk")
ZB_MG_OFF, ZB_AG_OFF, ZB_RG_OFF, ZB_SZ_OFF = 0, 6144, 7168, 8192
ZB_RV_OFF, ZB_AK_OFF, ZB_AV_OFF, ZB_RQ_OFF, ZB_RK_OFF = 9216, 10240, 10496, 10752, 11264
ZB_W = 11776
ZB_TILES = ZB_W // PROJ_TN
ZB_RAW_TILES = ZB_AK_OFF // PROJ_TN
ZF_TILE0 = ZQ_TILES + ZB_TILES
ZF_ORDER = ("sx", "sb", "sc", "sdt")
ZF_USED = 1024 + 256 + 256 + 32
assert ZF_USED - (2048 - PROJ_TN) <= LANES
ZF_W = 2048
ZF_XBC_OFF = 0
ZF_XBC_W = 1536
ZF_DT_OFF = 1536
W_PACKED = ZQ_TILES * PROJ_TN + ZB_W + ZF_W


def _sigmoid(x):
    return 0.5 * jnp.tanh(0.5 * x) + 0.5


def _silu(x):
    return x * _sigmoid(x)


def _cparams(sem):
    return pltpu.CompilerParams(dimension_semantics=sem, vmem_limit_bytes=VMEM_LIMIT)


def _dot(a, b):
    return jnp.dot(a, b, preferred_element_type=F32)


def _dot_nt(a, b):
    return lax.dot_general(a, b, (((1,), (1,)), ((), ())), preferred_element_type=F32)


def _dot_tn(a, b):
    return lax.dot_general(a, b, (((0,), (0,)), ((), ())), preferred_element_type=F32)


def _ada_kernel(c_ref, w_ref, b_ref, o_ref):
    s = _silu(c_ref[...]).astype(BF16)
    o_ref[0] = _dot(s, w_ref[0].astype(BF16)) + b_ref[0]


def _ada_call(cvec, ada_w, ada_b):
    depth, d, n = ada_w.shape
    tn = 768
    return pl.pallas_call(
        _ada_kernel,
        grid=(depth, n // tn),
        in_specs=[pl.BlockSpec((SUBLANES, d), lambda l, j: (0, 0)),
                  pl.BlockSpec((1, d, tn), lambda l, j: (l, 0, j)),
                  pl.BlockSpec((1, 1, tn), lambda l, j: (l, 0, j))],
        out_specs=pl.BlockSpec((1, SUBLANES, tn), lambda l, j: (l, 0, j)),
        out_shape=jax.ShapeDtypeStruct((depth, SUBLANES, n), F32),
        compiler_params=_cparams(("parallel", "parallel")),
    )(cvec, ada_w, ada_b.reshape(depth, 1, n))


def _rope(x, cos, sin_lo, sin_hi):
    return x * cos + pltpu.roll(x, 96, 1) * sin_lo + pltpu.roll(x, 32, 1) * sin_hi


def _modulation_rows(mod_ref, b, n_batch, is_ctx, lo, hi):
    mb = mod_ref[pl.ds(b, 1), lo:hi]
    mc = mod_ref[pl.ds(n_batch, 1), lo:hi]
    return jnp.where(is_ctx, mc, mb)


def _inproj_kernel(u_ref, w_ref, cos_ref, sl_ref, sh_ref, qn_ref, kn_ref,
                   zq_ref, zb_ref, zf_ref):
    j = pl.program_id(2)
    jb = j - ZQ_TILES

    def project():
        return _dot_nt(u_ref[0], w_ref[...])

    def rope(xh):
        return _rope(xh, cos_ref[...], sl_ref[...], sh_ref[...])

    def normed(w_row):
        def fn(xh):
            ms = jnp.mean(xh * xh, axis=-1, keepdims=True)
            return rope(xh * lax.rsqrt(ms + NORM_EPS) * w_row)
        return fn

    last = pl.num_programs(2) - 1

    @pl.when((j >= ZF_TILE0) & (j < last))
    def _():
        zf_ref[0] = project()

    @pl.when(j == last)
    def _():
        tm = u_ref.shape[1]
        zf_ref[0, :, :LANES] = _dot_nt(u_ref[0], w_ref[0:LANES, :])
        zf_ref[0, :, LANES:] = jnp.zeros((tm, PROJ_TN - LANES), F32)

    @pl.when((jb >= 0) & (jb < ZB_RAW_TILES))
    def _():
        zb_ref[0] = project().astype(BF16)

    @pl.when((j < ZQ_TILES) | ((jb >= ZB_RAW_TILES) & (j < ZF_TILE0)))
    def _():
        acc = project()

        def head(h):
            return acc[:, h * LANES:(h + 1) * LANES]

        def per_head(fn, n):
            return jnp.concatenate([fn(head(h)) for h in range(n)], axis=1)

        @pl.when(j < ZQ_TILES)
        def _():
            fn = normed(qn_ref[...])
            for h in range(ATTN_GROUP):
                zq_ref[0, h] = (fn(head(h)) * ATTN_Q_SCALE).astype(BF16)

        @pl.when(jb == ZB_AK_OFF // PROJ_TN)
        def _():
            k = per_head(normed(kn_ref[...]), 2)
            zb_ref[0] = jnp.concatenate([k, acc[:, 2 * LANES:]], axis=1).astype(BF16)

        @pl.when(jb == ZB_RQ_OFF // PROJ_TN)
        def _():
            zb_ref[0] = per_head(rope, 4).astype(BF16)

        @pl.when(jb == ZB_RK_OFF // PROJ_TN)
        def _():
            zb_ref[0] = per_head(lambda xh: rope(xh * (RET_QK_DIM ** -0.5)), 4).astype(BF16)


def _inproj_call(u, w, cos, sin_lo, sin_hi, qn, kn, *, layer, tm):
    bsz, t, d = u.shape
    rope_spec = pl.BlockSpec((tm, LANES), lambda b, i, j: (i, 0))
    vec_spec = pl.BlockSpec((None, 1, LANES), lambda b, i, j: (layer, 0, 0))
    return pl.pallas_call(
        _inproj_kernel,
        grid=(bsz, t // tm, W_PACKED // PROJ_TN),
        in_specs=[pl.BlockSpec((1, tm, d), lambda b, i, j: (b, i, 0)),
                  pl.BlockSpec((None, PROJ_TN, d), lambda b, i, j: (layer, j, 0)),
                  rope_spec, rope_spec, rope_spec, vec_spec, vec_spec],
        out_specs=[pl.BlockSpec((1, ATTN_GROUP, tm, HEAD_DIM),
                                lambda b, i, j: (b, jnp.minimum(j, ZQ_TILES - 1), i, 0)),
                   pl.BlockSpec((1, tm, PROJ_TN),
                                lambda b, i, j: (b, i, jnp.clip(j - ZQ_TILES, 0, ZB_TILES - 1))),
                   pl.BlockSpec((1, tm, PROJ_TN),
                                lambda b, i, j: (b, i, jnp.maximum(j - ZF_TILE0, 0)))],
        out_shape=[jax.ShapeDtypeStruct((bsz, ATTN_HEADS, t, HEAD_DIM), BF16),
                   jax.ShapeDtypeStruct((bsz, t, ZB_W), BF16),
                   jax.ShapeDtypeStruct((bsz, t, ZF_W), F32)],
        compiler_params=_cparams(("parallel", "parallel", "arbitrary")),
    )(u, w, cos, sin_lo, sin_hi, qn, kn)


NEG_BIG = -1e30
AHEAD = 2
ONES_ROWS = 2 * SUBLANES
BOUND_MARGIN = 1.02
MAX_SAFE_SHIFT = 50.0


def _attn_kernel(q_ref, k_ref, v_ref, o_ref, m_ref, acc_ref, s_ref, kmax_ref, *,
                 tq, tk, rb, t_len, ctx_len, unroll):
    qi = pl.program_id(2)
    nr = tq // rb
    per_kb = ATTN_GROUP * nr
    n_units = (t_len // tk) * per_kb

    @pl.when(qi == 0)
    def _():
        k = k_ref[0].astype(F32)
        kmax_ref[...] = jnp.broadcast_to(
            jnp.max(jnp.sum(k * k, axis=1, keepdims=True), axis=0, keepdims=True),
            kmax_ref.shape)

    q_all = q_ref[0].reshape(per_kb * rb, HEAD_DIM).astype(F32)
    qsq = _dot_nt(jnp.ones((SUBLANES, HEAD_DIM), BF16), (q_all * q_all).astype(BF16))[0:1]
    shift = jnp.sqrt(qsq * kmax_ref[0:1, 0:1]) * BOUND_MARGIN
    for slot in range(per_kb):
        m_ref[slot] = shift[:, slot * rb:(slot + 1) * rb]
    bounded = jnp.max(shift) <= MAX_SAFE_SHIFT
    acc_ref[...] = jnp.zeros(acc_ref.shape, F32)

    def unit(n):
        kb = n // per_kb
        return kb, n - kb * per_kb

    def scores(n):
        kb, slot = unit(n)
        h = slot // nr
        r0 = pl.multiple_of((slot - h * nr) * rb, rb)
        k0 = pl.multiple_of(kb * tk, tk)
        return _dot_nt(k_ref[0, pl.ds(k0, tk), :], q_ref[0, h, pl.ds(r0, rb), :])

    def update(n, s, mixed, fixed_shift):
        kb, slot = unit(n)
        if mixed:
            r0 = (slot - (slot // nr) * nr) * rb
            qrow = r0 + lax.broadcasted_iota(jnp.int32, (1, rb), 1)
            key = kb * tk + lax.broadcasted_iota(jnp.int32, (tk, 1), 0)
            s = jnp.where(qrow < ctx_len, jnp.where(key < ctx_len, s, NEG_BIG), s)
        if fixed_shift:
            p = jnp.exp2(s - m_ref[slot]).astype(BF16)
            acc_ref[slot] += _dot(v_ref[0, 0, kb], p)
        else:
            m_prev = m_ref[slot]
            m_new = jnp.maximum(m_prev, jnp.max(s, axis=0, keepdims=True))
            alpha = jnp.exp2(m_prev - m_new)
            p = jnp.exp2((s - m_new).astype(BF16))
            m_ref[slot] = m_new
            acc_ref[slot] = alpha * acc_ref[slot] + _dot(v_ref[0, 0, kb], p)

    def run(mixed, fixed_shift):
        if not fixed_shift:
            m_ref[...] = jnp.full(m_ref.shape, NEG_BIG, F32)
        for a in range(AHEAD):
            s_ref[a] = scores(a)

        def body(it, carry):
            n0 = it * unroll
            s = [s_ref[a] for a in range(AHEAD)]
            for u in range(0, unroll, AHEAD):
                s_next = [scores(jnp.minimum(n0 + u + AHEAD + a, n_units - 1))
                          for a in range(AHEAD)]
                for a in range(AHEAD):
                    update(n0 + u + a, s[a], mixed, fixed_shift)
                s = s_next
            for a in range(AHEAD):
                s_ref[a] = s[a]
            return carry

        lax.fori_loop(0, n_units // unroll, body, 0)

    first = qi == 0
    later = jnp.logical_not(first)
    unbounded = jnp.logical_not(bounded)
    pl.when(first & bounded)(lambda: run(True, True))
    pl.when(later & bounded)(lambda: run(False, True))
    pl.when(first & unbounded)(lambda: run(True, False))
    pl.when(later & unbounded)(lambda: run(False, False))

    for h in range(ATTN_GROUP):
        for r in range(nr):
            slot = h * nr + r
            o_t = acc_ref[slot, :HEAD_DIM] * (1.0 / acc_ref[slot, HEAD_DIM:HEAD_DIM + 1])
            o_ref[0, r * rb:(r + 1) * rb, h * HEAD_DIM:(h + 1) * HEAD_DIM] = o_t.T


def _attn_call(zq, zb, *, ctx_len, tq, tk):
    bsz, t, _ = zb.shape
    assert ctx_len <= tq and ctx_len <= tk
    rb = 256 if tq % 256 == 0 else tq
    per_kb = ATTN_GROUP * (tq // rb)
    unroll = max(u for u in range(AHEAD, 13, AHEAD) if per_kb % u == 0)
    k_blk = ZB_AK_OFF // HEAD_DIM
    v_t = zb[:, :, ZB_AV_OFF:ZB_AV_OFF + ATTN_KV_HEADS * HEAD_DIM]
    v_t = v_t.reshape(bsz, t // tk, tk, ATTN_KV_HEADS, HEAD_DIM).transpose(0, 3, 1, 4, 2)
    v_t = jnp.concatenate([v_t, jnp.ones(v_t.shape[:3] + (ONES_ROWS, tk), v_t.dtype)], axis=3)
    kern = functools.partial(_attn_kernel, tq=tq, tk=tk, rb=rb, t_len=t, ctx_len=ctx_len,
                             unroll=unroll)
    return pl.pallas_call(
        kern,
        grid=(bsz, ATTN_KV_HEADS, t // tq),
        in_specs=[pl.BlockSpec((1, ATTN_GROUP, tq, HEAD_DIM), lambda b, g, qi: (b, g, qi, 0)),
                  pl.BlockSpec((1, t, HEAD_DIM), lambda b, g, qi: (b, 0, k_blk + g)),
                  pl.BlockSpec((1, 1, t // tk, HEAD_DIM + ONES_ROWS, tk),
                               lambda b, g, qi: (b, g, 0, 0, 0))],
        out_specs=pl.BlockSpec((1, tq, ATTN_GROUP * HEAD_DIM), lambda b, g, qi: (b, qi, g)),
        out_shape=jax.ShapeDtypeStruct((bsz, t, ATTN_HEADS * HEAD_DIM), F32),
        scratch_shapes=[pltpu.VMEM((per_kb, 1, rb), F32),
                        pltpu.VMEM((per_kb, HEAD_DIM + ONES_ROWS, rb), F32),
                        pltpu.VMEM((AHEAD, tk, rb), F32),
                        pltpu.VMEM((SUBLANES, LANES), F32)],
        compiler_params=_cparams(("parallel", "parallel", "arbitrary")),
    )(zq, zb, v_t)


def _bwd_chunk(p, n_ctx_chunks, n_chunks):
    return jnp.where(p < n_ctx_chunks, n_ctx_chunks - 1 - p, n_chunks + n_ctx_chunks - 1 - p)


def _ret_kernel(ld_ref, qf_ref, kf_ref, vf_ref, qb_ref, kb_ref, vb_ref, yf_ref, yb_ref,
                sf_ref, sb_ref, *, chunk):
    c = chunk
    p = pl.program_id(1)

    @pl.when(p == 0)
    def _():
        sf_ref[...] = jnp.zeros(sf_ref.shape, F32)
        sb_ref[...] = jnp.zeros(sb_ref.shape, F32)

    ii = lax.broadcasted_iota(jnp.int32, (c, c), 0)
    jj = lax.broadcasted_iota(jnp.int32, (c, c), 1)
    dist = (ii - jj).astype(F32)
    pos = lax.broadcasted_iota(jnp.int32, (c, 1), 0).astype(F32)
    full = jnp.full((1, 1), float(c), F32)

    for h in range(RET_HEADS):
        ldf = ld_ref[0, h]
        ldb = ld_ref[1, h]
        qs = slice(h * RET_QK_DIM, (h + 1) * RET_QK_DIM)
        vs = slice(h * RET_V_DIM, (h + 1) * RET_V_DIM)
        decay = jnp.where(dist >= 0, jnp.exp(ldf * dist), 0.0) \
            + jnp.where(dist <= 0, jnp.exp(-ldb * dist), 0.0)
        q = qf_ref[0, :, qs]
        k = kf_ref[0, :, qs]
        v = vf_ref[0, :, vs]
        y = _dot((_dot_nt(q, k) * decay).astype(BF16), v)
        y = y + _dot(q, sf_ref[h].astype(BF16)) * jnp.exp(ldf * (pos + 1.0))
        yf_ref[0, :, vs] = y
        wv = (v.astype(F32) * jnp.exp(ldf * (c - 1.0 - pos))).astype(BF16)
        sf_ref[h] = sf_ref[h] * jnp.exp(ldf * full) + _dot_tn(k, wv)
        q = qb_ref[0, :, qs]
        k = kb_ref[0, :, qs]
        v = vb_ref[0, :, vs]
        yb_ref[0, :, vs] = _dot(q, sb_ref[h].astype(BF16)) * jnp.exp(ldb * (c - pos))
        wv = (v.astype(F32) * jnp.exp(ldb * pos)).astype(BF16)
        sb_ref[h] = sb_ref[h] * jnp.exp(ldb * full) + _dot_tn(k, wv)


def _ret_call(zb, log_decay, *, ctx_len, chunk):
    bsz, t, _ = zb.shape
    nc = t // chunk
    ncc = ctx_len // chunk
    qw = RET_HEADS * RET_QK_DIM
    vw = RET_HEADS * RET_V_DIM
    q_blk, k_blk, v_blk = ZB_RQ_OFF // qw, ZB_RK_OFF // qw, ZB_RV_OFF // vw
    bwd = functools.partial(_bwd_chunk, n_ctx_chunks=ncc, n_chunks=nc)
    kern = functools.partial(_ret_kernel, chunk=chunk)
    out = jax.ShapeDtypeStruct((bsz, t, vw), F32)
    return pl.pallas_call(
        kern,
        grid=(bsz, nc),
        in_specs=[pl.BlockSpec(memory_space=pltpu.SMEM),
                  pl.BlockSpec((1, chunk, qw), lambda b, p: (b, p, q_blk)),
                  pl.BlockSpec((1, chunk, qw), lambda b, p: (b, p, k_blk)),
                  pl.BlockSpec((1, chunk, vw), lambda b, p: (b, p, v_blk)),
                  pl.BlockSpec((1, chunk, qw), lambda b, p: (b, bwd(p), q_blk)),
                  pl.BlockSpec((1, chunk, qw), lambda b, p: (b, bwd(p), k_blk)),
                  pl.BlockSpec((1, chunk, vw), lambda b, p: (b, bwd(p), v_blk))],
        out_specs=[pl.BlockSpec((1, chunk, vw), lambda b, p: (b, p, 0)),
                   pl.BlockSpec((1, chunk, vw), lambda b, p: (b, bwd(p), 0))],
        out_shape=[out, out],
        scratch_shapes=[pltpu.VMEM((RET_HEADS, RET_QK_DIM, RET_V_DIM), F32),
                        pltpu.VMEM((RET_HEADS, RET_QK_DIM, RET_V_DIM), F32)],
        compiler_params=_cparams(("parallel", "arbitrary")),
    )(log_decay, zb, zb, zb, zb, zb, zb)


CONV_HALO = SUBLANES


def _conv_kernel(prev_ref, cur_ref, next_ref, w_ref, b_ref, o_ref, e_ref, *, tc, t_len, ctx_len):
    i = pl.program_id(1)
    e_ref[0:CONV_HALO] = prev_ref[0]
    e_ref[CONV_HALO:CONV_HALO + tc] = cur_ref[0]
    e_ref[CONV_HALO + tc:] = next_ref[0]
    pad = (SSM_CONV - 1) // 2
    lo = i * tc

    def conv(masked):
        tok = lo + lax.broadcasted_iota(jnp.int32, (tc, 1), 0)
        acc = jnp.broadcast_to(b_ref[...], (tc, ZF_XBC_W))
        for tap in range(SSM_CONV):
            off = tap - pad
            src = e_ref[CONV_HALO + off:CONV_HALO + off + tc, :]
            if masked:
                nb = tok + off
                same = ((tok - ctx_len) ^ (nb - ctx_len)) >= 0
                src = jnp.where(same & (nb >= 0) & (nb < t_len), src, 0.0)
            acc = acc + src * w_ref[tap:tap + 1, :]
        o_ref[0] = _silu(acc)

    def near(edge):
        return (lo - pad < edge) & (edge < lo + tc + pad)
    touches = near(0) | near(ctx_len) | near(t_len)
    pl.when(touches)(lambda: conv(True))
    pl.when(jnp.logical_not(touches))(lambda: conv(False))


def _conv_call(zf, conv_w, conv_b, *, ctx_len, tc):
    bsz, t, _ = zf.shape
    col = ZF_XBC_OFF // ZF_XBC_W
    per = tc // CONV_HALO
    last = t // CONV_HALO - 1
    kern = functools.partial(_conv_kernel, tc=tc, t_len=t, ctx_len=ctx_len)
    return pl.pallas_call(
        kern,
        grid=(bsz, t // tc),
        in_specs=[pl.BlockSpec((1, CONV_HALO, ZF_XBC_W),
                               lambda b, i: (b, jnp.maximum(i * per - 1, 0), col)),
                  pl.BlockSpec((1, tc, ZF_XBC_W), lambda b, i: (b, i, col)),
                  pl.BlockSpec((1, CONV_HALO, ZF_XBC_W),
                               lambda b, i: (b, jnp.minimum((i + 1) * per, last), col)),
                  pl.BlockSpec((SUBLANES, ZF_XBC_W), lambda b, i: (0, 0)),
                  pl.BlockSpec((1, ZF_XBC_W), lambda b, i: (0, 0))],
        out_specs=pl.BlockSpec((1, tc, ZF_XBC_W), lambda b, i: (b, i, 0)),
        out_shape=jax.ShapeDtypeStruct((bsz, t, ZF_XBC_W), F32),
        scratch_shapes=[pltpu.VMEM((tc + 2 * CONV_HALO, ZF_XBC_W), F32)],
        compiler_params=_cparams(("parallel", "parallel")),
    )(zf, zf, zf, conv_w, conv_b)


def _split3(x):
    p1 = x.astype(BF16)
    r1 = x - p1.astype(F32)
    p2 = r1.astype(BF16)
    p3 = (r1 - p2.astype(F32)).astype(BF16)
    return p1, p2, p3


def _cumsum_rows(x, tri):
    p1, p2, p3 = _split3(x)
    return _dot(tri, p1) + _dot(tri, p2) + _dot(tri, p3)


def _expand_heads(a, sel_ref):
    hi = a.astype(BF16)
    lo = (a - hi.astype(F32)).astype(BF16)
    return _dot(jnp.concatenate([hi, lo], axis=1), sel_ref[...])


def _head_selector(off):
    sel = np.zeros((LANES, SSM_INNER), np.float32)
    for h in range(SSM_HEADS):
        sel[off + h, h * SSM_HEAD_DIM:(h + 1) * SSM_HEAD_DIM] = 1.0
    return jnp.asarray(np.concatenate([sel, sel], axis=0), BF16)


def _ssd_kernel(xf_ref, df_ref, xb_ref, db_ref, bias_ref, alog_ref, skip_ref, self_ref, selb_ref,
                yf_ref, yb_ref, sf_ref, sb_ref, *, chunk, sub):
    c = chunk
    p = pl.program_id(1)
    nh = SSM_HEADS
    gw = SSM_HPG * SSM_HEAD_DIM
    bc_w = SSM_GROUPS * SSM_STATE

    @pl.when(p == 0)
    def _():
        sf_ref[...] = jnp.zeros(sf_ref.shape, F32)
        sb_ref[...] = jnp.zeros(sb_ref.shape, F32)

    ii = lax.broadcasted_iota(jnp.int32, (c, c), 0)
    jj = lax.broadcasted_iota(jnp.int32, (c, c), 1)
    lower = ii >= jj
    upper = jj >= ii
    tri = jnp.where(lower, 1.0, 0.0).astype(BF16)
    lane_lo = lax.broadcasted_iota(jnp.int32, (c, LANES), 1) < SSM_HEAD_DIM
    lane_hi = jnp.logical_not(lane_lo)

    def operands(x_ref, rows):
        xs = x_ref[0, rows, 0:SSM_INNER]
        bm = x_ref[0, rows, SSM_INNER:SSM_INNER + bc_w].astype(BF16)
        cm = x_ref[0, rows, SSM_INNER + bc_w:].astype(BF16)
        return xs, bm, cm

    def gates(d_ref, rows):
        raw = d_ref[0, rows, :] + bias_ref[...]
        dt = jnp.maximum(raw, 0.0) + jnp.log1p(jnp.exp(-jnp.abs(raw)))
        la = dt * -jnp.exp(alog_ref[...])
        acum = _cumsum_rows(la, tri)
        return dt, acum, acum - la

    def within(xs, bm, cm, dt, acum, excl):
        dt_t = dt.T
        acum_t = acum.T
        excl_t = excl.T
        xs_b = xs.astype(BF16)
        pieces = []
        for g in range(SSM_GROUPS):
            gs = slice(g * SSM_STATE, (g + 1) * SSM_STATE)
            scores = _dot_nt(cm[:, gs], bm[:, gs])
            for m in range(SSM_HPG // 2):
                pair = g * SSM_HPG // 2 + m
                xp = xs_b[:, pair * LANES:(pair + 1) * LANES]
                acc = None
                for e in range(2):
                    hf = 2 * pair + e
                    hb = nh + hf
                    dec_f = jnp.where(lower, jnp.exp(acum[:, hf:hf + 1] - acum_t[hf:hf + 1, :]), 0.0)
                    dec_b = jnp.where(upper, jnp.exp(excl_t[hb:hb + 1, :] - excl[:, hb:hb + 1]), 0.0)
                    wm = scores * (dec_f * dt_t[hf:hf + 1, :] + dec_b * dt_t[hb:hb + 1, :])
                    xh = jnp.where(lane_lo if e == 0 else lane_hi, xp, 0.0)
                    part = _dot(wm.astype(BF16), xh)
                    acc = part if acc is None else acc + part
                pieces.append(acc)
        return jnp.concatenate(pieces, axis=1)

    def carried(s_ref, xs, bm, cm, carry, wdt, keep):
        xw = (xs * wdt).astype(BF16)
        inter = []
        for g in range(SSM_GROUPS):
            gs = slice(g * SSM_STATE, (g + 1) * SSM_STATE)
            cs = slice(g * gw, (g + 1) * gw)
            inter.append(_dot(cm[:, gs], s_ref[g].astype(BF16)))
            s_ref[g] = s_ref[g] * keep[:, cs] + _dot_tn(bm[:, gs], xw[:, cs])
        return jnp.concatenate(inter, axis=1) * carry

    rows = [pl.ds(i * c, c) for i in range(sub)]
    fwd = [operands(xf_ref, r) + gates(df_ref, r) for r in rows]
    bwd = [operands(xb_ref, r) + gates(db_ref, r) for r in rows]

    fwd_scale = []
    for xs, bm, cm, dt, acum, excl in fwd:
        last = acum[c - 1:c, :]
        both = _expand_heads(
            jnp.concatenate([jnp.exp(acum), jnp.exp(last - acum) * dt], axis=0), self_ref)
        fwd_scale.append((both[:c], both[c:], both[c - 1:c]))
    bwd_scale = []
    for xs, bm, cm, dt, acum, excl in bwd:
        total = acum[c - 1:c, :]
        both = _expand_heads(
            jnp.concatenate([jnp.exp(total - excl), jnp.exp(excl) * dt], axis=0), selb_ref)
        bwd_scale.append((both[:c], both[c:], both[0:1]))

    y_within = [within(*f) for f in fwd]

    for i in range(sub):
        xs, bm, cm = fwd[i][:3]
        y = y_within[i] + carried(sf_ref, xs, bm, cm, *fwd_scale[i]) + skip_ref[...] * xs
        yf_ref[0, rows[i], :] = y
    for i in reversed(range(sub)):
        xs, bm, cm = bwd[i][:3]
        yb_ref[0, rows[i], :] = carried(sb_ref, xs, bm, cm, *bwd_scale[i])


def _ssd_call(xbc, zf, bias_row, alog_row, skip_row, *, ctx_len, chunk):
    bsz, t, _ = xbc.shape
    sub = 2 if ctx_len % (2 * chunk) == 0 and t % (2 * chunk) == 0 else 1
    blk = sub * chunk
    nblk = t // blk
    dt_blk = ZF_DT_OFF // LANES
    bwd = functools.partial(_bwd_chunk, n_ctx_chunks=ctx_len // blk, n_chunks=nblk)
    kern = functools.partial(_ssd_kernel, chunk=chunk, sub=sub)
    out = jax.ShapeDtypeStruct((bsz, t, SSM_INNER), F32)
    row = lambda w: pl.BlockSpec((1, w), lambda b, p: (0, 0))
    return pl.pallas_call(
        kern,
        grid=(bsz, nblk),
        in_specs=[pl.BlockSpec((1, blk, ZF_XBC_W), lambda b, p: (b, p, 0)),
                  pl.BlockSpec((1, blk, LANES), lambda b, p: (b, p, dt_blk)),
                  pl.BlockSpec((1, blk, ZF_XBC_W), lambda b, p: (b, bwd(p), 0)),
                  pl.BlockSpec((1, blk, LANES), lambda b, p: (b, bwd(p), dt_blk)),
                  row(LANES), row(LANES), row(SSM_INNER),
                  pl.BlockSpec((2 * LANES, SSM_INNER), lambda b, p: (0, 0)),
                  pl.BlockSpec((2 * LANES, SSM_INNER), lambda b, p: (0, 0))],
        out_specs=[pl.BlockSpec((1, blk, SSM_INNER), lambda b, p: (b, p, 0)),
                   pl.BlockSpec((1, blk, SSM_INNER), lambda b, p: (b, bwd(p), 0))],
        out_shape=[out, out],
        scratch_shapes=[pltpu.VMEM((SSM_GROUPS, SSM_STATE, SSM_HPG * SSM_HEAD_DIM), F32),
                        pltpu.VMEM((SSM_GROUPS, SSM_STATE, SSM_HPG * SSM_HEAD_DIM), F32)],
        compiler_params=_cparams(("parallel", "arbitrary")),
    )(xbc, zf, xbc, zf, bias_row, alog_row, skip_row,
      _head_selector(0), _head_selector(SSM_HEADS))


def _merge_kernel(attn_ref, retf_ref, retb_ref, ssmf_ref, ssmb_ref, mg_ref, ag_ref, rg_ref,
                  sz_ref, gnw_ref, snw_ref, wb_ref, o_ref):
    d = D_MODEL
    gate_in = lambda ref, lo=0, hi=None: ref[0, :, lo:hi].astype(F32)
    a = attn_ref[0] * _silu(gate_in(ag_ref))

    ret = retf_ref[0] + retb_ref[0]
    parts = []
    for h in range(RET_HEADS):
        r = ret[:, h * RET_V_DIM:(h + 1) * RET_V_DIM]
        rc = r - jnp.mean(r, axis=-1, keepdims=True)
        var = jnp.mean(rc * rc, axis=-1, keepdims=True)
        parts.append(rc * lax.rsqrt(var + NORM_EPS))
    r = jnp.concatenate(parts, axis=1) * gnw_ref[...] * _silu(gate_in(rg_ref))

    s = (ssmf_ref[0] + ssmb_ref[0]) * _silu(gate_in(sz_ref))
    gwid = SSM_INNER // SSM_GROUPS
    parts = []
    for g in range(SSM_GROUPS):
        sg = s[:, g * gwid:(g + 1) * gwid]
        parts.append(sg * lax.rsqrt(jnp.mean(sg * sg, axis=-1, keepdims=True) + NORM_EPS))
    s = jnp.concatenate(parts, axis=1) * snw_ref[...]

    merged = _sigmoid(gate_in(mg_ref, 0, d)) * _dot(a.astype(BF16), wb_ref[0])
    merged = merged + _sigmoid(gate_in(mg_ref, d, 2 * d)) * _dot(r.astype(BF16), wb_ref[1])
    merged = merged + _sigmoid(gate_in(mg_ref, 2 * d, 3 * d)) * _dot(s.astype(BF16), wb_ref[2])
    o_ref[0] = merged.astype(BF16)


def _merge_call(attn, retf, retb, ssmf, ssmb, zb, gnw, snw, wb, *, layer, tm):
    bsz, t, _ = attn.shape
    d = D_MODEL
    bw = BRANCH_WIDTH
    act = lambda blk: pl.BlockSpec((1, tm, bw), lambda b, i: (b, i, blk))
    vec = pl.BlockSpec((None, 1, bw), lambda b, i: (layer, 0, 0))
    return pl.pallas_call(
        _merge_kernel,
        grid=(bsz, t // tm),
        in_specs=[act(0), act(0), act(0), act(0), act(0),
                  pl.BlockSpec((1, tm, N_BRANCHES * d),
                               lambda b, i: (b, i, ZB_MG_OFF // (N_BRANCHES * d))),
                  act(ZB_AG_OFF // bw), act(ZB_RG_OFF // bw), act(ZB_SZ_OFF // bw),
                  vec, vec,
                  pl.BlockSpec((None, N_BRANCHES, bw, d), lambda b, i: (layer, 0, 0, 0),
                               pipeline_mode=pl.Buffered(1))],
        out_specs=pl.BlockSpec((1, tm, d), lambda b, i: (b, i, 0)),
        out_shape=jax.ShapeDtypeStruct((bsz, t, d), BF16),
        compiler_params=_cparams(("parallel", "parallel")),
    )(attn, retf, retb, ssmf, ssmb, zb, zb, zb, zb, gnw, snw, wb)


def _modulate(x, mod_ref, b, n_batch, is_ctx):
    d = D_MODEL
    shift = _modulation_rows(mod_ref, b, n_batch, is_ctx, 0, d)
    scale = _modulation_rows(mod_ref, b, n_batch, is_ctx, d, 2 * d)
    return (x * (1.0 + scale) + shift).astype(BF16)


def _modulate_kernel(x_ref, mod_ref, u_ref, *, tm, n_batch, ctx_len):
    row = pl.program_id(1) * tm + lax.broadcasted_iota(jnp.int32, (tm, 1), 0)
    u_ref[0] = _modulate(x_ref[0], mod_ref, pl.program_id(0), n_batch, row < ctx_len)


def _modulate_call(xs, mod, *, layer, n_batch, ctx_len, tm):
    bsz, t, d = xs.shape
    kern = functools.partial(_modulate_kernel, tm=tm, n_batch=n_batch, ctx_len=ctx_len)
    tok = pl.BlockSpec((1, tm, d), lambda b, i: (b, i, 0))
    return pl.pallas_call(
        kern,
        grid=(bsz, t // tm),
        in_specs=[tok, pl.BlockSpec((None, SUBLANES, 3 * d), lambda b, i: (layer, 0, 0))],
        out_specs=tok,
        out_shape=jax.ShapeDtypeStruct((bsz, t, d), BF16),
        compiler_params=_cparams(("parallel", "parallel")),
    )(xs, mod)


def _out_kernel(m_ref, x_ref, mod_ref, nxt_ref, w_ref, g_ref, b_ref, o_ref, *u_ref,
                tm, row0, n_batch, ctx_len):
    b = pl.program_id(0)
    d = D_MODEL
    row = row0 + pl.program_id(1) * tm + lax.broadcasted_iota(jnp.int32, (tm, 1), 0)
    is_ctx = row < ctx_len
    gate = _modulation_rows(mod_ref, b, n_batch, is_ctx, 2 * d, 3 * d)
    y = _dot(m_ref[0], w_ref[...])
    z = DEEPNORM_ALPHA * x_ref[0] + gate * y
    zc = z - jnp.mean(z, axis=-1, keepdims=True)
    var = jnp.mean(zc * zc, axis=-1, keepdims=True)
    x_new = zc * lax.rsqrt(var + NORM_EPS) * g_ref[...] + b_ref[...]
    o_ref[0] = x_new
    if u_ref:
        u_ref[0][0] = _modulate(x_new, nxt_ref, b, n_batch, is_ctx)


def _out_call(merged, xs, mod, w_out, ln_g, ln_b, *, layer, last, n_batch, ctx_len, tm):
    bsz, t, d = xs.shape
    depth = mod.shape[0]
    blk0 = ctx_len // tm if last else 0
    assert not last or ctx_len % tm == 0
    rows = t - blk0 * tm
    kern = functools.partial(_out_kernel, tm=tm, row0=blk0 * tm, n_batch=n_batch,
                             ctx_len=ctx_len)
    tok_in = pl.BlockSpec((1, tm, d), lambda b, i: (b, i + blk0, 0))
    tok_out = pl.BlockSpec((1, tm, d), lambda b, i: (b, i, 0))
    mod_at = lambda l: pl.BlockSpec((None, SUBLANES, 3 * d), lambda b, i: (l, 0, 0))
    vec = pl.BlockSpec((None, 1, d), lambda b, i: (layer, 0, 0))
    x_shape = jax.ShapeDtypeStruct((bsz, rows, d), F32)
    return pl.pallas_call(
        kern,
        grid=(bsz, rows // tm),
        in_specs=[tok_in, tok_in, mod_at(layer), mod_at(min(layer + 1, depth - 1)),
                  pl.BlockSpec((None, d, d), lambda b, i: (layer, 0, 0),
                               pipeline_mode=pl.Buffered(1)),
                  vec, vec],
        out_specs=tok_out if last else [tok_out, tok_out],
        out_shape=x_shape if last else [x_shape, jax.ShapeDtypeStruct((bsz, rows, d), BF16)],
        compiler_params=_cparams(("parallel", "parallel")),
    )(merged, xs, mod, mod, w_out, ln_g, ln_b)


PACK_COLS = 256


def _pack_kernel(w_ref, o_ref):
    row = 0
    for name in ("aq",) + ZB_ORDER + ZF_ORDER:
        src, width = _REF_SPLITS[name]
        o_ref[0, row:row + width, :] = w_ref[0, src:src + width, :].astype(BF16)
        row += width
    o_ref[0, row:, :] = jnp.zeros((W_PACKED - row, o_ref.shape[2]), BF16)


def _pack_w_in(w_in):
    depth, d, width = w_in.shape
    w_t = jnp.swapaxes(w_in, 1, 2)
    return pl.pallas_call(
        _pack_kernel,
        grid=(depth, d // PACK_COLS),
        in_specs=[pl.BlockSpec((1, width, PACK_COLS), lambda l, i: (l, 0, i))],
        out_specs=pl.BlockSpec((1, W_PACKED, PACK_COLS), lambda l, i: (l, 0, i)),
        out_shape=jax.ShapeDtypeStruct((depth, W_PACKED, d), BF16),
        compiler_params=_cparams(("parallel", "parallel")),
    )(w_t)


def _rope_tables(rows, ctx_len):
    row = jnp.repeat(jnp.arange(rows, dtype=F32), GRID_W)
    col = jnp.tile(jnp.arange(GRID_W, dtype=F32), rows)
    n_freq = HEAD_DIM // 4
    inv = ROPE_BASE ** (-jnp.arange(n_freq, dtype=F32) / n_freq)
    ang_r = row[:, None] * inv
    ang_c = col[:, None] * inv
    ang = jnp.concatenate([ang_r, ang_r, ang_c, ang_c], -1)
    cos = jnp.concatenate([jnp.ones((ctx_len, HEAD_DIM), F32), jnp.cos(ang)], 0)
    sin = jnp.concatenate([jnp.zeros((ctx_len, HEAD_DIM), F32), jnp.sin(ang)], 0)
    first = (np.arange(HEAD_DIM) % (HEAD_DIM // 2)) < HEAD_DIM // 4
    sin_lo = jnp.where(first, -sin, 0.0)
    sin_hi = jnp.where(first, 0.0, sin)
    return cos, sin_lo, sin_hi


def _pad_lanes(v, width):
    return jnp.pad(v.reshape(1, -1), ((0, 0), (0, width - v.size)))


def _tile_choices(t, ctx_len):
    def largest(cap, mult, also=0):
        best = mult
        for cand in range(mult, cap + 1, mult):
            if t % cand == 0 and also % cand == 0:
                best = cand
        return best
    return dict(proj=largest(2112, 16), attn=largest(768, LANES), conv=largest(528, SUBLANES),
                merge=largest(256, 16), out=largest(528, 16),
                out_last=largest(528, 16, also=ctx_len))


def kernel(x, c, ctx, c_ctx, ada_w, ada_b, w_in, attn_q_norm, attn_k_norm, ret_log_decay, ret_gn_w,
           ssm_conv_w, ssm_conv_b, ssm_dt_bias, ssm_a_log, ssm_d, ssm_norm_w, w_branch, w_out,
           ln_g, ln_b):
    bsz, seq, d = x.shape
    ctx_len = ctx.shape[1]
    depth = w_in.shape[0]
    t = seq + ctx_len
    chunk = LANES
    ret_chunk = 256 if ctx_len % 256 == 0 and t % 256 == 0 else chunk
    assert d == D_MODEL and bsz < SUBLANES and seq % GRID_W == 0
    assert ctx_len % chunk == 0 and t % chunk == 0
    tiles = _tile_choices(t, ctx_len)

    xs = jnp.concatenate([ctx, x], axis=1)
    cvec = jnp.zeros((SUBLANES, d), F32).at[:bsz].set(c).at[bsz].set(c_ctx)
    mod = _ada_call(cvec, ada_w, ada_b)
    cos, sin_lo, sin_hi = _rope_tables(seq // GRID_W, ctx_len)
    w_packed = _pack_w_in(w_in)
    wb = w_branch.astype(BF16)
    wo = w_out.astype(BF16)
    conv_w = jnp.pad(ssm_conv_w, ((0, 0), (0, SUBLANES - SSM_CONV), (0, 0)))
    per_layer_row = lambda v: v.reshape(depth, 1, -1)
    qn, kn = per_layer_row(attn_q_norm), per_layer_row(attn_k_norm)
    gnw, snw = per_layer_row(ret_gn_w), per_layer_row(ssm_norm_w)
    lng, lnb = per_layer_row(ln_g), per_layer_row(ln_b)

    u = _modulate_call(xs, mod, layer=0, n_batch=bsz, ctx_len=ctx_len, tm=tiles["out"])
    for l in range(depth):
        zq, zb, zf = _inproj_call(u, w_packed, cos, sin_lo, sin_hi, qn, kn, layer=l,
                                  tm=tiles["proj"])
        attn = _attn_call(zq, zb, ctx_len=ctx_len, tq=tiles["attn"], tk=tiles["attn"])
        retf, retb = _ret_call(zb, ret_log_decay[l], ctx_len=ctx_len, chunk=ret_chunk)
        xbc = _conv_call(zf, conv_w[l], ssm_conv_b[l].reshape(1, -1), ctx_len=ctx_len,
                         tc=tiles["conv"])
        ssmf, ssmb = _ssd_call(xbc, zf, _pad_lanes(ssm_dt_bias[l], LANES),
                               _pad_lanes(ssm_a_log[l], LANES),
                               jnp.repeat(ssm_d[l], SSM_HEAD_DIM).reshape(1, -1),
                               ctx_len=ctx_len, chunk=chunk)
        merged = _merge_call(attn, retf, retb, ssmf, ssmb, zb, gnw, snw, wb, layer=l,
                             tm=tiles["merge"])
        last = l == depth - 1
        out = _out_call(merged, xs, mod, wo, lng, lnb, layer=l, last=last, n_batch=bsz,
                        ctx_len=ctx_len, tm=tiles["out_last" if last else "out"])
        if last:
            return out
        xs, u = out
```

```python
import functools

import numpy as np
import jax
import jax.numpy as jnp
from jax import lax
from jax.experimental import pallas as pl
from jax.experimental.pallas import tpu as pltpu

F32 = jnp.float32
BF16 = jnp.bfloat16

D_MODEL = 2048
GRID_W = 64
HEAD_DIM = 128
ATTN_HEADS = 8
ATTN_KV_HEADS = 2
ATTN_GROUP = ATTN_HEADS // ATTN_KV_HEADS
ROPE_BASE = 10000.0
RET_HEADS = 4
RET_QK_DIM = 128
RET_V_DIM = 256
SSM_INNER = 1024
SSM_HEAD_DIM = 64
SSM_HEADS = 16
SSM_GROUPS = 2
SSM_HPG = SSM_HEADS // SSM_GROUPS
SSM_STATE = 128
SSM_CONV = 5
N_BRANCHES = 3
BRANCH_WIDTH = 1024
DEPTH = 4
DEEPNORM_ALPHA = (2 * DEPTH) ** 0.25
NORM_EPS = 1e-6
ATTN_Q_SCALE = HEAD_DIM ** -0.5 * float(np.log2(np.e))

_REF_SPLITS = dict(aq=(0, 1024), ak=(1024, 256), av=(1280, 256), ag=(1536, 1024),
                   rq=(2560, 512), rk=(3072, 512), rv=(3584, 1024), rg=(4608, 1024),
                   sx=(5632, 1024), sb=(6656, 256), sc=(6912, 256), sdt=(7168, 32),
                   sz=(7200, 1024), mg=(8224, 6144))

LANES = 128
SUBLANES = 8
VMEM_LIMIT = 56 * 1024 * 1024

PROJ_TN = 512
ZQ_TILES = ATTN_HEADS * HEAD_DIM // PROJ_TN
ZB_ORDER = ("mg", "ag", "rg", "sz", "rv", "ak", "av", "rq", "rk")
ZB_MG_OFF, ZB_AG_OFF, ZB_RG_OFF, ZB_SZ_OFF = 0, 6144, 7168, 8192
ZB_RV_OFF, ZB_AK_OFF, ZB_AV_OFF, ZB_RQ_OFF, ZB_RK_OFF = 9216, 10240, 10496, 10752, 11264
ZB_W = 11776
ZB_TILES = ZB_W // PROJ_TN
ZB_RAW_TILES = ZB_AK_OFF // PROJ_TN
ZF_TILE0 = ZQ_TILES + ZB_TILES
ZF_ORDER = ("sx", "sb", "sc", "sdt")
ZF_USED = 1024 + 256 + 256 + 32
ZF_W = 2048
ZF_XBC_OFF = 0
ZF_XBC_W = 1536
ZF_DT_OFF = 1536
W_PACKED = ZQ_TILES * PROJ_TN + ZB_W + ZF_W


def _sigmoid(x):
    return 0.5 * jnp.tanh(0.5 * x) + 0.5


def _silu(x):
    return x * _sigmoid(x)


def _cparams(sem):
    return pltpu.CompilerParams(dimension_semantics=sem, vmem_limit_bytes=VMEM_LIMIT)


def _dot(a, b):
    return jnp.dot(a, b, preferred_element_type=F32)


def _dot_nt(a, b):
    return lax.dot_general(a, b, (((1,), (1,)), ((), ())), preferred_element_type=F32)


def _dot_tn(a, b):
    return lax.dot_general(a, b, (((0,), (0,)), ((), ())), preferred_element_type=F32)


def _ada_kernel(c_ref, w_ref, b_ref, o_ref):
    s = _silu(c_ref[...]).astype(BF16)
    o_ref[0] = _dot(s, w_ref[0].astype(BF16)) + b_ref[0]


def _ada_call(cvec, ada_w, ada_b):
    depth, d, n = ada_w.shape
    tn = 768
    return pl.pallas_call(
        _ada_kernel,
        grid=(depth, n // tn),
        in_specs=[pl.BlockSpec((SUBLANES, d), lambda l, j: (0, 0)),
                  pl.BlockSpec((1, d, tn), lambda l, j: (l, 0, j)),
                  pl.BlockSpec((1, 1, tn), lambda l, j: (l, 0, j))],
        out_specs=pl.BlockSpec((1, SUBLANES, tn), lambda l, j: (l, 0, j)),
        out_shape=jax.ShapeDtypeStruct((depth, SUBLANES, n), F32),
        compiler_params=_cparams(("parallel", "parallel")),
    )(cvec, ada_w, ada_b.reshape(depth, 1, n))


def _rope(x, cos, sin_lo, sin_hi):
    return x * cos + pltpu.roll(x, 96, 1) * sin_lo + pltpu.roll(x, 32, 1) * sin_hi


def _modulation_rows(mod_ref, b, n_batch, is_ctx, lo, hi):
    mb = mod_ref[pl.ds(b, 1), lo:hi]
    mc = mod_ref[pl.ds(n_batch, 1), lo:hi]
    return jnp.where(is_ctx, mc, mb)


def _inproj_kernel(u_ref, w_ref, cos_ref, sl_ref, sh_ref, qn_ref, kn_ref,
                   zq_ref, zb_ref, zf_ref):
    j = pl.program_id(2)
    jb = j - ZQ_TILES

    def project():
        return _dot_nt(u_ref[0], w_ref[...])

    def rope(xh):
        return _rope(xh, cos_ref[...], sl_ref[...], sh_ref[...])

    def normed(w_row):
        def fn(xh):
            ms = jnp.mean(xh * xh, axis=-1, keepdims=True)
            return rope(xh * lax.rsqrt(ms + NORM_EPS) * w_row)
        return fn

    @pl.when(j >= ZF_TILE0)
    def _():
        zf_ref[0] = project()

    @pl.when((jb >= 0) & (jb < ZB_RAW_TILES))
    def _():
        zb_ref[0] = project().astype(BF16)

    @pl.when((j < ZQ_TILES) | ((jb >= ZB_RAW_TILES) & (j < ZF_TILE0)))
    def _():
        acc = project()

        def head(h):
            return acc[:, h * LANES:(h + 1) * LANES]

        def per_head(fn, n):
            return jnp.concatenate([fn(head(h)) for h in range(n)], axis=1)

        @pl.when(j < ZQ_TILES)
        def _():
            fn = normed(qn_ref[...])
            for h in range(ATTN_GROUP):
                zq_ref[0, h] = (fn(head(h)) * ATTN_Q_SCALE).astype(BF16)

        @pl.when(jb == ZB_AK_OFF // PROJ_TN)
        def _():
            k = per_head(normed(kn_ref[...]), 2)
            zb_ref[0] = jnp.concatenate([k, acc[:, 2 * LANES:]], axis=1).astype(BF16)

        @pl.when(jb == ZB_RQ_OFF // PROJ_TN)
        def _():
            zb_ref[0] = per_head(rope, 4).astype(BF16)

        @pl.when(jb == ZB_RK_OFF // PROJ_TN)
        def _():
            zb_ref[0] = per_head(lambda xh: rope(xh * (RET_QK_DIM ** -0.5)), 4).astype(BF16)


def _inproj_call(u, w, cos, sin_lo, sin_hi, qn, kn, *, layer, tm):
    bsz, t, d = u.shape
    rope_spec = pl.BlockSpec((tm, LANES), lambda b, i, j: (i, 0))
    vec_spec = pl.BlockSpec((None, 1, LANES), lambda b, i, j: (layer, 0, 0))
    return pl.pallas_call(
        _inproj_kernel,
        grid=(bsz, t // tm, W_PACKED // PROJ_TN),
        in_specs=[pl.BlockSpec((1, tm, d), lambda b, i, j: (b, i, 0)),
                  pl.BlockSpec((None, PROJ_TN, d), lambda b, i, j: (layer, j, 0)),
                  rope_spec, rope_spec, rope_spec, vec_spec, vec_spec],
        out_specs=[pl.BlockSpec((1, ATTN_GROUP, tm, HEAD_DIM),
                                lambda b, i, j: (b, jnp.minimum(j, ZQ_TILES - 1), i, 0)),
                   pl.BlockSpec((1, tm, PROJ_TN),
                                lambda b, i, j: (b, i, jnp.clip(j - ZQ_TILES, 0, ZB_TILES - 1))),
                   pl.BlockSpec((1, tm, PROJ_TN),
                                lambda b, i, j: (b, i, jnp.maximum(j - ZF_TILE0, 0)))],
        out_shape=[jax.ShapeDtypeStruct((bsz, ATTN_HEADS, t, HEAD_DIM), BF16),
                   jax.ShapeDtypeStruct((bsz, t, ZB_W), BF16),
                   jax.ShapeDtypeStruct((bsz, t, ZF_W), F32)],
        compiler_params=_cparams(("parallel", "parallel", "arbitrary")),
    )(u, w, cos, sin_lo, sin_hi, qn, kn)


NEG_BIG = -1e30
AHEAD = 2
ONES_ROWS = 2 * SUBLANES
BOUND_MARGIN = 1.02
MAX_SAFE_SHIFT = 50.0


def _attn_kernel(q_ref, k_ref, v_ref, o_ref, m_ref, acc_ref, s_ref, kmax_ref, *,
                 tq, tk, rb, t_len, ctx_len, unroll):
    qi = pl.program_id(2)
    nr = tq // rb
    per_kb = ATTN_GROUP * nr
    n_units = (t_len // tk) * per_kb

    @pl.when(qi == 0)
    def _():
        k = k_ref[0].astype(F32)
        kmax_ref[...] = jnp.broadcast_to(
            jnp.max(jnp.sum(k * k, axis=1, keepdims=True), axis=0, keepdims=True),
            kmax_ref.shape)

    q_all = q_ref[0].reshape(per_kb * rb, HEAD_DIM).astype(F32)
    qsq = _dot_nt(jnp.ones((SUBLANES, HEAD_DIM), BF16), (q_all * q_all).astype(BF16))[0:1]
    shift = jnp.sqrt(qsq * kmax_ref[0:1, 0:1]) * BOUND_MARGIN
    for slot in range(per_kb):
        m_ref[slot] = shift[:, slot * rb:(slot + 1) * rb]
    bounded = jnp.max(shift) <= MAX_SAFE_SHIFT
    acc_ref[...] = jnp.zeros(acc_ref.shape, F32)

    def unit(n):
        kb = n // per_kb
        return kb, n - kb * per_kb

    def scores(n):
        kb, slot = unit(n)
        h = slot // nr
        r0 = pl.multiple_of((slot - h * nr) * rb, rb)
        k0 = pl.multiple_of(kb * tk, tk)
        return _dot_nt(k_ref[0, pl.ds(k0, tk), :], q_ref[0, h, pl.ds(r0, rb), :])

    def update(n, s, mixed, fixed_shift):
        kb, slot = unit(n)
        if mixed:
            r0 = (slot - (slot // nr) * nr) * rb
            qrow = r0 + lax.broadcasted_iota(jnp.int32, (1, rb), 1)
            key = kb * tk + lax.broadcasted_iota(jnp.int32, (tk, 1), 0)
            s = jnp.where(qrow < ctx_len, jnp.where(key < ctx_len, s, NEG_BIG), s)
        if fixed_shift:
            p = jnp.exp2(s - m_ref[slot]).astype(BF16)
            acc_ref[slot] += _dot(v_ref[0, 0, kb], p)
        else:
            m_prev = m_ref[slot]
            m_new = jnp.maximum(m_prev, jnp.max(s, axis=0, keepdims=True))
            alpha = jnp.exp2(m_prev - m_new)
            p = jnp.exp2((s - m_new).astype(BF16))
            m_ref[slot] = m_new
            acc_ref[slot] = alpha * acc_ref[slot] + _dot(v_ref[0, 0, kb], p)

    def run(mixed, fixed_shift):
        if not fixed_shift:
            m_ref[...] = jnp.full(m_ref.shape, NEG_BIG, F32)
        for a in range(AHEAD):
            s_ref[a] = scores(a)

        def body(it, carry):
            n0 = it * unroll
            s = [s_ref[a] for a in range(AHEAD)]
            for u in range(0, unroll, AHEAD):
                s_next = [scores(jnp.minimum(n0 + u + AHEAD + a, n_units - 1))
                          for a in range(AHEAD)]
                for a in range(AHEAD):
                    update(n0 + u + a, s[a], mixed, fixed_shift)
                s = s_next
            for a in range(AHEAD):
                s_ref[a] = s[a]
            return carry

        lax.fori_loop(0, n_units // unroll, body, 0)

    first = qi == 0
    later = jnp.logical_not(first)
    unbounded = jnp.logical_not(bounded)
    pl.when(first & bounded)(lambda: run(True, True))
    pl.when(later & bounded)(lambda: run(False, True))
    pl.when(first & unbounded)(lambda: run(True, False))
    pl.when(later & unbounded)(lambda: run(False, False))

    for h in range(ATTN_GROUP):
        for r in range(nr):
            slot = h * nr + r
            o_t = acc_ref[slot, :HEAD_DIM] * (1.0 / acc_ref[slot, HEAD_DIM:HEAD_DIM + 1])
            o_ref[0, r * rb:(r + 1) * rb, h * HEAD_DIM:(h + 1) * HEAD_DIM] = o_t.T


def _attn_call(zq, zb, *, ctx_len, tq, tk):
    bsz, t, _ = zb.shape
    assert ctx_len <= tq and ctx_len <= tk
    rb = 256 if tq % 256 == 0 else tq
    per_kb = ATTN_GROUP * (tq // rb)
    unroll = max(u for u in range(AHEAD, 13, AHEAD) if per_kb % u == 0)
    k_blk = ZB_AK_OFF // HEAD_DIM
    v_t = zb[:, :, ZB_AV_OFF:ZB_AV_OFF + ATTN_KV_HEADS * HEAD_DIM]
    v_t = v_t.reshape(bsz, t // tk, tk, ATTN_KV_HEADS, HEAD_DIM).transpose(0, 3, 1, 4, 2)
    v_t = jnp.concatenate([v_t, jnp.ones(v_t.shape[:3] + (ONES_ROWS, tk), v_t.dtype)], axis=3)
    kern = functools.partial(_attn_kernel, tq=tq, tk=tk, rb=rb, t_len=t, ctx_len=ctx_len,
                             unroll=unroll)
    return pl.pallas_call(
        kern,
        grid=(bsz, ATTN_KV_HEADS, t // tq),
        in_specs=[pl.BlockSpec((1, ATTN_GROUP, tq, HEAD_DIM), lambda b, g, qi: (b, g, qi, 0)),
                  pl.BlockSpec((1, t, HEAD_DIM), lambda b, g, qi: (b, 0, k_blk + g)),
                  pl.BlockSpec((1, 1, t // tk, HEAD_DIM + ONES_ROWS, tk),
                               lambda b, g, qi: (b, g, 0, 0, 0))],
        out_specs=pl.BlockSpec((1, tq, ATTN_GROUP * HEAD_DIM), lambda b, g, qi: (b, qi, g)),
        out_shape=jax.ShapeDtypeStruct((bsz, t, ATTN_HEADS * HEAD_DIM), F32),
        scratch_shapes=[pltpu.VMEM((per_kb, 1, rb), F32),
                        pltpu.VMEM((per_kb, HEAD_DIM + ONES_ROWS, rb), F32),
                        pltpu.VMEM((AHEAD, tk, rb), F32),
                        pltpu.VMEM((SUBLANES, LANES), F32)],
        compiler_params=_cparams(("parallel", "parallel", "arbitrary")),
    )(zq, zb, v_t)


def _bwd_chunk(p, n_ctx_chunks, n_chunks):
    return jnp.where(p < n_ctx_chunks, n_ctx_chunks - 1 - p, n_chunks + n_ctx_chunks - 1 - p)


def _ret_kernel(ld_ref, qf_ref, kf_ref, vf_ref, qb_ref, kb_ref, vb_ref, yf_ref, yb_ref,
                sf_ref, sb_ref, *, chunk):
    c = chunk
    p = pl.program_id(1)

    @pl.when(p == 0)
    def _():
        sf_ref[...] = jnp.zeros(sf_ref.shape, F32)
        sb_ref[...] = jnp.zeros(sb_ref.shape, F32)

    ii = lax.broadcasted_iota(jnp.int32, (c, c), 0)
    jj = lax.broadcasted_iota(jnp.int32, (c, c), 1)
    dist = (ii - jj).astype(F32)
    pos = lax.broadcasted_iota(jnp.int32, (c, 1), 0).astype(F32)
    full = jnp.full((1, 1), float(c), F32)

    for h in range(RET_HEADS):
        ldf = ld_ref[0, h]
        ldb = ld_ref[1, h]
        qs = slice(h * RET_QK_DIM, (h + 1) * RET_QK_DIM)
        vs = slice(h * RET_V_DIM, (h + 1) * RET_V_DIM)
        decay = jnp.where(dist >= 0, jnp.exp(ldf * dist), 0.0) \
            + jnp.where(dist <= 0, jnp.exp(-ldb * dist), 0.0)
        q = qf_ref[0, :, qs]
        k = kf_ref[0, :, qs]
        v = vf_ref[0, :, vs]
        y = _dot((_dot_nt(q, k) * decay).astype(BF16), v)
        y = y + _dot(q, sf_ref[h].astype(BF16)) * jnp.exp(ldf * (pos + 1.0))
        yf_ref[0, :, vs] = y
        wv = (v.astype(F32) * jnp.exp(ldf * (c - 1.0 - pos))).astype(BF16)
        sf_ref[h] = sf_ref[h] * jnp.exp(ldf * full) + _dot_tn(k, wv)
        q = qb_ref[0, :, qs]
        k = kb_ref[0, :, qs]
        v = vb_ref[0, :, vs]
        yb_ref[0, :, vs] = _dot(q, sb_ref[h].astype(BF16)) * jnp.exp(ldb * (c - pos))
        wv = (v.astype(F32) * jnp.exp(ldb * pos)).astype(BF16)
        sb_ref[h] = sb_ref[h] * jnp.exp(ldb * full) + _dot_tn(k, wv)


def _ret_call(zb, log_decay, *, ctx_len, chunk):
    bsz, t, _ = zb.shape
    nc = t // chunk
    ncc = ctx_len // chunk
    qw = RET_HEADS * RET_QK_DIM
    vw = RET_HEADS * RET_V_DIM
    q_blk, k_blk, v_blk = ZB_RQ_OFF // qw, ZB_RK_OFF // qw, ZB_RV_OFF // vw
    bwd = functools.partial(_bwd_chunk, n_ctx_chunks=ncc, n_chunks=nc)
    kern = functools.partial(_ret_kernel, chunk=chunk)
    out = jax.ShapeDtypeStruct((bsz, t, vw), F32)
    return pl.pallas_call(
        kern,
        grid=(bsz, nc),
        in_specs=[pl.BlockSpec(memory_space=pltpu.SMEM),
                  pl.BlockSpec((1, chunk, qw), lambda b, p: (b, p, q_blk)),
                  pl.BlockSpec((1, chunk, qw), lambda b, p: (b, p, k_blk)),
                  pl.BlockSpec((1, chunk, vw), lambda b, p: (b, p, v_blk)),
                  pl.BlockSpec((1, chunk, qw), lambda b, p: (b, bwd(p), q_blk)),
                  pl.BlockSpec((1, chunk, qw), lambda b, p: (b, bwd(p), k_blk)),
                  pl.BlockSpec((1, chunk, vw), lambda b, p: (b, bwd(p), v_blk))],
        out_specs=[pl.BlockSpec((1, chunk, vw), lambda b, p: (b, p, 0)),
                   pl.BlockSpec((1, chunk, vw), lambda b, p: (b, bwd(p), 0))],
        out_shape=[out, out],
        scratch_shapes=[pltpu.VMEM((RET_HEADS, RET_QK_DIM, RET_V_DIM), F32),
                        pltpu.VMEM((RET_HEADS, RET_QK_DIM, RET_V_DIM), F32)],
        compiler_params=_cparams(("parallel", "arbitrary")),
    )(log_decay, zb, zb, zb, zb, zb, zb)


CONV_HALO = SUBLANES


def _conv_kernel(prev_ref, cur_ref, next_ref, w_ref, b_ref, o_ref, e_ref, *, tc, t_len, ctx_len):
    i = pl.program_id(1)
    e_ref[0:CONV_HALO] = prev_ref[0]
    e_ref[CONV_HALO:CONV_HALO + tc] = cur_ref[0]
    e_ref[CONV_HALO + tc:] = next_ref[0]
    pad = (SSM_CONV - 1) // 2
    lo = i * tc

    def conv(masked):
        tok = lo + lax.broadcasted_iota(jnp.int32, (tc, 1), 0)
        acc = jnp.broadcast_to(b_ref[...], (tc, ZF_XBC_W))
        for tap in range(SSM_CONV):
            off = tap - pad
            src = e_ref[CONV_HALO + off:CONV_HALO + off + tc, :]
            if masked:
                nb = tok + off
                same = ((tok - ctx_len) ^ (nb - ctx_len)) >= 0
                src = jnp.where(same & (nb >= 0) & (nb < t_len), src, 0.0)
            acc = acc + src * w_ref[tap:tap + 1, :]
        o_ref[0] = _silu(acc)

    def near(edge):
        return (lo - pad < edge) & (edge < lo + tc + pad)
    touches = near(0) | near(ctx_len) | near(t_len)
    pl.when(touches)(lambda: conv(True))
    pl.when(jnp.logical_not(touches))(lambda: conv(False))


def _conv_call(zf, conv_w, conv_b, *, ctx_len, tc):
    bsz, t, _ = zf.shape
    col = ZF_XBC_OFF // ZF_XBC_W
    per = tc // CONV_HALO
    last = t // CONV_HALO - 1
    kern = functools.partial(_conv_kernel, tc=tc, t_len=t, ctx_len=ctx_len)
    return pl.pallas_call(
        kern,
        grid=(bsz, t // tc),
        in_specs=[pl.BlockSpec((1, CONV_HALO, ZF_XBC_W),
                               lambda b, i: (b, jnp.maximum(i * per - 1, 0), col)),
                  pl.BlockSpec((1, tc, ZF_XBC_W), lambda b, i: (b, i, col)),
                  pl.BlockSpec((1, CONV_HALO, ZF_XBC_W),
                               lambda b, i: (b, jnp.minimum((i + 1) * per, last), col)),
                  pl.BlockSpec((SUBLANES, ZF_XBC_W), lambda b, i: (0, 0)),
                  pl.BlockSpec((1, ZF_XBC_W), lambda b, i: (0, 0))],
        out_specs=pl.BlockSpec((1, tc, ZF_XBC_W), lambda b, i: (b, i, 0)),
        out_shape=jax.ShapeDtypeStruct((bsz, t, ZF_XBC_W), F32),
        scratch_shapes=[pltpu.VMEM((tc + 2 * CONV_HALO, ZF_XBC_W), F32)],
        compiler_params=_cparams(("parallel", "parallel")),
    )(zf, zf, zf, conv_w, conv_b)


def _split3(x):
    p1 = x.astype(BF16)
    r1 = x - p1.astype(F32)
    p2 = r1.astype(BF16)
    p3 = (r1 - p2.astype(F32)).astype(BF16)
    return p1, p2, p3


def _cumsum_rows(x, tri):
    p1, p2, p3 = _split3(x)
    return _dot(tri, p1) + _dot(tri, p2) + _dot(tri, p3)


def _expand_heads(a, sel_ref):
    hi = a.astype(BF16)
    lo = (a - hi.astype(F32)).astype(BF16)
    return _dot(jnp.concatenate([hi, lo], axis=1), sel_ref[...])


def _head_selector(off):
    sel = np.zeros((LANES, SSM_INNER), np.float32)
    for h in range(SSM_HEADS):
        sel[off + h, h * SSM_HEAD_DIM:(h + 1) * SSM_HEAD_DIM] = 1.0
    return jnp.asarray(np.concatenate([sel, sel], axis=0), BF16)


def _ssd_kernel(xf_ref, df_ref, xb_ref, db_ref, bias_ref, alog_ref, skip_ref, self_ref, selb_ref,
                yf_ref, yb_ref, sf_ref, sb_ref, *, chunk, sub):
    c = chunk
    p = pl.program_id(1)
    nh = SSM_HEADS
    gw = SSM_HPG * SSM_HEAD_DIM
    bc_w = SSM_GROUPS * SSM_STATE

    @pl.when(p == 0)
    def _():
        sf_ref[...] = jnp.zeros(sf_ref.shape, F32)
        sb_ref[...] = jnp.zeros(sb_ref.shape, F32)

    ii = lax.broadcasted_iota(jnp.int32, (c, c), 0)
    jj = lax.broadcasted_iota(jnp.int32, (c, c), 1)
    lower = ii >= jj
    upper = jj >= ii
    tri = jnp.where(lower, 1.0, 0.0).astype(BF16)
    lane_lo = lax.broadcasted_iota(jnp.int32, (c, LANES), 1) < SSM_HEAD_DIM
    lane_hi = jnp.logical_not(lane_lo)

    def operands(x_ref, rows):
        xs = x_ref[0, rows, 0:SSM_INNER]
        bm = x_ref[0, rows, SSM_INNER:SSM_INNER + bc_w].astype(BF16)
        cm = x_ref[0, rows, SSM_INNER + bc_w:].astype(BF16)
        return xs, bm, cm

    def gates(d_ref, rows):
        raw = d_ref[0, rows, :] + bias_ref[...]
        dt = jnp.maximum(raw, 0.0) + jnp.log1p(jnp.exp(-jnp.abs(raw)))
        la = dt * -jnp.exp(alog_ref[...])
        acum = _cumsum_rows(la, tri)
        return dt, acum, acum - la

    def within(xs, bm, cm, dt, acum, excl):
        dt_t = dt.T
        acum_t = acum.T
        excl_t = excl.T
        xs_b = xs.astype(BF16)
        pieces = []
        for g in range(SSM_GROUPS):
            gs = slice(g * SSM_STATE, (g + 1) * SSM_STATE)
            scores = _dot_nt(cm[:, gs], bm[:, gs])
            for m in range(SSM_HPG // 2):
                pair = g * SSM_HPG // 2 + m
                xp = xs_b[:, pair * LANES:(pair + 1) * LANES]
                acc = None
                for e in range(2):
                    hf = 2 * pair + e
                    hb = nh + hf
                    dec_f = jnp.where(lower, jnp.exp(acum[:, hf:hf + 1] - acum_t[hf:hf + 1, :]), 0.0)
                    dec_b = jnp.where(upper, jnp.exp(excl_t[hb:hb + 1, :] - excl[:, hb:hb + 1]), 0.0)
                    wm = scores * (dec_f * dt_t[hf:hf + 1, :] + dec_b * dt_t[hb:hb + 1, :])
                    xh = jnp.where(lane_lo if e == 0 else lane_hi, xp, 0.0)
                    part = _dot(wm.astype(BF16), xh)
                    acc = part if acc is None else acc + part
                pieces.append(acc)
        return jnp.concatenate(pieces, axis=1)

    def carried(s_ref, xs, bm, cm, carry, wdt, keep):
        xw = (xs * wdt).astype(BF16)
        inter = []
        for g in range(SSM_GROUPS):
            gs = slice(g * SSM_STATE, (g + 1) * SSM_STATE)
            cs = slice(g * gw, (g + 1) * gw)
            inter.append(_dot(cm[:, gs], s_ref[g].astype(BF16)))
            s_ref[g] = s_ref[g] * keep[:, cs] + _dot_tn(bm[:, gs], xw[:, cs])
        return jnp.concatenate(inter, axis=1) * carry

    rows = [pl.ds(i * c, c) for i in range(sub)]
    fwd = [operands(xf_ref, r) + gates(df_ref, r) for r in rows]
    bwd = [operands(xb_ref, r) + gates(db_ref, r) for r in rows]

    fwd_scale = []
    for xs, bm, cm, dt, acum, excl in fwd:
        last = acum[c - 1:c, :]
        both = _expand_heads(
            jnp.concatenate([jnp.exp(acum), jnp.exp(last - acum) * dt], axis=0), self_ref)
        fwd_scale.append((both[:c], both[c:], both[c - 1:c]))
    bwd_scale = []
    for xs, bm, cm, dt, acum, excl in bwd:
        total = acum[c - 1:c, :]
        both = _expand_heads(
            jnp.concatenate([jnp.exp(total - excl), jnp.exp(excl) * dt], axis=0), selb_ref)
        bwd_scale.append((both[:c], both[c:], both[0:1]))

    y_within = [within(*f) for f in fwd]

    for i in range(sub):
        xs, bm, cm = fwd[i][:3]
        y = y_within[i] + carried(sf_ref, xs, bm, cm, *fwd_scale[i]) + skip_ref[...] * xs
        yf_ref[0, rows[i], :] = y
    for i in reversed(range(sub)):
        xs, bm, cm = bwd[i][:3]
        yb_ref[0, rows[i], :] = carried(sb_ref, xs, bm, cm, *bwd_scale[i])


def _ssd_call(xbc, zf, bias_row, alog_row, skip_row, *, ctx_len, chunk):
    bsz, t, _ = xbc.shape
    sub = 2 if ctx_len % (2 * chunk) == 0 and t % (2 * chunk) == 0 else 1
    blk = sub * chunk
    nblk = t // blk
    dt_blk = ZF_DT_OFF // LANES
    bwd = functools.partial(_bwd_chunk, n_ctx_chunks=ctx_len // blk, n_chunks=nblk)
    kern = functools.partial(_ssd_kernel, chunk=chunk, sub=sub)
    out = jax.ShapeDtypeStruct((bsz, t, SSM_INNER), F32)
    row = lambda w: pl.BlockSpec((1, w), lambda b, p: (0, 0))
    return pl.pallas_call(
        kern,
        grid=(bsz, nblk),
        in_specs=[pl.BlockSpec((1, blk, ZF_XBC_W), lambda b, p: (b, p, 0)),
                  pl.BlockSpec((1, blk, LANES), lambda b, p: (b, p, dt_blk)),
                  pl.BlockSpec((1, blk, ZF_XBC_W), lambda b, p: (b, bwd(p), 0)),
                  pl.BlockSpec((1, blk, LANES), lambda b, p: (b, bwd(p), dt_blk)),
                  row(LANES), row(LANES), row(SSM_INNER),
                  pl.BlockSpec((2 * LANES, SSM_INNER), lambda b, p: (0, 0)),
                  pl.BlockSpec((2 * LANES, SSM_INNER), lambda b, p: (0, 0))],
        out_specs=[pl.BlockSpec((1, blk, SSM_INNER), lambda b, p: (b, p, 0)),
                   pl.BlockSpec((1, blk, SSM_INNER), lambda b, p: (b, bwd(p), 0))],
        out_shape=[out, out],
        scratch_shapes=[pltpu.VMEM((SSM_GROUPS, SSM_STATE, SSM_HPG * SSM_HEAD_DIM), F32),
                        pltpu.VMEM((SSM_GROUPS, SSM_STATE, SSM_HPG * SSM_HEAD_DIM), F32)],
        compiler_params=_cparams(("parallel", "arbitrary")),
    )(xbc, zf, xbc, zf, bias_row, alog_row, skip_row,
      _head_selector(0), _head_selector(SSM_HEADS))


def _merge_kernel(attn_ref, retf_ref, retb_ref, ssmf_ref, ssmb_ref, mg_ref, ag_ref, rg_ref,
                  sz_ref, gnw_ref, snw_ref, wb_ref, o_ref):
    d = D_MODEL
    gate_in = lambda ref, lo=0, hi=None: ref[0, :, lo:hi].astype(F32)
    a = attn_ref[0] * _silu(gate_in(ag_ref))

    ret = retf_ref[0] + retb_ref[0]
    parts = []
    for h in range(RET_HEADS):
        r = ret[:, h * RET_V_DIM:(h + 1) * RET_V_DIM]
        rc = r - jnp.mean(r, axis=-1, keepdims=True)
        var = jnp.mean(rc * rc, axis=-1, keepdims=True)
        parts.append(rc * lax.rsqrt(var + NORM_EPS))
    r = jnp.concatenate(parts, axis=1) * gnw_ref[...] * _silu(gate_in(rg_ref))

    s = (ssmf_ref[0] + ssmb_ref[0]) * _silu(gate_in(sz_ref))
    gwid = SSM_INNER // SSM_GROUPS
    parts = []
    for g in range(SSM_GROUPS):
        sg = s[:, g * gwid:(g + 1) * gwid]
        parts.append(sg * lax.rsqrt(jnp.mean(sg * sg, axis=-1, keepdims=True) + NORM_EPS))
    s = jnp.concatenate(parts, axis=1) * snw_ref[...]

    merged = _sigmoid(gate_in(mg_ref, 0, d)) * _dot(a.astype(BF16), wb_ref[0])
    merged = merged + _sigmoid(gate_in(mg_ref, d, 2 * d)) * _dot(r.astype(BF16), wb_ref[1])
    merged = merged + _sigmoid(gate_in(mg_ref, 2 * d, 3 * d)) * _dot(s.astype(BF16), wb_ref[2])
    o_ref[0] = merged.astype(BF16)


def _merge_call(attn, retf, retb, ssmf, ssmb, zb, gnw, snw, wb, *, layer, tm):
    bsz, t, _ = attn.shape
    d = D_MODEL
    bw = BRANCH_WIDTH
    act = lambda blk: pl.BlockSpec((1, tm, bw), lambda b, i: (b, i, blk))
    vec = pl.BlockSpec((None, 1, bw), lambda b, i: (layer, 0, 0))
    return pl.pallas_call(
        _merge_kernel,
        grid=(bsz, t // tm),
        in_specs=[act(0), act(0), act(0), act(0), act(0),
                  pl.BlockSpec((1, tm, N_BRANCHES * d),
                               lambda b, i: (b, i, ZB_MG_OFF // (N_BRANCHES * d))),
                  act(ZB_AG_OFF // bw), act(ZB_RG_OFF // bw), act(ZB_SZ_OFF // bw),
                  vec, vec,
                  pl.BlockSpec((None, N_BRANCHES, bw, d), lambda b, i: (layer, 0, 0, 0),
                               pipeline_mode=pl.Buffered(1))],
        out_specs=pl.BlockSpec((1, tm, d), lambda b, i: (b, i, 0)),
        out_shape=jax.ShapeDtypeStruct((bsz, t, d), BF16),
        compiler_params=_cparams(("parallel", "parallel")),
    )(attn, retf, retb, ssmf, ssmb, zb, zb, zb, zb, gnw, snw, wb)


def _modulate(x, mod_ref, b, n_batch, is_ctx):
    d = D_MODEL
    shift = _modulation_rows(mod_ref, b, n_batch, is_ctx, 0, d)
    scale = _modulation_rows(mod_ref, b, n_batch, is_ctx, d, 2 * d)
    return (x * (1.0 + scale) + shift).astype(BF16)


def _modulate_kernel(x_ref, mod_ref, u_ref, *, tm, n_batch, ctx_len):
    row = pl.program_id(1) * tm + lax.broadcasted_iota(jnp.int32, (tm, 1), 0)
    u_ref[0] = _modulate(x_ref[0], mod_ref, pl.program_id(0), n_batch, row < ctx_len)


def _modulate_call(xs, mod, *, layer, n_batch, ctx_len, tm):
    bsz, t, d = xs.shape
    kern = functools.partial(_modulate_kernel, tm=tm, n_batch=n_batch, ctx_len=ctx_len)
    tok = pl.BlockSpec((1, tm, d), lambda b, i: (b, i, 0))
    return pl.pallas_call(
        kern,
        grid=(bsz, t // tm),
        in_specs=[tok, pl.BlockSpec((None, SUBLANES, 3 * d), lambda b, i: (layer, 0, 0))],
        out_specs=tok,
        out_shape=jax.ShapeDtypeStruct((bsz, t, d), BF16),
        compiler_params=_cparams(("parallel", "parallel")),
    )(xs, mod)


def _out_kernel(m_ref, x_ref, mod_ref, nxt_ref, w_ref, g_ref, b_ref, o_ref, *u_ref,
                tm, row0, n_batch, ctx_len):
    b = pl.program_id(0)
    d = D_MODEL
    row = row0 + pl.program_id(1) * tm + lax.broadcasted_iota(jnp.int32, (tm, 1), 0)
    is_ctx = row < ctx_len
    gate = _modulation_rows(mod_ref, b, n_batch, is_ctx, 2 * d, 3 * d)
    y = _dot(m_ref[0], w_ref[...])
    z = DEEPNORM_ALPHA * x_ref[0] + gate * y
    zc = z - jnp.mean(z, axis=-1, keepdims=True)
    var = jnp.mean(zc * zc, axis=-1, keepdims=True)
    x_new = zc * lax.rsqrt(var + NORM_EPS) * g_ref[...] + b_ref[...]
    o_ref[0] = x_new
    if u_ref:
        u_ref[0][0] = _modulate(x_new, nxt_ref, b, n_batch, is_ctx)


def _out_call(merged, xs, mod, w_out, ln_g, ln_b, *, layer, last, n_batch, ctx_len, tm):
    bsz, t, d = xs.shape
    depth = mod.shape[0]
    blk0 = ctx_len // tm if last else 0
    assert not last or ctx_len % tm == 0
    rows = t - blk0 * tm
    kern = functools.partial(_out_kernel, tm=tm, row0=blk0 * tm, n_batch=n_batch,
                             ctx_len=ctx_len)
    tok_in = pl.BlockSpec((1, tm, d), lambda b, i: (b, i + blk0, 0))
    tok_out = pl.BlockSpec((1, tm, d), lambda b, i: (b, i, 0))
    mod_at = lambda l: pl.BlockSpec((None, SUBLANES, 3 * d), lambda b, i: (l, 0, 0))
    vec = pl.BlockSpec((None, 1, d), lambda b, i: (layer, 0, 0))
    x_shape = jax.ShapeDtypeStruct((bsz, rows, d), F32)
    return pl.pallas_call(
        kern,
        grid=(bsz, rows // tm),
        in_specs=[tok_in, tok_in, mod_at(layer), mod_at(min(layer + 1, depth - 1)),
                  pl.BlockSpec((None, d, d), lambda b, i: (layer, 0, 0),
                               pipeline_mode=pl.Buffered(1)),
                  vec, vec],
        out_specs=tok_out if last else [tok_out, tok_out],
        out_shape=x_shape if last else [x_shape, jax.ShapeDtypeStruct((bsz, rows, d), BF16)],
        compiler_params=_cparams(("parallel", "parallel")),
    )(merged, xs, mod, mod, w_out, ln_g, ln_b)


PACK_COLS = 256


def _pack_kernel(w_ref, o_ref):
    row = 0
    for name in ("aq",) + ZB_ORDER + ZF_ORDER:
        src, width = _REF_SPLITS[name]
        o_ref[0, row:row + width, :] = w_ref[0, src:src + width, :].astype(BF16)
        row += width
    o_ref[0, row:, :] = jnp.zeros((W_PACKED - row, o_ref.shape[2]), BF16)


def _pack_w_in(w_in):
    depth, d, width = w_in.shape
    w_t = jnp.swapaxes(w_in, 1, 2)
    return pl.pallas_call(
        _pack_kernel,
        grid=(depth, d // PACK_COLS),
        in_specs=[pl.BlockSpec((1, width, PACK_COLS), lambda l, i: (l, 0, i))],
        out_specs=pl.BlockSpec((1, W_PACKED, PACK_COLS), lambda l, i: (l, 0, i)),
        out_shape=jax.ShapeDtypeStruct((depth, W_PACKED, d), BF16),
        compiler_params=_cparams(("parallel", "parallel")),
    )(w_t)


def _rope_tables(rows, ctx_len):
    row = jnp.repeat(jnp.arange(rows, dtype=F32), GRID_W)
    col = jnp.tile(jnp.arange(GRID_W, dtype=F32), rows)
    n_freq = HEAD_DIM // 4
    inv = ROPE_BASE ** (-jnp.arange(n_freq, dtype=F32) / n_freq)
    ang_r = row[:, None] * inv
    ang_c = col[:, None] * inv
    ang = jnp.concatenate([ang_r, ang_r, ang_c, ang_c], -1)
    cos = jnp.concatenate([jnp.ones((ctx_len, HEAD_DIM), F32), jnp.cos(ang)], 0)
    sin = jnp.concatenate([jnp.zeros((ctx_len, HEAD_DIM), F32), jnp.sin(ang)], 0)
    first = (np.arange(HEAD_DIM) % (HEAD_DIM // 2)) < HEAD_DIM // 4
    sin_lo = jnp.where(first, -sin, 0.0)
    sin_hi = jnp.where(first, 0.0, sin)
    return cos, sin_lo, sin_hi


def _pad_lanes(v, width):
    return jnp.pad(v.reshape(1, -1), ((0, 0), (0, width - v.size)))


def _tile_choices(t, ctx_len):
    def largest(cap, mult, also=0):
        best = mult
        for cand in range(mult, cap + 1, mult):
            if t % cand == 0 and also % cand == 0:
                best = cand
        return best
    return dict(proj=largest(2112, 16), attn=largest(768, LANES), conv=largest(528, SUBLANES),
                merge=largest(256, 16), out=largest(528, 16),
                out_last=largest(528, 16, also=ctx_len))


def kernel(x, c, ctx, c_ctx, ada_w, ada_b, w_in, attn_q_norm, attn_k_norm, ret_log_decay, ret_gn_w,
           ssm_conv_w, ssm_conv_b, ssm_dt_bias, ssm_a_log, ssm_d, ssm_norm_w, w_branch, w_out,
           ln_g, ln_b):
    bsz, seq, d = x.shape
    ctx_len = ctx.shape[1]
    depth = w_in.shape[0]
    t = seq + ctx_len
    chunk = LANES
    ret_chunk = 256 if ctx_len % 256 == 0 and t % 256 == 0 else chunk
    assert d == D_MODEL and bsz < SUBLANES and seq % GRID_W == 0
    assert ctx_len % chunk == 0 and t % chunk == 0
    tiles = _tile_choices(t, ctx_len)

    xs = jnp.concatenate([ctx, x], axis=1)
    cvec = jnp.zeros((SUBLANES, d), F32).at[:bsz].set(c).at[bsz].set(c_ctx)
    mod = _ada_call(cvec, ada_w, ada_b)
    cos, sin_lo, sin_hi = _rope_tables(seq // GRID_W, ctx_len)
    w_packed = _pack_w_in(w_in)
    wb = w_branch.astype(BF16)
    wo = w_out.astype(BF16)
    conv_w = jnp.pad(ssm_conv_w, ((0, 0), (0, SUBLANES - SSM_CONV), (0, 0)))
    per_layer_row = lambda v: v.reshape(depth, 1, -1)
    qn, kn = per_layer_row(attn_q_norm), per_layer_row(attn_k_norm)
    gnw, snw = per_layer_row(ret_gn_w), per_layer_row(ssm_norm_w)
    lng, lnb = per_layer_row(ln_g), per_layer_row(ln_b)

    u = _modulate_call(xs, mod, layer=0, n_batch=bsz, ctx_len=ctx_len, tm=tiles["out"])
    for l in range(depth):
        zq, zb, zf = _inproj_call(u, w_packed, cos, sin_lo, sin_hi, qn, kn, layer=l,
                                  tm=tiles["proj"])
        attn = _attn_call(zq, zb, ctx_len=ctx_len, tq=tiles["attn"], tk=tiles["attn"])
        retf, retb = _ret_call(zb, ret_log_decay[l], ctx_len=ctx_len, chunk=ret_chunk)
        xbc = _conv_call(zf, conv_w[l], ssm_conv_b[l].reshape(1, -1), ctx_len=ctx_len,
                         tc=tiles["conv"])
        ssmf, ssmb = _ssd_call(xbc, zf, _pad_lanes(ssm_dt_bias[l], LANES),
                               _pad_lanes(ssm_a_log[l], LANES),
                               jnp.repeat(ssm_d[l], SSM_HEAD_DIM).reshape(1, -1),
                               ctx_len=ctx_len, chunk=chunk)
        merged = _merge_call(attn, retf, retb, ssmf, ssmb, zb, gnw, snw, wb, layer=l,
                             tm=tiles["merge"])
        last = l == depth - 1
        out = _out_call(merged, xs, mod, wo, lng, lnb, layer=l, last=last, n_batch=bsz,
                        ctx_len=ctx_len, tm=tiles["out_last" if last else "out"])
        if last:
            return out
        xs, u = out
```

```python
import functools

import numpy as np
import jax
import jax.numpy as jnp
from jax import lax
from jax.experimental import pallas as pl
from jax.experimental.pallas import tpu as pltpu

F32 = jnp.float32
BF16 = jnp.bfloat16

D_MODEL = 2048
GRID_W = 64
HEAD_DIM = 128
ATTN_HEADS = 8
ATTN_KV_HEADS = 2
ATTN_GROUP = ATTN_HEADS // ATTN_KV_HEADS
ROPE_BASE = 10000.0
RET_HEADS = 4
RET_QK_DIM = 128
RET_V_DIM = 256
SSM_INNER = 1024
SSM_HEAD_DIM = 64
SSM_HEADS = 16
SSM_GROUPS = 2
SSM_HPG = SSM_HEADS // SSM_GROUPS
SSM_STATE = 128
SSM_CONV = 5
N_BRANCHES = 3
BRANCH_WIDTH = 1024
DEPTH = 4
DEEPNORM_ALPHA = (2 * DEPTH) ** 0.25
NORM_EPS = 1e-6
ATTN_Q_SCALE = HEAD_DIM ** -0.5 * float(np.log2(np.e))

_REF_SPLITS = dict(aq=(0, 1024), ak=(1024, 256), av=(1280, 256), ag=(1536, 1024),
                   rq=(2560, 512), rk=(3072, 512), rv=(3584, 1024), rg=(4608, 1024),
                   sx=(5632, 1024), sb=(6656, 256), sc=(6912, 256), sdt=(7168, 32),
                   sz=(7200, 1024), mg=(8224, 6144))

LANES = 128
SUBLANES = 8
VMEM_LIMIT = 56 * 1024 * 1024

PROJ_TN = 512
ZQ_TILES = ATTN_HEADS * HEAD_DIM // PROJ_TN
ZB_ORDER = ("mg", "ag", "rg", "sz", "rv", "ak", "av", "rq", "rk")
ZB_MG_OFF, ZB_AG_OFF, ZB_RG_OFF, ZB_SZ_OFF = 0, 6144, 7168, 8192
ZB_RV_OFF, ZB_AK_OFF, ZB_AV_OFF, ZB_RQ_OFF, ZB_RK_OFF = 9216, 10240, 10496, 10752, 11264
ZB_W = 11776
ZB_TILES = ZB_W // PROJ_TN
ZB_RAW_TILES = ZB_AK_OFF // PROJ_TN
ZF_TILE0 = ZQ_TILES + ZB_TILES
ZF_ORDER = ("sx", "sb", "sc", "sdt")
ZF_USED = 1024 + 256 + 256 + 32
ZF_W = 2048
ZF_XBC_OFF = 0
ZF_XBC_W = 1536
ZF_DT_OFF = 1536
W_PACKED = ZQ_TILES * PROJ_TN + ZB_W + ZF_W


def _sigmoid(x):
    return 0.5 * jnp.tanh(0.5 * x) + 0.5


def _silu(x):
    return x * _sigmoid(x)


def _cparams(sem):
    return pltpu.CompilerParams(dimension_semantics=sem, vmem_limit_bytes=VMEM_LIMIT)


def _dot(a, b):
    return jnp.dot(a, b, preferred_element_type=F32)


def _dot_nt(a, b):
    return lax.dot_general(a, b, (((1,), (1,)), ((), ())), preferred_element_type=F32)


def _dot_tn(a, b):
    return lax.dot_general(a, b, (((0,), (0,)), ((), ())), preferred_element_type=F32)


def _ada_kernel(c_ref, w_ref, b_ref, o_ref):
    s = _silu(c_ref[...]).astype(BF16)
    o_ref[0] = _dot(s, w_ref[0].astype(BF16)) + b_ref[0]


def _ada_call(cvec, ada_w, ada_b):
    depth, d, n = ada_w.shape
    tn = 768
    return pl.pallas_call(
        _ada_kernel,
        grid=(depth, n // tn),
        in_specs=[pl.BlockSpec((SUBLANES, d), lambda l, j: (0, 0)),
                  pl.BlockSpec((1, d, tn), lambda l, j: (l, 0, j)),
                  pl.BlockSpec((1, 1, tn), lambda l, j: (l, 0, j))],
        out_specs=pl.BlockSpec((1, SUBLANES, tn), lambda l, j: (l, 0, j)),
        out_shape=jax.ShapeDtypeStruct((depth, SUBLANES, n), F32),
        compiler_params=_cparams(("parallel", "parallel")),
    )(cvec, ada_w, ada_b.reshape(depth, 1, n))


def _rope(x, cos, sin_lo, sin_hi):
    return x * cos + pltpu.roll(x, 96, 1) * sin_lo + pltpu.roll(x, 32, 1) * sin_hi


def _modulation_rows(mod_ref, b, n_batch, is_ctx, lo, hi):
    mb = mod_ref[pl.ds(b, 1), lo:hi]
    mc = mod_ref[pl.ds(n_batch, 1), lo:hi]
    return jnp.where(is_ctx, mc, mb)


def _inproj_kernel(u_ref, w_ref, cos_ref, sl_ref, sh_ref, qn_ref, kn_ref,
                   zq_ref, zb_ref, zf_ref):
    j = pl.program_id(2)
    jb = j - ZQ_TILES

    def project():
        return _dot_nt(u_ref[0], w_ref[...])

    def rope(xh):
        return _rope(xh, cos_ref[...], sl_ref[...], sh_ref[...])

    def normed(w_row):
        def fn(xh):
            ms = jnp.mean(xh * xh, axis=-1, keepdims=True)
            return rope(xh * lax.rsqrt(ms + NORM_EPS) * w_row)
        return fn

    @pl.when(j >= ZF_TILE0)
    def _():
        zf_ref[0] = project()

    @pl.when((jb >= 0) & (jb < ZB_RAW_TILES))
    def _():
        zb_ref[0] = project().astype(BF16)

    @pl.when((j < ZQ_TILES) | ((jb >= ZB_RAW_TILES) & (j < ZF_TILE0)))
    def _():
        acc = project()

        def head(h):
            return acc[:, h * LANES:(h + 1) * LANES]

        def per_head(fn, n):
            return jnp.concatenate([fn(head(h)) for h in range(n)], axis=1)

        @pl.when(j < ZQ_TILES)
        def _():
            fn = normed(qn_ref[...])
            for h in range(ATTN_GROUP):
                zq_ref[0, h] = (fn(head(h)) * ATTN_Q_SCALE).astype(BF16)

        @pl.when(jb == ZB_AK_OFF // PROJ_TN)
        def _():
            k = per_head(normed(kn_ref[...]), 2)
            zb_ref[0] = jnp.concatenate([k, acc[:, 2 * LANES:]], axis=1).astype(BF16)

        @pl.when(jb == ZB_RQ_OFF // PROJ_TN)
        def _():
            zb_ref[0] = per_head(rope, 4).astype(BF16)

        @pl.when(jb == ZB_RK_OFF // PROJ_TN)
        def _():
            zb_ref[0] = per_head(lambda xh: rope(xh * (RET_QK_DIM ** -0.5)), 4).astype(BF16)


def _inproj_call(u, w, cos, sin_lo, sin_hi, qn, kn, *, layer, tm):
    bsz, t, d = u.shape
    rope_spec = pl.BlockSpec((tm, LANES), lambda b, i, j: (i, 0))
    vec_spec = pl.BlockSpec((None, 1, LANES), lambda b, i, j: (layer, 0, 0))
    return pl.pallas_call(
        _inproj_kernel,
        grid=(bsz, t // tm, W_PACKED // PROJ_TN),
        in_specs=[pl.BlockSpec((1, tm, d), lambda b, i, j: (b, i, 0)),
                  pl.BlockSpec((None, PROJ_TN, d), lambda b, i, j: (layer, j, 0)),
                  rope_spec, rope_spec, rope_spec, vec_spec, vec_spec],
        out_specs=[pl.BlockSpec((1, ATTN_GROUP, tm, HEAD_DIM),
                                lambda b, i, j: (b, jnp.minimum(j, ZQ_TILES - 1), i, 0)),
                   pl.BlockSpec((1, tm, PROJ_TN),
                                lambda b, i, j: (b, i, jnp.clip(j - ZQ_TILES, 0, ZB_TILES - 1))),
                   pl.BlockSpec((1, tm, PROJ_TN),
                                lambda b, i, j: (b, i, jnp.maximum(j - ZF_TILE0, 0)))],
        out_shape=[jax.ShapeDtypeStruct((bsz, ATTN_HEADS, t, HEAD_DIM), BF16),
                   jax.ShapeDtypeStruct((bsz, t, ZB_W), BF16),
                   jax.ShapeDtypeStruct((bsz, t, ZF_W), F32)],
        compiler_params=_cparams(("parallel", "parallel", "arbitrary")),
    )(u, w, cos, sin_lo, sin_hi, qn, kn)


NEG_BIG = -1e30
AHEAD = 1
ONES_ROWS = 2 * SUBLANES
BOUND_MARGIN = 1.02
MAX_SAFE_SHIFT = 50.0


def _attn_kernel(q_ref, k_ref, v_ref, o_ref, m_ref, acc_ref, s_ref, kmax_ref, *,
                 tq, tk, rb, t_len, ctx_len, unroll):
    qi = pl.program_id(2)
    nr = tq // rb
    per_kb = ATTN_GROUP * nr
    n_units = (t_len // tk) * per_kb

    @pl.when(qi == 0)
    def _():
        k = k_ref[0].astype(F32)
        kmax_ref[...] = jnp.broadcast_to(
            jnp.max(jnp.sum(k * k, axis=1, keepdims=True), axis=0, keepdims=True),
            kmax_ref.shape)

    q_all = q_ref[0].reshape(per_kb * rb, HEAD_DIM).astype(F32)
    qsq = _dot_nt(jnp.ones((SUBLANES, HEAD_DIM), BF16), (q_all * q_all).astype(BF16))[0:1]
    shift = jnp.sqrt(qsq * kmax_ref[0:1, 0:1]) * BOUND_MARGIN
    for slot in range(per_kb):
        m_ref[slot] = shift[:, slot * rb:(slot + 1) * rb]
    bounded = jnp.max(shift) <= MAX_SAFE_SHIFT
    acc_ref[...] = jnp.zeros(acc_ref.shape, F32)

    def unit(n):
        kb = n // per_kb
        return kb, n - kb * per_kb

    def scores(n):
        kb, slot = unit(n)
        h = slot // nr
        r0 = pl.multiple_of((slot - h * nr) * rb, rb)
        k0 = pl.multiple_of(kb * tk, tk)
        return _dot_nt(k_ref[0, pl.ds(k0, tk), :], q_ref[0, h, pl.ds(r0, rb), :])

    def update(n, s, mixed, fixed_shift):
        kb, slot = unit(n)
        if mixed:
            r0 = (slot - (slot // nr) * nr) * rb
            qrow = r0 + lax.broadcasted_iota(jnp.int32, (1, rb), 1)
            key = kb * tk + lax.broadcasted_iota(jnp.int32, (tk, 1), 0)
            s = jnp.where(qrow < ctx_len, jnp.where(key < ctx_len, s, NEG_BIG), s)
        if fixed_shift:
            p = jnp.exp2(s - m_ref[slot]).astype(BF16)
            acc_ref[slot] += _dot(v_ref[0, 0, kb], p)
        else:
            m_prev = m_ref[slot]
            m_new = jnp.maximum(m_prev, jnp.max(s, axis=0, keepdims=True))
            alpha = jnp.exp2(m_prev - m_new)
            p = jnp.exp2((s - m_new).astype(BF16))
            m_ref[slot] = m_new
            acc_ref[slot] = alpha * acc_ref[slot] + _dot(v_ref[0, 0, kb], p)

    def run(mixed, fixed_shift):
        if not fixed_shift:
            m_ref[...] = jnp.full(m_ref.shape, NEG_BIG, F32)
        for a in range(AHEAD):
            s_ref[a] = scores(a)

        def body(it, carry):
            n0 = it * unroll
            s = [s_ref[a] for a in range(AHEAD)]
            for u in range(0, unroll, AHEAD):
                s_next = [scores(jnp.minimum(n0 + u + AHEAD + a, n_units - 1))
                          for a in range(AHEAD)]
                for a in range(AHEAD):
                    update(n0 + u + a, s[a], mixed, fixed_shift)
                s = s_next
            for a in range(AHEAD):
                s_ref[a] = s[a]
            return carry

        lax.fori_loop(0, n_units // unroll, body, 0)

    first = qi == 0
    later = jnp.logical_not(first)
    unbounded = jnp.logical_not(bounded)
    pl.when(first & bounded)(lambda: run(True, True))
    pl.when(later & bounded)(lambda: run(False, True))
    pl.when(first & unbounded)(lambda: run(True, False))
    pl.when(later & unbounded)(lambda: run(False, False))

    for h in range(ATTN_GROUP):
        for r in range(nr):
            slot = h * nr + r
            o_t = acc_ref[slot, :HEAD_DIM] * (1.0 / acc_ref[slot, HEAD_DIM:HEAD_DIM + 1])
            o_ref[0, r * rb:(r + 1) * rb, h * HEAD_DIM:(h + 1) * HEAD_DIM] = o_t.T


def _attn_call(zq, zb, *, ctx_len, tq, tk):
    bsz, t, _ = zb.shape
    assert ctx_len <= tq and ctx_len <= tk
    rb = 256 if tq % 256 == 0 else tq
    per_kb = ATTN_GROUP * (tq // rb)
    unroll = max(u for u in range(AHEAD, 13, AHEAD) if per_kb % u == 0)
    k_blk = ZB_AK_OFF // HEAD_DIM
    v_t = zb[:, :, ZB_AV_OFF:ZB_AV_OFF + ATTN_KV_HEADS * HEAD_DIM]
    v_t = v_t.reshape(bsz, t // tk, tk, ATTN_KV_HEADS, HEAD_DIM).transpose(0, 3, 1, 4, 2)
    v_t = jnp.concatenate([v_t, jnp.ones(v_t.shape[:3] + (ONES_ROWS, tk), v_t.dtype)], axis=3)
    kern = functools.partial(_attn_kernel, tq=tq, tk=tk, rb=rb, t_len=t, ctx_len=ctx_len,
                             unroll=unroll)
    return pl.pallas_call(
        kern,
        grid=(bsz, ATTN_KV_HEADS, t // tq),
        in_specs=[pl.BlockSpec((1, ATTN_GROUP, tq, HEAD_DIM), lambda b, g, qi: (b, g, qi, 0)),
                  pl.BlockSpec((1, t, HEAD_DIM), lambda b, g, qi: (b, 0, k_blk + g)),
                  pl.BlockSpec((1, 1, t // tk, HEAD_DIM + ONES_ROWS, tk),
                               lambda b, g, qi: (b, g, 0, 0, 0))],
        out_specs=pl.BlockSpec((1, tq, ATTN_GROUP * HEAD_DIM), lambda b, g, qi: (b, qi, g)),
        out_shape=jax.ShapeDtypeStruct((bsz, t, ATTN_HEADS * HEAD_DIM), F32),
        scratch_shapes=[pltpu.VMEM((per_kb, 1, rb), F32),
                        pltpu.VMEM((per_kb, HEAD_DIM + ONES_ROWS, rb), F32),
                        pltpu.VMEM((AHEAD, tk, rb), F32),
                        pltpu.VMEM((SUBLANES, LANES), F32)],
        compiler_params=_cparams(("parallel", "parallel", "arbitrary")),
    )(zq, zb, v_t)


def _bwd_chunk(p, n_ctx_chunks, n_chunks):
    return jnp.where(p < n_ctx_chunks, n_ctx_chunks - 1 - p, n_chunks + n_ctx_chunks - 1 - p)


def _ret_kernel(ld_ref, qf_ref, kf_ref, vf_ref, qb_ref, kb_ref, vb_ref, yf_ref, yb_ref,
                sf_ref, sb_ref, *, chunk):
    c = chunk
    p = pl.program_id(1)

    @pl.when(p == 0)
    def _():
        sf_ref[...] = jnp.zeros(sf_ref.shape, F32)
        sb_ref[...] = jnp.zeros(sb_ref.shape, F32)

    ii = lax.broadcasted_iota(jnp.int32, (c, c), 0)
    jj = lax.broadcasted_iota(jnp.int32, (c, c), 1)
    dist = (ii - jj).astype(F32)
    pos = lax.broadcasted_iota(jnp.int32, (c, 1), 0).astype(F32)
    full = jnp.full((1, 1), float(c), F32)

    for h in range(RET_HEADS):
        ldf = ld_ref[0, h]
        ldb = ld_ref[1, h]
        qs = slice(h * RET_QK_DIM, (h + 1) * RET_QK_DIM)
        vs = slice(h * RET_V_DIM, (h + 1) * RET_V_DIM)
        decay = jnp.where(dist >= 0, jnp.exp(ldf * dist), 0.0) \
            + jnp.where(dist <= 0, jnp.exp(-ldb * dist), 0.0)
        q = qf_ref[0, :, qs]
        k = kf_ref[0, :, qs]
        v = vf_ref[0, :, vs]
        y = _dot((_dot_nt(q, k) * decay).astype(BF16), v)
        y = y + _dot(q, sf_ref[h].astype(BF16)) * jnp.exp(ldf * (pos + 1.0))
        yf_ref[0, :, vs] = y
        wv = (v.astype(F32) * jnp.exp(ldf * (c - 1.0 - pos))).astype(BF16)
        sf_ref[h] = sf_ref[h] * jnp.exp(ldf * full) + _dot_tn(k, wv)
        q = qb_ref[0, :, qs]
        k = kb_ref[0, :, qs]
        v = vb_ref[0, :, vs]
        yb_ref[0, :, vs] = _dot(q, sb_ref[h].astype(BF16)) * jnp.exp(ldb * (c - pos))
        wv = (v.astype(F32) * jnp.exp(ldb * pos)).astype(BF16)
        sb_ref[h] = sb_ref[h] * jnp.exp(ldb * full) + _dot_tn(k, wv)


def _ret_call(zb, log_decay, *, ctx_len, chunk):
    bsz, t, _ = zb.shape
    nc = t // chunk
    ncc = ctx_len // chunk
    qw = RET_HEADS * RET_QK_DIM
    vw = RET_HEADS * RET_V_DIM
    q_blk, k_blk, v_blk = ZB_RQ_OFF // qw, ZB_RK_OFF // qw, ZB_RV_OFF // vw
    bwd = functools.partial(_bwd_chunk, n_ctx_chunks=ncc, n_chunks=nc)
    kern = functools.partial(_ret_kernel, chunk=chunk)
    out = jax.ShapeDtypeStruct((bsz, t, vw), F32)
    return pl.pallas_call(
        kern,
        grid=(bsz, nc),
        in_specs=[pl.BlockSpec(memory_space=pltpu.SMEM),
                  pl.BlockSpec((1, chunk, qw), lambda b, p: (b, p, q_blk)),
                  pl.BlockSpec((1, chunk, qw), lambda b, p: (b, p, k_blk)),
                  pl.BlockSpec((1, chunk, vw), lambda b, p: (b, p, v_blk)),
                  pl.BlockSpec((1, chunk, qw), lambda b, p: (b, bwd(p), q_blk)),
                  pl.BlockSpec((1, chunk, qw), lambda b, p: (b, bwd(p), k_blk)),
                  pl.BlockSpec((1, chunk, vw), lambda b, p: (b, bwd(p), v_blk))],
        out_specs=[pl.BlockSpec((1, chunk, vw), lambda b, p: (b, p, 0)),
                   pl.BlockSpec((1, chunk, vw), lambda b, p: (b, bwd(p), 0))],
        out_shape=[out, out],
        scratch_shapes=[pltpu.VMEM((RET_HEADS, RET_QK_DIM, RET_V_DIM), F32),
                        pltpu.VMEM((RET_HEADS, RET_QK_DIM, RET_V_DIM), F32)],
        compiler_params=_cparams(("parallel", "arbitrary")),
    )(log_decay, zb, zb, zb, zb, zb, zb)


CONV_HALO = SUBLANES


def _conv_kernel(prev_ref, cur_ref, next_ref, w_ref, b_ref, o_ref, e_ref, *, tc, t_len, ctx_len):
    i = pl.program_id(1)
    e_ref[0:CONV_HALO] = prev_ref[0]
    e_ref[CONV_HALO:CONV_HALO + tc] = cur_ref[0]
    e_ref[CONV_HALO + tc:] = next_ref[0]
    pad = (SSM_CONV - 1) // 2
    lo = i * tc

    def conv(masked):
        tok = lo + lax.broadcasted_iota(jnp.int32, (tc, 1), 0)
        acc = jnp.broadcast_to(b_ref[...], (tc, ZF_XBC_W))
        for tap in range(SSM_CONV):
            off = tap - pad
            src = e_ref[CONV_HALO + off:CONV_HALO + off + tc, :]
            if masked:
                nb = tok + off
                same = ((tok - ctx_len) ^ (nb - ctx_len)) >= 0
                src = jnp.where(same & (nb >= 0) & (nb < t_len), src, 0.0)
            acc = acc + src * w_ref[tap:tap + 1, :]
        o_ref[0] = _silu(acc)

    def near(edge):
        return (lo - pad < edge) & (edge < lo + tc + pad)
    touches = near(0) | near(ctx_len) | near(t_len)
    pl.when(touches)(lambda: conv(True))
    pl.when(jnp.logical_not(touches))(lambda: conv(False))


def _conv_call(zf, conv_w, conv_b, *, ctx_len, tc):
    bsz, t, _ = zf.shape
    col = ZF_XBC_OFF // ZF_XBC_W
    per = tc // CONV_HALO
    last = t // CONV_HALO - 1
    kern = functools.partial(_conv_kernel, tc=tc, t_len=t, ctx_len=ctx_len)
    return pl.pallas_call(
        kern,
        grid=(bsz, t // tc),
        in_specs=[pl.BlockSpec((1, CONV_HALO, ZF_XBC_W),
                               lambda b, i: (b, jnp.maximum(i * per - 1, 0), col)),
                  pl.BlockSpec((1, tc, ZF_XBC_W), lambda b, i: (b, i, col)),
                  pl.BlockSpec((1, CONV_HALO, ZF_XBC_W),
                               lambda b, i: (b, jnp.minimum((i + 1) * per, last), col)),
                  pl.BlockSpec((SUBLANES, ZF_XBC_W), lambda b, i: (0, 0)),
                  pl.BlockSpec((1, ZF_XBC_W), lambda b, i: (0, 0))],
        out_specs=pl.BlockSpec((1, tc, ZF_XBC_W), lambda b, i: (b, i, 0)),
        out_shape=jax.ShapeDtypeStruct((bsz, t, ZF_XBC_W), F32),
        scratch_shapes=[pltpu.VMEM((tc + 2 * CONV_HALO, ZF_XBC_W), F32)],
        compiler_params=_cparams(("parallel", "parallel")),
    )(zf, zf, zf, conv_w, conv_b)


def _split3(x):
    p1 = x.astype(BF16)
    r1 = x - p1.astype(F32)
    p2 = r1.astype(BF16)
    p3 = (r1 - p2.astype(F32)).astype(BF16)
    return p1, p2, p3


def _cumsum_rows(x, tri):
    p1, p2, p3 = _split3(x)
    return _dot(tri, p1) + _dot(tri, p2) + _dot(tri, p3)


def _expand_heads(a, sel_ref):
    hi = a.astype(BF16)
    lo = (a - hi.astype(F32)).astype(BF16)
    return _dot(jnp.concatenate([hi, lo], axis=1), sel_ref[...])


def _head_selector(off):
    sel = np.zeros((LANES, SSM_INNER), np.float32)
    for h in range(SSM_HEADS):
        sel[off + h, h * SSM_HEAD_DIM:(h + 1) * SSM_HEAD_DIM] = 1.0
    return jnp.asarray(np.concatenate([sel, sel], axis=0), BF16)


def _ssd_kernel(xf_ref, df_ref, xb_ref, db_ref, bias_ref, alog_ref, skip_ref, self_ref, selb_ref,
                yf_ref, yb_ref, sf_ref, sb_ref, *, chunk, sub):
    c = chunk
    p = pl.program_id(1)
    nh = SSM_HEADS
    gw = SSM_HPG * SSM_HEAD_DIM
    bc_w = SSM_GROUPS * SSM_STATE

    @pl.when(p == 0)
    def _():
        sf_ref[...] = jnp.zeros(sf_ref.shape, F32)
        sb_ref[...] = jnp.zeros(sb_ref.shape, F32)

    ii = lax.broadcasted_iota(jnp.int32, (c, c), 0)
    jj = lax.broadcasted_iota(jnp.int32, (c, c), 1)
    lower = ii >= jj
    upper = jj >= ii
    tri = jnp.where(lower, 1.0, 0.0).astype(BF16)
    lane_lo = lax.broadcasted_iota(jnp.int32, (c, LANES), 1) < SSM_HEAD_DIM
    lane_hi = jnp.logical_not(lane_lo)

    def operands(x_ref, rows):
        xs = x_ref[0, rows, 0:SSM_INNER]
        bm = x_ref[0, rows, SSM_INNER:SSM_INNER + bc_w].astype(BF16)
        cm = x_ref[0, rows, SSM_INNER + bc_w:].astype(BF16)
        return xs, bm, cm

    def gates(d_ref, rows):
        raw = d_ref[0, rows, :] + bias_ref[...]
        dt = jnp.maximum(raw, 0.0) + jnp.log1p(jnp.exp(-jnp.abs(raw)))
        la = dt * -jnp.exp(alog_ref[...])
        acum = _cumsum_rows(la, tri)
        return dt, acum, acum - la

    def within(xs, bm, cm, dt, acum, excl):
        dt_t = dt.T
        acum_t = acum.T
        excl_t = excl.T
        xs_b = xs.astype(BF16)
        pieces = []
        for g in range(SSM_GROUPS):
            gs = slice(g * SSM_STATE, (g + 1) * SSM_STATE)
            scores = _dot_nt(cm[:, gs], bm[:, gs])
            for m in range(SSM_HPG // 2):
                pair = g * SSM_HPG // 2 + m
                xp = xs_b[:, pair * LANES:(pair + 1) * LANES]
                acc = None
                for e in range(2):
                    hf = 2 * pair + e
                    hb = nh + hf
                    dec_f = jnp.where(lower, jnp.exp(acum[:, hf:hf + 1] - acum_t[hf:hf + 1, :]), 0.0)
                    dec_b = jnp.where(upper, jnp.exp(excl_t[hb:hb + 1, :] - excl[:, hb:hb + 1]), 0.0)
                    wm = scores * (dec_f * dt_t[hf:hf + 1, :] + dec_b * dt_t[hb:hb + 1, :])
                    xh = jnp.where(lane_lo if e == 0 else lane_hi, xp, 0.0)
                    part = _dot(wm.astype(BF16), xh)
                    acc = part if acc is None else acc + part
                pieces.append(acc)
        return jnp.concatenate(pieces, axis=1)

    def carried(s_ref, xs, bm, cm, carry, wdt, keep):
        xw = (xs * wdt).astype(BF16)
        inter = []
        for g in range(SSM_GROUPS):
            gs = slice(g * SSM_STATE, (g + 1) * SSM_STATE)
            cs = slice(g * gw, (g + 1) * gw)
            inter.append(_dot(cm[:, gs], s_ref[g].astype(BF16)))
            s_ref[g] = s_ref[g] * keep[:, cs] + _dot_tn(bm[:, gs], xw[:, cs])
        return jnp.concatenate(inter, axis=1) * carry

    rows = [pl.ds(i * c, c) for i in range(sub)]
    fwd = [operands(xf_ref, r) + gates(df_ref, r) for r in rows]
    bwd = [operands(xb_ref, r) + gates(db_ref, r) for r in rows]

    fwd_scale = []
    for xs, bm, cm, dt, acum, excl in fwd:
        last = acum[c - 1:c, :]
        both = _expand_heads(
            jnp.concatenate([jnp.exp(acum), jnp.exp(last - acum) * dt], axis=0), self_ref)
        fwd_scale.append((both[:c], both[c:], both[c - 1:c]))
    bwd_scale = []
    for xs, bm, cm, dt, acum, excl in bwd:
        total = acum[c - 1:c, :]
        both = _expand_heads(
            jnp.concatenate([jnp.exp(total - excl), jnp.exp(excl) * dt], axis=0), selb_ref)
        bwd_scale.append((both[:c], both[c:], both[0:1]))

    y_within = [within(*f) for f in fwd]

    for i in range(sub):
        xs, bm, cm = fwd[i][:3]
        y = y_within[i] + carried(sf_ref, xs, bm, cm, *fwd_scale[i]) + skip_ref[...] * xs
        yf_ref[0, rows[i], :] = y
    for i in reversed(range(sub)):
        xs, bm, cm = bwd[i][:3]
        yb_ref[0, rows[i], :] = carried(sb_ref, xs, bm, cm, *bwd_scale[i])


def _ssd_call(xbc, zf, bias_row, alog_row, skip_row, *, ctx_len, chunk):
    bsz, t, _ = xbc.shape
    sub = 2 if ctx_len % (2 * chunk) == 0 and t % (2 * chunk) == 0 else 1
    blk = sub * chunk
    nblk = t // blk
    dt_blk = ZF_DT_OFF // LANES
    bwd = functools.partial(_bwd_chunk, n_ctx_chunks=ctx_len // blk, n_chunks=nblk)
    kern = functools.partial(_ssd_kernel, chunk=chunk, sub=sub)
    out = jax.ShapeDtypeStruct((bsz, t, SSM_INNER), F32)
    row = lambda w: pl.BlockSpec((1, w), lambda b, p: (0, 0))
    return pl.pallas_call(
        kern,
        grid=(bsz, nblk),
        in_specs=[pl.BlockSpec((1, blk, ZF_XBC_W), lambda b, p: (b, p, 0)),
                  pl.BlockSpec((1, blk, LANES), lambda b, p: (b, p, dt_blk)),
                  pl.BlockSpec((1, blk, ZF_XBC_W), lambda b, p: (b, bwd(p), 0)),
                  pl.BlockSpec((1, blk, LANES), lambda b, p: (b, bwd(p), dt_blk)),
                  row(LANES), row(LANES), row(SSM_INNER),
                  pl.BlockSpec((2 * LANES, SSM_INNER), lambda b, p: (0, 0)),
                  pl.BlockSpec((2 * LANES, SSM_INNER), lambda b, p: (0, 0))],
        out_specs=[pl.BlockSpec((1, blk, SSM_INNER), lambda b, p: (b, p, 0)),
                   pl.BlockSpec((1, blk, SSM_INNER), lambda b, p: (b, bwd(p), 0))],
        out_shape=[out, out],
        scratch_shapes=[pltpu.VMEM((SSM_GROUPS, SSM_STATE, SSM_HPG * SSM_HEAD_DIM), F32),
                        pltpu.VMEM((SSM_GROUPS, SSM_STATE, SSM_HPG * SSM_HEAD_DIM), F32)],
        compiler_params=_cparams(("parallel", "arbitrary")),
    )(xbc, zf, xbc, zf, bias_row, alog_row, skip_row,
      _head_selector(0), _head_selector(SSM_HEADS))


def _merge_kernel(attn_ref, retf_ref, retb_ref, ssmf_ref, ssmb_ref, mg_ref, ag_ref, rg_ref,
                  sz_ref, gnw_ref, snw_ref, wb_ref, o_ref):
    d = D_MODEL
    gate_in = lambda ref, lo=0, hi=None: ref[0, :, lo:hi].astype(F32)
    a = attn_ref[0] * _silu(gate_in(ag_ref))

    ret = retf_ref[0] + retb_ref[0]
    parts = []
    for h in range(RET_HEADS):
        r = ret[:, h * RET_V_DIM:(h + 1) * RET_V_DIM]
        rc = r - jnp.mean(r, axis=-1, keepdims=True)
        var = jnp.mean(rc * rc, axis=-1, keepdims=True)
        parts.append(rc * lax.rsqrt(var + NORM_EPS))
    r = jnp.concatenate(parts, axis=1) * gnw_ref[...] * _silu(gate_in(rg_ref))

    s = (ssmf_ref[0] + ssmb_ref[0]) * _silu(gate_in(sz_ref))
    gwid = SSM_INNER // SSM_GROUPS
    parts = []
    for g in range(SSM_GROUPS):
        sg = s[:, g * gwid:(g + 1) * gwid]
        parts.append(sg * lax.rsqrt(jnp.mean(sg * sg, axis=-1, keepdims=True) + NORM_EPS))
    s = jnp.concatenate(parts, axis=1) * snw_ref[...]

    merged = _sigmoid(gate_in(mg_ref, 0, d)) * _dot(a.astype(BF16), wb_ref[0])
    merged = merged + _sigmoid(gate_in(mg_ref, d, 2 * d)) * _dot(r.astype(BF16), wb_ref[1])
    merged = merged + _sigmoid(gate_in(mg_ref, 2 * d, 3 * d)) * _dot(s.astype(BF16), wb_ref[2])
    o_ref[0] = merged.astype(BF16)


def _merge_call(attn, retf, retb, ssmf, ssmb, zb, gnw, snw, wb, *, layer, tm):
    bsz, t, _ = attn.shape
    d = D_MODEL
    bw = BRANCH_WIDTH
    act = lambda blk: pl.BlockSpec((1, tm, bw), lambda b, i: (b, i, blk))
    vec = pl.BlockSpec((None, 1, bw), lambda b, i: (layer, 0, 0))
    return pl.pallas_call(
        _merge_kernel,
        grid=(bsz, t // tm),
        in_specs=[act(0), act(0), act(0), act(0), act(0),
                  pl.BlockSpec((1, tm, N_BRANCHES * d),
                               lambda b, i: (b, i, ZB_MG_OFF // (N_BRANCHES * d))),
                  act(ZB_AG_OFF // bw), act(ZB_RG_OFF // bw), act(ZB_SZ_OFF // bw),
                  vec, vec,
                  pl.BlockSpec((None, N_BRANCHES, bw, d), lambda b, i: (layer, 0, 0, 0),
                               pipeline_mode=pl.Buffered(1))],
        out_specs=pl.BlockSpec((1, tm, d), lambda b, i: (b, i, 0)),
        out_shape=jax.ShapeDtypeStruct((bsz, t, d), BF16),
        compiler_params=_cparams(("parallel", "parallel")),
    )(attn, retf, retb, ssmf, ssmb, zb, zb, zb, zb, gnw, snw, wb)


def _modulate(x, mod_ref, b, n_batch, is_ctx):
    d = D_MODEL
    shift = _modulation_rows(mod_ref, b, n_batch, is_ctx, 0, d)
    scale = _modulation_rows(mod_ref, b, n_batch, is_ctx, d, 2 * d)
    return (x * (1.0 + scale) + shift).astype(BF16)


def _modulate_kernel(x_ref, mod_ref, u_ref, *, tm, n_batch, ctx_len):
    row = pl.program_id(1) * tm + lax.broadcasted_iota(jnp.int32, (tm, 1), 0)
    u_ref[0] = _modulate(x_ref[0], mod_ref, pl.program_id(0), n_batch, row < ctx_len)


def _modulate_call(xs, mod, *, layer, n_batch, ctx_len, tm):
    bsz, t, d = xs.shape
    kern = functools.partial(_modulate_kernel, tm=tm, n_batch=n_batch, ctx_len=ctx_len)
    tok = pl.BlockSpec((1, tm, d), lambda b, i: (b, i, 0))
    return pl.pallas_call(
        kern,
        grid=(bsz, t // tm),
        in_specs=[tok, pl.BlockSpec((None, SUBLANES, 3 * d), lambda b, i: (layer, 0, 0))],
        out_specs=tok,
        out_shape=jax.ShapeDtypeStruct((bsz, t, d), BF16),
        compiler_params=_cparams(("parallel", "parallel")),
    )(xs, mod)


def _out_kernel(m_ref, x_ref, mod_ref, nxt_ref, w_ref, g_ref, b_ref, o_ref, *u_ref,
                tm, row0, n_batch, ctx_len):
    b = pl.program_id(0)
    d = D_MODEL
    row = row0 + pl.program_id(1) * tm + lax.broadcasted_iota(jnp.int32, (tm, 1), 0)
    is_ctx = row < ctx_len
    gate = _modulation_rows(mod_ref, b, n_batch, is_ctx, 2 * d, 3 * d)
    y = _dot(m_ref[0], w_ref[...])
    z = DEEPNORM_ALPHA * x_ref[0] + gate * y
    zc = z - jnp.mean(z, axis=-1, keepdims=True)
    var = jnp.mean(zc * zc, axis=-1, keepdims=True)
    x_new = zc * lax.rsqrt(var + NORM_EPS) * g_ref[...] + b_ref[...]
    o_ref[0] = x_new
    if u_ref:
        u_ref[0][0] = _modulate(x_new, nxt_ref, b, n_batch, is_ctx)


def _out_call(merged, xs, mod, w_out, ln_g, ln_b, *, layer, last, n_batch, ctx_len, tm):
    bsz, t, d = xs.shape
    depth = mod.shape[0]
    blk0 = ctx_len // tm if last else 0
    assert not last or ctx_len % tm == 0
    rows = t - blk0 * tm
    kern = functools.partial(_out_kernel, tm=tm, row0=blk0 * tm, n_batch=n_batch,
                             ctx_len=ctx_len)
    tok_in = pl.BlockSpec((1, tm, d), lambda b, i: (b, i + blk0, 0))
    tok_out = pl.BlockSpec((1, tm, d), lambda b, i: (b, i, 0))
    mod_at = lambda l: pl.BlockSpec((None, SUBLANES, 3 * d), lambda b, i: (l, 0, 0))
    vec = pl.BlockSpec((None, 1, d), lambda b, i: (layer, 0, 0))
    x_shape = jax.ShapeDtypeStruct((bsz, rows, d), F32)
    return pl.pallas_call(
        kern,
        grid=(bsz, rows // tm),
        in_specs=[tok_in, tok_in, mod_at(layer), mod_at(min(layer + 1, depth - 1)),
                  pl.BlockSpec((None, d, d), lambda b, i: (layer, 0, 0),
                               pipeline_mode=pl.Buffered(1)),
                  vec, vec],
        out_specs=tok_out if last else [tok_out, tok_out],
        out_shape=x_shape if last else [x_shape, jax.ShapeDtypeStruct((bsz, rows, d), BF16)],
        compiler_params=_cparams(("parallel", "parallel")),
    )(merged, xs, mod, mod, w_out, ln_g, ln_b)


PACK_COLS = 256


def _pack_kernel(w_ref, o_ref):
    row = 0
    for name in ("aq",) + ZB_ORDER + ZF_ORDER:
        src, width = _REF_SPLITS[name]
        o_ref[0, row:row + width, :] = w_ref[0, src:src + width, :].astype(BF16)
        row += width
    o_ref[0, row:, :] = jnp.zeros((W_PACKED - row, o_ref.shape[2]), BF16)


def _pack_w_in(w_in):
    depth, d, width = w_in.shape
    w_t = jnp.swapaxes(w_in, 1, 2)
    return pl.pallas_call(
        _pack_kernel,
        grid=(depth, d // PACK_COLS),
        in_specs=[pl.BlockSpec((1, width, PACK_COLS), lambda l, i: (l, 0, i))],
        out_specs=pl.BlockSpec((1, W_PACKED, PACK_COLS), lambda l, i: (l, 0, i)),
        out_shape=jax.ShapeDtypeStruct((depth, W_PACKED, d), BF16),
        compiler_params=_cparams(("parallel", "parallel")),
    )(w_t)


def _rope_tables(rows, ctx_len):
    row = jnp.repeat(jnp.arange(rows, dtype=F32), GRID_W)
    col = jnp.tile(jnp.arange(GRID_W, dtype=F32), rows)
    n_freq = HEAD_DIM // 4
    inv = ROPE_BASE ** (-jnp.arange(n_freq, dtype=F32) / n_freq)
    ang_r = row[:, None] * inv
    ang_c = col[:, None] * inv
    ang = jnp.concatenate([ang_r, ang_r, ang_c, ang_c], -1)
    cos = jnp.concatenate([jnp.ones((ctx_len, HEAD_DIM), F32), jnp.cos(ang)], 0)
    sin = jnp.concatenate([jnp.zeros((ctx_len, HEAD_DIM), F32), jnp.sin(ang)], 0)
    first = (np.arange(HEAD_DIM) % (HEAD_DIM // 2)) < HEAD_DIM // 4
    sin_lo = jnp.where(first, -sin, 0.0)
    sin_hi = jnp.where(first, 0.0, sin)
    return cos, sin_lo, sin_hi


def _pad_lanes(v, width):
    return jnp.pad(v.reshape(1, -1), ((0, 0), (0, width - v.size)))


def _tile_choices(t, ctx_len):
    def largest(cap, mult, also=0):
        best = mult
        for cand in range(mult, cap + 1, mult):
            if t % cand == 0 and also % cand == 0:
                best = cand
        return best
    return dict(proj=largest(2112, 16), attn=largest(768, LANES), conv=largest(528, SUBLANES),
                merge=largest(256, 16), out=largest(528, 16),
                out_last=largest(528, 16, also=ctx_len))


def kernel(x, c, ctx, c_ctx, ada_w, ada_b, w_in, attn_q_norm, attn_k_norm, ret_log_decay, ret_gn_w,
           ssm_conv_w, ssm_conv_b, ssm_dt_bias, ssm_a_log, ssm_d, ssm_norm_w, w_branch, w_out,
           ln_g, ln_b):
    bsz, seq, d = x.shape
    ctx_len = ctx.shape[1]
    depth = w_in.shape[0]
    t = seq + ctx_len
    chunk = LANES
    ret_chunk = 256 if ctx_len % 256 == 0 and t % 256 == 0 else chunk
    assert d == D_MODEL and bsz < SUBLANES and seq % GRID_W == 0
    assert ctx_len % chunk == 0 and t % chunk == 0
    tiles = _tile_choices(t, ctx_len)

    xs = jnp.concatenate([ctx, x], axis=1)
    cvec = jnp.zeros((SUBLANES, d), F32).at[:bsz].set(c).at[bsz].set(c_ctx)
    mod = _ada_call(cvec, ada_w, ada_b)
    cos, sin_lo, sin_hi = _rope_tables(seq // GRID_W, ctx_len)
    w_packed = _pack_w_in(w_in)
    wb = w_branch.astype(BF16)
    wo = w_out.astype(BF16)
    conv_w = jnp.pad(ssm_conv_w, ((0, 0), (0, SUBLANES - SSM_CONV), (0, 0)))
    per_layer_row = lambda v: v.reshape(depth, 1, -1)
    qn, kn = per_layer_row(attn_q_norm), per_layer_row(attn_k_norm)
    gnw, snw = per_layer_row(ret_gn_w), per_layer_row(ssm_norm_w)
    lng, lnb = per_layer_row(ln_g), per_layer_row(ln_b)

    u = _modulate_call(xs, mod, layer=0, n_batch=bsz, ctx_len=ctx_len, tm=tiles["out"])
    for l in range(depth):
        zq, zb, zf = _inproj_call(u, w_packed, cos, sin_lo, sin_hi, qn, kn, layer=l,
                                  tm=tiles["proj"])
        attn = _attn_call(zq, zb, ctx_len=ctx_len, tq=tiles["attn"], tk=tiles["attn"])
        retf, retb = _ret_call(zb, ret_log_decay[l], ctx_len=ctx_len, chunk=ret_chunk)
        xbc = _conv_call(zf, conv_w[l], ssm_conv_b[l].reshape(1, -1), ctx_len=ctx_len,
                         tc=tiles["conv"])
        ssmf, ssmb = _ssd_call(xbc, zf, _pad_lanes(ssm_dt_bias[l], LANES),
                               _pad_lanes(ssm_a_log[l], LANES),
                               jnp.repeat(ssm_d[l], SSM_HEAD_DIM).reshape(1, -1),
                               ctx_len=ctx_len, chunk=chunk)
        merged = _merge_call(attn, retf, retb, ssmf, ssmb, zb, gnw, snw, wb, layer=l,
                             tm=tiles["merge"])
        last = l == depth - 1
        out = _out_call(merged, xs, mod, wo, lng, lnb, layer=l, last=last, n_batch=bsz,
                        ctx_len=ctx_len, tm=tiles["out_last" if last else "out"])
        if last:
            return out
        xs, u = out
```

```python
import functools

import numpy as np
import jax
import jax.numpy as jnp
from jax import lax
from jax.experimental import pallas as pl
from jax.experimental.pallas import tpu as pltpu

F32 = jnp.float32
BF16 = jnp.bfloat16

D_MODEL = 2048
GRID_W = 64
HEAD_DIM = 128
ATTN_HEADS = 8
ATTN_KV_HEADS = 2
ATTN_GROUP = ATTN_HEADS // ATTN_KV_HEADS
ROPE_BASE = 10000.0
RET_HEADS = 4
RET_QK_DIM = 128
RET_V_DIM = 256
SSM_INNER = 1024
SSM_HEAD_DIM = 64
SSM_HEADS = 16
SSM_GROUPS = 2
SSM_HPG = SSM_HEADS // SSM_GROUPS
SSM_STATE = 128
SSM_CONV = 5
N_BRANCHES = 3
BRANCH_WIDTH = 1024
DEPTH = 4
DEEPNORM_ALPHA = (2 * DEPTH) ** 0.25
NORM_EPS = 1e-6
ATTN_Q_SCALE = HEAD_DIM ** -0.5 * float(np.log2(np.e))

_REF_SPLITS = dict(aq=(0, 1024), ak=(1024, 256), av=(1280, 256), ag=(1536, 1024),
                   rq=(2560, 512), rk=(3072, 512), rv=(3584, 1024), rg=(4608, 1024),
                   sx=(5632, 1024), sb=(6656, 256), sc=(6912, 256), sdt=(7168, 32),
                   sz=(7200, 1024), mg=(8224, 6144))

LANES = 128
SUBLANES = 8
VMEM_LIMIT = 56 * 1024 * 1024

PROJ_TN = 512
ZQ_TILES = ATTN_HEADS * HEAD_DIM // PROJ_TN
ZB_ORDER = ("mg", "ag", "rg", "sz", "rv", "ak", "av", "rq", "rk")
ZB_MG_OFF, ZB_AG_OFF, ZB_RG_OFF, ZB_SZ_OFF = 0, 6144, 7168, 8192
ZB_RV_OFF, ZB_AK_OFF, ZB_AV_OFF, ZB_RQ_OFF, ZB_RK_OFF = 9216, 10240, 10496, 10752, 11264
ZB_W = 11776
ZB_TILES = ZB_W // PROJ_TN
ZB_RAW_TILES = ZB_AK_OFF // PROJ_TN
ZF_TILE0 = ZQ_TILES + ZB_TILES
ZF_ORDER = ("sx", "sb", "sc", "sdt")
ZF_USED = 1024 + 256 + 256 + 32
ZF_W = 2048
ZF_XBC_OFF = 0
ZF_XBC_W = 1536
ZF_DT_OFF = 1536
W_PACKED = ZQ_TILES * PROJ_TN + ZB_W + ZF_W


def _sigmoid(x):
    return 0.5 * jnp.tanh(0.5 * x) + 0.5


def _silu(x):
    return x * _sigmoid(x)


def _cparams(sem):
    return pltpu.CompilerParams(dimension_semantics=sem, vmem_limit_bytes=VMEM_LIMIT)


def _dot(a, b):
    return jnp.dot(a, b, preferred_element_type=F32)


def _dot_nt(a, b):
    return lax.dot_general(a, b, (((1,), (1,)), ((), ())), preferred_element_type=F32)


def _dot_tn(a, b):
    return lax.dot_general(a, b, (((0,), (0,)), ((), ())), preferred_element_type=F32)


def _ada_kernel(c_ref, w_ref, b_ref, o_ref):
    s = _silu(c_ref[...]).astype(BF16)
    o_ref[0] = _dot(s, w_ref[0].astype(BF16)) + b_ref[0]


def _ada_call(cvec, ada_w, ada_b):
    depth, d, n = ada_w.shape
    tn = 768
    return pl.pallas_call(
        _ada_kernel,
        grid=(depth, n // tn),
        in_specs=[pl.BlockSpec((SUBLANES, d), lambda l, j: (0, 0)),
                  pl.BlockSpec((1, d, tn), lambda l, j: (l, 0, j)),
                  pl.BlockSpec((1, 1, tn), lambda l, j: (l, 0, j))],
        out_specs=pl.BlockSpec((1, SUBLANES, tn), lambda l, j: (l, 0, j)),
        out_shape=jax.ShapeDtypeStruct((depth, SUBLANES, n), F32),
        compiler_params=_cparams(("parallel", "parallel")),
    )(cvec, ada_w, ada_b.reshape(depth, 1, n))


def _rope(x, cos, sin_lo, sin_hi):
    return x * cos + pltpu.roll(x, 96, 1) * sin_lo + pltpu.roll(x, 32, 1) * sin_hi


def _modulation_rows(mod_ref, b, n_batch, is_ctx, lo, hi):
    mb = mod_ref[pl.ds(b, 1), lo:hi]
    mc = mod_ref[pl.ds(n_batch, 1), lo:hi]
    return jnp.where(is_ctx, mc, mb)


def _inproj_kernel(u_ref, w_ref, cos_ref, sl_ref, sh_ref, qn_ref, kn_ref,
                   zq_ref, zb_ref, zf_ref):
    j = pl.program_id(2)
    jb = j - ZQ_TILES

    def project():
        return _dot_nt(u_ref[0], w_ref[...])

    def rope(xh):
        return _rope(xh, cos_ref[...], sl_ref[...], sh_ref[...])

    def normed(w_row):
        def fn(xh):
            ms = jnp.mean(xh * xh, axis=-1, keepdims=True)
            return rope(xh * lax.rsqrt(ms + NORM_EPS) * w_row)
        return fn

    @pl.when((jb >= 0) & (jb < ZB_RAW_TILES))
    def _():
        zb_ref[0] = project().astype(BF16)

    @pl.when((j < ZQ_TILES) | (jb >= ZB_RAW_TILES))
    def _():
        acc = project()

        @pl.when(j >= ZF_TILE0)
        def _():
            zf_ref[0] = acc

        def head(h):
            return acc[:, h * LANES:(h + 1) * LANES]

        def per_head(fn, n):
            return jnp.concatenate([fn(head(h)) for h in range(n)], axis=1)

        @pl.when(j < ZQ_TILES)
        def _():
            fn = normed(qn_ref[...])
            for h in range(ATTN_GROUP):
                zq_ref[0, h] = (fn(head(h)) * ATTN_Q_SCALE).astype(BF16)

        @pl.when(jb == ZB_AK_OFF // PROJ_TN)
        def _():
            k = per_head(normed(kn_ref[...]), 2)
            zb_ref[0] = jnp.concatenate([k, acc[:, 2 * LANES:]], axis=1).astype(BF16)

        @pl.when(jb == ZB_RQ_OFF // PROJ_TN)
        def _():
            zb_ref[0] = per_head(rope, 4).astype(BF16)

        @pl.when(jb == ZB_RK_OFF // PROJ_TN)
        def _():
            zb_ref[0] = per_head(lambda xh: rope(xh * (RET_QK_DIM ** -0.5)), 4).astype(BF16)


def _inproj_call(u, w, cos, sin_lo, sin_hi, qn, kn, *, layer, tm):
    bsz, t, d = u.shape
    rope_spec = pl.BlockSpec((tm, LANES), lambda b, i, j: (i, 0))
    vec_spec = pl.BlockSpec((None, 1, LANES), lambda b, i, j: (layer, 0, 0))
    return pl.pallas_call(
        _inproj_kernel,
        grid=(bsz, t // tm, W_PACKED // PROJ_TN),
        in_specs=[pl.BlockSpec((1, tm, d), lambda b, i, j: (b, i, 0)),
                  pl.BlockSpec((None, PROJ_TN, d), lambda b, i, j: (layer, j, 0)),
                  rope_spec, rope_spec, rope_spec, vec_spec, vec_spec],
        out_specs=[pl.BlockSpec((1, ATTN_GROUP, tm, HEAD_DIM),
                                lambda b, i, j: (b, jnp.minimum(j, ZQ_TILES - 1), i, 0)),
                   pl.BlockSpec((1, tm, PROJ_TN),
                                lambda b, i, j: (b, i, jnp.clip(j - ZQ_TILES, 0, ZB_TILES - 1))),
                   pl.BlockSpec((1, tm, PROJ_TN),
                                lambda b, i, j: (b, i, jnp.maximum(j - ZF_TILE0, 0)))],
        out_shape=[jax.ShapeDtypeStruct((bsz, ATTN_HEADS, t, HEAD_DIM), BF16),
                   jax.ShapeDtypeStruct((bsz, t, ZB_W), BF16),
                   jax.ShapeDtypeStruct((bsz, t, ZF_W), F32)],
        compiler_params=_cparams(("parallel", "parallel", "arbitrary")),
    )(u, w, cos, sin_lo, sin_hi, qn, kn)


NEG_BIG = -1e30
AHEAD = 1
ONES_ROWS = 2 * SUBLANES
BOUND_MARGIN = 1.02
MAX_SAFE_SHIFT = 50.0


def _attn_kernel(q_ref, k_ref, v_ref, o_ref, m_ref, acc_ref, s_ref, kmax_ref, *,
                 tq, tk, rb, t_len, ctx_len, unroll):
    qi = pl.program_id(2)
    nr = tq // rb
    per_kb = ATTN_GROUP * nr
    n_units = (t_len // tk) * per_kb

    @pl.when(qi == 0)
    def _():
        k = k_ref[0].astype(F32)
        kmax_ref[...] = jnp.broadcast_to(
            jnp.max(jnp.sum(k * k, axis=1, keepdims=True), axis=0, keepdims=True),
            kmax_ref.shape)

    q_all = q_ref[0].reshape(per_kb * rb, HEAD_DIM).astype(F32)
    qsq = _dot_nt(jnp.ones((SUBLANES, HEAD_DIM), BF16), (q_all * q_all).astype(BF16))[0:1]
    shift = jnp.sqrt(qsq * kmax_ref[0:1, 0:1]) * BOUND_MARGIN
    for slot in range(per_kb):
        m_ref[slot] = shift[:, slot * rb:(slot + 1) * rb]
    bounded = jnp.max(shift) <= MAX_SAFE_SHIFT
    acc_ref[...] = jnp.zeros(acc_ref.shape, F32)

    def unit(n):
        kb = n // per_kb
        return kb, n - kb * per_kb

    def scores(n):
        kb, slot = unit(n)
        h = slot // nr
        r0 = pl.multiple_of((slot - h * nr) * rb, rb)
        k0 = pl.multiple_of(kb * tk, tk)
        return _dot_nt(k_ref[0, pl.ds(k0, tk), :], q_ref[0, h, pl.ds(r0, rb), :])

    def update(n, s, mixed, fixed_shift):
        kb, slot = unit(n)
        if mixed:
            r0 = (slot - (slot // nr) * nr) * rb
            qrow = r0 + lax.broadcasted_iota(jnp.int32, (1, rb), 1)
            key = kb * tk + lax.broadcasted_iota(jnp.int32, (tk, 1), 0)
            s = jnp.where(qrow < ctx_len, jnp.where(key < ctx_len, s, NEG_BIG), s)
        if fixed_shift:
            p = jnp.exp2(s - m_ref[slot]).astype(BF16)
            acc_ref[slot] += _dot(v_ref[0, 0, kb], p)
        else:
            m_prev = m_ref[slot]
            m_new = jnp.maximum(m_prev, jnp.max(s, axis=0, keepdims=True))
            alpha = jnp.exp2(m_prev - m_new)
            p = jnp.exp2((s - m_new).astype(BF16))
            m_ref[slot] = m_new
            acc_ref[slot] = alpha * acc_ref[slot] + _dot(v_ref[0, 0, kb], p)

    def run(mixed, fixed_shift):
        if not fixed_shift:
            m_ref[...] = jnp.full(m_ref.shape, NEG_BIG, F32)
        for a in range(AHEAD):
            s_ref[a] = scores(a)

        def body(it, carry):
            n0 = it * unroll
            s = [s_ref[a] for a in range(AHEAD)]
            for u in range(0, unroll, AHEAD):
                s_next = [scores(jnp.minimum(n0 + u + AHEAD + a, n_units - 1))
                          for a in range(AHEAD)]
                for a in range(AHEAD):
                    update(n0 + u + a, s[a], mixed, fixed_shift)
                s = s_next
            for a in range(AHEAD):
                s_ref[a] = s[a]
            return carry

        lax.fori_loop(0, n_units // unroll, body, 0)

    first = qi == 0
    later = jnp.logical_not(first)
    unbounded = jnp.logical_not(bounded)
    pl.when(first & bounded)(lambda: run(True, True))
    pl.when(later & bounded)(lambda: run(False, True))
    pl.when(first & unbounded)(lambda: run(True, False))
    pl.when(later & unbounded)(lambda: run(False, False))

    for h in range(ATTN_GROUP):
        for r in range(nr):
            slot = h * nr + r
            o_t = acc_ref[slot, :HEAD_DIM] * (1.0 / acc_ref[slot, HEAD_DIM:HEAD_DIM + 1])
            o_ref[0, r * rb:(r + 1) * rb, h * HEAD_DIM:(h + 1) * HEAD_DIM] = o_t.T


def _attn_call(zq, zb, *, ctx_len, tq, tk):
    bsz, t, _ = zb.shape
    assert ctx_len <= tq and ctx_len <= tk
    rb = 256 if tq % 256 == 0 else tq
    per_kb = ATTN_GROUP * (tq // rb)
    unroll = max(u for u in range(AHEAD, 13, AHEAD) if per_kb % u == 0)
    k_blk = ZB_AK_OFF // HEAD_DIM
    v_t = zb[:, :, ZB_AV_OFF:ZB_AV_OFF + ATTN_KV_HEADS * HEAD_DIM]
    v_t = v_t.reshape(bsz, t // tk, tk, ATTN_KV_HEADS, HEAD_DIM).transpose(0, 3, 1, 4, 2)
    v_t = jnp.concatenate([v_t, jnp.ones(v_t.shape[:3] + (ONES_ROWS, tk), v_t.dtype)], axis=3)
    kern = functools.partial(_attn_kernel, tq=tq, tk=tk, rb=rb, t_len=t, ctx_len=ctx_len,
                             unroll=unroll)
    return pl.pallas_call(
        kern,
        grid=(bsz, ATTN_KV_HEADS, t // tq),
        in_specs=[pl.BlockSpec((1, ATTN_GROUP, tq, HEAD_DIM), lambda b, g, qi: (b, g, qi, 0)),
                  pl.BlockSpec((1, t, HEAD_DIM), lambda b, g, qi: (b, 0, k_blk + g)),
                  pl.BlockSpec((1, 1, t // tk, HEAD_DIM + ONES_ROWS, tk),
                               lambda b, g, qi: (b, g, 0, 0, 0))],
        out_specs=pl.BlockSpec((1, tq, ATTN_GROUP * HEAD_DIM), lambda b, g, qi: (b, qi, g)),
        out_shape=jax.ShapeDtypeStruct((bsz, t, ATTN_HEADS * HEAD_DIM), F32),
        scratch_shapes=[pltpu.VMEM((per_kb, 1, rb), F32),
                        pltpu.VMEM((per_kb, HEAD_DIM + ONES_ROWS, rb), F32),
                        pltpu.VMEM((AHEAD, tk, rb), F32),
                        pltpu.VMEM((SUBLANES, LANES), F32)],
        compiler_params=_cparams(("parallel", "parallel", "arbitrary")),
    )(zq, zb, v_t)


def _bwd_chunk(p, n_ctx_chunks, n_chunks):
    return jnp.where(p < n_ctx_chunks, n_ctx_chunks - 1 - p, n_chunks + n_ctx_chunks - 1 - p)


def _ret_kernel(ld_ref, qf_ref, kf_ref, vf_ref, qb_ref, kb_ref, vb_ref, yf_ref, yb_ref,
                sf_ref, sb_ref, *, chunk):
    c = chunk
    p = pl.program_id(1)

    @pl.when(p == 0)
    def _():
        sf_ref[...] = jnp.zeros(sf_ref.shape, F32)
        sb_ref[...] = jnp.zeros(sb_ref.shape, F32)

    ii = lax.broadcasted_iota(jnp.int32, (c, c), 0)
    jj = lax.broadcasted_iota(jnp.int32, (c, c), 1)
    dist = (ii - jj).astype(F32)
    pos = lax.broadcasted_iota(jnp.int32, (c, 1), 0).astype(F32)
    full = jnp.full((1, 1), float(c), F32)

    for h in range(RET_HEADS):
        ldf = ld_ref[0, h]
        ldb = ld_ref[1, h]
        qs = slice(h * RET_QK_DIM, (h + 1) * RET_QK_DIM)
        vs = slice(h * RET_V_DIM, (h + 1) * RET_V_DIM)
        decay = jnp.where(dist >= 0, jnp.exp(ldf * dist), 0.0) \
            + jnp.where(dist <= 0, jnp.exp(-ldb * dist), 0.0)
        q = qf_ref[0, :, qs]
        k = kf_ref[0, :, qs]
        v = vf_ref[0, :, vs]
        y = _dot((_dot_nt(q, k) * decay).astype(BF16), v)
        y = y + _dot(q, sf_ref[h].astype(BF16)) * jnp.exp(ldf * (pos + 1.0))
        yf_ref[0, :, vs] = y
        wv = (v.astype(F32) * jnp.exp(ldf * (c - 1.0 - pos))).astype(BF16)
        sf_ref[h] = sf_ref[h] * jnp.exp(ldf * full) + _dot_tn(k, wv)
        q = qb_ref[0, :, qs]
        k = kb_ref[0, :, qs]
        v = vb_ref[0, :, vs]
        yb_ref[0, :, vs] = _dot(q, sb_ref[h].astype(BF16)) * jnp.exp(ldb * (c - pos))
        wv = (v.astype(F32) * jnp.exp(ldb * pos)).astype(BF16)
        sb_ref[h] = sb_ref[h] * jnp.exp(ldb * full) + _dot_tn(k, wv)


def _ret_call(zb, log_decay, *, ctx_len, chunk):
    bsz, t, _ = zb.shape
    nc = t // chunk
    ncc = ctx_len // chunk
    qw = RET_HEADS * RET_QK_DIM
    vw = RET_HEADS * RET_V_DIM
    q_blk, k_blk, v_blk = ZB_RQ_OFF // qw, ZB_RK_OFF // qw, ZB_RV_OFF // vw
    bwd = functools.partial(_bwd_chunk, n_ctx_chunks=ncc, n_chunks=nc)
    kern = functools.partial(_ret_kernel, chunk=chunk)
    out = jax.ShapeDtypeStruct((bsz, t, vw), F32)
    return pl.pallas_call(
        kern,
        grid=(bsz, nc),
        in_specs=[pl.BlockSpec(memory_space=pltpu.SMEM),
                  pl.BlockSpec((1, chunk, qw), lambda b, p: (b, p, q_blk)),
                  pl.BlockSpec((1, chunk, qw), lambda b, p: (b, p, k_blk)),
                  pl.BlockSpec((1, chunk, vw), lambda b, p: (b, p, v_blk)),
                  pl.BlockSpec((1, chunk, qw), lambda b, p: (b, bwd(p), q_blk)),
                  pl.BlockSpec((1, chunk, qw), lambda b, p: (b, bwd(p), k_blk)),
                  pl.BlockSpec((1, chunk, vw), lambda b, p: (b, bwd(p), v_blk))],
        out_specs=[pl.BlockSpec((1, chunk, vw), lambda b, p: (b, p, 0)),
                   pl.BlockSpec((1, chunk, vw), lambda b, p: (b, bwd(p), 0))],
        out_shape=[out, out],
        scratch_shapes=[pltpu.VMEM((RET_HEADS, RET_QK_DIM, RET_V_DIM), F32),
                        pltpu.VMEM((RET_HEADS, RET_QK_DIM, RET_V_DIM), F32)],
        compiler_params=_cparams(("parallel", "arbitrary")),
    )(log_decay, zb, zb, zb, zb, zb, zb)


CONV_HALO = SUBLANES


def _conv_kernel(prev_ref, cur_ref, next_ref, w_ref, b_ref, o_ref, e_ref, *, tc, t_len, ctx_len):
    i = pl.program_id(1)
    e_ref[0:CONV_HALO] = prev_ref[0]
    e_ref[CONV_HALO:CONV_HALO + tc] = cur_ref[0]
    e_ref[CONV_HALO + tc:] = next_ref[0]
    pad = (SSM_CONV - 1) // 2
    lo = i * tc

    def conv(masked):
        tok = lo + lax.broadcasted_iota(jnp.int32, (tc, 1), 0)
        acc = jnp.broadcast_to(b_ref[...], (tc, ZF_XBC_W))
        for tap in range(SSM_CONV):
            off = tap - pad
            src = e_ref[CONV_HALO + off:CONV_HALO + off + tc, :]
            if masked:
                nb = tok + off
                same = ((tok - ctx_len) ^ (nb - ctx_len)) >= 0
                src = jnp.where(same & (nb >= 0) & (nb < t_len), src, 0.0)
            acc = acc + src * w_ref[tap:tap + 1, :]
        o_ref[0] = _silu(acc)

    def near(edge):
        return (lo - pad < edge) & (edge < lo + tc + pad)
    touches = near(0) | near(ctx_len) | near(t_len)
    pl.when(touches)(lambda: conv(True))
    pl.when(jnp.logical_not(touches))(lambda: conv(False))


def _conv_call(zf, conv_w, conv_b, *, ctx_len, tc):
    bsz, t, _ = zf.shape
    col = ZF_XBC_OFF // ZF_XBC_W
    per = tc // CONV_HALO
    last = t // CONV_HALO - 1
    kern = functools.partial(_conv_kernel, tc=tc, t_len=t, ctx_len=ctx_len)
    return pl.pallas_call(
        kern,
        grid=(bsz, t // tc),
        in_specs=[pl.BlockSpec((1, CONV_HALO, ZF_XBC_W),
                               lambda b, i: (b, jnp.maximum(i * per - 1, 0), col)),
                  pl.BlockSpec((1, tc, ZF_XBC_W), lambda b, i: (b, i, col)),
                  pl.BlockSpec((1, CONV_HALO, ZF_XBC_W),
                               lambda b, i: (b, jnp.minimum((i + 1) * per, last), col)),
                  pl.BlockSpec((SUBLANES, ZF_XBC_W), lambda b, i: (0, 0)),
                  pl.BlockSpec((1, ZF_XBC_W), lambda b, i: (0, 0))],
        out_specs=pl.BlockSpec((1, tc, ZF_XBC_W), lambda b, i: (b, i, 0)),
        out_shape=jax.ShapeDtypeStruct((bsz, t, ZF_XBC_W), F32),
        scratch_shapes=[pltpu.VMEM((tc + 2 * CONV_HALO, ZF_XBC_W), F32)],
        compiler_params=_cparams(("parallel", "parallel")),
    )(zf, zf, zf, conv_w, conv_b)


def _split3(x):
    p1 = x.astype(BF16)
    r1 = x - p1.astype(F32)
    p2 = r1.astype(BF16)
    p3 = (r1 - p2.astype(F32)).astype(BF16)
    return p1, p2, p3


def _cumsum_rows(x, tri):
    p1, p2, p3 = _split3(x)
    return _dot(tri, p1) + _dot(tri, p2) + _dot(tri, p3)


def _expand_heads(a, sel_ref):
    hi = a.astype(BF16)
    lo = (a - hi.astype(F32)).astype(BF16)
    return _dot(jnp.concatenate([hi, lo], axis=1), sel_ref[...])


def _head_selector(off):
    sel = np.zeros((LANES, SSM_INNER), np.float32)
    for h in range(SSM_HEADS):
        sel[off + h, h * SSM_HEAD_DIM:(h + 1) * SSM_HEAD_DIM] = 1.0
    return jnp.asarray(np.concatenate([sel, sel], axis=0), BF16)


def _ssd_kernel(xf_ref, df_ref, xb_ref, db_ref, bias_ref, alog_ref, skip_ref, self_ref, selb_ref,
                yf_ref, yb_ref, sf_ref, sb_ref, *, chunk, sub):
    c = chunk
    p = pl.program_id(1)
    nh = SSM_HEADS
    gw = SSM_HPG * SSM_HEAD_DIM
    bc_w = SSM_GROUPS * SSM_STATE

    @pl.when(p == 0)
    def _():
        sf_ref[...] = jnp.zeros(sf_ref.shape, F32)
        sb_ref[...] = jnp.zeros(sb_ref.shape, F32)

    ii = lax.broadcasted_iota(jnp.int32, (c, c), 0)
    jj = lax.broadcasted_iota(jnp.int32, (c, c), 1)
    lower = ii >= jj
    upper = jj >= ii
    tri = jnp.where(lower, 1.0, 0.0).astype(BF16)
    lane_lo = lax.broadcasted_iota(jnp.int32, (c, LANES), 1) < SSM_HEAD_DIM
    lane_hi = jnp.logical_not(lane_lo)

    def operands(x_ref, rows):
        xs = x_ref[0, rows, 0:SSM_INNER]
        bm = x_ref[0, rows, SSM_INNER:SSM_INNER + bc_w].astype(BF16)
        cm = x_ref[0, rows, SSM_INNER + bc_w:].astype(BF16)
        return xs, bm, cm

    def gates(d_ref, rows):
        raw = d_ref[0, rows, :] + bias_ref[...]
        dt = jnp.maximum(raw, 0.0) + jnp.log1p(jnp.exp(-jnp.abs(raw)))
        la = dt * -jnp.exp(alog_ref[...])
        acum = _cumsum_rows(la, tri)
        return dt, acum, acum - la

    def within(xs, bm, cm, dt, acum, excl):
        dt_t = dt.T
        acum_t = acum.T
        excl_t = excl.T
        xs_b = xs.astype(BF16)
        pieces = []
        for g in range(SSM_GROUPS):
            gs = slice(g * SSM_STATE, (g + 1) * SSM_STATE)
            scores = _dot_nt(cm[:, gs], bm[:, gs])
            for m in range(SSM_HPG // 2):
                pair = g * SSM_HPG // 2 + m
                xp = xs_b[:, pair * LANES:(pair + 1) * LANES]
                acc = None
                for e in range(2):
                    hf = 2 * pair + e
                    hb = nh + hf
                    dec_f = jnp.where(lower, jnp.exp(acum[:, hf:hf + 1] - acum_t[hf:hf + 1, :]), 0.0)
                    dec_b = jnp.where(upper, jnp.exp(excl_t[hb:hb + 1, :] - excl[:, hb:hb + 1]), 0.0)
                    wm = scores * (dec_f * dt_t[hf:hf + 1, :] + dec_b * dt_t[hb:hb + 1, :])
                    xh = jnp.where(lane_lo if e == 0 else lane_hi, xp, 0.0)
                    part = _dot(wm.astype(BF16), xh)
                    acc = part if acc is None else acc + part
                pieces.append(acc)
        return jnp.concatenate(pieces, axis=1)

    def carried(s_ref, xs, bm, cm, carry, wdt, keep):
        xw = (xs * wdt).astype(BF16)
        inter = []
        for g in range(SSM_GROUPS):
            gs = slice(g * SSM_STATE, (g + 1) * SSM_STATE)
            cs = slice(g * gw, (g + 1) * gw)
            inter.append(_dot(cm[:, gs], s_ref[g].astype(BF16)))
            s_ref[g] = s_ref[g] * keep[:, cs] + _dot_tn(bm[:, gs], xw[:, cs])
        return jnp.concatenate(inter, axis=1) * carry

    rows = [pl.ds(i * c, c) for i in range(sub)]
    fwd = [operands(xf_ref, r) + gates(df_ref, r) for r in rows]
    bwd = [operands(xb_ref, r) + gates(db_ref, r) for r in rows]

    fwd_scale = []
    for xs, bm, cm, dt, acum, excl in fwd:
        last = acum[c - 1:c, :]
        both = _expand_heads(
            jnp.concatenate([jnp.exp(acum), jnp.exp(last - acum) * dt], axis=0), self_ref)
        fwd_scale.append((both[:c], both[c:], both[c - 1:c]))
    bwd_scale = []
    for xs, bm, cm, dt, acum, excl in bwd:
        total = acum[c - 1:c, :]
        both = _expand_heads(
            jnp.concatenate([jnp.exp(total - excl), jnp.exp(excl) * dt], axis=0), selb_ref)
        bwd_scale.append((both[:c], both[c:], both[0:1]))

    y_within = [within(*f) for f in fwd]

    for i in range(sub):
        xs, bm, cm = fwd[i][:3]
        y = y_within[i] + carried(sf_ref, xs, bm, cm, *fwd_scale[i]) + skip_ref[...] * xs
        yf_ref[0, rows[i], :] = y
    for i in reversed(range(sub)):
        xs, bm, cm = bwd[i][:3]
        yb_ref[0, rows[i], :] = carried(sb_ref, xs, bm, cm, *bwd_scale[i])


def _ssd_call(xbc, zf, bias_row, alog_row, skip_row, *, ctx_len, chunk):
    bsz, t, _ = xbc.shape
    sub = 2 if ctx_len % (2 * chunk) == 0 and t % (2 * chunk) == 0 else 1
    blk = sub * chunk
    nblk = t // blk
    dt_blk = ZF_DT_OFF // LANES
    bwd = functools.partial(_bwd_chunk, n_ctx_chunks=ctx_len // blk, n_chunks=nblk)
    kern = functools.partial(_ssd_kernel, chunk=chunk, sub=sub)
    out = jax.ShapeDtypeStruct((bsz, t, SSM_INNER), F32)
    row = lambda w: pl.BlockSpec((1, w), lambda b, p: (0, 0))
    return pl.pallas_call(
        kern,
        grid=(bsz, nblk),
        in_specs=[pl.BlockSpec((1, blk, ZF_XBC_W), lambda b, p: (b, p, 0)),
                  pl.BlockSpec((1, blk, LANES), lambda b, p: (b, p, dt_blk)),
                  pl.BlockSpec((1, blk, ZF_XBC_W), lambda b, p: (b, bwd(p), 0)),
                  pl.BlockSpec((1, blk, LANES), lambda b, p: (b, bwd(p), dt_blk)),
                  row(LANES), row(LANES), row(SSM_INNER),
                  pl.BlockSpec((2 * LANES, SSM_INNER), lambda b, p: (0, 0)),
                  pl.BlockSpec((2 * LANES, SSM_INNER), lambda b, p: (0, 0))],
        out_specs=[pl.BlockSpec((1, blk, SSM_INNER), lambda b, p: (b, p, 0)),
                   pl.BlockSpec((1, blk, SSM_INNER), lambda b, p: (b, bwd(p), 0))],
        out_shape=[out, out],
        scratch_shapes=[pltpu.VMEM((SSM_GROUPS, SSM_STATE, SSM_HPG * SSM_HEAD_DIM), F32),
                        pltpu.VMEM((SSM_GROUPS, SSM_STATE, SSM_HPG * SSM_HEAD_DIM), F32)],
        compiler_params=_cparams(("parallel", "arbitrary")),
    )(xbc, zf, xbc, zf, bias_row, alog_row, skip_row,
      _head_selector(0), _head_selector(SSM_HEADS))


def _merge_kernel(attn_ref, retf_ref, retb_ref, ssmf_ref, ssmb_ref, mg_ref, ag_ref, rg_ref,
                  sz_ref, gnw_ref, snw_ref, wb_ref, o_ref):
    d = D_MODEL
    gate_in = lambda ref, lo=0, hi=None: ref[0, :, lo:hi].astype(F32)
    a = attn_ref[0] * _silu(gate_in(ag_ref))

    ret = retf_ref[0] + retb_ref[0]
    parts = []
    for h in range(RET_HEADS):
        r = ret[:, h * RET_V_DIM:(h + 1) * RET_V_DIM]
        rc = r - jnp.mean(r, axis=-1, keepdims=True)
        var = jnp.mean(rc * rc, axis=-1, keepdims=True)
        parts.append(rc * lax.rsqrt(var + NORM_EPS))
    r = jnp.concatenate(parts, axis=1) * gnw_ref[...] * _silu(gate_in(rg_ref))

    s = (ssmf_ref[0] + ssmb_ref[0]) * _silu(gate_in(sz_ref))
    gwid = SSM_INNER // SSM_GROUPS
    parts = []
    for g in range(SSM_GROUPS):
        sg = s[:, g * gwid:(g + 1) * gwid]
        parts.append(sg * lax.rsqrt(jnp.mean(sg * sg, axis=-1, keepdims=True) + NORM_EPS))
    s = jnp.concatenate(parts, axis=1) * snw_ref[...]

    merged = _sigmoid(gate_in(mg_ref, 0, d)) * _dot(a.astype(BF16), wb_ref[0])
    merged = merged + _sigmoid(gate_in(mg_ref, d, 2 * d)) * _dot(r.astype(BF16), wb_ref[1])
    merged = merged + _sigmoid(gate_in(mg_ref, 2 * d, 3 * d)) * _dot(s.astype(BF16), wb_ref[2])
    o_ref[0] = merged.astype(BF16)


def _merge_call(attn, retf, retb, ssmf, ssmb, zb, gnw, snw, wb, *, layer, tm):
    bsz, t, _ = attn.shape
    d = D_MODEL
    bw = BRANCH_WIDTH
    act = lambda blk: pl.BlockSpec((1, tm, bw), lambda b, i: (b, i, blk))
    vec = pl.BlockSpec((None, 1, bw), lambda b, i: (layer, 0, 0))
    return pl.pallas_call(
        _merge_kernel,
        grid=(bsz, t // tm),
        in_specs=[act(0), act(0), act(0), act(0), act(0),
                  pl.BlockSpec((1, tm, N_BRANCHES * d),
                               lambda b, i: (b, i, ZB_MG_OFF // (N_BRANCHES * d))),
                  act(ZB_AG_OFF // bw), act(ZB_RG_OFF // bw), act(ZB_SZ_OFF // bw),
                  vec, vec,
                  pl.BlockSpec((None, N_BRANCHES, bw, d), lambda b, i: (layer, 0, 0, 0),
                               pipeline_mode=pl.Buffered(1))],
        out_specs=pl.BlockSpec((1, tm, d), lambda b, i: (b, i, 0)),
        out_shape=jax.ShapeDtypeStruct((bsz, t, d), BF16),
        compiler_params=_cparams(("parallel", "parallel")),
    )(attn, retf, retb, ssmf, ssmb, zb, zb, zb, zb, gnw, snw, wb)


def _modulate(x, mod_ref, b, n_batch, is_ctx):
    d = D_MODEL
    shift = _modulation_rows(mod_ref, b, n_batch, is_ctx, 0, d)
    scale = _modulation_rows(mod_ref, b, n_batch, is_ctx, d, 2 * d)
    return (x * (1.0 + scale) + shift).astype(BF16)


def _modulate_kernel(x_ref, mod_ref, u_ref, *, tm, n_batch, ctx_len):
    row = pl.program_id(1) * tm + lax.broadcasted_iota(jnp.int32, (tm, 1), 0)
    u_ref[0] = _modulate(x_ref[0], mod_ref, pl.program_id(0), n_batch, row < ctx_len)


def _modulate_call(xs, mod, *, layer, n_batch, ctx_len, tm):
    bsz, t, d = xs.shape
    kern = functools.partial(_modulate_kernel, tm=tm, n_batch=n_batch, ctx_len=ctx_len)
    tok = pl.BlockSpec((1, tm, d), lambda b, i: (b, i, 0))
    return pl.pallas_call(
        kern,
        grid=(bsz, t // tm),
        in_specs=[tok, pl.BlockSpec((None, SUBLANES, 3 * d), lambda b, i: (layer, 0, 0))],
        out_specs=tok,
        out_shape=jax.ShapeDtypeStruct((bsz, t, d), BF16),
        compiler_params=_cparams(("parallel", "parallel")),
    )(xs, mod)


def _out_kernel(m_ref, x_ref, mod_ref, nxt_ref, w_ref, g_ref, b_ref, o_ref, *u_ref,
                tm, row0, n_batch, ctx_len):
    b = pl.program_id(0)
    d = D_MODEL
    row = row0 + pl.program_id(1) * tm + lax.broadcasted_iota(jnp.int32, (tm, 1), 0)
    is_ctx = row < ctx_len
    gate = _modulation_rows(mod_ref, b, n_batch, is_ctx, 2 * d, 3 * d)
    y = _dot(m_ref[0], w_ref[...])
    z = DEEPNORM_ALPHA * x_ref[0] + gate * y
    zc = z - jnp.mean(z, axis=-1, keepdims=True)
    var = jnp.mean(zc * zc, axis=-1, keepdims=True)
    x_new = zc * lax.rsqrt(var + NORM_EPS) * g_ref[...] + b_ref[...]
    o_ref[0] = x_new
    if u_ref:
        u_ref[0][0] = _modulate(x_new, nxt_ref, b, n_batch, is_ctx)


def _out_call(merged, xs, mod, w_out, ln_g, ln_b, *, layer, last, n_batch, ctx_len, tm):
    bsz, t, d = xs.shape
    depth = mod.shape[0]
    blk0 = ctx_len // tm if last else 0
    assert not last or ctx_len % tm == 0
    rows = t - blk0 * tm
    kern = functools.partial(_out_kernel, tm=tm, row0=blk0 * tm, n_batch=n_batch,
                             ctx_len=ctx_len)
    tok_in = pl.BlockSpec((1, tm, d), lambda b, i: (b, i + blk0, 0))
    tok_out = pl.BlockSpec((1, tm, d), lambda b, i: (b, i, 0))
    mod_at = lambda l: pl.BlockSpec((None, SUBLANES, 3 * d), lambda b, i: (l, 0, 0))
    vec = pl.BlockSpec((None, 1, d), lambda b, i: (layer, 0, 0))
    x_shape = jax.ShapeDtypeStruct((bsz, rows, d), F32)
    return pl.pallas_call(
        kern,
        grid=(bsz, rows // tm),
        in_specs=[tok_in, tok_in, mod_at(layer), mod_at(min(layer + 1, depth - 1)),
                  pl.BlockSpec((None, d, d), lambda b, i: (layer, 0, 0),
                               pipeline_mode=pl.Buffered(1)),
                  vec, vec],
        out_specs=tok_out if last else [tok_out, tok_out],
        out_shape=x_shape if last else [x_shape, jax.ShapeDtypeStruct((bsz, rows, d), BF16)],
        compiler_params=_cparams(("parallel", "parallel")),
    )(merged, xs, mod, mod, w_out, ln_g, ln_b)


PACK_COLS = 256


def _pack_kernel(w_ref, o_ref):
    row = 0
    for name in ("aq",) + ZB_ORDER + ZF_ORDER:
        src, width = _REF_SPLITS[name]
        o_ref[0, row:row + width, :] = w_ref[0, src:src + width, :].astype(BF16)
        row += width
    o_ref[0, row:, :] = jnp.zeros((W_PACKED - row, o_ref.shape[2]), BF16)


def _pack_w_in(w_in):
    depth, d, width = w_in.shape
    w_t = jnp.swapaxes(w_in, 1, 2)
    return pl.pallas_call(
        _pack_kernel,
        grid=(depth, d // PACK_COLS),
        in_specs=[pl.BlockSpec((1, width, PACK_COLS), lambda l, i: (l, 0, i))],
        out_specs=pl.BlockSpec((1, W_PACKED, PACK_COLS), lambda l, i: (l, 0, i)),
        out_shape=jax.ShapeDtypeStruct((depth, W_PACKED, d), BF16),
        compiler_params=_cparams(("parallel", "parallel")),
    )(w_t)


def _rope_tables(rows, ctx_len):
    row = jnp.repeat(jnp.arange(rows, dtype=F32), GRID_W)
    col = jnp.tile(jnp.arange(GRID_W, dtype=F32), rows)
    n_freq = HEAD_DIM // 4
    inv = ROPE_BASE ** (-jnp.arange(n_freq, dtype=F32) / n_freq)
    ang_r = row[:, None] * inv
    ang_c = col[:, None] * inv
    ang = jnp.concatenate([ang_r, ang_r, ang_c, ang_c], -1)
    cos = jnp.concatenate([jnp.ones((ctx_len, HEAD_DIM), F32), jnp.cos(ang)], 0)
    sin = jnp.concatenate([jnp.zeros((ctx_len, HEAD_DIM), F32), jnp.sin(ang)], 0)
    first = (np.arange(HEAD_DIM) % (HEAD_DIM // 2)) < HEAD_DIM // 4
    sin_lo = jnp.where(first, -sin, 0.0)
    sin_hi = jnp.where(first, 0.0, sin)
    return cos, sin_lo, sin_hi


def _pad_lanes(v, width):
    return jnp.pad(v.reshape(1, -1), ((0, 0), (0, width - v.size)))


def _tile_choices(t, ctx_len):
    def largest(cap, mult, also=0):
        best = mult
        for cand in range(mult, cap + 1, mult):
            if t % cand == 0 and also % cand == 0:
                best = cand
        return best
    return dict(proj=largest(2112, 16), attn=largest(768, LANES), conv=largest(528, SUBLANES),
                merge=largest(256, 16), out=largest(528, 16),
                out_last=largest(528, 16, also=ctx_len))


def kernel(x, c, ctx, c_ctx, ada_w, ada_b, w_in, attn_q_norm, attn_k_norm, ret_log_decay, ret_gn_w,
           ssm_conv_w, ssm_conv_b, ssm_dt_bias, ssm_a_log, ssm_d, ssm_norm_w, w_branch, w_out,
           ln_g, ln_b):
    bsz, seq, d = x.shape
    ctx_len = ctx.shape[1]
    depth = w_in.shape[0]
    t = seq + ctx_len
    chunk = LANES
    ret_chunk = 256 if ctx_len % 256 == 0 and t % 256 == 0 else chunk
    assert d == D_MODEL and bsz < SUBLANES and seq % GRID_W == 0
    assert ctx_len % chunk == 0 and t % chunk == 0
    tiles = _tile_choices(t, ctx_len)

    xs = jnp.concatenate([ctx, x], axis=1)
    cvec = jnp.zeros((SUBLANES, d), F32).at[:bsz].set(c).at[bsz].set(c_ctx)
    mod = _ada_call(cvec, ada_w, ada_b)
    cos, sin_lo, sin_hi = _rope_tables(seq // GRID_W, ctx_len)
    w_packed = _pack_w_in(w_in)
    wb = w_branch.astype(BF16)
    wo = w_out.astype(BF16)
    conv_w = jnp.pad(ssm_conv_w, ((0, 0), (0, SUBLANES - SSM_CONV), (0, 0)))
    per_layer_row = lambda v: v.reshape(depth, 1, -1)
    qn, kn = per_layer_row(attn_q_norm), per_layer_row(attn_k_norm)
    gnw, snw = per_layer_row(ret_gn_w), per_layer_row(ssm_norm_w)
    lng, lnb = per_layer_row(ln_g), per_layer_row(ln_b)

    u = _modulate_call(xs, mod, layer=0, n_batch=bsz, ctx_len=ctx_len, tm=tiles["out"])
    for l in range(depth):
        zq, zb, zf = _inproj_call(u, w_packed, cos, sin_lo, sin_hi, qn, kn, layer=l,
                                  tm=tiles["proj"])
        attn = _attn_call(zq, zb, ctx_len=ctx_len, tq=tiles["attn"], tk=tiles["attn"])
        retf, retb = _ret_call(zb, ret_log_decay[l], ctx_len=ctx_len, chunk=ret_chunk)
        xbc = _conv_call(zf, conv_w[l], ssm_conv_b[l].reshape(1, -1), ctx_len=ctx_len,
                         tc=tiles["conv"])
        ssmf, ssmb = _ssd_call(xbc, zf, _pad_lanes(ssm_dt_bias[l], LANES),
                               _pad_lanes(ssm_a_log[l], LANES),
                               jnp.repeat(ssm_d[l], SSM_HEAD_DIM).reshape(1, -1),
                               ctx_len=ctx_len, chunk=chunk)
        merged = _merge_call(attn, retf, retb, ssmf, ssmb, zb, gnw, snw, wb, layer=l,
                             tm=tiles["merge"])
        last = l == depth - 1
        out = _out_call(merged, xs, mod, wo, lng, lnb, layer=l, last=last, n_batch=bsz,
                        ctx_len=ctx_len, tm=tiles["out_last" if last else "out"])
        if last:
            return out
        xs, u = out
```

```python
import functools

import numpy as np
import jax
import jax.numpy as jnp
from jax import lax
from jax.experimental import pallas as pl
from jax.experimental.pallas import tpu as pltpu

F32 = jnp.float32
BF16 = jnp.bfloat16

D_MODEL = 2048
GRID_W = 64
HEAD_DIM = 128
ATTN_HEADS = 8
ATTN_KV_HEADS = 2
ATTN_GROUP = ATTN_HEADS // ATTN_KV_HEADS
ROPE_BASE = 10000.0
RET_HEADS = 4
RET_QK_DIM = 128
RET_V_DIM = 256
SSM_INNER = 1024
SSM_HEAD_DIM = 64
SSM_HEADS = 16
SSM_GROUPS = 2
SSM_HPG = SSM_HEADS // SSM_GROUPS
SSM_STATE = 128
SSM_CONV = 5
N_BRANCHES = 3
BRANCH_WIDTH = 1024
DEPTH = 4
DEEPNORM_ALPHA = (2 * DEPTH) ** 0.25
NORM_EPS = 1e-6
ATTN_Q_SCALE = HEAD_DIM ** -0.5 * float(np.log2(np.e))

_REF_SPLITS = dict(aq=(0, 1024), ak=(1024, 256), av=(1280, 256), ag=(1536, 1024),
                   rq=(2560, 512), rk=(3072, 512), rv=(3584, 1024), rg=(4608, 1024),
                   sx=(5632, 1024), sb=(6656, 256), sc=(6912, 256), sdt=(7168, 32),
                   sz=(7200, 1024), mg=(8224, 6144))

LANES = 128
SUBLANES = 8
VMEM_LIMIT = 56 * 1024 * 1024

PROJ_TN = 512
ZQ_TILES = ATTN_HEADS * HEAD_DIM // PROJ_TN
ZB_ORDER = ("mg", "ag", "rg", "sz", "rv", "ak", "av", "rq", "rk")
ZB_MG_OFF, ZB_AG_OFF, ZB_RG_OFF, ZB_SZ_OFF = 0, 6144, 7168, 8192
ZB_RV_OFF, ZB_AK_OFF, ZB_AV_OFF, ZB_RQ_OFF, ZB_RK_OFF = 9216, 10240, 10496, 10752, 11264
ZB_W = 11776
ZB_TILES = ZB_W // PROJ_TN
ZB_RAW_TILES = ZB_AK_OFF // PROJ_TN
ZF_TILE0 = ZQ_TILES + ZB_TILES
ZF_ORDER = ("sx", "sb", "sc", "sdt")
ZF_USED = 1024 + 256 + 256 + 32
ZF_W = 2048
ZF_XBC_OFF = 0
ZF_XBC_W = 1536
ZF_DT_OFF = 1536
W_PACKED = ZQ_TILES * PROJ_TN + ZB_W + ZF_W


def _sigmoid(x):
    return 0.5 * jnp.tanh(0.5 * x) + 0.5


def _silu(x):
    return x * _sigmoid(x)


def _cparams(sem):
    return pltpu.CompilerParams(dimension_semantics=sem, vmem_limit_bytes=VMEM_LIMIT)


def _dot(a, b):
    return jnp.dot(a, b, preferred_element_type=F32)


def _dot_nt(a, b):
    return lax.dot_general(a, b, (((1,), (1,)), ((), ())), preferred_element_type=F32)


def _dot_tn(a, b):
    return lax.dot_general(a, b, (((0,), (0,)), ((), ())), preferred_element_type=F32)


def _ada_kernel(c_ref, w_ref, b_ref, o_ref):
    s = _silu(c_ref[...]).astype(BF16)
    o_ref[0] = _dot(s, w_ref[0].astype(BF16)) + b_ref[0]


def _ada_call(cvec, ada_w, ada_b):
    depth, d, n = ada_w.shape
    tn = 768
    return pl.pallas_call(
        _ada_kernel,
        grid=(depth, n // tn),
        in_specs=[pl.BlockSpec((SUBLANES, d), lambda l, j: (0, 0)),
                  pl.BlockSpec((1, d, tn), lambda l, j: (l, 0, j)),
                  pl.BlockSpec((1, 1, tn), lambda l, j: (l, 0, j))],
        out_specs=pl.BlockSpec((1, SUBLANES, tn), lambda l, j: (l, 0, j)),
        out_shape=jax.ShapeDtypeStruct((depth, SUBLANES, n), F32),
        compiler_params=_cparams(("parallel", "parallel")),
    )(cvec, ada_w, ada_b.reshape(depth, 1, n))


def _rope(x, cos, sin_lo, sin_hi):
    return x * cos + pltpu.roll(x, 96, 1) * sin_lo + pltpu.roll(x, 32, 1) * sin_hi


def _modulation_rows(mod_ref, b, n_batch, is_ctx, lo, hi):
    mb = mod_ref[pl.ds(b, 1), lo:hi]
    mc = mod_ref[pl.ds(n_batch, 1), lo:hi]
    return jnp.where(is_ctx, mc, mb)


def _inproj_kernel(u_ref, w_ref, cos_ref, sl_ref, sh_ref, qn_ref, kn_ref,
                   zq_ref, zb_ref, zf_ref):
    j = pl.program_id(2)
    jb = j - ZQ_TILES

    def project():
        return _dot_nt(u_ref[0], w_ref[...])

    def rope(xh):
        return _rope(xh, cos_ref[...], sl_ref[...], sh_ref[...])

    def normed(w_row):
        def fn(xh):
            ms = jnp.mean(xh * xh, axis=-1, keepdims=True)
            return rope(xh * lax.rsqrt(ms + NORM_EPS) * w_row)
        return fn

    @pl.when((jb >= 0) & (jb < ZB_RAW_TILES))
    def _():
        zb_ref[0] = project().astype(BF16)

    @pl.when((j < ZQ_TILES) | (jb >= ZB_RAW_TILES))
    def _():
        acc = project()

        @pl.when(j >= ZF_TILE0)
        def _():
            zf_ref[0] = acc

        def head(h):
            return acc[:, h * LANES:(h + 1) * LANES]

        def per_head(fn, n):
            return jnp.concatenate([fn(head(h)) for h in range(n)], axis=1)

        @pl.when(j < ZQ_TILES)
        def _():
            fn = normed(qn_ref[...])
            for h in range(ATTN_GROUP):
                zq_ref[0, h] = (fn(head(h)) * ATTN_Q_SCALE).astype(BF16)

        @pl.when(jb == ZB_AK_OFF // PROJ_TN)
        def _():
            k = per_head(normed(kn_ref[...]), 2)
            zb_ref[0] = jnp.concatenate([k, acc[:, 2 * LANES:]], axis=1).astype(BF16)

        @pl.when(jb == ZB_RQ_OFF // PROJ_TN)
        def _():
            zb_ref[0] = per_head(rope, 4).astype(BF16)

        @pl.when(jb == ZB_RK_OFF // PROJ_TN)
        def _():
            zb_ref[0] = per_head(lambda xh: rope(xh * (RET_QK_DIM ** -0.5)), 4).astype(BF16)


def _inproj_call(u, w, cos, sin_lo, sin_hi, qn, kn, *, layer, tm):
    bsz, t, d = u.shape
    rope_spec = pl.BlockSpec((tm, LANES), lambda b, i, j: (i, 0))
    vec_spec = pl.BlockSpec((None, 1, LANES), lambda b, i, j: (layer, 0, 0))
    return pl.pallas_call(
        _inproj_kernel,
        grid=(bsz, t // tm, W_PACKED // PROJ_TN),
        in_specs=[pl.BlockSpec((1, tm, d), lambda b, i, j: (b, i, 0)),
                  pl.BlockSpec((None, PROJ_TN, d), lambda b, i, j: (layer, j, 0)),
                  rope_spec, rope_spec, rope_spec, vec_spec, vec_spec],
        out_specs=[pl.BlockSpec((1, ATTN_GROUP, tm, HEAD_DIM),
                                lambda b, i, j: (b, jnp.minimum(j, ZQ_TILES - 1), i, 0)),
                   pl.BlockSpec((1, tm, PROJ_TN),
                                lambda b, i, j: (b, i, jnp.clip(j - ZQ_TILES, 0, ZB_TILES - 1))),
                   pl.BlockSpec((1, tm, PROJ_TN),
                                lambda b, i, j: (b, i, jnp.maximum(j - ZF_TILE0, 0)))],
        out_shape=[jax.ShapeDtypeStruct((bsz, ATTN_HEADS, t, HEAD_DIM), BF16),
                   jax.ShapeDtypeStruct((bsz, t, ZB_W), BF16),
                   jax.ShapeDtypeStruct((bsz, t, ZF_W), F32)],
        compiler_params=_cparams(("parallel", "parallel", "arbitrary")),
    )(u, w, cos, sin_lo, sin_hi, qn, kn)


NEG_BIG = -1e30
AHEAD = 1
ONES_ROWS = 2 * SUBLANES
BOUND_MARGIN = 1.02
MAX_SAFE_SHIFT = 50.0


def _attn_kernel(q_ref, k_ref, v_ref, o_ref, m_ref, acc_ref, s_ref, kmax_ref, *,
                 tq, tk, rb, t_len, ctx_len, unroll):
    qi = pl.program_id(2)
    nr = tq // rb
    per_kb = ATTN_GROUP * nr
    n_units = (t_len // tk) * per_kb

    @pl.when(qi == 0)
    def _():
        k = k_ref[0].astype(F32)
        kmax_ref[...] = jnp.broadcast_to(
            jnp.max(jnp.sum(k * k, axis=1, keepdims=True), axis=0, keepdims=True),
            kmax_ref.shape)

    q_all = q_ref[0].reshape(per_kb * rb, HEAD_DIM).astype(F32)
    qsq = _dot_nt(jnp.ones((SUBLANES, HEAD_DIM), BF16), (q_all * q_all).astype(BF16))[0:1]
    shift = jnp.sqrt(qsq * kmax_ref[0:1, 0:1]) * BOUND_MARGIN
    for slot in range(per_kb):
        m_ref[slot] = shift[:, slot * rb:(slot + 1) * rb]
    bounded = jnp.max(shift) <= MAX_SAFE_SHIFT
    acc_ref[...] = jnp.zeros(acc_ref.shape, F32)

    def unit(n):
        kb = n // per_kb
        return kb, n - kb * per_kb

    def scores(n):
        kb, slot = unit(n)
        h = slot // nr
        r0 = pl.multiple_of((slot - h * nr) * rb, rb)
        k0 = pl.multiple_of(kb * tk, tk)
        return _dot_nt(k_ref[0, pl.ds(k0, tk), :], q_ref[0, h, pl.ds(r0, rb), :])

    def update(n, s, mixed, fixed_shift):
        kb, slot = unit(n)
        if mixed:
            r0 = (slot - (slot // nr) * nr) * rb
            qrow = r0 + lax.broadcasted_iota(jnp.int32, (1, rb), 1)
            key = kb * tk + lax.broadcasted_iota(jnp.int32, (tk, 1), 0)
            s = jnp.where(qrow < ctx_len, jnp.where(key < ctx_len, s, NEG_BIG), s)
        if fixed_shift:
            p = jnp.exp2(s - m_ref[slot]).astype(BF16)
            acc_ref[slot] += _dot(v_ref[0, 0, kb], p)
        else:
            m_prev = m_ref[slot]
            m_new = jnp.maximum(m_prev, jnp.max(s, axis=0, keepdims=True))
            alpha = jnp.exp2(m_prev - m_new)
            p = jnp.exp2((s - m_new).astype(BF16))
            m_ref[slot] = m_new
            acc_ref[slot] = alpha * acc_ref[slot] + _dot(v_ref[0, 0, kb], p)

    def run(mixed, fixed_shift):
        if not fixed_shift:
            m_ref[...] = jnp.full(m_ref.shape, NEG_BIG, F32)
        for a in range(AHEAD):
            s_ref[a] = scores(a)

        def body(it, carry):
            n0 = it * unroll
            s = [s_ref[a] for a in range(AHEAD)]
            for u in range(0, unroll, AHEAD):
                s_next = [scores(jnp.minimum(n0 + u + AHEAD + a, n_units - 1))
                          for a in range(AHEAD)]
                for a in range(AHEAD):
                    update(n0 + u + a, s[a], mixed, fixed_shift)
                s = s_next
            for a in range(AHEAD):
                s_ref[a] = s[a]
            return carry

        lax.fori_loop(0, n_units // unroll, body, 0)

    first = qi == 0
    later = jnp.logical_not(first)
    unbounded = jnp.logical_not(bounded)
    pl.when(first & bounded)(lambda: run(True, True))
    pl.when(later & bounded)(lambda: run(False, True))
    pl.when(first & unbounded)(lambda: run(True, False))
    pl.when(later & unbounded)(lambda: run(False, False))

    for h in range(ATTN_GROUP):
        for r in range(nr):
            slot = h * nr + r
            o_t = acc_ref[slot, :HEAD_DIM] * (1.0 / acc_ref[slot, HEAD_DIM:HEAD_DIM + 1])
            o_ref[0, r * rb:(r + 1) * rb, h * HEAD_DIM:(h + 1) * HEAD_DIM] = o_t.T


def _attn_call(zq, zb, *, ctx_len, tq, tk):
    bsz, t, _ = zb.shape
    assert ctx_len <= tq and ctx_len <= tk
    rb = 256 if tq % 256 == 0 else tq
    per_kb = ATTN_GROUP * (tq // rb)
    unroll = max(u for u in range(AHEAD, 25, AHEAD) if (t // tk * per_kb) % u == 0)
    k_blk = ZB_AK_OFF // HEAD_DIM
    v_t = zb[:, :, ZB_AV_OFF:ZB_AV_OFF + ATTN_KV_HEADS * HEAD_DIM]
    v_t = v_t.reshape(bsz, t // tk, tk, ATTN_KV_HEADS, HEAD_DIM).transpose(0, 3, 1, 4, 2)
    v_t = jnp.concatenate([v_t, jnp.ones(v_t.shape[:3] + (ONES_ROWS, tk), v_t.dtype)], axis=3)
    kern = functools.partial(_attn_kernel, tq=tq, tk=tk, rb=rb, t_len=t, ctx_len=ctx_len,
                             unroll=unroll)
    return pl.pallas_call(
        kern,
        grid=(bsz, ATTN_KV_HEADS, t // tq),
        in_specs=[pl.BlockSpec((1, ATTN_GROUP, tq, HEAD_DIM), lambda b, g, qi: (b, g, qi, 0)),
                  pl.BlockSpec((1, t, HEAD_DIM), lambda b, g, qi: (b, 0, k_blk + g)),
                  pl.BlockSpec((1, 1, t // tk, HEAD_DIM + ONES_ROWS, tk),
                               lambda b, g, qi: (b, g, 0, 0, 0))],
        out_specs=pl.BlockSpec((1, tq, ATTN_GROUP * HEAD_DIM), lambda b, g, qi: (b, qi, g)),
        out_shape=jax.ShapeDtypeStruct((bsz, t, ATTN_HEADS * HEAD_DIM), F32),
        scratch_shapes=[pltpu.VMEM((per_kb, 1, rb), F32),
                        pltpu.VMEM((per_kb, HEAD_DIM + ONES_ROWS, rb), F32),
                        pltpu.VMEM((AHEAD, tk, rb), F32),
                        pltpu.VMEM((SUBLANES, LANES), F32)],
        compiler_params=_cparams(("parallel", "parallel", "arbitrary")),
    )(zq, zb, v_t)


def _bwd_chunk(p, n_ctx_chunks, n_chunks):
    return jnp.where(p < n_ctx_chunks, n_ctx_chunks - 1 - p, n_chunks + n_ctx_chunks - 1 - p)


def _ret_kernel(ld_ref, qf_ref, kf_ref, vf_ref, qb_ref, kb_ref, vb_ref, yf_ref, yb_ref,
                sf_ref, sb_ref, *, chunk):
    c = chunk
    p = pl.program_id(1)

    @pl.when(p == 0)
    def _():
        sf_ref[...] = jnp.zeros(sf_ref.shape, F32)
        sb_ref[...] = jnp.zeros(sb_ref.shape, F32)

    ii = lax.broadcasted_iota(jnp.int32, (c, c), 0)
    jj = lax.broadcasted_iota(jnp.int32, (c, c), 1)
    dist = (ii - jj).astype(F32)
    pos = lax.broadcasted_iota(jnp.int32, (c, 1), 0).astype(F32)
    full = jnp.full((1, 1), float(c), F32)

    for h in range(RET_HEADS):
        ldf = ld_ref[0, h]
        ldb = ld_ref[1, h]
        qs = slice(h * RET_QK_DIM, (h + 1) * RET_QK_DIM)
        vs = slice(h * RET_V_DIM, (h + 1) * RET_V_DIM)
        decay = jnp.where(dist >= 0, jnp.exp(ldf * dist), 0.0) \
            + jnp.where(dist <= 0, jnp.exp(-ldb * dist), 0.0)
        q = qf_ref[0, :, qs]
        k = kf_ref[0, :, qs]
        v = vf_ref[0, :, vs]
        y = _dot((_dot_nt(q, k) * decay).astype(BF16), v)
        y = y + _dot(q, sf_ref[h].astype(BF16)) * jnp.exp(ldf * (pos + 1.0))
        yf_ref[0, :, vs] = y
        wv = (v.astype(F32) * jnp.exp(ldf * (c - 1.0 - pos))).astype(BF16)
        sf_ref[h] = sf_ref[h] * jnp.exp(ldf * full) + _dot_tn(k, wv)
        q = qb_ref[0, :, qs]
        k = kb_ref[0, :, qs]
        v = vb_ref[0, :, vs]
        yb_ref[0, :, vs] = _dot(q, sb_ref[h].astype(BF16)) * jnp.exp(ldb * (c - pos))
        wv = (v.astype(F32) * jnp.exp(ldb * pos)).astype(BF16)
        sb_ref[h] = sb_ref[h] * jnp.exp(ldb * full) + _dot_tn(k, wv)


def _ret_call(zb, log_decay, *, ctx_len, chunk):
    bsz, t, _ = zb.shape
    nc = t // chunk
    ncc = ctx_len // chunk
    qw = RET_HEADS * RET_QK_DIM
    vw = RET_HEADS * RET_V_DIM
    q_blk, k_blk, v_blk = ZB_RQ_OFF // qw, ZB_RK_OFF // qw, ZB_RV_OFF // vw
    bwd = functools.partial(_bwd_chunk, n_ctx_chunks=ncc, n_chunks=nc)
    kern = functools.partial(_ret_kernel, chunk=chunk)
    out = jax.ShapeDtypeStruct((bsz, t, vw), F32)
    return pl.pallas_call(
        kern,
        grid=(bsz, nc),
        in_specs=[pl.BlockSpec(memory_space=pltpu.SMEM),
                  pl.BlockSpec((1, chunk, qw), lambda b, p: (b, p, q_blk)),
                  pl.BlockSpec((1, chunk, qw), lambda b, p: (b, p, k_blk)),
                  pl.BlockSpec((1, chunk, vw), lambda b, p: (b, p, v_blk)),
                  pl.BlockSpec((1, chunk, qw), lambda b, p: (b, bwd(p), q_blk)),
                  pl.BlockSpec((1, chunk, qw), lambda b, p: (b, bwd(p), k_blk)),
                  pl.BlockSpec((1, chunk, vw), lambda b, p: (b, bwd(p), v_blk))],
        out_specs=[pl.BlockSpec((1, chunk, vw), lambda b, p: (b, p, 0)),
                   pl.BlockSpec((1, chunk, vw), lambda b, p: (b, bwd(p), 0))],
        out_shape=[out, out],
        scratch_shapes=[pltpu.VMEM((RET_HEADS, RET_QK_DIM, RET_V_DIM), F32),
                        pltpu.VMEM((RET_HEADS, RET_QK_DIM, RET_V_DIM), F32)],
        compiler_params=_cparams(("parallel", "arbitrary")),
    )(log_decay, zb, zb, zb, zb, zb, zb)


CONV_HALO = SUBLANES


def _conv_kernel(prev_ref, cur_ref, next_ref, w_ref, b_ref, o_ref, e_ref, *, tc, t_len, ctx_len):
    i = pl.program_id(1)
    e_ref[0:CONV_HALO] = prev_ref[0]
    e_ref[CONV_HALO:CONV_HALO + tc] = cur_ref[0]
    e_ref[CONV_HALO + tc:] = next_ref[0]
    pad = (SSM_CONV - 1) // 2
    lo = i * tc

    def conv(masked):
        tok = lo + lax.broadcasted_iota(jnp.int32, (tc, 1), 0)
        acc = jnp.broadcast_to(b_ref[...], (tc, ZF_XBC_W))
        for tap in range(SSM_CONV):
            off = tap - pad
            src = e_ref[CONV_HALO + off:CONV_HALO + off + tc, :]
            if masked:
                nb = tok + off
                same = ((tok - ctx_len) ^ (nb - ctx_len)) >= 0
                src = jnp.where(same & (nb >= 0) & (nb < t_len), src, 0.0)
            acc = acc + src * w_ref[tap:tap + 1, :]
        o_ref[0] = _silu(acc)

    def near(edge):
        return (lo - pad < edge) & (edge < lo + tc + pad)
    touches = near(0) | near(ctx_len) | near(t_len)
    pl.when(touches)(lambda: conv(True))
    pl.when(jnp.logical_not(touches))(lambda: conv(False))


def _conv_call(zf, conv_w, conv_b, *, ctx_len, tc):
    bsz, t, _ = zf.shape
    col = ZF_XBC_OFF // ZF_XBC_W
    per = tc // CONV_HALO
    last = t // CONV_HALO - 1
    kern = functools.partial(_conv_kernel, tc=tc, t_len=t, ctx_len=ctx_len)
    return pl.pallas_call(
        kern,
        grid=(bsz, t // tc),
        in_specs=[pl.BlockSpec((1, CONV_HALO, ZF_XBC_W),
                               lambda b, i: (b, jnp.maximum(i * per - 1, 0), col)),
                  pl.BlockSpec((1, tc, ZF_XBC_W), lambda b, i: (b, i, col)),
                  pl.BlockSpec((1, CONV_HALO, ZF_XBC_W),
                               lambda b, i: (b, jnp.minimum((i + 1) * per, last), col)),
                  pl.BlockSpec((SUBLANES, ZF_XBC_W), lambda b, i: (0, 0)),
                  pl.BlockSpec((1, ZF_XBC_W), lambda b, i: (0, 0))],
        out_specs=pl.BlockSpec((1, tc, ZF_XBC_W), lambda b, i: (b, i, 0)),
        out_shape=jax.ShapeDtypeStruct((bsz, t, ZF_XBC_W), F32),
        scratch_shapes=[pltpu.VMEM((tc + 2 * CONV_HALO, ZF_XBC_W), F32)],
        compiler_params=_cparams(("parallel", "parallel")),
    )(zf, zf, zf, conv_w, conv_b)


def _split3(x):
    p1 = x.astype(BF16)
    r1 = x - p1.astype(F32)
    p2 = r1.astype(BF16)
    p3 = (r1 - p2.astype(F32)).astype(BF16)
    return p1, p2, p3


def _cumsum_rows(x, tri):
    p1, p2, p3 = _split3(x)
    return _dot(tri, p1) + _dot(tri, p2) + _dot(tri, p3)


def _expand_heads(a, sel_ref):
    hi = a.astype(BF16)
    lo = (a - hi.astype(F32)).astype(BF16)
    return _dot(jnp.concatenate([hi, lo], axis=1), sel_ref[...])


def _head_selector(off):
    sel = np.zeros((LANES, SSM_INNER), np.float32)
    for h in range(SSM_HEADS):
        sel[off + h, h * SSM_HEAD_DIM:(h + 1) * SSM_HEAD_DIM] = 1.0
    return jnp.asarray(np.concatenate([sel, sel], axis=0), BF16)


def _ssd_kernel(xf_ref, df_ref, xb_ref, db_ref, bias_ref, alog_ref, skip_ref, self_ref, selb_ref,
                yf_ref, yb_ref, sf_ref, sb_ref, *, chunk, sub):
    c = chunk
    p = pl.program_id(1)
    nh = SSM_HEADS
    gw = SSM_HPG * SSM_HEAD_DIM
    bc_w = SSM_GROUPS * SSM_STATE

    @pl.when(p == 0)
    def _():
        sf_ref[...] = jnp.zeros(sf_ref.shape, F32)
        sb_ref[...] = jnp.zeros(sb_ref.shape, F32)

    ii = lax.broadcasted_iota(jnp.int32, (c, c), 0)
    jj = lax.broadcasted_iota(jnp.int32, (c, c), 1)
    lower = ii >= jj
    upper = jj >= ii
    tri = jnp.where(lower, 1.0, 0.0).astype(BF16)
    lane_lo = lax.broadcasted_iota(jnp.int32, (c, LANES), 1) < SSM_HEAD_DIM
    lane_hi = jnp.logical_not(lane_lo)

    def operands(x_ref, rows):
        xs = x_ref[0, rows, 0:SSM_INNER]
        bm = x_ref[0, rows, SSM_INNER:SSM_INNER + bc_w].astype(BF16)
        cm = x_ref[0, rows, SSM_INNER + bc_w:].astype(BF16)
        return xs, bm, cm

    def gates(d_ref, rows):
        raw = d_ref[0, rows, :] + bias_ref[...]
        dt = jnp.maximum(raw, 0.0) + jnp.log1p(jnp.exp(-jnp.abs(raw)))
        la = dt * -jnp.exp(alog_ref[...])
        acum = _cumsum_rows(la, tri)
        return dt, acum, acum - la

    def within(xs, bm, cm, dt, acum, excl):
        dt_t = dt.T
        acum_t = acum.T
        excl_t = excl.T
        xs_b = xs.astype(BF16)
        pieces = []
        for g in range(SSM_GROUPS):
            gs = slice(g * SSM_STATE, (g + 1) * SSM_STATE)
            scores = _dot_nt(cm[:, gs], bm[:, gs])
            for m in range(SSM_HPG // 2):
                pair = g * SSM_HPG // 2 + m
                xp = xs_b[:, pair * LANES:(pair + 1) * LANES]
                acc = None
                for e in range(2):
                    hf = 2 * pair + e
                    hb = nh + hf
                    dec_f = jnp.where(lower, jnp.exp(acum[:, hf:hf + 1] - acum_t[hf:hf + 1, :]), 0.0)
                    dec_b = jnp.where(upper, jnp.exp(excl_t[hb:hb + 1, :] - excl[:, hb:hb + 1]), 0.0)
                    wm = scores * (dec_f * dt_t[hf:hf + 1, :] + dec_b * dt_t[hb:hb + 1, :])
                    xh = jnp.where(lane_lo if e == 0 else lane_hi, xp, 0.0)
                    part = _dot(wm.astype(BF16), xh)
                    acc = part if acc is None else acc + part
                pieces.append(acc)
        return jnp.concatenate(pieces, axis=1)

    def carried(s_ref, xs, bm, cm, carry, wdt, keep):
        xw = (xs * wdt).astype(BF16)
        inter = []
        for g in range(SSM_GROUPS):
            gs = slice(g * SSM_STATE, (g + 1) * SSM_STATE)
            cs = slice(g * gw, (g + 1) * gw)
            inter.append(_dot(cm[:, gs], s_ref[g].astype(BF16)))
            s_ref[g] = s_ref[g] * keep[:, cs] + _dot_tn(bm[:, gs], xw[:, cs])
        return jnp.concatenate(inter, axis=1) * carry

    rows = [pl.ds(i * c, c) for i in range(sub)]
    fwd = [operands(xf_ref, r) + gates(df_ref, r) for r in rows]
    bwd = [operands(xb_ref, r) + gates(db_ref, r) for r in rows]

    fwd_scale = []
    for xs, bm, cm, dt, acum, excl in fwd:
        last = acum[c - 1:c, :]
        both = _expand_heads(
            jnp.concatenate([jnp.exp(acum), jnp.exp(last - acum) * dt], axis=0), self_ref)
        fwd_scale.append((both[:c], both[c:], both[c - 1:c]))
    bwd_scale = []
    for xs, bm, cm, dt, acum, excl in bwd:
        total = acum[c - 1:c, :]
        both = _expand_heads(
            jnp.concatenate([jnp.exp(total - excl), jnp.exp(excl) * dt], axis=0), selb_ref)
        bwd_scale.append((both[:c], both[c:], both[0:1]))

    y_within = [within(*f) for f in fwd]

    for i in range(sub):
        xs, bm, cm = fwd[i][:3]
        y = y_within[i] + carried(sf_ref, xs, bm, cm, *fwd_scale[i]) + skip_ref[...] * xs
        yf_ref[0, rows[i], :] = y
    for i in reversed(range(sub)):
        xs, bm, cm = bwd[i][:3]
        yb_ref[0, rows[i], :] = carried(sb_ref, xs, bm, cm, *bwd_scale[i])


def _ssd_call(xbc, zf, bias_row, alog_row, skip_row, *, ctx_len, chunk):
    bsz, t, _ = xbc.shape
    sub = 2 if ctx_len % (2 * chunk) == 0 and t % (2 * chunk) == 0 else 1
    blk = sub * chunk
    nblk = t // blk
    dt_blk = ZF_DT_OFF // LANES
    bwd = functools.partial(_bwd_chunk, n_ctx_chunks=ctx_len // blk, n_chunks=nblk)
    kern = functools.partial(_ssd_kernel, chunk=chunk, sub=sub)
    out = jax.ShapeDtypeStruct((bsz, t, SSM_INNER), F32)
    row = lambda w: pl.BlockSpec((1, w), lambda b, p: (0, 0))
    return pl.pallas_call(
        kern,
        grid=(bsz, nblk),
        in_specs=[pl.BlockSpec((1, blk, ZF_XBC_W), lambda b, p: (b, p, 0)),
                  pl.BlockSpec((1, blk, LANES), lambda b, p: (b, p, dt_blk)),
                  pl.BlockSpec((1, blk, ZF_XBC_W), lambda b, p: (b, bwd(p), 0)),
                  pl.BlockSpec((1, blk, LANES), lambda b, p: (b, bwd(p), dt_blk)),
                  row(LANES), row(LANES), row(SSM_INNER),
                  pl.BlockSpec((2 * LANES, SSM_INNER), lambda b, p: (0, 0)),
                  pl.BlockSpec((2 * LANES, SSM_INNER), lambda b, p: (0, 0))],
        out_specs=[pl.BlockSpec((1, blk, SSM_INNER), lambda b, p: (b, p, 0)),
                   pl.BlockSpec((1, blk, SSM_INNER), lambda b, p: (b, bwd(p), 0))],
        out_shape=[out, out],
        scratch_shapes=[pltpu.VMEM((SSM_GROUPS, SSM_STATE, SSM_HPG * SSM_HEAD_DIM), F32),
                        pltpu.VMEM((SSM_GROUPS, SSM_STATE, SSM_HPG * SSM_HEAD_DIM), F32)],
        compiler_params=_cparams(("parallel", "arbitrary")),
    )(xbc, zf, xbc, zf, bias_row, alog_row, skip_row,
      _head_selector(0), _head_selector(SSM_HEADS))


def _merge_kernel(attn_ref, retf_ref, retb_ref, ssmf_ref, ssmb_ref, mg_ref, ag_ref, rg_ref,
                  sz_ref, gnw_ref, snw_ref, wb_ref, o_ref):
    d = D_MODEL
    gate_in = lambda ref, lo=0, hi=None: ref[0, :, lo:hi].astype(F32)
    a = attn_ref[0] * _silu(gate_in(ag_ref))

    ret = retf_ref[0] + retb_ref[0]
    parts = []
    for h in range(RET_HEADS):
        r = ret[:, h * RET_V_DIM:(h + 1) * RET_V_DIM]
        rc = r - jnp.mean(r, axis=-1, keepdims=True)
        var = jnp.mean(rc * rc, axis=-1, keepdims=True)
        parts.append(rc * lax.rsqrt(var + NORM_EPS))
    r = jnp.concatenate(parts, axis=1) * gnw_ref[...] * _silu(gate_in(rg_ref))

    s = (ssmf_ref[0] + ssmb_ref[0]) * _silu(gate_in(sz_ref))
    gwid = SSM_INNER // SSM_GROUPS
    parts = []
    for g in range(SSM_GROUPS):
        sg = s[:, g * gwid:(g + 1) * gwid]
        parts.append(sg * lax.rsqrt(jnp.mean(sg * sg, axis=-1, keepdims=True) + NORM_EPS))
    s = jnp.concatenate(parts, axis=1) * snw_ref[...]

    merged = _sigmoid(gate_in(mg_ref, 0, d)) * _dot(a.astype(BF16), wb_ref[0])
    merged = merged + _sigmoid(gate_in(mg_ref, d, 2 * d)) * _dot(r.astype(BF16), wb_ref[1])
    merged = merged + _sigmoid(gate_in(mg_ref, 2 * d, 3 * d)) * _dot(s.astype(BF16), wb_ref[2])
    o_ref[0] = merged.astype(BF16)


def _merge_call(attn, retf, retb, ssmf, ssmb, zb, gnw, snw, wb, *, layer, tm):
    bsz, t, _ = attn.shape
    d = D_MODEL
    bw = BRANCH_WIDTH
    act = lambda blk: pl.BlockSpec((1, tm, bw), lambda b, i: (b, i, blk))
    vec = pl.BlockSpec((None, 1, bw), lambda b, i: (layer, 0, 0))
    return pl.pallas_call(
        _merge_kernel,
        grid=(bsz, t // tm),
        in_specs=[act(0), act(0), act(0), act(0), act(0),
                  pl.BlockSpec((1, tm, N_BRANCHES * d),
                               lambda b, i: (b, i, ZB_MG_OFF // (N_BRANCHES * d))),
                  act(ZB_AG_OFF // bw), act(ZB_RG_OFF // bw), act(ZB_SZ_OFF // bw),
                  vec, vec,
                  pl.BlockSpec((None, N_BRANCHES, bw, d), lambda b, i: (layer, 0, 0, 0),
                               pipeline_mode=pl.Buffered(1))],
        out_specs=pl.BlockSpec((1, tm, d), lambda b, i: (b, i, 0)),
        out_shape=jax.ShapeDtypeStruct((bsz, t, d), BF16),
        compiler_params=_cparams(("parallel", "parallel")),
    )(attn, retf, retb, ssmf, ssmb, zb, zb, zb, zb, gnw, snw, wb)


def _modulate(x, mod_ref, b, n_batch, is_ctx):
    d = D_MODEL
    shift = _modulation_rows(mod_ref, b, n_batch, is_ctx, 0, d)
    scale = _modulation_rows(mod_ref, b, n_batch, is_ctx, d, 2 * d)
    return (x * (1.0 + scale) + shift).astype(BF16)


def _modulate_kernel(x_ref, mod_ref, u_ref, *, tm, n_batch, ctx_len):
    row = pl.program_id(1) * tm + lax.broadcasted_iota(jnp.int32, (tm, 1), 0)
    u_ref[0] = _modulate(x_ref[0], mod_ref, pl.program_id(0), n_batch, row < ctx_len)


def _modulate_call(xs, mod, *, layer, n_batch, ctx_len, tm):
    bsz, t, d = xs.shape
    kern = functools.partial(_modulate_kernel, tm=tm, n_batch=n_batch, ctx_len=ctx_len)
    tok = pl.BlockSpec((1, tm, d), lambda b, i: (b, i, 0))
    return pl.pallas_call(
        kern,
        grid=(bsz, t // tm),
        in_specs=[tok, pl.BlockSpec((None, SUBLANES, 3 * d), lambda b, i: (layer, 0, 0))],
        out_specs=tok,
        out_shape=jax.ShapeDtypeStruct((bsz, t, d), BF16),
        compiler_params=_cparams(("parallel", "parallel")),
    )(xs, mod)


def _out_kernel(m_ref, x_ref, mod_ref, nxt_ref, w_ref, g_ref, b_ref, o_ref, *u_ref,
                tm, row0, n_batch, ctx_len):
    b = pl.program_id(0)
    d = D_MODEL
    row = row0 + pl.program_id(1) * tm + lax.broadcasted_iota(jnp.int32, (tm, 1), 0)
    is_ctx = row < ctx_len
    gate = _modulation_rows(mod_ref, b, n_batch, is_ctx, 2 * d, 3 * d)
    y = _dot(m_ref[0], w_ref[...])
    z = DEEPNORM_ALPHA * x_ref[0] + gate * y
    zc = z - jnp.mean(z, axis=-1, keepdims=True)
    var = jnp.mean(zc * zc, axis=-1, keepdims=True)
    x_new = zc * lax.rsqrt(var + NORM_EPS) * g_ref[...] + b_ref[...]
    o_ref[0] = x_new
    if u_ref:
        u_ref[0][0] = _modulate(x_new, nxt_ref, b, n_batch, is_ctx)


def _out_call(merged, xs, mod, w_out, ln_g, ln_b, *, layer, last, n_batch, ctx_len, tm):
    bsz, t, d = xs.shape
    depth = mod.shape[0]
    blk0 = ctx_len // tm if last else 0
    assert not last or ctx_len % tm == 0
    rows = t - blk0 * tm
    kern = functools.partial(_out_kernel, tm=tm, row0=blk0 * tm, n_batch=n_batch,
                             ctx_len=ctx_len)
    tok_in = pl.BlockSpec((1, tm, d), lambda b, i: (b, i + blk0, 0))
    tok_out = pl.BlockSpec((1, tm, d), lambda b, i: (b, i, 0))
    mod_at = lambda l: pl.BlockSpec((None, SUBLANES, 3 * d), lambda b, i: (l, 0, 0))
    vec = pl.BlockSpec((None, 1, d), lambda b, i: (layer, 0, 0))
    x_shape = jax.ShapeDtypeStruct((bsz, rows, d), F32)
    return pl.pallas_call(
        kern,
        grid=(bsz, rows // tm),
        in_specs=[tok_in, tok_in, mod_at(layer), mod_at(min(layer + 1, depth - 1)),
                  pl.BlockSpec((None, d, d), lambda b, i: (layer, 0, 0),
                               pipeline_mode=pl.Buffered(1)),
                  vec, vec],
        out_specs=tok_out if last else [tok_out, tok_out],
        out_shape=x_shape if last else [x_shape, jax.ShapeDtypeStruct((bsz, rows, d), BF16)],
        compiler_params=_cparams(("parallel", "parallel")),
    )(merged, xs, mod, mod, w_out, ln_g, ln_b)


PACK_COLS = 256


def _pack_kernel(w_ref, o_ref):
    row = 0
    for name in ("aq",) + ZB_ORDER + ZF_ORDER:
        src, width = _REF_SPLITS[name]
        o_ref[0, row:row + width, :] = w_ref[0, src:src + width, :].astype(BF16)
        row += width
    o_ref[0, row:, :] = jnp.zeros((W_PACKED - row, o_ref.shape[2]), BF16)


def _pack_w_in(w_in):
    depth, d, width = w_in.shape
    w_t = jnp.swapaxes(w_in, 1, 2)
    return pl.pallas_call(
        _pack_kernel,
        grid=(depth, d // PACK_COLS),
        in_specs=[pl.BlockSpec((1, width, PACK_COLS), lambda l, i: (l, 0, i))],
        out_specs=pl.BlockSpec((1, W_PACKED, PACK_COLS), lambda l, i: (l, 0, i)),
        out_shape=jax.ShapeDtypeStruct((depth, W_PACKED, d), BF16),
        compiler_params=_cparams(("parallel", "parallel")),
    )(w_t)


def _rope_tables(rows, ctx_len):
    row = jnp.repeat(jnp.arange(rows, dtype=F32), GRID_W)
    col = jnp.tile(jnp.arange(GRID_W, dtype=F32), rows)
    n_freq = HEAD_DIM // 4
    inv = ROPE_BASE ** (-jnp.arange(n_freq, dtype=F32) / n_freq)
    ang_r = row[:, None] * inv
    ang_c = col[:, None] * inv
    ang = jnp.concatenate([ang_r, ang_r, ang_c, ang_c], -1)
    cos = jnp.concatenate([jnp.ones((ctx_len, HEAD_DIM), F32), jnp.cos(ang)], 0)
    sin = jnp.concatenate([jnp.zeros((ctx_len, HEAD_DIM), F32), jnp.sin(ang)], 0)
    first = (np.arange(HEAD_DIM) % (HEAD_DIM // 2)) < HEAD_DIM // 4
    sin_lo = jnp.where(first, -sin, 0.0)
    sin_hi = jnp.where(first, 0.0, sin)
    return cos, sin_lo, sin_hi


def _pad_lanes(v, width):
    return jnp.pad(v.reshape(1, -1), ((0, 0), (0, width - v.size)))


def _tile_choices(t, ctx_len):
    def largest(cap, mult, also=0):
        best = mult
        for cand in range(mult, cap + 1, mult):
            if t % cand == 0 and also % cand == 0:
                best = cand
        return best
    return dict(proj=largest(2112, 16), attn=largest(768, LANES), conv=largest(528, SUBLANES),
                merge=largest(256, 16), out=largest(528, 16),
                out_last=largest(528, 16, also=ctx_len))


def kernel(x, c, ctx, c_ctx, ada_w, ada_b, w_in, attn_q_norm, attn_k_norm, ret_log_decay, ret_gn_w,
           ssm_conv_w, ssm_conv_b, ssm_dt_bias, ssm_a_log, ssm_d, ssm_norm_w, w_branch, w_out,
           ln_g, ln_b):
    bsz, seq, d = x.shape
    ctx_len = ctx.shape[1]
    depth = w_in.shape[0]
    t = seq + ctx_len
    chunk = LANES
    ret_chunk = 256 if ctx_len % 256 == 0 and t % 256 == 0 else chunk
    assert d == D_MODEL and bsz < SUBLANES and seq % GRID_W == 0
    assert ctx_len % chunk == 0 and t % chunk == 0
    tiles = _tile_choices(t, ctx_len)

    xs = jnp.concatenate([ctx, x], axis=1)
    cvec = jnp.zeros((SUBLANES, d), F32).at[:bsz].set(c).at[bsz].set(c_ctx)
    mod = _ada_call(cvec, ada_w, ada_b)
    cos, sin_lo, sin_hi = _rope_tables(seq // GRID_W, ctx_len)
    w_packed = _pack_w_in(w_in)
    wb = w_branch.astype(BF16)
    wo = w_out.astype(BF16)
    conv_w = jnp.pad(ssm_conv_w, ((0, 0), (0, SUBLANES - SSM_CONV), (0, 0)))
    per_layer_row = lambda v: v.reshape(depth, 1, -1)
    qn, kn = per_layer_row(attn_q_norm), per_layer_row(attn_k_norm)
    gnw, snw = per_layer_row(ret_gn_w), per_layer_row(ssm_norm_w)
    lng, lnb = per_layer_row(ln_g), per_layer_row(ln_b)

    u = _modulate_call(xs, mod, layer=0, n_batch=bsz, ctx_len=ctx_len, tm=tiles["out"])
    for l in range(depth):
        zq, zb, zf = _inproj_call(u, w_packed, cos, sin_lo, sin_hi, qn, kn, layer=l,
                                  tm=tiles["proj"])
        attn = _attn_call(zq, zb, ctx_len=ctx_len, tq=tiles["attn"], tk=tiles["attn"])
        retf, retb = _ret_call(zb, ret_log_decay[l], ctx_len=ctx_len, chunk=ret_chunk)
        xbc = _conv_call(zf, conv_w[l], ssm_conv_b[l].reshape(1, -1), ctx_len=ctx_len,
                         tc=tiles["conv"])
        ssmf, ssmb = _ssd_call(xbc, zf, _pad_lanes(ssm_dt_bias[l], LANES),
                               _pad_lanes(ssm_a_log[l], LANES),
                               jnp.repeat(ssm_d[l], SSM_HEAD_DIM).reshape(1, -1),
                               ctx_len=ctx_len, chunk=chunk)
        merged = _merge_call(attn, retf, retb, ssmf, ssmb, zb, gnw, snw, wb, layer=l,
                             tm=tiles["merge"])
        last = l == depth - 1
        out = _out_call(merged, xs, mod, wo, lng, lnb, layer=l, last=last, n_batch=bsz,
                        ctx_len=ctx_len, tm=tiles["out_last" if last else "out"])
        if last:
            return out
        xs, u = out
```

```python
import functools

import numpy as np
import jax
import jax.numpy as jnp
from jax import lax
from jax.experimental import pallas as pl
from jax.experimental.pallas import tpu as pltpu

F32 = jnp.float32
BF16 = jnp.bfloat16

D_MODEL = 2048
GRID_W = 64
HEAD_DIM = 128
ATTN_HEADS = 8
ATTN_KV_HEADS = 2
ATTN_GROUP = ATTN_HEADS // ATTN_KV_HEADS
ROPE_BASE = 10000.0
RET_HEADS = 4
RET_QK_DIM = 128
RET_V_DIM = 256
SSM_INNER = 1024
SSM_HEAD_DIM = 64
SSM_HEADS = 16
SSM_GROUPS = 2
SSM_HPG = SSM_HEADS // SSM_GROUPS
SSM_STATE = 128
SSM_CONV = 5
N_BRANCHES = 3
BRANCH_WIDTH = 1024
DEPTH = 4
DEEPNORM_ALPHA = (2 * DEPTH) ** 0.25
NORM_EPS = 1e-6
ATTN_Q_SCALE = HEAD_DIM ** -0.5 * float(np.log2(np.e))

_REF_SPLITS = dict(aq=(0, 1024), ak=(1024, 256), av=(1280, 256), ag=(1536, 1024),
                   rq=(2560, 512), rk=(3072, 512), rv=(3584, 1024), rg=(4608, 1024),
                   sx=(5632, 1024), sb=(6656, 256), sc=(6912, 256), sdt=(7168, 32),
                   sz=(7200, 1024), mg=(8224, 6144))

LANES = 128
SUBLANES = 8
VMEM_LIMIT = 56 * 1024 * 1024

PROJ_TN = 512
ZQ_TILES = ATTN_HEADS * HEAD_DIM // PROJ_TN
ZB_ORDER = ("mg", "ag", "rg", "sz", "rv", "ak", "av", "rq", "rk")
ZB_MG_OFF, ZB_AG_OFF, ZB_RG_OFF, ZB_SZ_OFF = 0, 6144, 7168, 8192
ZB_RV_OFF, ZB_AK_OFF, ZB_AV_OFF, ZB_RQ_OFF, ZB_RK_OFF = 9216, 10240, 10496, 10752, 11264
ZB_W = 11776
ZB_TILES = ZB_W // PROJ_TN
ZB_RAW_TILES = ZB_AK_OFF // PROJ_TN
ZF_TILE0 = ZQ_TILES + ZB_TILES
ZF_ORDER = ("sx", "sb", "sc", "sdt")
ZF_USED = 1024 + 256 + 256 + 32
ZF_W = 2048
ZF_XBC_OFF = 0
ZF_XBC_W = 1536
ZF_DT_OFF = 1536
W_PACKED = ZQ_TILES * PROJ_TN + ZB_W + ZF_W


def _sigmoid(x):
    return 0.5 * jnp.tanh(0.5 * x) + 0.5


def _silu(x):
    return x * _sigmoid(x)


def _cparams(sem):
    return pltpu.CompilerParams(dimension_semantics=sem, vmem_limit_bytes=VMEM_LIMIT)


def _dot(a, b):
    return jnp.dot(a, b, preferred_element_type=F32)


def _dot_nt(a, b):
    return lax.dot_general(a, b, (((1,), (1,)), ((), ())), preferred_element_type=F32)


def _dot_tn(a, b):
    return lax.dot_general(a, b, (((0,), (0,)), ((), ())), preferred_element_type=F32)


def _ada_kernel(c_ref, w_ref, b_ref, o_ref):
    s = _silu(c_ref[...]).astype(BF16)
    o_ref[0] = _dot(s, w_ref[0].astype(BF16)) + b_ref[0]


def _ada_call(cvec, ada_w, ada_b):
    depth, d, n = ada_w.shape
    tn = 768
    return pl.pallas_call(
        _ada_kernel,
        grid=(depth, n // tn),
        in_specs=[pl.BlockSpec((SUBLANES, d), lambda l, j: (0, 0)),
                  pl.BlockSpec((1, d, tn), lambda l, j: (l, 0, j)),
                  pl.BlockSpec((1, 1, tn), lambda l, j: (l, 0, j))],
        out_specs=pl.BlockSpec((1, SUBLANES, tn), lambda l, j: (l, 0, j)),
        out_shape=jax.ShapeDtypeStruct((depth, SUBLANES, n), F32),
        compiler_params=_cparams(("parallel", "parallel")),
    )(cvec, ada_w, ada_b.reshape(depth, 1, n))


def _rope(x, cos, sin_lo, sin_hi):
    return x * cos + pltpu.roll(x, 96, 1) * sin_lo + pltpu.roll(x, 32, 1) * sin_hi


def _modulation_rows(mod_ref, b, n_batch, is_ctx, lo, hi):
    mb = mod_ref[pl.ds(b, 1), lo:hi]
    mc = mod_ref[pl.ds(n_batch, 1), lo:hi]
    return jnp.where(is_ctx, mc, mb)


def _inproj_kernel(u_ref, w_ref, cos_ref, sl_ref, sh_ref, qn_ref, kn_ref,
                   zq_ref, zb_ref, zf_ref):
    j = pl.program_id(2)
    jb = j - ZQ_TILES

    def project():
        return _dot_nt(u_ref[0], w_ref[...])

    def rope(xh):
        return _rope(xh, cos_ref[...], sl_ref[...], sh_ref[...])

    def normed(w_row):
        def fn(xh):
            ms = jnp.mean(xh * xh, axis=-1, keepdims=True)
            return rope(xh * lax.rsqrt(ms + NORM_EPS) * w_row)
        return fn

    @pl.when((jb >= 0) & (jb < ZB_RAW_TILES))
    def _():
        zb_ref[0] = project().astype(BF16)

    @pl.when((j < ZQ_TILES) | (jb >= ZB_RAW_TILES))
    def _():
        acc = project()

        @pl.when(j >= ZF_TILE0)
        def _():
            zf_ref[0] = acc

        def head(h):
            return acc[:, h * LANES:(h + 1) * LANES]

        def per_head(fn, n):
            return jnp.concatenate([fn(head(h)) for h in range(n)], axis=1)

        @pl.when(j < ZQ_TILES)
        def _():
            fn = normed(qn_ref[...])
            for h in range(ATTN_GROUP):
                zq_ref[0, h] = (fn(head(h)) * ATTN_Q_SCALE).astype(BF16)

        @pl.when(jb == ZB_AK_OFF // PROJ_TN)
        def _():
            k = per_head(normed(kn_ref[...]), 2)
            zb_ref[0] = jnp.concatenate([k, acc[:, 2 * LANES:]], axis=1).astype(BF16)

        @pl.when(jb == ZB_RQ_OFF // PROJ_TN)
        def _():
            zb_ref[0] = per_head(rope, 4).astype(BF16)

        @pl.when(jb == ZB_RK_OFF // PROJ_TN)
        def _():
            zb_ref[0] = per_head(lambda xh: rope(xh * (RET_QK_DIM ** -0.5)), 4).astype(BF16)


def _inproj_call(u, w, cos, sin_lo, sin_hi, qn, kn, *, layer, tm):
    bsz, t, d = u.shape
    rope_spec = pl.BlockSpec((tm, LANES), lambda b, i, j: (i, 0))
    vec_spec = pl.BlockSpec((None, 1, LANES), lambda b, i, j: (layer, 0, 0))
    return pl.pallas_call(
        _inproj_kernel,
        grid=(bsz, t // tm, W_PACKED // PROJ_TN),
        in_specs=[pl.BlockSpec((1, tm, d), lambda b, i, j: (b, i, 0)),
                  pl.BlockSpec((None, PROJ_TN, d), lambda b, i, j: (layer, j, 0)),
                  rope_spec, rope_spec, rope_spec, vec_spec, vec_spec],
        out_specs=[pl.BlockSpec((1, ATTN_GROUP, tm, HEAD_DIM),
                                lambda b, i, j: (b, jnp.minimum(j, ZQ_TILES - 1), i, 0)),
                   pl.BlockSpec((1, tm, PROJ_TN),
                                lambda b, i, j: (b, i, jnp.clip(j - ZQ_TILES, 0, ZB_TILES - 1))),
                   pl.BlockSpec((1, tm, PROJ_TN),
                                lambda b, i, j: (b, i, jnp.maximum(j - ZF_TILE0, 0)))],
        out_shape=[jax.ShapeDtypeStruct((bsz, ATTN_HEADS, t, HEAD_DIM), BF16),
                   jax.ShapeDtypeStruct((bsz, t, ZB_W), BF16),
                   jax.ShapeDtypeStruct((bsz, t, ZF_W), F32)],
        compiler_params=_cparams(("parallel", "parallel", "arbitrary")),
    )(u, w, cos, sin_lo, sin_hi, qn, kn)


NEG_BIG = -1e30
AHEAD = 1
ONES_ROWS = 2 * SUBLANES
BOUND_MARGIN = 1.02
MAX_SAFE_SHIFT = 50.0


def _attn_kernel(q_ref, k_ref, v_ref, o_ref, m_ref, acc_ref, s_ref, kmax_ref, *,
                 tq, tk, rb, t_len, ctx_len, unroll):
    qi = pl.program_id(2)
    nr = tq // rb
    per_kb = ATTN_GROUP * nr
    n_units = (t_len // tk) * per_kb

    @pl.when(qi == 0)
    def _():
        k = k_ref[0].astype(F32)
        kmax_ref[...] = jnp.broadcast_to(
            jnp.max(jnp.sum(k * k, axis=1, keepdims=True), axis=0, keepdims=True),
            kmax_ref.shape)

    q_all = q_ref[0].reshape(per_kb * rb, HEAD_DIM).astype(F32)
    qsq = _dot_nt(jnp.ones((SUBLANES, HEAD_DIM), BF16), (q_all * q_all).astype(BF16))[0:1]
    shift = jnp.sqrt(qsq * kmax_ref[0:1, 0:1]) * BOUND_MARGIN
    for slot in range(per_kb):
        m_ref[slot] = shift[:, slot * rb:(slot + 1) * rb]
    bounded = jnp.max(shift) <= MAX_SAFE_SHIFT
    acc_ref[...] = jnp.zeros(acc_ref.shape, F32)

    def unit(n):
        kb = n // per_kb
        return kb, n - kb * per_kb

    def scores(n):
        kb, slot = unit(n)
        h = slot // nr
        r0 = pl.multiple_of((slot - h * nr) * rb, rb)
        k0 = pl.multiple_of(kb * tk, tk)
        return _dot_nt(k_ref[0, pl.ds(k0, tk), :], q_ref[0, h, pl.ds(r0, rb), :])

    def update(n, s, mixed, fixed_shift):
        kb, slot = unit(n)
        if mixed:
            r0 = (slot - (slot // nr) * nr) * rb
            qrow = r0 + lax.broadcasted_iota(jnp.int32, (1, rb), 1)
            key = kb * tk + lax.broadcasted_iota(jnp.int32, (tk, 1), 0)
            s = jnp.where(qrow < ctx_len, jnp.where(key < ctx_len, s, NEG_BIG), s)
        if fixed_shift:
            p = jnp.exp2(s - m_ref[slot]).astype(BF16)
            acc_ref[slot] += _dot(v_ref[0, 0, kb], p)
        else:
            m_prev = m_ref[slot]
            m_new = jnp.maximum(m_prev, jnp.max(s, axis=0, keepdims=True))
            alpha = jnp.exp2(m_prev - m_new)
            p = jnp.exp2((s - m_new).astype(BF16))
            m_ref[slot] = m_new
            acc_ref[slot] = alpha * acc_ref[slot] + _dot(v_ref[0, 0, kb], p)

    def run(mixed, fixed_shift):
        if not fixed_shift:
            m_ref[...] = jnp.full(m_ref.shape, NEG_BIG, F32)
        for a in range(AHEAD):
            s_ref[a] = scores(a)

        def body(it, carry):
            n0 = it * unroll
            s = [s_ref[a] for a in range(AHEAD)]
            for u in range(0, unroll, AHEAD):
                s_next = [scores(jnp.minimum(n0 + u + AHEAD + a, n_units - 1))
                          for a in range(AHEAD)]
                for a in range(AHEAD):
                    update(n0 + u + a, s[a], mixed, fixed_shift)
                s = s_next
            for a in range(AHEAD):
                s_ref[a] = s[a]
            return carry

        lax.fori_loop(0, n_units // unroll, body, 0)

    first = qi == 0
    later = jnp.logical_not(first)
    unbounded = jnp.logical_not(bounded)
    pl.when(first & bounded)(lambda: run(True, True))
    pl.when(later & bounded)(lambda: run(False, True))
    pl.when(first & unbounded)(lambda: run(True, False))
    pl.when(later & unbounded)(lambda: run(False, False))

    for h in range(ATTN_GROUP):
        for r in range(nr):
            slot = h * nr + r
            o_t = acc_ref[slot, :HEAD_DIM] * (1.0 / acc_ref[slot, HEAD_DIM:HEAD_DIM + 1])
            o_ref[0, r * rb:(r + 1) * rb, h * HEAD_DIM:(h + 1) * HEAD_DIM] = o_t.T


def _attn_call(zq, zb, *, ctx_len, tq, tk):
    bsz, t, _ = zb.shape
    assert ctx_len <= tq and ctx_len <= tk
    rb = 256 if tq % 256 == 0 else tq
    per_kb = ATTN_GROUP * (tq // rb)
    unroll = max(u for u in range(AHEAD, 25, AHEAD) if (t // tk * per_kb) % u == 0)
    k_blk = ZB_AK_OFF // HEAD_DIM
    v_t = zb[:, :, ZB_AV_OFF:ZB_AV_OFF + ATTN_KV_HEADS * HEAD_DIM]
    v_t = v_t.reshape(bsz, t // tk, tk, ATTN_KV_HEADS, HEAD_DIM).transpose(0, 3, 1, 4, 2)
    v_t = jnp.concatenate([v_t, jnp.ones(v_t.shape[:3] + (ONES_ROWS, tk), v_t.dtype)], axis=3)
    kern = functools.partial(_attn_kernel, tq=tq, tk=tk, rb=rb, t_len=t, ctx_len=ctx_len,
                             unroll=unroll)
    return pl.pallas_call(
        kern,
        grid=(bsz, ATTN_KV_HEADS, t // tq),
        in_specs=[pl.BlockSpec((1, ATTN_GROUP, tq, HEAD_DIM), lambda b, g, qi: (b, g, qi, 0)),
                  pl.BlockSpec((1, t, HEAD_DIM), lambda b, g, qi: (b, 0, k_blk + g)),
                  pl.BlockSpec((1, 1, t // tk, HEAD_DIM + ONES_ROWS, tk),
                               lambda b, g, qi: (b, g, 0, 0, 0))],
        out_specs=pl.BlockSpec((1, tq, ATTN_GROUP * HEAD_DIM), lambda b, g, qi: (b, qi, g)),
        out_shape=jax.ShapeDtypeStruct((bsz, t, ATTN_HEADS * HEAD_DIM), F32),
        scratch_shapes=[pltpu.VMEM((per_kb, 1, rb), F32),
                        pltpu.VMEM((per_kb, HEAD_DIM + ONES_ROWS, rb), F32),
                        pltpu.VMEM((AHEAD, tk, rb), F32),
                        pltpu.VMEM((SUBLANES, LANES), F32)],
        compiler_params=_cparams(("parallel", "parallel", "arbitrary")),
    )(zq, zb, v_t)


def _bwd_chunk(p, n_ctx_chunks, n_chunks):
    return jnp.where(p < n_ctx_chunks, n_ctx_chunks - 1 - p, n_chunks + n_ctx_chunks - 1 - p)


def _ret_kernel(ld_ref, qf_ref, kf_ref, vf_ref, qb_ref, kb_ref, vb_ref, yf_ref, yb_ref,
                sf_ref, sb_ref, *, chunk):
    c = chunk
    p = pl.program_id(1)

    @pl.when(p == 0)
    def _():
        sf_ref[...] = jnp.zeros(sf_ref.shape, F32)
        sb_ref[...] = jnp.zeros(sb_ref.shape, F32)

    ii = lax.broadcasted_iota(jnp.int32, (c, c), 0)
    jj = lax.broadcasted_iota(jnp.int32, (c, c), 1)
    dist = (ii - jj).astype(F32)
    pos = lax.broadcasted_iota(jnp.int32, (c, 1), 0).astype(F32)
    full = jnp.full((1, 1), float(c), F32)

    for h in range(RET_HEADS):
        ldf = ld_ref[0, h]
        ldb = ld_ref[1, h]
        qs = slice(h * RET_QK_DIM, (h + 1) * RET_QK_DIM)
        vs = slice(h * RET_V_DIM, (h + 1) * RET_V_DIM)
        decay = jnp.where(dist >= 0, jnp.exp(ldf * dist), 0.0) \
            + jnp.where(dist <= 0, jnp.exp(-ldb * dist), 0.0)
        q = qf_ref[0, :, qs]
        k = kf_ref[0, :, qs]
        v = vf_ref[0, :, vs]
        y = _dot((_dot_nt(q, k) * decay).astype(BF16), v)
        y = y + _dot(q, sf_ref[h].astype(BF16)) * jnp.exp(ldf * (pos + 1.0))
        yf_ref[0, :, vs] = y
        wv = (v.astype(F32) * jnp.exp(ldf * (c - 1.0 - pos))).astype(BF16)
        sf_ref[h] = sf_ref[h] * jnp.exp(ldf * full) + _dot_tn(k, wv)
        q = qb_ref[0, :, qs]
        k = kb_ref[0, :, qs]
        v = vb_ref[0, :, vs]
        yb_ref[0, :, vs] = _dot(q, sb_ref[h].astype(BF16)) * jnp.exp(ldb * (c - pos))
        wv = (v.astype(F32) * jnp.exp(ldb * pos)).astype(BF16)
        sb_ref[h] = sb_ref[h] * jnp.exp(ldb * full) + _dot_tn(k, wv)


def _ret_call(zb, log_decay, *, ctx_len, chunk):
    bsz, t, _ = zb.shape
    nc = t // chunk
    ncc = ctx_len // chunk
    qw = RET_HEADS * RET_QK_DIM
    vw = RET_HEADS * RET_V_DIM
    q_blk, k_blk, v_blk = ZB_RQ_OFF // qw, ZB_RK_OFF // qw, ZB_RV_OFF // vw
    bwd = functools.partial(_bwd_chunk, n_ctx_chunks=ncc, n_chunks=nc)
    kern = functools.partial(_ret_kernel, chunk=chunk)
    out = jax.ShapeDtypeStruct((bsz, t, vw), F32)
    return pl.pallas_call(
        kern,
        grid=(bsz, nc),
        in_specs=[pl.BlockSpec(memory_space=pltpu.SMEM),
                  pl.BlockSpec((1, chunk, qw), lambda b, p: (b, p, q_blk)),
                  pl.BlockSpec((1, chunk, qw), lambda b, p: (b, p, k_blk)),
                  pl.BlockSpec((1, chunk, vw), lambda b, p: (b, p, v_blk)),
                  pl.BlockSpec((1, chunk, qw), lambda b, p: (b, bwd(p), q_blk)),
                  pl.BlockSpec((1, chunk, qw), lambda b, p: (b, bwd(p), k_blk)),
                  pl.BlockSpec((1, chunk, vw), lambda b, p: (b, bwd(p), v_blk))],
        out_specs=[pl.BlockSpec((1, chunk, vw), lambda b, p: (b, p, 0)),
                   pl.BlockSpec((1, chunk, vw), lambda b, p: (b, bwd(p), 0))],
        out_shape=[out, out],
        scratch_shapes=[pltpu.VMEM((RET_HEADS, RET_QK_DIM, RET_V_DIM), F32),
                        pltpu.VMEM((RET_HEADS, RET_QK_DIM, RET_V_DIM), F32)],
        compiler_params=_cparams(("parallel", "arbitrary")),
    )(log_decay, zb, zb, zb, zb, zb, zb)


CONV_HALO = SUBLANES


def _conv_kernel(prev_ref, cur_ref, next_ref, w_ref, b_ref, o_ref, e_ref, *, tc, t_len, ctx_len):
    i = pl.program_id(1)
    e_ref[0:CONV_HALO] = prev_ref[0]
    e_ref[CONV_HALO:CONV_HALO + tc] = cur_ref[0]
    e_ref[CONV_HALO + tc:] = next_ref[0]
    pad = (SSM_CONV - 1) // 2
    lo = i * tc

    def conv(masked):
        tok = lo + lax.broadcasted_iota(jnp.int32, (tc, 1), 0)
        acc = jnp.broadcast_to(b_ref[...], (tc, ZF_XBC_W))
        for tap in range(SSM_CONV):
            off = tap - pad
            src = e_ref[CONV_HALO + off:CONV_HALO + off + tc, :]
            if masked:
                nb = tok + off
                same = ((tok - ctx_len) ^ (nb - ctx_len)) >= 0
                src = jnp.where(same & (nb >= 0) & (nb < t_len), src, 0.0)
            acc = acc + src * w_ref[tap:tap + 1, :]
        o_ref[0] = _silu(acc)

    def near(edge):
        return (lo - pad < edge) & (edge < lo + tc + pad)
    touches = near(0) | near(ctx_len) | near(t_len)
    pl.when(touches)(lambda: conv(True))
    pl.when(jnp.logical_not(touches))(lambda: conv(False))


def _conv_call(zf, conv_w, conv_b, *, ctx_len, tc):
    bsz, t, _ = zf.shape
    col = ZF_XBC_OFF // ZF_XBC_W
    per = tc // CONV_HALO
    last = t // CONV_HALO - 1
    kern = functools.partial(_conv_kernel, tc=tc, t_len=t, ctx_len=ctx_len)
    return pl.pallas_call(
        kern,
        grid=(bsz, t // tc),
        in_specs=[pl.BlockSpec((1, CONV_HALO, ZF_XBC_W),
                               lambda b, i: (b, jnp.maximum(i * per - 1, 0), col)),
                  pl.BlockSpec((1, tc, ZF_XBC_W), lambda b, i: (b, i, col)),
                  pl.BlockSpec((1, CONV_HALO, ZF_XBC_W),
                               lambda b, i: (b, jnp.minimum((i + 1) * per, last), col)),
                  pl.BlockSpec((SUBLANES, ZF_XBC_W), lambda b, i: (0, 0)),
                  pl.BlockSpec((1, ZF_XBC_W), lambda b, i: (0, 0))],
        out_specs=pl.BlockSpec((1, tc, ZF_XBC_W), lambda b, i: (b, i, 0)),
        out_shape=jax.ShapeDtypeStruct((bsz, t, ZF_XBC_W), F32),
        scratch_shapes=[pltpu.VMEM((tc + 2 * CONV_HALO, ZF_XBC_W), F32)],
        compiler_params=_cparams(("parallel", "parallel")),
    )(zf, zf, zf, conv_w, conv_b)


def _split3(x):
    p1 = x.astype(BF16)
    r1 = x - p1.astype(F32)
    p2 = r1.astype(BF16)
    p3 = (r1 - p2.astype(F32)).astype(BF16)
    return p1, p2, p3


def _cumsum_rows(x, tri):
    p1, p2, p3 = _split3(x)
    return _dot(tri, p1) + _dot(tri, p2) + _dot(tri, p3)


def _expand_heads(a, sel_ref):
    hi = a.astype(BF16)
    lo = (a - hi.astype(F32)).astype(BF16)
    return _dot(jnp.concatenate([hi, lo], axis=1), sel_ref[...])


def _head_selector(off):
    sel = np.zeros((LANES, SSM_INNER), np.float32)
    for h in range(SSM_HEADS):
        sel[off + h, h * SSM_HEAD_DIM:(h + 1) * SSM_HEAD_DIM] = 1.0
    return jnp.asarray(np.concatenate([sel, sel], axis=0), BF16)


def _ssd_kernel(xf_ref, df_ref, xb_ref, db_ref, bias_ref, alog_ref, skip_ref, self_ref, selb_ref,
                yf_ref, yb_ref, sf_ref, sb_ref, *, chunk, sub):
    c = chunk
    p = pl.program_id(1)
    nh = SSM_HEADS
    gw = SSM_HPG * SSM_HEAD_DIM
    bc_w = SSM_GROUPS * SSM_STATE

    @pl.when(p == 0)
    def _():
        sf_ref[...] = jnp.zeros(sf_ref.shape, F32)
        sb_ref[...] = jnp.zeros(sb_ref.shape, F32)

    ii = lax.broadcasted_iota(jnp.int32, (c, c), 0)
    jj = lax.broadcasted_iota(jnp.int32, (c, c), 1)
    lower = ii >= jj
    upper = jj >= ii
    tri = jnp.where(lower, 1.0, 0.0).astype(BF16)
    lane_lo = lax.broadcasted_iota(jnp.int32, (c, LANES), 1) < SSM_HEAD_DIM
    lane_hi = jnp.logical_not(lane_lo)

    def operands(x_ref, rows):
        xs = x_ref[0, rows, 0:SSM_INNER]
        bm = x_ref[0, rows, SSM_INNER:SSM_INNER + bc_w].astype(BF16)
        cm = x_ref[0, rows, SSM_INNER + bc_w:].astype(BF16)
        return xs, bm, cm

    def gates(d_ref, rows):
        raw = d_ref[0, rows, :] + bias_ref[...]
        dt = jnp.maximum(raw, 0.0) + jnp.log1p(jnp.exp(-jnp.abs(raw)))
        la = dt * -jnp.exp(alog_ref[...])
        acum = _cumsum_rows(la, tri)
        return dt, acum, acum - la

    def within(xs, bm, cm, dt, acum, excl):
        dt_t = dt.T
        acum_t = acum.T
        excl_t = excl.T
        xs_b = xs.astype(BF16)
        pieces = []
        for g in range(SSM_GROUPS):
            gs = slice(g * SSM_STATE, (g + 1) * SSM_STATE)
            scores = _dot_nt(cm[:, gs], bm[:, gs])
            for m in range(SSM_HPG // 2):
                pair = g * SSM_HPG // 2 + m
                xp = xs_b[:, pair * LANES:(pair + 1) * LANES]
                acc = None
                for e in range(2):
                    hf = 2 * pair + e
                    hb = nh + hf
                    dec_f = jnp.where(lower, jnp.exp(acum[:, hf:hf + 1] - acum_t[hf:hf + 1, :]), 0.0)
                    dec_b = jnp.where(upper, jnp.exp(excl_t[hb:hb + 1, :] - excl[:, hb:hb + 1]), 0.0)
                    wm = scores * (dec_f * dt_t[hf:hf + 1, :] + dec_b * dt_t[hb:hb + 1, :])
                    xh = jnp.where(lane_lo if e == 0 else lane_hi, xp, 0.0)
                    part = _dot(wm.astype(BF16), xh)
                    acc = part if acc is None else acc + part
                pieces.append(acc)
        return jnp.concatenate(pieces, axis=1)

    def carried(s_ref, xs, bm, cm, carry, wdt, keep):
        xw = (xs * wdt).astype(BF16)
        inter = []
        for g in range(SSM_GROUPS):
            gs = slice(g * SSM_STATE, (g + 1) * SSM_STATE)
            cs = slice(g * gw, (g + 1) * gw)
            inter.append(_dot(cm[:, gs], s_ref[g].astype(BF16)))
            s_ref[g] = s_ref[g] * keep[:, cs] + _dot_tn(bm[:, gs], xw[:, cs])
        return jnp.concatenate(inter, axis=1) * carry

    rows = [pl.ds(i * c, c) for i in range(sub)]
    fwd = [operands(xf_ref, r) + gates(df_ref, r) for r in rows]
    bwd = [operands(xb_ref, r) + gates(db_ref, r) for r in rows]

    fwd_scale = []
    for xs, bm, cm, dt, acum, excl in fwd:
        last = acum[c - 1:c, :]
        both = _expand_heads(
            jnp.concatenate([jnp.exp(acum), jnp.exp(last - acum) * dt], axis=0), self_ref)
        fwd_scale.append((both[:c], both[c:], both[c - 1:c]))
    bwd_scale = []
    for xs, bm, cm, dt, acum, excl in bwd:
        total = acum[c - 1:c, :]
        both = _expand_heads(
            jnp.concatenate([jnp.exp(total - excl), jnp.exp(excl) * dt], axis=0), selb_ref)
        bwd_scale.append((both[:c], both[c:], both[0:1]))

    y_within = [within(*f) for f in fwd]

    for i in range(sub):
        xs, bm, cm = fwd[i][:3]
        y = y_within[i] + carried(sf_ref, xs, bm, cm, *fwd_scale[i]) + skip_ref[...] * xs
        yf_ref[0, rows[i], :] = y
    for i in reversed(range(sub)):
        xs, bm, cm = bwd[i][:3]
        yb_ref[0, rows[i], :] = carried(sb_ref, xs, bm, cm, *bwd_scale[i])


def _ssd_call(xbc, zf, bias_row, alog_row, skip_row, *, ctx_len, chunk):
    bsz, t, _ = xbc.shape
    sub = 2 if ctx_len % (2 * chunk) == 0 and t % (2 * chunk) == 0 else 1
    blk = sub * chunk
    nblk = t // blk
    dt_blk = ZF_DT_OFF // LANES
    bwd = functools.partial(_bwd_chunk, n_ctx_chunks=ctx_len // blk, n_chunks=nblk)
    kern = functools.partial(_ssd_kernel, chunk=chunk, sub=sub)
    out = jax.ShapeDtypeStruct((bsz, t, SSM_INNER), F32)
    row = lambda w: pl.BlockSpec((1, w), lambda b, p: (0, 0))
    return pl.pallas_call(
        kern,
        grid=(bsz, nblk),
        in_specs=[pl.BlockSpec((1, blk, ZF_XBC_W), lambda b, p: (b, p, 0)),
                  pl.BlockSpec((1, blk, LANES), lambda b, p: (b, p, dt_blk)),
                  pl.BlockSpec((1, blk, ZF_XBC_W), lambda b, p: (b, bwd(p), 0)),
                  pl.BlockSpec((1, blk, LANES), lambda b, p: (b, bwd(p), dt_blk)),
                  row(LANES), row(LANES), row(SSM_INNER),
                  pl.BlockSpec((2 * LANES, SSM_INNER), lambda b, p: (0, 0)),
                  pl.BlockSpec((2 * LANES, SSM_INNER), lambda b, p: (0, 0))],
        out_specs=[pl.BlockSpec((1, blk, SSM_INNER), lambda b, p: (b, p, 0)),
                   pl.BlockSpec((1, blk, SSM_INNER), lambda b, p: (b, bwd(p), 0))],
        out_shape=[out, out],
        scratch_shapes=[pltpu.VMEM((SSM_GROUPS, SSM_STATE, SSM_HPG * SSM_HEAD_DIM), F32),
                        pltpu.VMEM((SSM_GROUPS, SSM_STATE, SSM_HPG * SSM_HEAD_DIM), F32)],
        compiler_params=_cparams(("parallel", "arbitrary")),
    )(xbc, zf, xbc, zf, bias_row, alog_row, skip_row,
      _head_selector(0), _head_selector(SSM_HEADS))


def _merge_kernel(attn_ref, retf_ref, retb_ref, ssmf_ref, ssmb_ref, mg_ref, ag_ref, rg_ref,
                  sz_ref, gnw_ref, snw_ref, wb_ref, o_ref):
    d = D_MODEL
    gate_in = lambda ref, lo=0, hi=None: ref[0, :, lo:hi].astype(F32)
    a = attn_ref[0] * _silu(gate_in(ag_ref))

    ret = retf_ref[0] + retb_ref[0]
    parts = []
    for h in range(RET_HEADS):
        r = ret[:, h * RET_V_DIM:(h + 1) * RET_V_DIM]
        rc = r - jnp.mean(r, axis=-1, keepdims=True)
        var = jnp.mean(rc * rc, axis=-1, keepdims=True)
        parts.append(rc * lax.rsqrt(var + NORM_EPS))
    r = jnp.concatenate(parts, axis=1) * gnw_ref[...] * _silu(gate_in(rg_ref))

    s = (ssmf_ref[0] + ssmb_ref[0]) * _silu(gate_in(sz_ref))
    gwid = SSM_INNER // SSM_GROUPS
    parts = []
    for g in range(SSM_GROUPS):
        sg = s[:, g * gwid:(g + 1) * gwid]
        parts.append(sg * lax.rsqrt(jnp.mean(sg * sg, axis=-1, keepdims=True) + NORM_EPS))
    s = jnp.concatenate(parts, axis=1) * snw_ref[...]

    branches = [x.astype(BF16) for x in (a, r, s)]
    half = d // 2
    for c0 in range(0, d, half):
        merged = None
        for k, x in enumerate(branches):
            term = _sigmoid(gate_in(mg_ref, k * d + c0, k * d + c0 + half)) \
                * _dot(x, wb_ref[k, :, c0:c0 + half])
            merged = term if merged is None else merged + term
        o_ref[0, :, c0:c0 + half] = merged.astype(BF16)


def _merge_call(attn, retf, retb, ssmf, ssmb, zb, gnw, snw, wb, *, layer, tm):
    bsz, t, _ = attn.shape
    d = D_MODEL
    bw = BRANCH_WIDTH
    act = lambda blk: pl.BlockSpec((1, tm, bw), lambda b, i: (b, i, blk))
    vec = pl.BlockSpec((None, 1, bw), lambda b, i: (layer, 0, 0))
    return pl.pallas_call(
        _merge_kernel,
        grid=(bsz, t // tm),
        in_specs=[act(0), act(0), act(0), act(0), act(0),
                  pl.BlockSpec((1, tm, N_BRANCHES * d),
                               lambda b, i: (b, i, ZB_MG_OFF // (N_BRANCHES * d))),
                  act(ZB_AG_OFF // bw), act(ZB_RG_OFF // bw), act(ZB_SZ_OFF // bw),
                  vec, vec,
                  pl.BlockSpec((None, N_BRANCHES, bw, d), lambda b, i: (layer, 0, 0, 0),
                               pipeline_mode=pl.Buffered(1))],
        out_specs=pl.BlockSpec((1, tm, d), lambda b, i: (b, i, 0)),
        out_shape=jax.ShapeDtypeStruct((bsz, t, d), BF16),
        compiler_params=_cparams(("parallel", "parallel")),
    )(attn, retf, retb, ssmf, ssmb, zb, zb, zb, zb, gnw, snw, wb)


def _modulate(x, mod_ref, b, n_batch, is_ctx):
    d = D_MODEL
    shift = _modulation_rows(mod_ref, b, n_batch, is_ctx, 0, d)
    scale = _modulation_rows(mod_ref, b, n_batch, is_ctx, d, 2 * d)
    return (x * (1.0 + scale) + shift).astype(BF16)


def _modulate_kernel(x_ref, mod_ref, u_ref, *, tm, n_batch, ctx_len):
    row = pl.program_id(1) * tm + lax.broadcasted_iota(jnp.int32, (tm, 1), 0)
    u_ref[0] = _modulate(x_ref[0], mod_ref, pl.program_id(0), n_batch, row < ctx_len)


def _modulate_call(xs, mod, *, layer, n_batch, ctx_len, tm):
    bsz, t, d = xs.shape
    kern = functools.partial(_modulate_kernel, tm=tm, n_batch=n_batch, ctx_len=ctx_len)
    tok = pl.BlockSpec((1, tm, d), lambda b, i: (b, i, 0))
    return pl.pallas_call(
        kern,
        grid=(bsz, t // tm),
        in_specs=[tok, pl.BlockSpec((None, SUBLANES, 3 * d), lambda b, i: (layer, 0, 0))],
        out_specs=tok,
        out_shape=jax.ShapeDtypeStruct((bsz, t, d), BF16),
        compiler_params=_cparams(("parallel", "parallel")),
    )(xs, mod)


def _out_kernel(m_ref, x_ref, mod_ref, nxt_ref, w_ref, g_ref, b_ref, o_ref, *u_ref,
                tm, row0, n_batch, ctx_len):
    b = pl.program_id(0)
    d = D_MODEL
    row = row0 + pl.program_id(1) * tm + lax.broadcasted_iota(jnp.int32, (tm, 1), 0)
    is_ctx = row < ctx_len
    gate = _modulation_rows(mod_ref, b, n_batch, is_ctx, 2 * d, 3 * d)
    y = _dot(m_ref[0], w_ref[...])
    z = DEEPNORM_ALPHA * x_ref[0] + gate * y
    zc = z - jnp.mean(z, axis=-1, keepdims=True)
    var = jnp.mean(zc * zc, axis=-1, keepdims=True)
    x_new = zc * lax.rsqrt(var + NORM_EPS) * g_ref[...] + b_ref[...]
    o_ref[0] = x_new
    if u_ref:
        u_ref[0][0] = _modulate(x_new, nxt_ref, b, n_batch, is_ctx)


def _out_call(merged, xs, mod, w_out, ln_g, ln_b, *, layer, last, n_batch, ctx_len, tm):
    bsz, t, d = xs.shape
    depth = mod.shape[0]
    blk0 = ctx_len // tm if last else 0
    assert not last or ctx_len % tm == 0
    rows = t - blk0 * tm
    kern = functools.partial(_out_kernel, tm=tm, row0=blk0 * tm, n_batch=n_batch,
                             ctx_len=ctx_len)
    tok_in = pl.BlockSpec((1, tm, d), lambda b, i: (b, i + blk0, 0))
    tok_out = pl.BlockSpec((1, tm, d), lambda b, i: (b, i, 0))
    mod_at = lambda l: pl.BlockSpec((None, SUBLANES, 3 * d), lambda b, i: (l, 0, 0))
    vec = pl.BlockSpec((None, 1, d), lambda b, i: (layer, 0, 0))
    x_shape = jax.ShapeDtypeStruct((bsz, rows, d), F32)
    return pl.pallas_call(
        kern,
        grid=(bsz, rows // tm),
        in_specs=[tok_in, tok_in, mod_at(layer), mod_at(min(layer + 1, depth - 1)),
                  pl.BlockSpec((None, d, d), lambda b, i: (layer, 0, 0),
                               pipeline_mode=pl.Buffered(1)),
                  vec, vec],
        out_specs=tok_out if last else [tok_out, tok_out],
        out_shape=x_shape if last else [x_shape, jax.ShapeDtypeStruct((bsz, rows, d), BF16)],
        compiler_params=_cparams(("parallel", "parallel")),
    )(merged, xs, mod, mod, w_out, ln_g, ln_b)


PACK_COLS = 256


def _pack_kernel(w_ref, o_ref):
    row = 0
    for name in ("aq",) + ZB_ORDER + ZF_ORDER:
        src, width = _REF_SPLITS[name]
        o_ref[0, row:row + width, :] = w_ref[0, src:src + width, :].astype(BF16)
        row += width
    o_ref[0, row:, :] = jnp.zeros((W_PACKED - row, o_ref.shape[2]), BF16)


def _pack_w_in(w_in):
    depth, d, width = w_in.shape
    w_t = jnp.swapaxes(w_in, 1, 2)
    return pl.pallas_call(
        _pack_kernel,
        grid=(depth, d // PACK_COLS),
        in_specs=[pl.BlockSpec((1, width, PACK_COLS), lambda l, i: (l, 0, i))],
        out_specs=pl.BlockSpec((1, W_PACKED, PACK_COLS), lambda l, i: (l, 0, i)),
        out_shape=jax.ShapeDtypeStruct((depth, W_PACKED, d), BF16),
        compiler_params=_cparams(("parallel", "parallel")),
    )(w_t)


def _rope_tables(rows, ctx_len):
    row = jnp.repeat(jnp.arange(rows, dtype=F32), GRID_W)
    col = jnp.tile(jnp.arange(GRID_W, dtype=F32), rows)
    n_freq = HEAD_DIM // 4
    inv = ROPE_BASE ** (-jnp.arange(n_freq, dtype=F32) / n_freq)
    ang_r = row[:, None] * inv
    ang_c = col[:, None] * inv
    ang = jnp.concatenate([ang_r, ang_r, ang_c, ang_c], -1)
    cos = jnp.concatenate([jnp.ones((ctx_len, HEAD_DIM), F32), jnp.cos(ang)], 0)
    sin = jnp.concatenate([jnp.zeros((ctx_len, HEAD_DIM), F32), jnp.sin(ang)], 0)
    first = (np.arange(HEAD_DIM) % (HEAD_DIM // 2)) < HEAD_DIM // 4
    sin_lo = jnp.where(first, -sin, 0.0)
    sin_hi = jnp.where(first, 0.0, sin)
    return cos, sin_lo, sin_hi


def _pad_lanes(v, width):
    return jnp.pad(v.reshape(1, -1), ((0, 0), (0, width - v.size)))


def _tile_choices(t, ctx_len):
    def largest(cap, mult, also=0):
        best = mult
        for cand in range(mult, cap + 1, mult):
            if t % cand == 0 and also % cand == 0:
                best = cand
        return best
    return dict(proj=largest(2112, 16), attn=largest(768, LANES), conv=largest(528, SUBLANES),
                merge=largest(256, 16), out=largest(528, 16),
                out_last=largest(528, 16, also=ctx_len))


def kernel(x, c, ctx, c_ctx, ada_w, ada_b, w_in, attn_q_norm, attn_k_norm, ret_log_decay, ret_gn_w,
           ssm_conv_w, ssm_conv_b, ssm_dt_bias, ssm_a_log, ssm_d, ssm_norm_w, w_branch, w_out,
           ln_g, ln_b):
    bsz, seq, d = x.shape
    ctx_len = ctx.shape[1]
    depth = w_in.shape[0]
    t = seq + ctx_len
    chunk = LANES
    ret_chunk = 256 if ctx_len % 256 == 0 and t % 256 == 0 else chunk
    assert d == D_MODEL and bsz < SUBLANES and seq % GRID_W == 0
    assert ctx_len % chunk == 0 and t % chunk == 0
    tiles = _tile_choices(t, ctx_len)

    xs = jnp.concatenate([ctx, x], axis=1)
    cvec = jnp.zeros((SUBLANES, d), F32).at[:bsz].set(c).at[bsz].set(c_ctx)
    mod = _ada_call(cvec, ada_w, ada_b)
    cos, sin_lo, sin_hi = _rope_tables(seq // GRID_W, ctx_len)
    w_packed = _pack_w_in(w_in)
    wb = w_branch.astype(BF16)
    wo = w_out.astype(BF16)
    conv_w = jnp.pad(ssm_conv_w, ((0, 0), (0, SUBLANES - SSM_CONV), (0, 0)))
    per_layer_row = lambda v: v.reshape(depth, 1, -1)
    qn, kn = per_layer_row(attn_q_norm), per_layer_row(attn_k_norm)
    gnw, snw = per_layer_row(ret_gn_w), per_layer_row(ssm_norm_w)
    lng, lnb = per_layer_row(ln_g), per_layer_row(ln_b)

    u = _modulate_call(xs, mod, layer=0, n_batch=bsz, ctx_len=ctx_len, tm=tiles["out"])
    for l in range(depth):
        zq, zb, zf = _inproj_call(u, w_packed, cos, sin_lo, sin_hi, qn, kn, layer=l,
                                  tm=tiles["proj"])
        attn = _attn_call(zq, zb, ctx_len=ctx_len, tq=tiles["attn"], tk=tiles["attn"])
        retf, retb = _ret_call(zb, ret_log_decay[l], ctx_len=ctx_len, chunk=ret_chunk)
        xbc = _conv_call(zf, conv_w[l], ssm_conv_b[l].reshape(1, -1), ctx_len=ctx_len,
                         tc=tiles["conv"])
        ssmf, ssmb = _ssd_call(xbc, zf, _pad_lanes(ssm_dt_bias[l], LANES),
                               _pad_lanes(ssm_a_log[l], LANES),
                               jnp.repeat(ssm_d[l], SSM_HEAD_DIM).reshape(1, -1),
                               ctx_len=ctx_len, chunk=chunk)
        merged = _merge_call(attn, retf, retb, ssmf, ssmb, zb, gnw, snw, wb, layer=l,
                             tm=tiles["merge"])
        last = l == depth - 1
        out = _out_call(merged, xs, mod, wo, lng, lnb, layer=l, last=last, n_batch=bsz,
                        ctx_len=ctx_len, tm=tiles["out_last" if last else "out"])
        if last:
            return out
        xs, u = out
```
